```python
import jax, jax.numpy as jnp
from jax import lax
import numpy as np

D_MODEL = 1024
BATCH = 8
SEQ = 8192
DEPTH = 2

CHUNK = 64
SB_BLOCK = 128
EPS = 1e-6
GLA_HEADS = 4
GLA_HEAD_K = 64
GLA_HEAD_V = 128
GLA_DK = GLA_HEADS * GLA_HEAD_K
GLA_DV = GLA_HEADS * GLA_HEAD_V
GLA_GATE_RANK = 16
GLA_GATE_NORMALIZER = 16.0
SB_HEADS = 8
SB_HEAD_DIM = 64
SB_D = SB_HEADS * SB_HEAD_DIM
MIX_WIDTH = GLA_DV + SB_D
IN_WIDTH = 2 * GLA_DK + 2 * GLA_DV + GLA_GATE_RANK + 3 * SB_D
CONV_WIDTH = 31
D_FF = -(-8 * D_MODEL // (3 * 256)) * 256
N_EVEN = (DEPTH + 1) // 2
N_ODD = DEPTH // 2

kernel_name = "hybrid_gla_stickbreaking_conformer_trunk"


def _rms_f32(xf, g):
    return xf * lax.rsqrt(jnp.mean(xf * xf, axis=-1, keepdims=True) + EPS) * g.astype(jnp.float32)


def rms_norm(x, g):
    return _rms_f32(x.astype(jnp.float32), g).astype(x.dtype)


def gla_mixer(q, k, v, r, gate_lr, w_gate2, b_gate, g_out):
    f32 = jnp.float32
    B, T, _ = q.shape
    nc = T // CHUNK
    log_a = jax.nn.log_sigmoid((gate_lr @ w_gate2 + b_gate).astype(f32)) / GLA_GATE_NORMALIZER

    def split(t, hd):
        return t.astype(f32).reshape(B, nc, CHUNK, -1, hd).transpose(0, 3, 1, 2, 4)

    qc = split(q, GLA_HEAD_K) * (GLA_HEAD_K ** -0.5)
    kc = split(k, GLA_HEAD_K)
    vc = split(v, GLA_HEAD_V)
    bc = jnp.cumsum(split(log_a, GLA_HEAD_K), axis=3)
    b_end = bc[:, :, :, -1:, :]
    k_end = kc * jnp.exp(b_end - bc)
    scores = jnp.einsum('bhnik,bhnjk->bhnij', qc, k_end)
    intra = jnp.einsum('bhnij,bhnjv->bhniv', scores, vc)
    chunk_update = jnp.einsum('bhnjk,bhnjv->bhnkv', k_end, vc)
    chunk_decay = jnp.exp(b_end[:, :, :, 0, :])

    def step(s, inp):
        a, u = inp
        return a[..., None] * s + u, s

    s0 = jnp.zeros((B, GLA_HEADS, GLA_HEAD_K, GLA_HEAD_V), f32)
    _, s_prev = lax.scan(step, s0, (jnp.moveaxis(chunk_decay, 2, 0), jnp.moveaxis(chunk_update, 2, 0)))
    s_prev = jnp.moveaxis(s_prev, 0, 2)
    inter = jnp.einsum('bhnik,bhnkv->bhniv', qc * jnp.exp(b_end), s_prev)
    o = _rms_f32(intra + inter, g_out)
    o = o.transpose(0, 2, 3, 1, 4).reshape(B, T, GLA_DV)
    return o * jax.nn.silu(r.astype(f32))


def stick_breaking_mixer(q, k, v, g_q, g_k):
    f32 = jnp.float32
    B, T, _ = q.shape

    def heads(t):
        return t.astype(f32).reshape(B, T, SB_HEADS, SB_HEAD_DIM).transpose(0, 2, 1, 3)

    qh = _rms_f32(heads(q), g_q) * (SB_HEAD_DIM ** -0.5)
    kh = _rms_f32(heads(k), g_k)
    vh = heads(v)
    outs = []
    for blk in range(T // SB_BLOCK):
        q0 = blk * SB_BLOCK
        kv_len = q0 + SB_BLOCK
        past = np.arange(kv_len)[None, :] < np.arange(q0, kv_len)[:, None]
        z = jnp.einsum('bhtd,bhsd->bhts', qh[:, :, q0:kv_len], kh[:, :, :kv_len])
        log_keep = jnp.where(past, jax.nn.log_sigmoid(-z), 0.0)
        between = lax.cumsum(log_keep, axis=3, reverse=True) - log_keep
        w = jnp.where(past, jnp.exp(jax.nn.log_sigmoid(z) + between), 0.0)
        outs.append(jnp.einsum('bhts,bhsd->bhtd', w, vh[:, :, :kv_len]))
    o = jnp.concatenate(outs, axis=2)
    return o.transpose(0, 2, 1, 3).reshape(B, T, SB_D)


def hybrid_mixer(h, w_in, w_gate2, b_gate, g_gla, g_q, g_k, w_out):
    proj = h @ w_in
    cuts = np.cumsum([GLA_DK, GLA_DK, GLA_DV, GLA_DV, GLA_GATE_RANK, SB_D, SB_D])
    gq, gk, gv, gr, glr, sq, sk, sv = jnp.split(proj, cuts, axis=-1)
    o_gla = gla_mixer(gq, gk, gv, gr, glr, w_gate2, b_gate, g_gla)
    o_sb = stick_breaking_mixer(sq, sk, sv, g_q, g_k)
    o = jnp.concatenate([o_gla, o_sb], axis=-1).astype(h.dtype)
    return o @ w_out


def conformer_conv(h, w_pw1, b_pw1, w_dw, b_dw, ln_g, ln_b, w_pw2, b_pw2):
    a = h @ w_pw1 + b_pw1
    u = a[..., :D_MODEL] * jax.nn.sigmoid(a[..., D_MODEL:])
    u = lax.conv_general_dilated(u, w_dw[:, None, :].astype(u.dtype), window_strides=(1,),
                                 padding=[(CONV_WIDTH - 1, 0)],
                                 dimension_numbers=('NWC', 'WIO', 'NWC'),
                                 feature_group_count=D_MODEL) + b_dw
    uf = u.astype(jnp.float32)
    mu = jnp.mean(uf, axis=-1, keepdims=True)
    var = jnp.mean(jnp.square(uf - mu), axis=-1, keepdims=True)
    u = ((uf - mu) * lax.rsqrt(var + EPS) * ln_g + ln_b).astype(h.dtype)
    return jax.nn.silu(u) @ w_pw2 + b_pw2


def swiglu(h, wg, wu, wd):
    return (jax.nn.silu(h @ wg) * (h @ wu)) @ wd


def _fwd_setup_inputs(seed: int = 0) -> dict:
    key = jax.random.key(seed)
    ks = iter(jax.random.split(key, 32))

    def nrm(shape, scale):
        return jax.random.normal(next(ks), shape, jnp.float32) * scale

    def gain(shape):
        return 1.0 + nrm(shape, 0.05)

    return {
        "x": nrm((BATCH, SEQ, D_MODEL), 1.0),
        "mix_norm": gain((DEPTH, D_MODEL)),
        "ffn_norm": gain((DEPTH, D_MODEL)),
        "hy_w_in": nrm((N_EVEN, D_MODEL, IN_WIDTH), D_MODEL ** -0.5),
        "hy_w_gate2": nrm((N_EVEN, GLA_GATE_RANK, GLA_DK), GLA_GATE_RANK ** -0.5),
        "hy_b_gate": nrm((N_EVEN, GLA_DK), 0.1),
        "hy_gla_norm": gain((N_EVEN, GLA_HEAD_V)),
        "hy_sb_q_norm": gain((N_EVEN, SB_HEAD_DIM)),
        "hy_sb_k_norm": gain((N_EVEN, SB_HEAD_DIM)),
        "hy_w_out": nrm((N_EVEN, MIX_WIDTH, D_MODEL), MIX_WIDTH ** -0.5),
        "cv_w_pw1": nrm((N_ODD, D_MODEL, 2 * D_MODEL), D_MODEL ** -0.5),
        "cv_b_pw1": nrm((N_ODD, 2 * D_MODEL), 0.02),
        "cv_w_dw": nrm((N_ODD, CONV_WIDTH, D_MODEL), CONV_WIDTH ** -0.5),
        "cv_b_dw": nrm((N_ODD, D_MODEL), 0.02),
        "cv_ln_g": gain((N_ODD, D_MODEL)),
        "cv_ln_b": nrm((N_ODD, D_MODEL), 0.02),
        "cv_w_pw2": nrm((N_ODD, D_MODEL, D_MODEL), D_MODEL ** -0.5),
        "cv_b_pw2": nrm((N_ODD, D_MODEL), 0.02),
        "ffn_w_gate": nrm((DEPTH, D_MODEL, D_FF), D_MODEL ** -0.5),
        "ffn_w_up": nrm((DEPTH, D_MODEL, D_FF), D_MODEL ** -0.5),
        "ffn_w_down": nrm((DEPTH, D_FF, D_MODEL), D_FF ** -0.5),
    }


def _fwd_reference(x, mix_norm, ffn_norm, hy_w_in, hy_w_gate2, hy_b_gate, hy_gla_norm,
              hy_sb_q_norm, hy_sb_k_norm, hy_w_out, cv_w_pw1, cv_b_pw1, cv_w_dw, cv_b_dw,
              cv_ln_g, cv_ln_b, cv_w_pw2, cv_b_pw2, ffn_w_gate, ffn_w_up, ffn_w_down):
    h = x
    for layer in range(DEPTH):
        hn = rms_norm(h, mix_norm[layer])
        if layer % 2 == 0:
            i = layer // 2
            mix = hybrid_mixer(hn, hy_w_in[i], hy_w_gate2[i], hy_b_gate[i], hy_gla_norm[i],
                               hy_sb_q_norm[i], hy_sb_k_norm[i], hy_w_out[i])
        else:
            i = layer // 2
            mix = conformer_conv(hn, cv_w_pw1[i], cv_b_pw1[i], cv_w_dw[i], cv_b_dw[i],
                                 cv_ln_g[i], cv_ln_b[i], cv_w_pw2[i], cv_b_pw2[i])
        h = h + mix.astype(h.dtype)
        h = h + swiglu(rms_norm(h, ffn_norm[layer]), ffn_w_gate[layer], ffn_w_up[layer],
                       ffn_w_down[layer]).astype(h.dtype)
    return h


import jax as _jax
import jax.numpy as _jnp

TWIN_FORMAT = 'train_step'
FWD_PARAMS = ['x', 'mix_norm', 'ffn_norm', 'hy_w_in', 'hy_w_gate2', 'hy_b_gate', 'hy_gla_norm', 'hy_sb_q_norm', 'hy_sb_k_norm', 'hy_w_out', 'cv_w_pw1', 'cv_b_pw1', 'cv_w_dw', 'cv_b_dw', 'cv_ln_g', 'cv_ln_b', 'cv_w_pw2', 'cv_b_pw2', 'ffn_w_gate', 'ffn_w_up', 'ffn_w_down']
TWIN_WEIGHTS = ['mix_norm', 'ffn_norm', 'hy_w_in', 'hy_w_gate2', 'hy_b_gate', 'hy_gla_norm', 'hy_sb_q_norm', 'hy_sb_k_norm', 'hy_w_out', 'cv_w_pw1', 'cv_b_pw1', 'cv_w_dw', 'cv_b_dw', 'cv_ln_g', 'cv_ln_b', 'cv_w_pw2', 'cv_b_pw2', 'ffn_w_gate', 'ffn_w_up', 'ffn_w_down']
TWIN_DIFF_INPUT = 'x'
TWIN_INPUTS = ['x', 'mix_norm', 'ffn_norm', 'hy_w_in', 'hy_w_gate2', 'hy_b_gate', 'hy_gla_norm', 'hy_sb_q_norm', 'hy_sb_k_norm', 'hy_w_out', 'cv_w_pw1', 'cv_b_pw1', 'cv_w_dw', 'cv_b_dw', 'cv_ln_g', 'cv_ln_b', 'cv_w_pw2', 'cv_b_pw2', 'ffn_w_gate', 'ffn_w_up', 'ffn_w_down', 'loss_target', 'm_mix_norm', 'm_ffn_norm', 'm_hy_w_in', 'm_hy_w_gate2', 'm_hy_b_gate', 'm_hy_gla_norm', 'm_hy_sb_q_norm', 'm_hy_sb_k_norm', 'm_hy_w_out', 'm_cv_w_pw1', 'm_cv_b_pw1', 'm_cv_w_dw', 'm_cv_b_dw', 'm_cv_ln_g', 'm_cv_ln_b', 'm_cv_w_pw2', 'm_cv_b_pw2', 'm_ffn_w_gate', 'm_ffn_w_up', 'm_ffn_w_down', 'v_mix_norm', 'v_ffn_norm', 'v_hy_w_in', 'v_hy_w_gate2', 'v_hy_b_gate', 'v_hy_gla_norm', 'v_hy_sb_q_norm', 'v_hy_sb_k_norm', 'v_hy_w_out', 'v_cv_w_pw1', 'v_cv_b_pw1', 'v_cv_w_dw', 'v_cv_b_dw', 'v_cv_ln_g', 'v_cv_ln_b', 'v_cv_w_pw2', 'v_cv_b_pw2', 'v_ffn_w_gate', 'v_ffn_w_up', 'v_ffn_w_down']
TWIN_OUTPUTS = ['loss', 'grad_x', 'grad_mix_norm', 'grad_ffn_norm', 'grad_hy_w_in', 'grad_hy_w_gate2', 'grad_hy_b_gate', 'grad_hy_gla_norm', 'grad_hy_sb_q_norm', 'grad_hy_sb_k_norm', 'grad_hy_w_out', 'grad_cv_w_pw1', 'grad_cv_b_pw1', 'grad_cv_w_dw', 'grad_cv_b_dw', 'grad_cv_ln_g', 'grad_cv_ln_b', 'grad_cv_w_pw2', 'grad_cv_b_pw2', 'grad_ffn_w_gate', 'grad_ffn_w_up', 'grad_ffn_w_down', 'delta_mix_norm', 'delta_ffn_norm', 'delta_hy_w_in', 'delta_hy_w_gate2', 'delta_hy_b_gate', 'delta_hy_gla_norm', 'delta_hy_sb_q_norm', 'delta_hy_sb_k_norm', 'delta_hy_w_out', 'delta_cv_w_pw1', 'delta_cv_b_pw1', 'delta_cv_w_dw', 'delta_cv_b_dw', 'delta_cv_ln_g', 'delta_cv_ln_b', 'delta_cv_w_pw2', 'delta_cv_b_pw2', 'delta_ffn_w_gate', 'delta_ffn_w_up', 'delta_ffn_w_down', 'new_m_mix_norm', 'new_m_ffn_norm', 'new_m_hy_w_in', 'new_m_hy_w_gate2', 'new_m_hy_b_gate', 'new_m_hy_gla_norm', 'new_m_hy_sb_q_norm', 'new_m_hy_sb_k_norm', 'new_m_hy_w_out', 'new_m_cv_w_pw1', 'new_m_cv_b_pw1', 'new_m_cv_w_dw', 'new_m_cv_b_dw', 'new_m_cv_ln_g', 'new_m_cv_ln_b', 'new_m_cv_w_pw2', 'new_m_cv_b_pw2', 'new_m_ffn_w_gate', 'new_m_ffn_w_up', 'new_m_ffn_w_down', 'new_v_mix_norm', 'new_v_ffn_norm', 'new_v_hy_w_in', 'new_v_hy_w_gate2', 'new_v_hy_b_gate', 'new_v_hy_gla_norm', 'new_v_hy_sb_q_norm', 'new_v_hy_sb_k_norm', 'new_v_hy_w_out', 'new_v_cv_w_pw1', 'new_v_cv_b_pw1', 'new_v_cv_w_dw', 'new_v_cv_b_dw', 'new_v_cv_ln_g', 'new_v_cv_ln_b', 'new_v_cv_w_pw2', 'new_v_cv_b_pw2', 'new_v_ffn_w_gate', 'new_v_ffn_w_up', 'new_v_ffn_w_down']
TWIN_LEAF_KINDS = {'loss': 'loss', 'grad_x': 'grad_x', 'grad_mix_norm': 'grad_w', 'grad_ffn_norm': 'grad_w', 'grad_hy_w_in': 'grad_w', 'grad_hy_w_gate2': 'grad_w', 'grad_hy_b_gate': 'grad_w', 'grad_hy_gla_norm': 'grad_w', 'grad_hy_sb_q_norm': 'grad_w', 'grad_hy_sb_k_norm': 'grad_w', 'grad_hy_w_out': 'grad_w', 'grad_cv_w_pw1': 'grad_w', 'grad_cv_b_pw1': 'grad_w', 'grad_cv_w_dw': 'grad_w', 'grad_cv_b_dw': 'grad_w', 'grad_cv_ln_g': 'grad_w', 'grad_cv_ln_b': 'grad_w', 'grad_cv_w_pw2': 'grad_w', 'grad_cv_b_pw2': 'grad_w', 'grad_ffn_w_gate': 'grad_w', 'grad_ffn_w_up': 'grad_w', 'grad_ffn_w_down': 'grad_w', 'delta_mix_norm': 'delta_w', 'delta_ffn_norm': 'delta_w', 'delta_hy_w_in': 'delta_w', 'delta_hy_w_gate2': 'delta_w', 'delta_hy_b_gate': 'delta_w', 'delta_hy_gla_norm': 'delta_w', 'delta_hy_sb_q_norm': 'delta_w', 'delta_hy_sb_k_norm': 'delta_w', 'delta_hy_w_out': 'delta_w', 'delta_cv_w_pw1': 'delta_w', 'delta_cv_b_pw1': 'delta_w', 'delta_cv_w_dw': 'delta_w', 'delta_cv_b_dw': 'delta_w', 'delta_cv_ln_g': 'delta_w', 'delta_cv_ln_b': 'delta_w', 'delta_cv_w_pw2': 'delta_w', 'delta_cv_b_pw2': 'delta_w', 'delta_ffn_w_gate': 'delta_w', 'delta_ffn_w_up': 'delta_w', 'delta_ffn_w_down': 'delta_w', 'new_m_mix_norm': 'new_m', 'new_m_ffn_norm': 'new_m', 'new_m_hy_w_in': 'new_m', 'new_m_hy_w_gate2': 'new_m', 'new_m_hy_b_gate': 'new_m', 'new_m_hy_gla_norm': 'new_m', 'new_m_hy_sb_q_norm': 'new_m', 'new_m_hy_sb_k_norm': 'new_m', 'new_m_hy_w_out': 'new_m', 'new_m_cv_w_pw1': 'new_m', 'new_m_cv_b_pw1': 'new_m', 'new_m_cv_w_dw': 'new_m', 'new_m_cv_b_dw': 'new_m', 'new_m_cv_ln_g': 'new_m', 'new_m_cv_ln_b': 'new_m', 'new_m_cv_w_pw2': 'new_m', 'new_m_cv_b_pw2': 'new_m', 'new_m_ffn_w_gate': 'new_m', 'new_m_ffn_w_up': 'new_m', 'new_m_ffn_w_down': 'new_m', 'new_v_mix_norm': 'new_v', 'new_v_ffn_norm': 'new_v', 'new_v_hy_w_in': 'new_v', 'new_v_hy_w_gate2': 'new_v', 'new_v_hy_b_gate': 'new_v', 'new_v_hy_gla_norm': 'new_v', 'new_v_hy_sb_q_norm': 'new_v', 'new_v_hy_sb_k_norm': 'new_v', 'new_v_hy_w_out': 'new_v', 'new_v_cv_w_pw1': 'new_v', 'new_v_cv_b_pw1': 'new_v', 'new_v_cv_w_dw': 'new_v', 'new_v_cv_b_dw': 'new_v', 'new_v_cv_ln_g': 'new_v', 'new_v_cv_ln_b': 'new_v', 'new_v_cv_w_pw2': 'new_v', 'new_v_cv_b_pw2': 'new_v', 'new_v_ffn_w_gate': 'new_v', 'new_v_ffn_w_up': 'new_v', 'new_v_ffn_w_down': 'new_v'}


def _forward(args):
    return _fwd_reference(*[args[k] for k in FWD_PARAMS])


def _output_shape():
    def fwd():
        inp = _fwd_setup_inputs(0)
        return _fwd_reference(*[inp[k] for k in FWD_PARAMS])
    out = _jax.eval_shape(fwd)
    return out.shape, out.dtype

N_MICROBATCH = 1
ADAM_LR = 0.001
ADAM_B1 = 0.9
ADAM_B2 = 0.999
ADAM_EPS = 1e-08
ADAM_WD = 0.01
ADAM_STEP = 10
PER_EXAMPLE_BATCH_AXIS = {'x': 0, 'loss_target': 0}
SHARED_INPUTS = []
_WEIGHT_DTYPES = {'mix_norm': _jnp.float32, 'ffn_norm': _jnp.float32, 'hy_w_in': _jnp.float32, 'hy_w_gate2': _jnp.float32, 'hy_b_gate': _jnp.float32, 'hy_gla_norm': _jnp.float32, 'hy_sb_q_norm': _jnp.float32, 'hy_sb_k_norm': _jnp.float32, 'hy_w_out': _jnp.float32, 'cv_w_pw1': _jnp.float32, 'cv_b_pw1': _jnp.float32, 'cv_w_dw': _jnp.float32, 'cv_b_dw': _jnp.float32, 'cv_ln_g': _jnp.float32, 'cv_ln_b': _jnp.float32, 'cv_w_pw2': _jnp.float32, 'cv_b_pw2': _jnp.float32, 'ffn_w_gate': _jnp.float32, 'ffn_w_up': _jnp.float32, 'ffn_w_down': _jnp.float32}
MOMENT_SCALE = {'mix_norm': 1.851608e+01, 'ffn_norm': 4.987285e+01, 'hy_w_in': 6.050860e-01, 'hy_w_gate2': 9.065539e-02, 'hy_b_gate': 3.674325e-01, 'hy_gla_norm': 9.062039e+01, 'hy_sb_q_norm': 3.180642e+01, 'hy_sb_k_norm': 3.180603e+01, 'hy_w_out': 7.444468e-01, 'cv_w_pw1': 3.181351e-01, 'cv_b_pw1': 6.423610e+00, 'cv_w_dw': 9.094047e-01, 'cv_b_dw': 1.521644e+01, 'cv_ln_g': 2.893967e+01, 'cv_ln_b': 2.002399e+01, 'cv_w_pw2': 3.231908e+00, 'cv_b_pw2': 1.628786e+01, 'ffn_w_gate': 5.785154e-01, 'ffn_w_up': 4.062720e-01, 'ffn_w_down': 6.352979e-01}


def _to_microbatches(a, axis):
    t = _jnp.moveaxis(a, axis, 0)
    t = t.reshape((N_MICROBATCH, t.shape[0] // N_MICROBATCH) + t.shape[1:])
    return _jnp.moveaxis(t, 1, axis + 1)


def setup_inputs(seed: int = 0) -> dict:
    inp = _fwd_setup_inputs(seed)
    key = _jax.random.fold_in(_jax.random.key(seed), 7919)
    shape, _ = _output_shape()
    out = dict(inp)
    out["loss_target"] = _jax.random.normal(_jax.random.fold_in(key, 0), shape, _jnp.float32)
    for i, name in enumerate(TWIN_WEIGHTS):
        w = inp[name].astype(_jnp.float32)
        if MOMENT_SCALE is None:
            s = _jnp.sqrt(_jnp.mean(_jnp.square(w)) + 1e-30)
        else:
            s = MOMENT_SCALE[name]
        km, kv = _jax.random.split(_jax.random.fold_in(key, i + 1))
        out[name] = w
        out["m_" + name] = s * _jax.random.normal(km, w.shape, _jnp.float32)
        out["v_" + name] = (s * s) * _jax.random.uniform(kv, w.shape, _jnp.float32, 0.5, 1.5)
    if N_MICROBATCH > 1:
        for name, axis in PER_EXAMPLE_BATCH_AXIS.items():
            out[name] = _to_microbatches(out[name], axis)
    return {'x': out['x'], 'mix_norm': out['mix_norm'], 'ffn_norm': out['ffn_norm'], 'hy_w_in': out['hy_w_in'], 'hy_w_gate2': out['hy_w_gate2'], 'hy_b_gate': out['hy_b_gate'], 'hy_gla_norm': out['hy_gla_norm'], 'hy_sb_q_norm': out['hy_sb_q_norm'], 'hy_sb_k_norm': out['hy_sb_k_norm'], 'hy_w_out': out['hy_w_out'], 'cv_w_pw1': out['cv_w_pw1'], 'cv_b_pw1': out['cv_b_pw1'], 'cv_w_dw': out['cv_w_dw'], 'cv_b_dw': out['cv_b_dw'], 'cv_ln_g': out['cv_ln_g'], 'cv_ln_b': out['cv_ln_b'], 'cv_w_pw2': out['cv_w_pw2'], 'cv_b_pw2': out['cv_b_pw2'], 'ffn_w_gate': out['ffn_w_gate'], 'ffn_w_up': out['ffn_w_up'], 'ffn_w_down': out['ffn_w_down'], 'loss_target': out['loss_target'], 'm_mix_norm': out['m_mix_norm'], 'm_ffn_norm': out['m_ffn_norm'], 'm_hy_w_in': out['m_hy_w_in'], 'm_hy_w_gate2': out['m_hy_w_gate2'], 'm_hy_b_gate': out['m_hy_b_gate'], 'm_hy_gla_norm': out['m_hy_gla_norm'], 'm_hy_sb_q_norm': out['m_hy_sb_q_norm'], 'm_hy_sb_k_norm': out['m_hy_sb_k_norm'], 'm_hy_w_out': out['m_hy_w_out'], 'm_cv_w_pw1': out['m_cv_w_pw1'], 'm_cv_b_pw1': out['m_cv_b_pw1'], 'm_cv_w_dw': out['m_cv_w_dw'], 'm_cv_b_dw': out['m_cv_b_dw'], 'm_cv_ln_g': out['m_cv_ln_g'], 'm_cv_ln_b': out['m_cv_ln_b'], 'm_cv_w_pw2': out['m_cv_w_pw2'], 'm_cv_b_pw2': out['m_cv_b_pw2'], 'm_ffn_w_gate': out['m_ffn_w_gate'], 'm_ffn_w_up': out['m_ffn_w_up'], 'm_ffn_w_down': out['m_ffn_w_down'], 'v_mix_norm': out['v_mix_norm'], 'v_ffn_norm': out['v_ffn_norm'], 'v_hy_w_in': out['v_hy_w_in'], 'v_hy_w_gate2': out['v_hy_w_gate2'], 'v_hy_b_gate': out['v_hy_b_gate'], 'v_hy_gla_norm': out['v_hy_gla_norm'], 'v_hy_sb_q_norm': out['v_hy_sb_q_norm'], 'v_hy_sb_k_norm': out['v_hy_sb_k_norm'], 'v_hy_w_out': out['v_hy_w_out'], 'v_cv_w_pw1': out['v_cv_w_pw1'], 'v_cv_b_pw1': out['v_cv_b_pw1'], 'v_cv_w_dw': out['v_cv_w_dw'], 'v_cv_b_dw': out['v_cv_b_dw'], 'v_cv_ln_g': out['v_cv_ln_g'], 'v_cv_ln_b': out['v_cv_ln_b'], 'v_cv_w_pw2': out['v_cv_w_pw2'], 'v_cv_b_pw2': out['v_cv_b_pw2'], 'v_ffn_w_gate': out['v_ffn_w_gate'], 'v_ffn_w_up': out['v_ffn_w_up'], 'v_ffn_w_down': out['v_ffn_w_down']}


def _loss(weights, diff, rest, loss_target):
    with _jax.named_scope("forward"):
        args = {**rest, TWIN_DIFF_INPUT: diff, **{k: w.astype(_WEIGHT_DTYPES[k]) for k, w in weights.items()}}
        y = _forward(args)
    with _jax.named_scope("loss_head"):
        err = _jnp.square(y.astype(_jnp.float32) - loss_target)
        return 0.5 * _jnp.sum(_jnp.mean(err, axis=-1)) if err.ndim else 0.5 * err


def _adamw(w, g, m, v):
    m = ADAM_B1 * m + (1.0 - ADAM_B1) * g
    v = ADAM_B2 * v + (1.0 - ADAM_B2) * _jnp.square(g)
    m_hat = m / (1.0 - ADAM_B1 ** ADAM_STEP)
    v_hat = v / (1.0 - ADAM_B2 ** ADAM_STEP)
    delta = -ADAM_LR * (m_hat / (_jnp.sqrt(v_hat) + ADAM_EPS) + ADAM_WD * w)
    return delta, m, v


def reference(x, mix_norm, ffn_norm, hy_w_in, hy_w_gate2, hy_b_gate, hy_gla_norm, hy_sb_q_norm, hy_sb_k_norm, hy_w_out, cv_w_pw1, cv_b_pw1, cv_w_dw, cv_b_dw, cv_ln_g, cv_ln_b, cv_w_pw2, cv_b_pw2, ffn_w_gate, ffn_w_up, ffn_w_down, loss_target, m_mix_norm, m_ffn_norm, m_hy_w_in, m_hy_w_gate2, m_hy_b_gate, m_hy_gla_norm, m_hy_sb_q_norm, m_hy_sb_k_norm, m_hy_w_out, m_cv_w_pw1, m_cv_b_pw1, m_cv_w_dw, m_cv_b_dw, m_cv_ln_g, m_cv_ln_b, m_cv_w_pw2, m_cv_b_pw2, m_ffn_w_gate, m_ffn_w_up, m_ffn_w_down, v_mix_norm, v_ffn_norm, v_hy_w_in, v_hy_w_gate2, v_hy_b_gate, v_hy_gla_norm, v_hy_sb_q_norm, v_hy_sb_k_norm, v_hy_w_out, v_cv_w_pw1, v_cv_b_pw1, v_cv_w_dw, v_cv_b_dw, v_cv_ln_g, v_cv_ln_b, v_cv_w_pw2, v_cv_b_pw2, v_ffn_w_gate, v_ffn_w_up, v_ffn_w_down):
    given = dict(x=x, mix_norm=mix_norm, ffn_norm=ffn_norm, hy_w_in=hy_w_in, hy_w_gate2=hy_w_gate2, hy_b_gate=hy_b_gate, hy_gla_norm=hy_gla_norm, hy_sb_q_norm=hy_sb_q_norm, hy_sb_k_norm=hy_sb_k_norm, hy_w_out=hy_w_out, cv_w_pw1=cv_w_pw1, cv_b_pw1=cv_b_pw1, cv_w_dw=cv_w_dw, cv_b_dw=cv_b_dw, cv_ln_g=cv_ln_g, cv_ln_b=cv_ln_b, cv_w_pw2=cv_w_pw2, cv_b_pw2=cv_b_pw2, ffn_w_gate=ffn_w_gate, ffn_w_up=ffn_w_up, ffn_w_down=ffn_w_down, loss_target=loss_target, m_mix_norm=m_mix_norm, m_ffn_norm=m_ffn_norm, m_hy_w_in=m_hy_w_in, m_hy_w_gate2=m_hy_w_gate2, m_hy_b_gate=m_hy_b_gate, m_hy_gla_norm=m_hy_gla_norm, m_hy_sb_q_norm=m_hy_sb_q_norm, m_hy_sb_k_norm=m_hy_sb_k_norm, m_hy_w_out=m_hy_w_out, m_cv_w_pw1=m_cv_w_pw1, m_cv_b_pw1=m_cv_b_pw1, m_cv_w_dw=m_cv_w_dw, m_cv_b_dw=m_cv_b_dw, m_cv_ln_g=m_cv_ln_g, m_cv_ln_b=m_cv_ln_b, m_cv_w_pw2=m_cv_w_pw2, m_cv_b_pw2=m_cv_b_pw2, m_ffn_w_gate=m_ffn_w_gate, m_ffn_w_up=m_ffn_w_up, m_ffn_w_down=m_ffn_w_down, v_mix_norm=v_mix_norm, v_ffn_norm=v_ffn_norm, v_hy_w_in=v_hy_w_in, v_hy_w_gate2=v_hy_w_gate2, v_hy_b_gate=v_hy_b_gate, v_hy_gla_norm=v_hy_gla_norm, v_hy_sb_q_norm=v_hy_sb_q_norm, v_hy_sb_k_norm=v_hy_sb_k_norm, v_hy_w_out=v_hy_w_out, v_cv_w_pw1=v_cv_w_pw1, v_cv_b_pw1=v_cv_b_pw1, v_cv_w_dw=v_cv_w_dw, v_cv_b_dw=v_cv_b_dw, v_cv_ln_g=v_cv_ln_g, v_cv_ln_b=v_cv_ln_b, v_cv_w_pw2=v_cv_w_pw2, v_cv_b_pw2=v_cv_b_pw2, v_ffn_w_gate=v_ffn_w_gate, v_ffn_w_up=v_ffn_w_up, v_ffn_w_down=v_ffn_w_down)
    weights = {n: given[n] for n in TWIN_WEIGHTS}
    shared = {n: given[n] for n in SHARED_INPUTS}
    per_example = {n: given[n] for n in ['x']}
    grad_fn = _jax.value_and_grad(_loss, argnums=(0, 1))

    def one_microbatch(ex, loss_target):
        ex = dict(ex)
        diff = ex.pop(TWIN_DIFF_INPUT)
        return grad_fn(weights, diff, {**shared, **ex}, loss_target)

    if N_MICROBATCH == 1:
        loss, (grad_w, grad_x) = one_microbatch(per_example, given["loss_target"])
    else:
        def body(carry, xs):
            loss_sum, grad_sum = carry
            l_k, (gw_k, gx_k) = one_microbatch(xs[0], xs[1])
            with _jax.named_scope("update"):
                return (loss_sum + l_k, _jax.tree.map(_jnp.add, grad_sum, gw_k)), gx_k

        init = (_jnp.zeros((), _jnp.float32), _jax.tree.map(_jnp.zeros_like, weights))
        (loss, grad_w), grad_x = _jax.lax.scan(body, init, (per_example, given["loss_target"]))
    with _jax.named_scope("update"):
        delta_w, new_m, new_v = {}, {}, {}
        for n in TWIN_WEIGHTS:
            delta_w[n], new_m[n], new_v[n] = _adamw(weights[n], grad_w[n], given["m_" + n], given["v_" + n])
    return (loss, grad_x, *[grad_w[n] for n in TWIN_WEIGHTS], *[delta_w[n] for n in TWIN_WEIGHTS],
            *[new_m[n] for n in TWIN_WEIGHTS], *[new_v[n] for n in TWIN_WEIGHTS])
```

```python
import functools
import math

import numpy as np
import jax
import jax.numpy as jnp
from jax import lax
from jax.experimental import pallas as pl
from jax.experimental.pallas import tpu as pltpu

F32 = jnp.float32
MXU = jnp.bfloat16
EPS = 1e-6
LANES = 128
VMEM_LIMIT = 56 * 1024 * 1024

D = 1024
F = 2816
CHUNK = 64
GLA_H, GLA_K, GLA_V, GLA_RANK = 4, 64, 128, 16
SB_H, SB_D = 8, 64
CONV_W = 31
HALO = 32

ADAM_LR, ADAM_B1, ADAM_B2, ADAM_EPS, ADAM_WD, ADAM_STEP = 0.001, 0.9, 0.999, 1e-08, 0.01, 10

NN = (((1,), (0,)), ((), ()))
NT = (((1,), (1,)), ((), ()))
TN = (((0,), (0,)), ((), ()))


def _dot(a, b, dims=NN):
    return lax.dot_general(a.astype(MXU), b.astype(MXU), dims, preferred_element_type=F32)


def _dot_split(a, b, dims=NN, a_split=True):
    x = a if a_split else b
    hi = x.astype(MXU)
    lo = (x - hi.astype(F32)).astype(MXU)
    if a_split:
        return _dot(hi, b, dims) + _dot(lo, b, dims)
    return _dot(a, hi, dims) + _dot(a, lo, dims)


def _sigmoid(x):
    return 1.0 / (1.0 + jnp.exp(-x))


def _softplus_neg_abs(z):
    return jnp.log(1.0 + jnp.exp(-jnp.abs(z)))


def _call(body, *, name, grid, ins, outs, scratch=()):
    res = pl.pallas_call(
        body,
        name=name,
        grid=grid,
        in_specs=[pl.BlockSpec(b, m) for _, b, m in ins],
        out_specs=[pl.BlockSpec(b, m) for _, _, b, m in outs],
        out_shape=[jax.ShapeDtypeStruct(s, d) for s, d, _, _ in outs],
        scratch_shapes=list(scratch),
        compiler_params=pltpu.CompilerParams(
            dimension_semantics=("arbitrary",) * len(grid), vmem_limit_bytes=VMEM_LIMIT),
    )(*[a for a, _, _ in ins])
    return res


def _rows(T, tm):
    tm = min(tm, T)
    assert T % tm == 0, (T, tm)
    return tm


def mm(pairs, *, name, trans_b=False, bias=None, res=None, out_dtype=F32, tm=512, tn=512):
    M = pairs[0][0].shape[0]
    N = pairs[0][1].shape[0] if trans_b else pairs[0][1].shape[1]
    tm = _rows(M, tm)
    tn = min(tn, N)
    assert N % tn == 0, (N, tn)
    np_ = len(pairs)

    def body(*refs):
        o_ref = refs[-1]
        acc = None
        for p in range(np_):
            d = _dot(refs[2 * p][...], refs[2 * p + 1][...], NT if trans_b else NN)
            acc = d if acc is None else acc + d
        k = 2 * np_
        if bias is not None:
            acc = acc + refs[k][...]
            k += 1
        if res is not None:
            acc = acc + refs[k][...]
        o_ref[...] = acc.astype(out_dtype)

    ins = []
    for a, b in pairs:
        K = a.shape[1]
        ins.append((a, (tm, K), lambda i, j: (i, 0)))
        if trans_b:
            ins.append((b, (tn, K), lambda i, j: (j, 0)))
        else:
            ins.append((b, (K, tn), lambda i, j: (0, j)))
    if bias is not None:
        ins.append((bias, (1, tn), lambda i, j: (0, j)))
    if res is not None:
        ins.append((res, (tm, tn), lambda i, j: (i, j)))
    return _call(body, name=name, grid=(M // tm, N // tn), ins=ins,
                 outs=[((M, N), out_dtype, (tm, tn), lambda i, j: (i, j))])[0]


def mm_tn(a, b, *, name, tm=512, tn=512, tk=512):
    T, M = a.shape
    N = b.shape[1]
    tm, tn, tk = min(tm, M), min(tn, N), min(tk, T)
    assert M % tm == 0 and N % tn == 0 and T % tk == 0, (M, N, T, tm, tn, tk)

    def body(a_ref, b_ref, o_ref):
        @pl.when(pl.program_id(2) == 0)
        def _():
            o_ref[...] = jnp.zeros_like(o_ref)
        o_ref[...] += _dot(a_ref[...], b_ref[...], TN)

    return _call(body, name=name, grid=(M // tm, N // tn, T // tk),
                 ins=[(a, (tk, tm), lambda i, j, k: (k, i)), (b, (tk, tn), lambda i, j, k: (k, j))],
                 outs=[((M, N), F32, (tm, tn), lambda i, j, k: (i, j))])[0]


def rms_fwd(x, g, *, name, tm=256):
    T = x.shape[0]
    tm = _rows(T, tm)

    def body(x_ref, g_ref, o_ref):
        x = x_ref[...]
        r = lax.rsqrt(jnp.mean(x * x, axis=-1, keepdims=True) + EPS)
        o_ref[...] = (x * r * g_ref[...]).astype(MXU)

    return _call(body, name=name, grid=(T // tm,),
                 ins=[(x, (tm, D), lambda i: (i, 0)), (g, (1, D), lambda i: (0, 0))],
                 outs=[((T, D), MXU, (tm, D), lambda i: (i, 0))])[0]


def rms_bwd(x, g, dhn, dres, *, name, tm=256):
    T = x.shape[0]
    tm = _rows(T, tm)

    def body(x_ref, g_ref, d_ref, r_ref, dx_ref, dg_ref, cs_ref):
        @pl.when(pl.program_id(0) == 0)
        def _():
            dg_ref[...] = jnp.zeros_like(dg_ref)
            cs_ref[...] = jnp.zeros_like(cs_ref)
        x = x_ref[...]
        d = d_ref[...]
        r = lax.rsqrt(jnp.mean(x * x, axis=-1, keepdims=True) + EPS)
        dg_ref[...] += jnp.sum(d * x * r, axis=0, keepdims=True)
        t = d * g_ref[...]
        m = jnp.mean(t * x, axis=-1, keepdims=True)
        dx = r_ref[...] + t * r - x * (r * r * r) * m
        dx_ref[...] = dx
        cs_ref[...] += jnp.sum(dx, axis=0, keepdims=True)

    row = lambda i: (i, 0)
    fix = lambda i: (0, 0)
    return _call(body, name=name, grid=(T // tm,),
                 ins=[(x, (tm, D), row), (g, (1, D), fix), (dhn, (tm, D), row), (dres, (tm, D), row)],
                 outs=[((T, D), F32, (tm, D), row), ((1, D), F32, (1, D), fix), ((1, D), F32, (1, D), fix)])


def loss_grad(y, t, *, name, tm=256):
    T = y.shape[0]
    tm = _rows(T, tm)

    def body(y_ref, t_ref, dy_ref, l_ref):
        @pl.when(pl.program_id(0) == 0)
        def _():
            l_ref[...] = jnp.zeros_like(l_ref)
        e = y_ref[...] - t_ref[...]
        dy_ref[...] = e * (1.0 / D)
        l_ref[...] += jnp.sum(e * e, axis=0, keepdims=True) * (0.5 / D)

    row = lambda i: (i, 0)
    return _call(body, name=name, grid=(T // tm,),
                 ins=[(y, (tm, D), row), (t, (tm, D), row)],
                 outs=[((T, D), F32, (tm, D), row), ((1, D), F32, (1, D), lambda i: (0, 0))])


def ffn_up(hn, wg, wu, *, name, tm=512, tn=256):
    T = hn.shape[0]
    tm = _rows(T, tm)

    def body(h_ref, wg_ref, wu_ref, g_ref, u_ref, a_ref):
        h = h_ref[...]
        g = _dot(h, wg_ref[...])
        u = _dot(h, wu_ref[...])
        g_ref[...] = g
        u_ref[...] = u
        a_ref[...] = (g * _sigmoid(g) * u).astype(MXU)

    tile = lambda i, j: (i, j)
    return _call(body, name=name, grid=(T // tm, F // tn),
                 ins=[(hn, (tm, D), lambda i, j: (i, 0)), (wg, (D, tn), lambda i, j: (0, j)),
                      (wu, (D, tn), lambda i, j: (0, j))],
                 outs=[((T, F), F32, (tm, tn), tile), ((T, F), F32, (tm, tn), tile),
                       ((T, F), MXU, (tm, tn), tile)])


def ffn_bwd_act(dy, wd, G, U, *, name, tm=512, tn=256):
    T = dy.shape[0]
    tm = _rows(T, tm)

    def body(dy_ref, wd_ref, g_ref, u_ref, dg_ref, du_ref, a_ref):
        da = _dot(dy_ref[...], wd_ref[...], NT)
        g = g_ref[...]
        u = u_ref[...]
        s = _sigmoid(g)
        sil = g * s
        a_ref[...] = (sil * u).astype(MXU)
        du_ref[...] = (da * sil).astype(MXU)
        dg_ref[...] = (da * u * (s * (1.0 + g * (1.0 - s)))).astype(MXU)

    tile = lambda i, j: (i, j)
    return _call(body, name=name, grid=(T // tm, F // tn),
                 ins=[(dy, (tm, D), lambda i, j: (i, 0)), (wd, (tn, D), lambda i, j: (j, 0)),
                      (G, (tm, tn), tile), (U, (tm, tn), tile)],
                 outs=[((T, F), MXU, (tm, tn), tile), ((T, F), MXU, (tm, tn), tile),
                       ((T, F), MXU, (tm, tn), tile)])


def ffn_fwd(h, g_norm, wg, wu, wd, tag):
    hn = rms_fwd(h, g_norm, name=f"ffn{tag}_norm")
    G, U, act = ffn_up(hn, wg, wu, name=f"ffn{tag}_up")
    h_out = mm([(act, wd)], res=h, name=f"ffn{tag}_down")
    return h_out, (hn, G, U)


def ffn_bwd(dy, h, g_norm, wg, wu, wd, saved, tag):
    hn, G, U = saved
    dG, dU, act = ffn_bwd_act(dy, wd, G, U, name=f"ffn{tag}_bwd_act")
    d_wd = mm_tn(act, dy, name=f"ffn{tag}_dwd", tm=1408, tn=512)
    d_wg = mm_tn(hn, dG, name=f"ffn{tag}_dwg", tm=512, tn=1408)
    d_wu = mm_tn(hn, dU, name=f"ffn{tag}_dwu", tm=512, tn=1408)
    dhn = mm([(dG, wg), (dU, wu)], trans_b=True, name=f"ffn{tag}_dhn")
    dh, d_g, cs = rms_bwd(h, g_norm, dhn, dy, name=f"ffn{tag}_norm_bwd")
    return dh, d_g, d_wg, d_wu, d_wd, cs


def conv_fwd(A, w_dw, b_dw, ln_g, ln_b, *, name, tm=128, rb=32):
    T = A.shape[0]
    tm = _rows(T, tm)

    def body(a_ref, ap_ref, w_ref, b_ref, g_ref, bb_ref, s_ref, c_ref, win):
        i = pl.program_id(0)
        a = a_ref[...]
        win[pl.ds(HALO, tm), :] = a[:, :D] * _sigmoid(a[:, D:])
        ap = ap_ref[pl.ds(tm - HALO, HALO), :]
        up = ap[:, :D] * _sigmoid(ap[:, D:])
        win[pl.ds(0, HALO), :] = jnp.where(i > 0, up, 0.0)
        for r0 in range(0, tm, rb):
            acc = jnp.broadcast_to(b_ref[...], (rb, D))
            for k in range(CONV_W):
                acc = acc + w_ref[pl.ds(k, 1), :] * win[pl.ds(r0 + k + HALO - (CONV_W - 1), rb), :]
            c_ref[pl.ds(r0, rb), :] = acc
        c = c_ref[...]
        mu = jnp.mean(c, axis=-1, keepdims=True)
        cc = c - mu
        var = jnp.mean(cc * cc, axis=-1, keepdims=True)
        z = cc * lax.rsqrt(var + EPS) * g_ref[...] + bb_ref[...]
        s_ref[...] = (z * _sigmoid(z)).astype(MXU)

    row = lambda i: (i, 0)
    fix = lambda i: (0, 0)
    return _call(body, name=name, grid=(T // tm,),
                 ins=[(A, (tm, 2 * D), row), (A, (tm, 2 * D), lambda i: (jnp.maximum(i - 1, 0), 0)),
                      (w_dw, (HALO, D), fix), (b_dw, (1, D), fix), (ln_g, (1, D), fix), (ln_b, (1, D), fix)],
                 outs=[((T, D), MXU, (tm, D), row), ((T, D), F32, (tm, D), row)],
                 scratch=[pltpu.VMEM((tm + HALO, D), F32)])


def ln_swish_bwd(c, ds, ln_g, ln_b, *, name, tm=256):
    T = c.shape[0]
    tm = _rows(T, tm)

    def body(c_ref, ds_ref, g_ref, b_ref, dc_ref, dg_ref, db_ref):
        @pl.when(pl.program_id(0) == 0)
        def _():
            dg_ref[...] = jnp.zeros_like(dg_ref)
            db_ref[...] = jnp.zeros_like(db_ref)
        c = c_ref[...]
        mu = jnp.mean(c, axis=-1, keepdims=True)
        cc = c - mu
        rstd = lax.rsqrt(jnp.mean(cc * cc, axis=-1, keepdims=True) + EPS)
        n = cc * rstd
        z = n * g_ref[...] + b_ref[...]
        s = _sigmoid(z)
        dz = ds_ref[...] * (s * (1.0 + z * (1.0 - s)))
        dg_ref[...] += jnp.sum(dz * n, axis=0, keepdims=True)
        db_ref[...] += jnp.sum(dz, axis=0, keepdims=True)
        dn = dz * g_ref[...]
        dc_ref[...] = rstd * (dn - jnp.mean(dn, axis=-1, keepdims=True)
                              - n * jnp.mean(dn * n, axis=-1, keepdims=True))

    row = lambda i: (i, 0)
    fix = lambda i: (0, 0)
    return _call(body, name=name, grid=(T // tm,),
                 ins=[(c, (tm, D), row), (ds, (tm, D), row), (ln_g, (1, D), fix), (ln_b, (1, D), fix)],
                 outs=[((T, D), F32, (tm, D), row), ((1, D), F32, (1, D), fix), ((1, D), F32, (1, D), fix)])


def conv_bwd(dc, A, w_dw, *, name, tm=128, rb=32):
    T = A.shape[0]
    tm = _rows(T, tm)
    n = T // tm
    SUB = 8

    def body(dc_ref, dn_ref, a_ref, ap_ref, w_ref, da_ref, dw_ref, dbd_ref, dbp_ref, wdc, wu, accw):
        i = pl.program_id(0)

        @pl.when(i == 0)
        def _():
            accw[...] = jnp.zeros_like(accw)
            dbd_ref[...] = jnp.zeros_like(dbd_ref)
            dbp_ref[...] = jnp.zeros_like(dbp_ref)
        a = a_ref[...]
        a1, a2 = a[:, :D], a[:, D:]
        sg = _sigmoid(a2)
        wu[pl.ds(HALO, tm), :] = a1 * sg
        ap = ap_ref[pl.ds(tm - HALO, HALO), :]
        wu[pl.ds(0, HALO), :] = jnp.where(i > 0, ap[:, :D] * _sigmoid(ap[:, D:]), 0.0)
        dc = dc_ref[...]
        wdc[pl.ds(0, tm), :] = dc
        wdc[pl.ds(tm, HALO), :] = jnp.where(i < n - 1, dn_ref[pl.ds(0, HALO), :], 0.0)
        dbd_ref[...] += jnp.sum(dc, axis=0, keepdims=True)
        for r0 in range(0, tm, rb):
            du = jnp.zeros((rb, D), F32)
            dcs = wdc[pl.ds(r0, rb), :]
            for k in range(CONV_W):
                du = du + w_ref[pl.ds(k, 1), :] * wdc[pl.ds(r0 + (CONV_W - 1) - k, rb), :]
                p = dcs * wu[pl.ds(r0 + k + HALO - (CONV_W - 1), rb), :]
                accw[pl.ds(SUB * k, SUB), :] += jnp.sum(p.reshape(rb // SUB, SUB, D), axis=0)
            s = sg[r0:r0 + rb]
            da_ref[pl.ds(r0, rb), pl.ds(0, D)] = (du * s).astype(MXU)
            da_ref[pl.ds(r0, rb), pl.ds(D, D)] = (du * a1[r0:r0 + rb] * s * (1.0 - s)).astype(MXU)
        da = da_ref[...].astype(F32)
        dbp_ref[...] += jnp.sum(da, axis=0, keepdims=True)

        @pl.when(i == n - 1)
        def _():
            dw_ref[...] = jnp.zeros_like(dw_ref)
            for k in range(CONV_W):
                dw_ref[pl.ds(k, 1), :] = jnp.sum(accw[pl.ds(SUB * k, SUB), :], axis=0, keepdims=True)

    row = lambda i: (i, 0)
    fix = lambda i: (0, 0)
    return _call(body, name=name, grid=(n,),
                 ins=[(dc, (tm, D), row), (dc, (tm, D), lambda i: (jnp.minimum(i + 1, n - 1), 0)),
                      (A, (tm, 2 * D), row), (A, (tm, 2 * D), lambda i: (jnp.maximum(i - 1, 0), 0)),
                      (w_dw, (HALO, D), fix)],
                 outs=[((T, 2 * D), MXU, (tm, 2 * D), row), ((HALO, D), F32, (HALO, D), fix),
                       ((1, D), F32, (1, D), fix), ((1, 2 * D), F32, (1, 2 * D), fix)],
                 scratch=[pltpu.VMEM((tm + HALO, D), F32), pltpu.VMEM((tm + HALO, D), F32),
                          pltpu.VMEM((SUB * HALO, D), F32)])


def sb_prep(ps, gq, gk, *, name, tm=256):
    T = ps.shape[0]
    tm = _rows(T, tm)
    W = SB_H * SB_D

    def body(q_ref, k_ref, v_ref, gq_ref, gk_ref, qo, ko, vo):
        for h in range(SB_H):
            sl = slice(h * SB_D, (h + 1) * SB_D)
            q = q_ref[:, sl]
            k = k_ref[:, sl]
            rq = lax.rsqrt(jnp.mean(q * q, axis=-1, keepdims=True) + EPS)
            rk = lax.rsqrt(jnp.mean(k * k, axis=-1, keepdims=True) + EPS)
            qo[h] = (q * rq * gq_ref[...] * (SB_D ** -0.5)).astype(MXU)
            ko[h] = (k * rk * gk_ref[...]).astype(MXU)
            vo[h] = v_ref[:, sl].astype(MXU)

    hm = lambda i: (0, i, 0)
    fix = lambda i: (0, 0)
    return _call(body, name=name, grid=(T // tm,),
                 ins=[(ps, (tm, W), lambda i: (i, 0)), (ps, (tm, W), lambda i: (i, 1)),
                      (ps, (tm, W), lambda i: (i, 2)), (gq, (1, SB_D), fix), (gk, (1, SB_D), fix)],
                 outs=[((SB_H, T, SB_D), MXU, (SB_H, tm, SB_D), hm)] * 3)


def _sb_masks(tq):
    row = lax.broadcasted_iota(jnp.int32, (tq, tq), 0)
    col = lax.broadcasted_iota(jnp.int32, (tq, tq), 1)
    tri = jnp.where(row >= col, 1.0, 0.0).astype(MXU)
    past = col < row
    return tri, past


def _sb_scores(q, k, past):
    z = _dot(q, k, NT)
    sp = _softplus_neg_abs(z)
    ls = jnp.minimum(z, 0.0) - sp
    lk = -jnp.maximum(z, 0.0) - sp
    if past is not None:
        lk = jnp.where(past, lk, 0.0)
    return ls, lk


def sb_fwd(qn, kn, v, *, name, tq=256):
    H, T, _ = qn.shape
    tq = _rows(T, tq)

    assert T // tq <= LANES

    def body(q_ref, k_ref, v_ref, o_ref, cb_ref):
        i = pl.program_id(1)
        q = q_ref[0]
        tri, past = _sb_masks(tq)
        lane = lax.broadcasted_iota(jnp.int32, (tq, LANES), 1)
        cb_ref[0] = jnp.zeros((tq, LANES), F32)

        def block(j, carry, masked):
            acc, c = carry
            cb_ref[0] = jnp.where(lane == j, c, cb_ref[0])
            start = pl.multiple_of(j * tq, tq)
            k = k_ref[0, pl.ds(start, tq), :]
            vv = v_ref[0, pl.ds(start, tq), :]
            ls, lk = _sb_scores(q, k, past if masked else None)
            binc = _dot_split(lk, tri)
            w = jnp.exp(ls + c + binc - lk)
            if masked:
                w = jnp.where(past, w, 0.0)
            acc = acc + _dot(w, vv)
            return acc, c + jnp.sum(lk, axis=-1, keepdims=True)

        carry = block(i, (jnp.zeros((tq, SB_D), F32), jnp.zeros((tq, 1), F32)), True)
        acc, _ = lax.fori_loop(0, i, lambda jj, car: block(i - 1 - jj, car, False), carry)
        o_ref[0] = acc

    return _call(body, name=name, grid=(H, T // tq),
                 ins=[(qn, (1, tq, SB_D), lambda h, i: (h, i, 0)), (kn, (1, T, SB_D), lambda h, i: (h, 0, 0)),
                      (v, (1, T, SB_D), lambda h, i: (h, 0, 0))],
                 outs=[((H, T, SB_D), F32, (1, tq, SB_D), lambda h, i: (h, i, 0)),
                       ((H, T, LANES), F32, (1, tq, LANES), lambda h, i: (h, i, 0))])


def sb_bwd(qn, kn, v, cb, do, *, name, tq=256):
    H, T, _ = qn.shape
    tq = _rows(T, tq)

    def body(q_ref, k_ref, v_ref, cb_ref, do_ref, dq_ref, dk_ref, dv_ref):
        i = pl.program_id(1)

        @pl.when(i == 0)
        def _():
            dk_ref[...] = jnp.zeros_like(dk_ref)
            dv_ref[...] = jnp.zeros_like(dv_ref)
        q = q_ref[0]
        do_b = do_ref[0].astype(MXU)
        tri, past = _sb_masks(tq)
        row = lax.broadcasted_iota(jnp.int32, (tq, tq), 0)
        col = lax.broadcasted_iota(jnp.int32, (tq, tq), 1)
        tri_lt = jnp.where(row < col, 1.0, 0.0).astype(MXU)
        lane = lax.broadcasted_iota(jnp.int32, (tq, LANES), 1)

        def block(j, carry, masked):
            dq, cg = carry
            start = pl.multiple_of(j * tq, tq)
            k = k_ref[0, pl.ds(start, tq), :]
            vv = v_ref[0, pl.ds(start, tq), :]
            ls, lk = _sb_scores(q, k, past if masked else None)
            c = jnp.sum(jnp.where(lane == j, cb_ref[0], 0.0), axis=-1, keepdims=True)
            binc = _dot_split(lk, tri)
            w = jnp.exp(ls + c + binc - lk)
            if masked:
                w = jnp.where(past, w, 0.0)
            g = w * _dot(do_b, vv, NT)
            gpre = cg + _dot_split(g, tri_lt)
            sig = jnp.exp(ls)
            dz = g * (1.0 - sig) - gpre * sig
            if masked:
                dz = jnp.where(past, dz, 0.0)
            dzb = dz.astype(MXU)
            dq = dq + _dot(dzb, k)
            dk_ref[0, pl.ds(start, tq), :] += _dot(dzb, q, TN)
            dv_ref[0, pl.ds(start, tq), :] += _dot(w, do_b, TN)
            return dq, cg + jnp.sum(g, axis=-1, keepdims=True)

        carry = (jnp.zeros((tq, SB_D), F32), jnp.zeros((tq, 1), F32))
        carry = lax.fori_loop(0, i, lambda j, car: block(j, car, False), carry)
        dq, _ = block(i, carry, True)
        dq_ref[0] = dq

    blk = lambda h, i: (h, i, 0)
    full = lambda h, i: (h, 0, 0)
    return _call(body, name=name, grid=(H, T // tq),
                 ins=[(qn, (1, tq, SB_D), blk), (kn, (1, T, SB_D), full), (v, (1, T, SB_D), full),
                      (cb, (1, tq, LANES), blk), (do, (1, tq, SB_D), blk)],
                 outs=[((H, T, SB_D), F32, (1, tq, SB_D), blk), ((H, T, SB_D), F32, (1, T, SB_D), full),
                       ((H, T, SB_D), F32, (1, T, SB_D), full)])


def heads_to_cols(o, *, name, tm=256):
    H, T, _ = o.shape
    tm = _rows(T, tm)

    def body(o_ref, out_ref):
        out_ref[...] = jnp.concatenate([o_ref[h] for h in range(H)], axis=-1).astype(MXU)

    return _call(body, name=name, grid=(T // tm,),
                 ins=[(o, (H, tm, SB_D), lambda i: (0, i, 0))],
                 outs=[((T, H * SB_D), MXU, (tm, H * SB_D), lambda i: (i, 0))])[0]


def cols_to_heads(dmix, col_block, *, name, tm=256):
    T = dmix.shape[0]
    tm = _rows(T, tm)
    W = SB_H * SB_D

    def body(d_ref, out_ref):
        for h in range(SB_H):
            out_ref[h] = d_ref[:, h * SB_D:(h + 1) * SB_D]

    return _call(body, name=name, grid=(T // tm,),
                 ins=[(dmix, (tm, W), lambda i: (i, col_block))],
                 outs=[((SB_H, T, SB_D), F32, (SB_H, tm, SB_D), lambda i: (0, i, 0))])[0]


def sb_post_bwd(dqn, dkn, dv, ps, gq, gk, *, name, tm=256):
    T = ps.shape[0]
    tm = _rows(T, tm)
    W = SB_H * SB_D

    def body(dq_ref, dk_ref, dv_ref, q_ref, k_ref, gq_ref, gk_ref, out_ref, dgq_ref, dgk_ref):
        @pl.when(pl.program_id(0) == 0)
        def _():
            dgq_ref[...] = jnp.zeros_like(dgq_ref)
            dgk_ref[...] = jnp.zeros_like(dgk_ref)

        def norm_bwd(x, d, g):
            r = lax.rsqrt(jnp.mean(x * x, axis=-1, keepdims=True) + EPS)
            dg = jnp.sum(d * x * r, axis=0, keepdims=True)
            t = d * g
            return t * r - x * (r * r * r) * jnp.mean(t * x, axis=-1, keepdims=True), dg

        dqs, dks, dvs = [], [], []
        dgq = jnp.zeros((1, SB_D), F32)
        dgk = jnp.zeros((1, SB_D), F32)
        for h in range(SB_H):
            sl = slice(h * SB_D, (h + 1) * SB_D)
            a, ga = norm_bwd(q_ref[:, sl], dq_ref[h] * (SB_D ** -0.5), gq_ref[...])
            b, gb = norm_bwd(k_ref[:, sl], dk_ref[h], gk_ref[...])
            dqs.append(a)
            dks.append(b)
            dvs.append(dv_ref[h])
            dgq = dgq + ga
            dgk = dgk + gb
        out_ref[...] = jnp.concatenate(dqs + dks + dvs, axis=-1).astype(MXU)
        dgq_ref[...] += dgq
        dgk_ref[...] += dgk

    hm = lambda i: (0, i, 0)
    fix = lambda i: (0, 0)
    return _call(body, name=name, grid=(T // tm,),
                 ins=[(dqn, (SB_H, tm, SB_D), hm), (dkn, (SB_H, tm, SB_D), hm), (dv, (SB_H, tm, SB_D), hm),
                      (ps, (tm, W), lambda i: (i, 0)), (ps, (tm, W), lambda i: (i, 1)),
                      (gq, (1, SB_D), fix), (gk, (1, SB_D), fix)],
                 outs=[((T, 3 * W), MXU, (tm, 3 * W), lambda i: (i, 0)),
                       ((1, SB_D), F32, (1, SB_D), fix), ((1, SB_D), F32, (1, SB_D), fix)])


GLA_TM = 512
GLA_DK = GLA_H * GLA_K
GLA_DV = GLA_H * GLA_V


def _gla_masks(tm):
    row = lax.broadcasted_iota(jnp.int32, (tm, tm), 0)
    col = lax.broadcasted_iota(jnp.int32, (tm, tm), 1)
    same = (row // CHUNK) == (col // CHUNK)
    return row, col, same


def _gla_gate(glr, w2, b, tm):
    pre = _dot(glr, w2) + b
    la = (jnp.minimum(pre, 0.0) - _softplus_neg_abs(pre)) * (1.0 / 16.0)
    row, col, same = _gla_masks(tm)
    m_incl = jnp.where(same & (col <= row), 1.0, 0.0).astype(MXU)
    m_full = jnp.where(same, 1.0, 0.0).astype(MXU)
    bc = _dot_split(m_incl, la, a_split=False)
    tot = _dot_split(m_full, la, a_split=False)
    return pre, bc, tot


def gla_fwd(pg, glr, w2, b_gate, g_out, *, name):
    T = pg.shape[0]
    tm = _rows(T, GLA_TM)
    ncb = tm // CHUNK
    NC = T // CHUNK

    def body(q_ref, k_ref, v_ref, r_ref, l_ref, w2_ref, b_ref, g_ref, o_ref, st_ref, S):
        @pl.when(pl.program_id(0) == 0)
        def _():
            S[...] = jnp.zeros_like(S)
        _, bc, tot = _gla_gate(l_ref[...], w2_ref[...], b_ref[...], tm)
        kend = k_ref[...] * jnp.exp(tot - bc)
        qs = q_ref[...] * (GLA_K ** -0.5)
        a_all = jnp.exp(tot)
        v = v_ref[...]
        r = r_ref[...]
        for c in range(ncb):
            rows = slice(c * CHUNK, (c + 1) * CHUNK)
            outs = []
            for h in range(GLA_H):
                hk = slice(h * GLA_K, (h + 1) * GLA_K)
                hv = slice(h * GLA_V, (h + 1) * GLA_V)
                ut = _dot(v[rows, hv], kend[rows, hk], TN)
                s_new = S[h] * a_all[c * CHUNK:c * CHUNK + 1, hk] + ut
                S[h] = s_new
                st_ref[c, h] = s_new
                o = _dot(qs[rows, hk], s_new, NT)
                rinv = lax.rsqrt(jnp.mean(o * o, axis=-1, keepdims=True) + EPS)
                rr = r[rows, hv]
                outs.append(o * rinv * g_ref[...] * (rr * _sigmoid(rr)))
            o_ref[pl.ds(c * CHUNK, CHUNK), :] = jnp.concatenate(outs, axis=-1).astype(MXU)

    fix = lambda i: (0, 0)
    return _call(body, name=name, grid=(T // tm,),
                 ins=[(pg, (tm, GLA_DK), lambda i: (i, 0)), (pg, (tm, GLA_DK), lambda i: (i, 1)),
                      (pg, (tm, GLA_DV), lambda i: (i, 1)), (pg, (tm, GLA_DV), lambda i: (i, 2)),
                      (glr, (tm, LANES), lambda i: (i, 0)), (w2, (LANES, GLA_DK), fix),
                      (b_gate, (1, GLA_DK), fix), (g_out, (1, GLA_V), fix)],
                 outs=[((T, GLA_DV), MXU, (tm, GLA_DV), lambda i: (i, 0)),
                       ((NC, GLA_H, GLA_V, GLA_K), F32, (ncb, GLA_H, GLA_V, GLA_K), lambda i: (i, 0, 0, 0))],
                 scratch=[pltpu.VMEM((GLA_H, GLA_V, GLA_K), F32)])


def gla_bwd(pg, glr, w2, b_gate, g_out, st, dmix, *, name):
    T = pg.shape[0]
    tm = _rows(T, GLA_TM)
    ncb = tm // CHUNK
    n = T // tm

    def body(q_ref, k_ref, v_ref, r_ref, l_ref, w2_ref, b_ref, g_ref, st_ref, sp_ref, d_ref,
             dpg_ref, dl_ref, dw2_ref, db_ref, dg_ref, dS, dkend, extra):
        i = pl.program_id(0)

        @pl.when(i == 0)
        def _():
            dS[...] = jnp.zeros_like(dS)
            dw2_ref[...] = jnp.zeros_like(dw2_ref)
            db_ref[...] = jnp.zeros_like(db_ref)
            dg_ref[...] = jnp.zeros_like(dg_ref)
        first_tile = i == n - 1
        glr_v = l_ref[...]
        pre, bc, tot = _gla_gate(glr_v, w2_ref[...], b_ref[...], tm)
        dec = jnp.exp(tot - bc)
        k = k_ref[...]
        kend = k * dec
        qs = q_ref[...] * (GLA_K ** -0.5)
        a_all = jnp.exp(tot)
        v = v_ref[...]
        r = r_ref[...]
        g = g_ref[...]
        dgg = jnp.zeros((1, GLA_V), F32)
        for c in reversed(range(ncb)):
            rows = slice(c * CHUNK, (c + 1) * CHUNK)
            dq_l, dv_l, dr_l, dk_l, ex_l = [], [], [], [], []
            for h in range(GLA_H):
                hk = slice(h * GLA_K, (h + 1) * GLA_K)
                hv = slice(h * GLA_V, (h + 1) * GLA_V)
                s_c = st_ref[c, h]
                if c > 0:
                    s_p = st_ref[c - 1, h]
                else:
                    s_p = jnp.where(first_tile, 0.0, sp_ref[0, h])
                qh = qs[rows, hk]
                o = _dot(qh, s_c, NT)
                rinv = lax.rsqrt(jnp.mean(o * o, axis=-1, keepdims=True) + EPS)
                nrm = o * rinv
                rr = r[rows, hv]
                sg = _sigmoid(rr)
                sil = rr * sg
                d = d_ref[pl.ds(c * CHUNK, CHUNK), pl.ds(h * GLA_V, GLA_V)]
                dr_l.append(d * nrm * g * (sg * (1.0 + rr * (1.0 - sg))))
                dgg = dgg + jnp.sum(d * sil * nrm, axis=0, keepdims=True)
                dn = d * sil * g
                do = rinv * (dn - nrm * jnp.mean(dn * nrm, axis=-1, keepdims=True))
                dq_l.append(_dot(do, s_c) * (GLA_K ** -0.5))
                dst = dS[h] + _dot(do, qh, TN)
                a = a_all[c * CHUNK:c * CHUNK + 1, hk]
                da = jnp.sum(dst * s_p, axis=0, keepdims=True)
                dS[h] = dst * a
                dk_l.append(_dot(v[rows, hv], dst))
                dv_l.append(_dot(kend[rows, hk], dst, NT))
                ex_l.append(jnp.broadcast_to(da * a, (CHUNK, GLA_K)))
            cr = pl.ds(c * CHUNK, CHUNK)
            dpg_ref[cr, pl.ds(0, GLA_DK)] = jnp.concatenate(dq_l, axis=-1).astype(MXU)
            dpg_ref[cr, pl.ds(2 * GLA_DK, GLA_DV)] = jnp.concatenate(dv_l, axis=-1).astype(MXU)
            dpg_ref[cr, pl.ds(2 * GLA_DK + GLA_DV, GLA_DV)] = jnp.concatenate(dr_l, axis=-1).astype(MXU)
            dkend[cr, :] = jnp.concatenate(dk_l, axis=-1)
            extra[cr, :] = jnp.concatenate(ex_l, axis=-1)
        dke = dkend[...]
        dpg_ref[:, pl.ds(GLA_DK, GLA_DK)] = (dke * dec).astype(MXU)
        e = dke * kend
        row, col, same = _gla_masks(tm)
        m_lt = jnp.where(same & (col < row), 1.0, 0.0).astype(MXU)
        dla = _dot_split(m_lt, e, a_split=False) + extra[...]
        sp = _softplus_neg_abs(pre)
        one_m_sig = jnp.exp(-jnp.maximum(pre, 0.0) - sp)
        dpre = dla * (1.0 / 16.0) * one_m_sig
        dl_ref[...] = _dot(dpre, w2_ref[...], NT).astype(MXU)
        dw2_ref[...] += _dot(glr_v, dpre, TN)
        db_ref[...] += jnp.sum(dpre, axis=0, keepdims=True)
        dg_ref[...] += dgg

    fix = lambda i: (0, 0)
    rev = lambda i: n - 1 - i
    return _call(body, name=name, grid=(n,),
                 ins=[(pg, (tm, GLA_DK), lambda i: (rev(i), 0)), (pg, (tm, GLA_DK), lambda i: (rev(i), 1)),
                      (pg, (tm, GLA_DV), lambda i: (rev(i), 1)), (pg, (tm, GLA_DV), lambda i: (rev(i), 2)),
                      (glr, (tm, LANES), lambda i: (rev(i), 0)), (w2, (LANES, GLA_DK), fix),
                      (b_gate, (1, GLA_DK), fix), (g_out, (1, GLA_V), fix),
                      (st, (ncb, GLA_H, GLA_V, GLA_K), lambda i: (rev(i), 0, 0, 0)),
                      (st, (1, GLA_H, GLA_V, GLA_K), lambda i: (jnp.maximum(rev(i) * ncb - 1, 0), 0, 0, 0)),
                      (dmix, (tm, GLA_DV), lambda i: (rev(i), 0))],
                 outs=[((T, 2 * GLA_DK + 2 * GLA_DV), MXU, (tm, 2 * GLA_DK + 2 * GLA_DV), lambda i: (rev(i), 0)),
                       ((T, LANES), MXU, (tm, LANES), lambda i: (rev(i), 0)),
                       ((LANES, GLA_DK), F32, (LANES, GLA_DK), fix),
                       ((1, GLA_DK), F32, (1, GLA_DK), fix), ((1, GLA_V), F32, (1, GLA_V), fix)],
                 scratch=[pltpu.VMEM((GLA_H, GLA_V, GLA_K), F32), pltpu.VMEM((tm, GLA_DK), F32),
                          pltpu.VMEM((tm, GLA_DK), F32)])


def local_step(x, tgt, W):
    row1 = lambda a, l: a[l:l + 1]
    hn0 = rms_fwd(x, row1(W["mix_norm"], 0), name="l0_norm")
    pg = mm([(hn0, W["wi_g"])], name="l0_proj_gla", tn=512)
    ps = mm([(hn0, W["wi_s"])], name="l0_proj_sb", tn=512)
    glr = mm([(hn0, W["wi_l"])], name="l0_proj_gate", out_dtype=MXU)
    og, st = gla_fwd(pg, glr, W["w2"], W["b_gate"], W["g_gla"], name="gla_fwd")
    qn, kn, vh = sb_prep(ps, W["g_q"], W["g_k"], name="sb_prep")
    osb_h, sb_cb = sb_fwd(qn, kn, vh, name="sb_fwd")
    osb = heads_to_cols(osb_h, name="sb_out")
    h1 = mm([(og, W["wo_g"]), (osb, W["wo_s"])], res=x, name="l0_out")
    h2, ffn0 = ffn_fwd(h1, row1(W["ffn_norm"], 0), W["wg0"], W["wu0"], W["wd0"], 0)
    hn1 = rms_fwd(h2, row1(W["mix_norm"], 1), name="l1_norm")
    A = mm([(hn1, W["pw1"])], bias=W["b_pw1"], name="l1_pw1")
    s, cconv = conv_fwd(A, W["w_dw"], W["b_dw"], W["ln_g"], W["ln_b"], name="conv_fwd")
    h3 = mm([(s, W["pw2"])], bias=W["b_pw2"], res=h2, name="l1_pw2")
    y, ffn1 = ffn_fwd(h3, row1(W["ffn_norm"], 1), W["wg1"], W["wu1"], W["wd1"], 1)
    dy, loss_lanes = loss_grad(y, tgt, name="loss")
    G = {}
    dh3, g_fn1, G["wg1"], G["wu1"], G["wd1"], cs3 = ffn_bwd(
        dy, h3, row1(W["ffn_norm"], 1), W["wg1"], W["wu1"], W["wd1"], ffn1, 1)
    G["b_pw2"] = cs3
    ds = mm([(dh3, W["pw2"])], trans_b=True, name="l1_ds")
    G["pw2"] = mm_tn(s, dh3, name="l1_dpw2", tm=1024, tn=1024)
    dc, G["ln_g"], G["ln_b"] = ln_swish_bwd(cconv, ds, W["ln_g"], W["ln_b"], name="ln_bwd")
    dA, G["w_dw"], G["b_dw"], G["b_pw1"] = conv_bwd(dc, A, W["w_dw"], name="conv_bwd")
    G["pw1"] = mm_tn(hn1, dA, name="l1_dpw1", tm=1024, tn=1024)
    dhn1 = mm([(dA, W["pw1"])], trans_b=True, name="l1_dhn")
    dh2, g_mn1, _ = rms_bwd(h2, row1(W["mix_norm"], 1), dhn1, dh3, name="l1_norm_bwd")
    dh1, g_fn0, G["wg0"], G["wu0"], G["wd0"], _ = ffn_bwd(
        dh2, h1, row1(W["ffn_norm"], 0), W["wg0"], W["wu0"], W["wd0"], ffn0, 0)
    dmix = mm([(dh1, W["wo_gs"])], trans_b=True, name="l0_dmix")
    G["wo_g"] = mm_tn(og, dh1, name="l0_dwo_g", tm=512, tn=1024)
    G["wo_s"] = mm_tn(osb, dh1, name="l0_dwo_s", tm=512, tn=1024)
    do_h = cols_to_heads(dmix, 1, name="sb_dout")
    dqn, dkn, dvh = sb_bwd(qn, kn, vh, sb_cb, do_h, name="sb_bwd")
    dps, G["g_q"], G["g_k"] = sb_post_bwd(dqn, dkn, dvh, ps, W["g_q"], W["g_k"], name="sb_post_bwd")
    dpg, dglr, G["w2"], G["b_gate"], G["g_gla"] = gla_bwd(
        pg, glr, W["w2"], W["b_gate"], W["g_gla"], st, dmix, name="gla_bwd")
    G["wi_g"] = mm_tn(hn0, dpg, name="l0_dwi_g", tm=1024, tn=512)
    G["wi_s"] = mm_tn(hn0, dps, name="l0_dwi_s", tm=1024, tn=512)
    G["wi_l"] = mm_tn(hn0, dglr, name="l0_dwi_l", tm=1024, tn=LANES)
    dhn0 = mm([(dpg, W["wi_g"]), (dps, W["wi_s"]), (dglr, W["wi_l"])], trans_b=True, name="l0_dhn")
    dx, g_mn0, _ = rms_bwd(x, row1(W["mix_norm"], 0), dhn0, dh1, name="l0_norm_bwd")
    G["mix_norm"] = jnp.concatenate([g_mn0, g_mn1], axis=0)
    G["ffn_norm"] = jnp.concatenate([g_fn0, g_fn1], axis=0)
    return loss_lanes, dx, G


_C_GLA = 2 * GLA_DK + 2 * GLA_DV
_C_SB0 = _C_GLA + GLA_RANK


def to_kernel_layout(full):
    w_in = full["hy_w_in"][0]
    W = {
        "wi_g": w_in[:, :_C_GLA].astype(MXU),
        "wi_s": w_in[:, _C_SB0:].astype(MXU),
        "wi_l": jnp.pad(w_in[:, _C_GLA:_C_SB0], ((0, 0), (0, LANES - GLA_RANK))).astype(MXU),
        "w2": jnp.pad(full["hy_w_gate2"][0], ((0, LANES - GLA_RANK), (0, 0))),
        "b_gate": full["hy_b_gate"], "g_gla": full["hy_gla_norm"],
        "g_q": full["hy_sb_q_norm"], "g_k": full["hy_sb_k_norm"],
        "wo_gs": full["hy_w_out"][0].astype(MXU),
        "pw1": full["cv_w_pw1"][0].astype(MXU), "b_pw1": full["cv_b_pw1"],
        "w_dw": jnp.pad(full["cv_w_dw"][0], ((0, HALO - CONV_W), (0, 0))),
        "b_dw": full["cv_b_dw"], "ln_g": full["cv_ln_g"], "ln_b": full["cv_ln_b"],
        "pw2": full["cv_w_pw2"][0].astype(MXU), "b_pw2": full["cv_b_pw2"],
        "mix_norm": full["mix_norm"], "ffn_norm": full["ffn_norm"],
    }
    W["wo_g"] = W["wo_gs"][:GLA_DV]
    W["wo_s"] = W["wo_gs"][GLA_DV:]
    for l in range(2):
        W[f"wg{l}"] = full["ffn_w_gate"][l].astype(MXU)
        W[f"wu{l}"] = full["ffn_w_up"][l].astype(MXU)
        W[f"wd{l}"] = full["ffn_w_down"][l].astype(MXU)
    return W


def to_reference_layout(G):
    return {
        "mix_norm": G["mix_norm"], "ffn_norm": G["ffn_norm"],
        "hy_w_in": jnp.concatenate([G["wi_g"], G["wi_l"][:, :GLA_RANK], G["wi_s"]], axis=1)[None],
        "hy_w_gate2": G["w2"][:GLA_RANK][None],
        "hy_b_gate": G["b_gate"], "hy_gla_norm": G["g_gla"],
        "hy_sb_q_norm": G["g_q"], "hy_sb_k_norm": G["g_k"],
        "hy_w_out": jnp.concatenate([G["wo_g"], G["wo_s"]], axis=0)[None],
        "cv_w_pw1": G["pw1"][None], "cv_b_pw1": G["b_pw1"],
        "cv_w_dw": G["w_dw"][:CONV_W][None], "cv_b_dw": G["b_dw"],
        "cv_ln_g": G["ln_g"], "cv_ln_b": G["ln_b"],
        "cv_w_pw2": G["pw2"][None], "cv_b_pw2": G["b_pw2"],
        "ffn_w_gate": jnp.stack([G["wg0"], G["wg1"]]),
        "ffn_w_up": jnp.stack([G["wu0"], G["wu1"]]),
        "ffn_w_down": jnp.stack([G["wd0"], G["wd1"]]),
    }


WEIGHTS = ["mix_norm", "ffn_norm", "hy_w_in", "hy_w_gate2", "hy_b_gate", "hy_gla_norm", "hy_sb_q_norm",
           "hy_sb_k_norm", "hy_w_out", "cv_w_pw1", "cv_b_pw1", "cv_w_dw", "cv_b_dw", "cv_ln_g", "cv_ln_b",
           "cv_w_pw2", "cv_b_pw2", "ffn_w_gate", "ffn_w_up", "ffn_w_down"]
SHARD_AXIS = {"hy_w_in": 2, "hy_w_out": 1, "cv_w_pw1": 2, "cv_w_pw2": 1, "ffn_w_gate": 2, "ffn_w_up": 2,
              "ffn_w_down": 1, "hy_w_gate2": 2, "cv_b_pw1": 1, "cv_w_dw": 2, "cv_b_dw": 1, "cv_ln_g": 1,
              "cv_ln_b": 1, "cv_b_pw2": 1}
BIG = ["hy_w_in", "hy_w_out", "cv_w_pw1", "cv_w_pw2", "ffn_w_gate", "ffn_w_up", "ffn_w_down"]
SMALL = ["hy_w_gate2", "cv_b_pw1", "cv_w_dw", "cv_b_dw", "cv_ln_g", "cv_ln_b", "cv_b_pw2"]
REPL = ["mix_norm", "ffn_norm", "hy_b_gate", "hy_gla_norm", "hy_sb_q_norm", "hy_sb_k_norm"]
N_CHIPS = 4
N_DEV = 8


def _pack(arrs, rows_multiple, dtype):
    flat = jnp.concatenate([a.reshape(-1).astype(dtype) for a in arrs])
    n = flat.shape[0]
    rows = -(-n // (LANES * rows_multiple)) * rows_multiple
    return jnp.pad(flat, (0, rows * LANES - n)).reshape(rows, LANES)


def _unpack(flat2d, shapes):
    flat = flat2d.reshape(-1)
    out, off = [], 0
    for s in shapes:
        n = math.prod(s)
        out.append(flat[off:off + n].reshape(s))
        off += n
    return out


def _coords():
    return lax.axis_index("x"), lax.axis_index("y"), lax.axis_index("c")


def _flip(pos, f):
    return tuple(1 - p if b else p for p, b in zip(pos, f))


def exchange(src, n_dst, flips, src_idx, dst_idx, local_idx, *, name):
    _, R, L = src.shape
    n = len(flips)

    def body(src_ref, dst_ref, send_sems, recv_sems, loc_sem):
        me = _coords()
        loc = None
        if local_idx is not None:
            si, di = local_idx(me)
            loc = pltpu.make_async_copy(src_ref.at[si], dst_ref.at[di], loc_sem)
            loc.start()
        sends, recvs = [], []
        for k, f in enumerate(flips):
            peer = _flip(me, f)
            send = pltpu.make_async_remote_copy(
                src_ref=src_ref.at[src_idx(me, peer)], dst_ref=dst_ref.at[dst_idx(me)],
                send_sem=send_sems.at[k], recv_sem=recv_sems.at[k],
                device_id=peer, device_id_type=pl.DeviceIdType.MESH)
            send.start()
            sends.append(send)
            recvs.append(pltpu.make_async_remote_copy(
                src_ref=src_ref.at[src_idx(peer, me)], dst_ref=dst_ref.at[dst_idx(peer)],
                send_sem=send_sems.at[k], recv_sem=recv_sems.at[k],
                device_id=peer, device_id_type=pl.DeviceIdType.MESH))
        for s in sends:
            s.wait_send()
        for r in recvs:
            r.wait_recv()
        if loc is not None:
            loc.wait()

    return pl.pallas_call(
        body, name=name,
        out_shape=jax.ShapeDtypeStruct((n_dst, R, L), src.dtype),
        in_specs=[pl.BlockSpec(memory_space=pl.ANY)],
        out_specs=pl.BlockSpec(memory_space=pl.ANY),
        scratch_shapes=[pltpu.SemaphoreType.DMA((n,)), pltpu.SemaphoreType.DMA((n,)), pltpu.SemaphoreType.DMA(())],
    )(src)


CHIP_FLIPS = [(1, 0, 0), (0, 1, 0), (1, 1, 0)]
SIBLING = [(0, 0, 1)]
ALL_FLIPS = [(a, b, c) for a in (0, 1) for b in (0, 1) for c in (0, 1) if (a, b, c) != (0, 0, 0)]


def _chip(pos):
    return 2 * pos[0] + pos[1]


def _dev(pos):
    return 4 * pos[0] + 2 * pos[1] + pos[2]


def gather_shards(shard, *, name):
    return exchange(shard[None], N_CHIPS, CHIP_FLIPS, lambda me, peer: 0, _chip,
                    lambda me: (0, _chip(me)), name=name)


def add_halves(G2, recv, c, *, name, tr=512):
    _, nk, Rh, L = G2.shape
    tr = _rows(Rh, tr)
    mine = lax.dynamic_index_in_dim(G2, c, 0, keepdims=False)

    def body(a_ref, b_ref, o_ref):
        o_ref[...] = a_ref[...] + b_ref[...]

    blk = lambda k, i: (k, i, 0)
    return _call(body, name=name, grid=(nk, Rh // tr),
                 ins=[(mine, (1, tr, L), blk), (recv, (1, tr, L), blk)],
                 outs=[((nk, Rh, L), F32, (1, tr, L), blk)])[0]


def sum_slots(x, *, name, tr=512):
    n, R, L = x.shape
    tr = _rows(R, tr)

    def body(x_ref, o_ref):
        acc = x_ref[0]
        for k in range(1, n):
            acc = acc + x_ref[k]
        o_ref[...] = acc

    return _call(body, name=name, grid=(R // tr,),
                 ins=[(x, (n, tr, L), lambda i: (0, i, 0))],
                 outs=[((R, L), F32, (tr, L), lambda i: (i, 0))])[0]


def adamw(w, g, m, v, *, name, tr=512):
    R, L = w.shape
    tr = _rows(R, tr)

    def body(w_ref, g_ref, m_ref, v_ref, d_ref, mo_ref, vo_ref):
        g = g_ref[...]
        m = ADAM_B1 * m_ref[...] + (1.0 - ADAM_B1) * g
        v = ADAM_B2 * v_ref[...] + (1.0 - ADAM_B2) * (g * g)
        m_hat = m / (1.0 - ADAM_B1 ** ADAM_STEP)
        v_hat = v / (1.0 - ADAM_B2 ** ADAM_STEP)
        d_ref[...] = -ADAM_LR * (m_hat / (jnp.sqrt(v_hat) + ADAM_EPS) + ADAM_WD * w_ref[...])
        mo_ref[...] = m
        vo_ref[...] = v

    row = lambda i: (i, 0)
    return _call(body, name=name, grid=(R // tr,),
                 ins=[(a, (tr, L), row) for a in (w, g, m, v)],
                 outs=[((R, L), F32, (tr, L), row)] * 3)


def kernel(x, mix_norm, ffn_norm, hy_w_in, hy_w_gate2, hy_b_gate, hy_gla_norm, hy_sb_q_norm, hy_sb_k_norm, hy_w_out, cv_w_pw1, cv_b_pw1, cv_w_dw, cv_b_dw, cv_ln_g, cv_ln_b, cv_w_pw2, cv_b_pw2, ffn_w_gate, ffn_w_up, ffn_w_down, loss_target, m_mix_norm, m_ffn_norm, m_hy_w_in, m_hy_w_gate2, m_hy_b_gate, m_hy_gla_norm, m_hy_sb_q_norm, m_hy_sb_k_norm, m_hy_w_out, m_cv_w_pw1, m_cv_b_pw1, m_cv_w_dw, m_cv_b_dw, m_cv_ln_g, m_cv_ln_b, m_cv_w_pw2, m_cv_b_pw2, m_ffn_w_gate, m_ffn_w_up, m_ffn_w_down, v_mix_norm, v_ffn_norm, v_hy_w_in, v_hy_w_gate2, v_hy_b_gate, v_hy_gla_norm, v_hy_sb_q_norm, v_hy_sb_k_norm, v_hy_w_out, v_cv_w_pw1, v_cv_b_pw1, v_cv_w_dw, v_cv_b_dw, v_cv_ln_g, v_cv_ln_b, v_cv_w_pw2, v_cv_b_pw2, v_ffn_w_gate, v_ffn_w_up, v_ffn_w_down):
    w = dict(zip(WEIGHTS, (mix_norm, ffn_norm, hy_w_in, hy_w_gate2, hy_b_gate, hy_gla_norm, hy_sb_q_norm, hy_sb_k_norm, hy_w_out, cv_w_pw1, cv_b_pw1, cv_w_dw, cv_b_dw, cv_ln_g, cv_ln_b, cv_w_pw2, cv_b_pw2, ffn_w_gate, ffn_w_up, ffn_w_down)))
    m = dict(zip(WEIGHTS, (m_mix_norm, m_ffn_norm, m_hy_w_in, m_hy_w_gate2, m_hy_b_gate, m_hy_gla_norm, m_hy_sb_q_norm, m_hy_sb_k_norm, m_hy_w_out, m_cv_w_pw1, m_cv_b_pw1, m_cv_w_dw, m_cv_b_dw, m_cv_ln_g, m_cv_ln_b, m_cv_w_pw2, m_cv_b_pw2, m_ffn_w_gate, m_ffn_w_up, m_ffn_w_down)))
    v = dict(zip(WEIGHTS, (v_mix_norm, v_ffn_norm, v_hy_w_in, v_hy_w_gate2, v_hy_b_gate, v_hy_gla_norm, v_hy_sb_q_norm, v_hy_sb_k_norm, v_hy_w_out, v_cv_w_pw1, v_cv_b_pw1, v_cv_w_dw, v_cv_b_dw, v_cv_ln_g, v_cv_ln_b, v_cv_w_pw2, v_cv_b_pw2, v_ffn_w_gate, v_ffn_w_up, v_ffn_w_down)))
    me = _coords()
    c = me[2]
    sharded = BIG + SMALL
    shard_shapes = {n: w[n].shape for n in sharded}

    big = gather_shards(_pack([w[n] for n in BIG], 32, MXU), name="gather_big")
    small = gather_shards(_pack([w[n] for n in SMALL], 8, F32), name="gather_small")
    full = {n: w[n] for n in REPL}
    for names, buf in ((BIG, big), (SMALL, small)):
        parts = [_unpack(buf[k], [shard_shapes[n] for n in names]) for k in range(N_CHIPS)]
        for j, n in enumerate(names):
            full[n] = jnp.concatenate([parts[k][j] for k in range(N_CHIPS)], axis=SHARD_AXIS[n])

    loss_lanes, dx, Gk = local_step(x[0], loss_target[0], to_kernel_layout(full))
    Gr = to_reference_layout(Gk)

    pieces = [jnp.split(Gr[n], N_CHIPS, axis=SHARD_AXIS[n]) for n in sharded]
    per_chip = [_pack([p[k] for p in pieces], 1024, F32) for k in range(N_CHIPS)]
    R = per_chip[0].shape[0]
    Rh = R // 2
    G2 = jnp.stack([pc.reshape(2, Rh, LANES) for pc in per_chip], axis=1)
    recv_pair = exchange(G2.reshape(2, N_CHIPS * Rh, LANES), 1, SIBLING,
                         lambda me_, peer: 1 - me_[2], lambda me_: 0, None, name="reduce_pair")
    chip_part = add_halves(G2, recv_pair.reshape(N_CHIPS, Rh, LANES), c, name="reduce_pair_add")
    slots = exchange(chip_part, N_CHIPS, CHIP_FLIPS, lambda me_, peer: _chip(peer), _chip,
                     lambda me_: (_chip(me_), _chip(me_)), name="reduce_chips")
    half = sum_slots(slots, name="reduce_chips_sum")
    g_flat = exchange(half[None], 2, SIBLING, lambda me_, peer: 0, lambda me_: me_[2],
                      lambda me_: (0, me_[2]), name="reduce_share").reshape(R, LANES)

    loss_row = jnp.pad(jnp.sum(loss_lanes).reshape(1), (0, LANES - 1))
    rep = _pack([Gr[n] for n in REPL] + [loss_row], 8, F32)
    rep_all = exchange(rep[None], N_DEV, ALL_FLIPS, lambda me_, peer: 0, _dev, lambda me_: (0, _dev(me_)),
                       name="reduce_small")
    rep_sum = sum_slots(rep_all, name="reduce_small_sum", tr=rep.shape[0])
    rep_shapes = [w[n].shape for n in REPL]
    rep_n = sum(math.prod(s) for s in rep_shapes)
    loss = rep_sum.reshape(-1)[rep_n]

    w_flat = _pack([w[n] for n in sharded], 1024, F32)
    m_flat = _pack([m[n] for n in sharded], 1024, F32)
    v_flat = _pack([v[n] for n in sharded], 1024, F32)
    d_flat, mo_flat, vo_flat = adamw(w_flat, g_flat, m_flat, v_flat, name="adamw")
    shapes = [shard_shapes[n] for n in sharded]
    out = {"grad": dict(zip(sharded, _unpack(g_flat, shapes))), "delta": dict(zip(sharded, _unpack(d_flat, shapes))),
           "new_m": dict(zip(sharded, _unpack(mo_flat, shapes))), "new_v": dict(zip(sharded, _unpack(vo_flat, shapes)))}
    rw = _pack([w[n] for n in REPL], 8, F32)
    rm = _pack([m[n] for n in REPL], 8, F32)
    rv = _pack([v[n] for n in REPL], 8, F32)
    rg = _pack(_unpack(rep_sum, rep_shapes), 8, F32)
    rd, rmo, rvo = adamw(rw, rg, rm, rv, name="adamw_small", tr=rw.shape[0])
    for key, buf in (("grad", rg), ("delta", rd), ("new_m", rmo), ("new_v", rvo)):
        out[key].update(dict(zip(REPL, _unpack(buf, rep_shapes))))

    return (loss, dx[None], *[out["grad"][n] for n in WEIGHTS], *[out["delta"][n] for n in WEIGHTS],
            *[out["new_m"][n] for n in WEIGHTS], *[out["new_v"][n] for n in WEIGHTS])
```

```python
import functools
import math

import numpy as np
import jax
import jax.numpy as jnp
from jax import lax
from jax.experimental import pallas as pl
from jax.experimental.pallas import tpu as pltpu

F32 = jnp.float32
MXU = jnp.bfloat16
EPS = 1e-6
LANES = 128
VMEM_LIMIT = 56 * 1024 * 1024

D = 1024
F = 2816
CHUNK = 64
GLA_H, GLA_K, GLA_V, GLA_RANK = 4, 64, 128, 16
SB_H, SB_D = 8, 64
CONV_W = 31
HALO = 32

ADAM_LR, ADAM_B1, ADAM_B2, ADAM_EPS, ADAM_WD, ADAM_STEP = 0.001, 0.9, 0.999, 1e-08, 0.01, 10

NN = (((1,), (0,)), ((), ()))
NT = (((1,), (1,)), ((), ()))
TN = (((0,), (0,)), ((), ()))


def _dot(a, b, dims=NN):
    return lax.dot_general(a.astype(MXU), b.astype(MXU), dims, preferred_element_type=F32)


def _dot_split(a, b, dims=NN, a_split=True):
    x = a if a_split else b
    hi = x.astype(MXU)
    lo = (x - hi.astype(F32)).astype(MXU)
    if a_split:
        return _dot(hi, b, dims) + _dot(lo, b, dims)
    return _dot(a, hi, dims) + _dot(a, lo, dims)


def _sigmoid(x):
    return 1.0 / (1.0 + jnp.exp(-x))


def _softplus_neg_abs(z):
    return jnp.log(1.0 + jnp.exp(-jnp.abs(z)))


def _call(body, *, name, grid, ins, outs, scratch=()):
    spec = lambda b, m: pl.BlockSpec(memory_space=pl.ANY) if b is None else pl.BlockSpec(b, m)
    res = pl.pallas_call(
        body,
        name=name,
        grid=grid,
        in_specs=[spec(b, m) for _, b, m in ins],
        out_specs=[spec(b, m) for _, _, b, m in outs],
        out_shape=[jax.ShapeDtypeStruct(s, d) for s, d, _, _ in outs],
        scratch_shapes=list(scratch),
        compiler_params=pltpu.CompilerParams(
            dimension_semantics=("arbitrary",) * len(grid), vmem_limit_bytes=VMEM_LIMIT),
    )(*[a for a, _, _ in ins])
    return res


def _rows(T, tm):
    tm = min(tm, T)
    assert T % tm == 0, (T, tm)
    return tm


def mm(pairs, *, name, trans_b=False, bias=None, res=None, out_dtype=F32, tm=512, tn=512):
    M = pairs[0][0].shape[0]
    N = pairs[0][1].shape[0] if trans_b else pairs[0][1].shape[1]
    tm = _rows(M, tm)
    tn = min(tn, N)
    assert N % tn == 0, (N, tn)
    np_ = len(pairs)

    def body(*refs):
        o_ref = refs[-1]
        acc = None
        for p in range(np_):
            d = _dot(refs[2 * p][...], refs[2 * p + 1][...], NT if trans_b else NN)
            acc = d if acc is None else acc + d
        k = 2 * np_
        if bias is not None:
            acc = acc + refs[k][...]
            k += 1
        if res is not None:
            acc = acc + refs[k][...]
        o_ref[...] = acc.astype(out_dtype)

    ins = []
    for a, b in pairs:
        K = a.shape[1]
        ins.append((a, (tm, K), lambda i, j: (i, 0)))
        if trans_b:
            ins.append((b, (tn, K), lambda i, j: (j, 0)))
        else:
            ins.append((b, (K, tn), lambda i, j: (0, j)))
    if bias is not None:
        ins.append((bias, (1, tn), lambda i, j: (0, j)))
    if res is not None:
        ins.append((res, (tm, tn), lambda i, j: (i, j)))
    return _call(body, name=name, grid=(M // tm, N // tn), ins=ins,
                 outs=[((M, N), out_dtype, (tm, tn), lambda i, j: (i, j))])[0]


def mm_tn(a, b, *, name, tm=512, tn=512, tk=512):
    T, M = a.shape
    N = b.shape[1]
    tm, tn, tk = min(tm, M), min(tn, N), min(tk, T)
    assert M % tm == 0 and N % tn == 0 and T % tk == 0, (M, N, T, tm, tn, tk)

    def body(a_ref, b_ref, o_ref):
        @pl.when(pl.program_id(2) == 0)
        def _():
            o_ref[...] = jnp.zeros_like(o_ref)
        o_ref[...] += _dot(a_ref[...], b_ref[...], TN)

    return _call(body, name=name, grid=(M // tm, N // tn, T // tk),
                 ins=[(a, (tk, tm), lambda i, j, k: (k, i)), (b, (tk, tn), lambda i, j, k: (k, j))],
                 outs=[((M, N), F32, (tm, tn), lambda i, j, k: (i, j))])[0]


def rms_fwd(x, g, *, name, tm=256):
    T = x.shape[0]
    tm = _rows(T, tm)

    def body(x_ref, g_ref, o_ref):
        x = x_ref[...]
        r = lax.rsqrt(jnp.mean(x * x, axis=-1, keepdims=True) + EPS)
        o_ref[...] = (x * r * g_ref[...]).astype(MXU)

    return _call(body, name=name, grid=(T // tm,),
                 ins=[(x, (tm, D), lambda i: (i, 0)), (g, (1, D), lambda i: (0, 0))],
                 outs=[((T, D), MXU, (tm, D), lambda i: (i, 0))])[0]


def rms_bwd(x, g, dhn, dres, *, name, tm=256):
    T = x.shape[0]
    tm = _rows(T, tm)

    def body(x_ref, g_ref, d_ref, r_ref, dx_ref, dg_ref, cs_ref):
        @pl.when(pl.program_id(0) == 0)
        def _():
            dg_ref[...] = jnp.zeros_like(dg_ref)
            cs_ref[...] = jnp.zeros_like(cs_ref)
        x = x_ref[...]
        d = d_ref[...]
        r = lax.rsqrt(jnp.mean(x * x, axis=-1, keepdims=True) + EPS)
        dg_ref[...] += jnp.sum(d * x * r, axis=0, keepdims=True)
        t = d * g_ref[...]
        m = jnp.mean(t * x, axis=-1, keepdims=True)
        dx = r_ref[...] + t * r - x * (r * r * r) * m
        dx_ref[...] = dx
        cs_ref[...] += jnp.sum(dx, axis=0, keepdims=True)

    row = lambda i: (i, 0)
    fix = lambda i: (0, 0)
    return _call(body, name=name, grid=(T // tm,),
                 ins=[(x, (tm, D), row), (g, (1, D), fix), (dhn, (tm, D), row), (dres, (tm, D), row)],
                 outs=[((T, D), F32, (tm, D), row), ((1, D), F32, (1, D), fix), ((1, D), F32, (1, D), fix)])


def loss_grad(y, t, *, name, tm=256):
    T = y.shape[0]
    tm = _rows(T, tm)

    def body(y_ref, t_ref, dy_ref, l_ref):
        @pl.when(pl.program_id(0) == 0)
        def _():
            l_ref[...] = jnp.zeros_like(l_ref)
        e = y_ref[...] - t_ref[...]
        dy_ref[...] = e * (1.0 / D)
        l_ref[...] += jnp.sum(e * e, axis=0, keepdims=True) * (0.5 / D)

    row = lambda i: (i, 0)
    return _call(body, name=name, grid=(T // tm,),
                 ins=[(y, (tm, D), row), (t, (tm, D), row)],
                 outs=[((T, D), F32, (tm, D), row), ((1, D), F32, (1, D), lambda i: (0, 0))])


def ffn_up(hn, wg, wu, *, name, tm=512, tn=256):
    T = hn.shape[0]
    tm = _rows(T, tm)

    def body(h_ref, wg_ref, wu_ref, g_ref, u_ref, a_ref):
        h = h_ref[...]
        g = _dot(h, wg_ref[...])
        u = _dot(h, wu_ref[...])
        g_ref[...] = g
        u_ref[...] = u
        a_ref[...] = (g * _sigmoid(g) * u).astype(MXU)

    tile = lambda i, j: (i, j)
    return _call(body, name=name, grid=(T // tm, F // tn),
                 ins=[(hn, (tm, D), lambda i, j: (i, 0)), (wg, (D, tn), lambda i, j: (0, j)),
                      (wu, (D, tn), lambda i, j: (0, j))],
                 outs=[((T, F), F32, (tm, tn), tile), ((T, F), F32, (tm, tn), tile),
                       ((T, F), MXU, (tm, tn), tile)])


def ffn_bwd_act(dy, wd, G, U, *, name, tm=512, tn=256):
    T = dy.shape[0]
    tm = _rows(T, tm)

    def body(dy_ref, wd_ref, g_ref, u_ref, dg_ref, du_ref, a_ref):
        da = _dot(dy_ref[...], wd_ref[...], NT)
        g = g_ref[...]
        u = u_ref[...]
        s = _sigmoid(g)
        sil = g * s
        a_ref[...] = (sil * u).astype(MXU)
        du_ref[...] = (da * sil).astype(MXU)
        dg_ref[...] = (da * u * (s * (1.0 + g * (1.0 - s)))).astype(MXU)

    tile = lambda i, j: (i, j)
    return _call(body, name=name, grid=(T // tm, F // tn),
                 ins=[(dy, (tm, D), lambda i, j: (i, 0)), (wd, (tn, D), lambda i, j: (j, 0)),
                      (G, (tm, tn), tile), (U, (tm, tn), tile)],
                 outs=[((T, F), MXU, (tm, tn), tile), ((T, F), MXU, (tm, tn), tile),
                       ((T, F), MXU, (tm, tn), tile)])


def ffn_fwd(h, g_norm, wg, wu, wd, tag):
    hn = rms_fwd(h, g_norm, name=f"ffn{tag}_norm")
    G, U, act = ffn_up(hn, wg, wu, name=f"ffn{tag}_up")
    h_out = mm([(act, wd)], res=h, name=f"ffn{tag}_down")
    return h_out, (hn, G, U)


def ffn_bwd(dy, h, g_norm, wg, wu, wd, saved, tag):
    hn, G, U = saved
    dG, dU, act = ffn_bwd_act(dy, wd, G, U, name=f"ffn{tag}_bwd_act")
    d_wd = mm_tn(act, dy, name=f"ffn{tag}_dwd", tm=1408, tn=512)
    d_wg = mm_tn(hn, dG, name=f"ffn{tag}_dwg", tm=512, tn=1408)
    d_wu = mm_tn(hn, dU, name=f"ffn{tag}_dwu", tm=512, tn=1408)
    dhn = mm([(dG, wg), (dU, wu)], trans_b=True, name=f"ffn{tag}_dhn")
    dh, d_g, cs = rms_bwd(h, g_norm, dhn, dy, name=f"ffn{tag}_norm_bwd")
    return dh, d_g, d_wg, d_wu, d_wd, cs


def conv_fwd(A, w_dw, b_dw, ln_g, ln_b, *, name, tm=128, rb=32):
    T = A.shape[0]
    tm = _rows(T, tm)

    def body(a_ref, ap_ref, w_ref, b_ref, g_ref, bb_ref, s_ref, c_ref, win):
        i = pl.program_id(0)
        a = a_ref[...]
        win[pl.ds(HALO, tm), :] = a[:, :D] * _sigmoid(a[:, D:])
        ap = ap_ref[pl.ds(tm - HALO, HALO), :]
        up = ap[:, :D] * _sigmoid(ap[:, D:])
        win[pl.ds(0, HALO), :] = jnp.where(i > 0, up, 0.0)
        for r0 in range(0, tm, rb):
            acc = jnp.broadcast_to(b_ref[...], (rb, D))
            for k in range(CONV_W):
                acc = acc + w_ref[pl.ds(k, 1), :] * win[pl.ds(r0 + k + HALO - (CONV_W - 1), rb), :]
            c_ref[pl.ds(r0, rb), :] = acc
        c = c_ref[...]
        mu = jnp.mean(c, axis=-1, keepdims=True)
        cc = c - mu
        var = jnp.mean(cc * cc, axis=-1, keepdims=True)
        z = cc * lax.rsqrt(var + EPS) * g_ref[...] + bb_ref[...]
        s_ref[...] = (z * _sigmoid(z)).astype(MXU)

    row = lambda i: (i, 0)
    fix = lambda i: (0, 0)
    return _call(body, name=name, grid=(T // tm,),
                 ins=[(A, (tm, 2 * D), row), (A, (tm, 2 * D), lambda i: (jnp.maximum(i - 1, 0), 0)),
                      (w_dw, (HALO, D), fix), (b_dw, (1, D), fix), (ln_g, (1, D), fix), (ln_b, (1, D), fix)],
                 outs=[((T, D), MXU, (tm, D), row), ((T, D), F32, (tm, D), row)],
                 scratch=[pltpu.VMEM((tm + HALO, D), F32)])


def ln_swish_bwd(c, ds, ln_g, ln_b, *, name, tm=256):
    T = c.shape[0]
    tm = _rows(T, tm)

    def body(c_ref, ds_ref, g_ref, b_ref, dc_ref, dg_ref, db_ref):
        @pl.when(pl.program_id(0) == 0)
        def _():
            dg_ref[...] = jnp.zeros_like(dg_ref)
            db_ref[...] = jnp.zeros_like(db_ref)
        c = c_ref[...]
        mu = jnp.mean(c, axis=-1, keepdims=True)
        cc = c - mu
        rstd = lax.rsqrt(jnp.mean(cc * cc, axis=-1, keepdims=True) + EPS)
        n = cc * rstd
        z = n * g_ref[...] + b_ref[...]
        s = _sigmoid(z)
        dz = ds_ref[...] * (s * (1.0 + z * (1.0 - s)))
        dg_ref[...] += jnp.sum(dz * n, axis=0, keepdims=True)
        db_ref[...] += jnp.sum(dz, axis=0, keepdims=True)
        dn = dz * g_ref[...]
        dc_ref[...] = rstd * (dn - jnp.mean(dn, axis=-1, keepdims=True)
                              - n * jnp.mean(dn * n, axis=-1, keepdims=True))

    row = lambda i: (i, 0)
    fix = lambda i: (0, 0)
    return _call(body, name=name, grid=(T // tm,),
                 ins=[(c, (tm, D), row), (ds, (tm, D), row), (ln_g, (1, D), fix), (ln_b, (1, D), fix)],
                 outs=[((T, D), F32, (tm, D), row), ((1, D), F32, (1, D), fix), ((1, D), F32, (1, D), fix)])


def conv_bwd(dc, A, w_dw, *, name, tm=128, rb=32):
    T = A.shape[0]
    tm = _rows(T, tm)
    n = T // tm
    SUB = 8

    def body(dc_ref, dn_ref, a_ref, ap_ref, w_ref, da_ref, dw_ref, dbd_ref, dbp_ref, wdc, wu, accw):
        i = pl.program_id(0)

        @pl.when(i == 0)
        def _():
            accw[...] = jnp.zeros_like(accw)
            dbd_ref[...] = jnp.zeros_like(dbd_ref)
            dbp_ref[...] = jnp.zeros_like(dbp_ref)
        a = a_ref[...]
        a1, a2 = a[:, :D], a[:, D:]
        sg = _sigmoid(a2)
        wu[pl.ds(HALO, tm), :] = a1 * sg
        ap = ap_ref[pl.ds(tm - HALO, HALO), :]
        wu[pl.ds(0, HALO), :] = jnp.where(i > 0, ap[:, :D] * _sigmoid(ap[:, D:]), 0.0)
        dc = dc_ref[...]
        wdc[pl.ds(0, tm), :] = dc
        wdc[pl.ds(tm, HALO), :] = jnp.where(i < n - 1, dn_ref[pl.ds(0, HALO), :], 0.0)
        dbd_ref[...] += jnp.sum(dc, axis=0, keepdims=True)
        for r0 in range(0, tm, rb):
            du = jnp.zeros((rb, D), F32)
            dcs = wdc[pl.ds(r0, rb), :]
            for k in range(CONV_W):
                du = du + w_ref[pl.ds(k, 1), :] * wdc[pl.ds(r0 + (CONV_W - 1) - k, rb), :]
                p = dcs * wu[pl.ds(r0 + k + HALO - (CONV_W - 1), rb), :]
                accw[pl.ds(SUB * k, SUB), :] += jnp.sum(p.reshape(rb // SUB, SUB, D), axis=0)
            s = sg[r0:r0 + rb]
            da_ref[pl.ds(r0, rb), pl.ds(0, D)] = (du * s).astype(MXU)
            da_ref[pl.ds(r0, rb), pl.ds(D, D)] = (du * a1[r0:r0 + rb] * s * (1.0 - s)).astype(MXU)
        da = da_ref[...].astype(F32)
        dbp_ref[...] += jnp.sum(da, axis=0, keepdims=True)

        @pl.when(i == n - 1)
        def _():
            dw_ref[...] = jnp.zeros_like(dw_ref)
            for k in range(CONV_W):
                dw_ref[pl.ds(k, 1), :] = jnp.sum(accw[pl.ds(SUB * k, SUB), :], axis=0, keepdims=True)

    row = lambda i: (i, 0)
    fix = lambda i: (0, 0)
    return _call(body, name=name, grid=(n,),
                 ins=[(dc, (tm, D), row), (dc, (tm, D), lambda i: (jnp.minimum(i + 1, n - 1), 0)),
                      (A, (tm, 2 * D), row), (A, (tm, 2 * D), lambda i: (jnp.maximum(i - 1, 0), 0)),
                      (w_dw, (HALO, D), fix)],
                 outs=[((T, 2 * D), MXU, (tm, 2 * D), row), ((HALO, D), F32, (HALO, D), fix),
                       ((1, D), F32, (1, D), fix), ((1, 2 * D), F32, (1, 2 * D), fix)],
                 scratch=[pltpu.VMEM((tm + HALO, D), F32), pltpu.VMEM((tm + HALO, D), F32),
                          pltpu.VMEM((SUB * HALO, D), F32)])


def sb_prep(ps, gq, gk, *, name, tm=256):
    T = ps.shape[0]
    tm = _rows(T, tm)
    W = SB_H * SB_D

    def body(q_ref, k_ref, v_ref, gq_ref, gk_ref, qo, ko, vo):
        for h in range(SB_H):
            sl = slice(h * SB_D, (h + 1) * SB_D)
            q = q_ref[:, sl]
            k = k_ref[:, sl]
            rq = lax.rsqrt(jnp.mean(q * q, axis=-1, keepdims=True) + EPS)
            rk = lax.rsqrt(jnp.mean(k * k, axis=-1, keepdims=True) + EPS)
            qo[h] = (q * rq * gq_ref[...] * (SB_D ** -0.5)).astype(MXU)
            ko[h] = (k * rk * gk_ref[...]).astype(MXU)
            vo[h] = v_ref[:, sl].astype(MXU)

    hm = lambda i: (0, i, 0)
    fix = lambda i: (0, 0)
    return _call(body, name=name, grid=(T // tm,),
                 ins=[(ps, (tm, W), lambda i: (i, 0)), (ps, (tm, W), lambda i: (i, 1)),
                      (ps, (tm, W), lambda i: (i, 2)), (gq, (1, SB_D), fix), (gk, (1, SB_D), fix)],
                 outs=[((SB_H, T, SB_D), MXU, (SB_H, tm, SB_D), hm)] * 3)


def _sb_masks(tq):
    row = lax.broadcasted_iota(jnp.int32, (tq, tq), 0)
    col = lax.broadcasted_iota(jnp.int32, (tq, tq), 1)
    tri = jnp.where(row >= col, 1.0, 0.0).astype(MXU)
    past = col < row
    return tri, past


def _sb_scores(q, k, past):
    return _sb_logs(_dot(q, k, NT), past)


def _sb_logs(z, past):
    sp = _softplus_neg_abs(z)
    ls = jnp.minimum(z, 0.0) - sp
    lk = -jnp.maximum(z, 0.0) - sp
    if past is not None:
        lk = jnp.where(past, lk, 0.0)
    return ls, lk


def _fused_exchange(fused, steps_pred):
    if fused is None:
        return [], [], [], lambda refs: None, lambda refs: None
    src, n_dst, spec = fused
    n = len(spec["flips"])
    ins = [(src, None, None)]
    outs = [((n_dst,) + src.shape[1:], src.dtype, None, None)]
    scratch = [pltpu.SemaphoreType.DMA((n,)), pltpu.SemaphoreType.DMA((n,)), pltpu.SemaphoreType.DMA(())]

    def at_start(refs):
        @pl.when(steps_pred()[0])
        def _():
            _exchange_start(*refs, **spec)

    def at_end(refs):
        @pl.when(steps_pred()[1])
        def _():
            _exchange_wait(*refs, **spec)

    return ins, outs, scratch, at_start, at_end


def sb_fwd(qn, kn, v, *, name, tq=256, hg=4, fused=None):
    H, T, _ = qn.shape
    tq = _rows(T, tq)
    assert T // tq <= LANES and H % hg == 0
    ng, ni = H // hg, T // tq
    pred = lambda: ((pl.program_id(0) == 0) & (pl.program_id(1) == 0),
                    (pl.program_id(0) == ng - 1) & (pl.program_id(1) == ni - 1))
    x_ins, x_outs, x_scratch, x_start, x_end = _fused_exchange(fused, pred)
    nx = len(x_ins)

    def body(*refs):
        q_ref, k_ref, v_ref = refs[:3]
        o_ref, cb_ref = refs[3 + nx:5 + nx]
        x_refs = refs[3:3 + nx] + refs[5 + nx:]
        x_start(x_refs)
        i = pl.program_id(1)
        tri, past = _sb_masks(tq)
        lane = lax.broadcasted_iota(jnp.int32, (tq, LANES), 1)
        cb_ref[...] = jnp.zeros((hg, tq, LANES), F32)
        qs = [q_ref[h] for h in range(hg)]

        def block(j, carry, masked):
            accs, cs = carry
            start = pl.multiple_of(j * tq, tq)
            heads = range(hg)
            for h in heads:
                cb_ref[h] = jnp.where(lane == j, cs[h], cb_ref[h])
            zs = [_dot(qs[h], k_ref[h, pl.ds(start, tq), :], NT) for h in heads]
            sc = [_sb_logs(zs[h], past if masked else None) for h in heads]
            bincs = [_dot_split(sc[h][1], tri) for h in heads]
            ws = [jnp.exp(sc[h][0] + cs[h] + bincs[h] - sc[h][1]) for h in heads]
            if masked:
                ws = [jnp.where(past, w, 0.0) for w in ws]
            new_a = [accs[h] + _dot(ws[h], v_ref[h, pl.ds(start, tq), :]) for h in heads]
            new_c = [cs[h] + jnp.sum(sc[h][1], axis=-1, keepdims=True) for h in heads]
            return tuple(new_a), tuple(new_c)

        carry = ((jnp.zeros((tq, SB_D), F32),) * hg, (jnp.zeros((tq, 1), F32),) * hg)
        carry = block(i, carry, True)
        accs, _ = lax.fori_loop(0, i, lambda jj, car: block(i - 1 - jj, car, False), carry)
        for h in range(hg):
            o_ref[h] = accs[h]
        x_end(x_refs)

    blk = lambda g, i: (g, i, 0)
    full = lambda g, i: (g, 0, 0)
    return _call(body, name=name, grid=(ng, ni),
                 ins=[(qn, (hg, tq, SB_D), blk), (kn, (hg, T, SB_D), full), (v, (hg, T, SB_D), full)] + x_ins,
                 outs=[((H, T, SB_D), F32, (hg, tq, SB_D), blk), ((H, T, LANES), F32, (hg, tq, LANES), blk)] + x_outs,
                 scratch=x_scratch)


def sb_bwd(qn, kn, v, cb, do, *, name, tq=256, hg=2, fused=None):
    H, T, _ = qn.shape
    tq = _rows(T, tq)
    assert H % hg == 0
    ng, ni = H // hg, T // tq
    pred = lambda: ((pl.program_id(0) == 0) & (pl.program_id(1) == 0),
                    (pl.program_id(0) == ng - 1) & (pl.program_id(1) == ni - 1))
    x_ins, x_outs, x_scratch, x_start, x_end = _fused_exchange(fused, pred)
    nx = len(x_ins)

    def body(*refs):
        q_ref, k_ref, v_ref, cb_ref, do_ref = refs[:5]
        dq_ref, dk_ref, dv_ref = refs[5 + nx:8 + nx]
        x_refs = refs[5:5 + nx] + refs[8 + nx:]
        x_start(x_refs)
        i = pl.program_id(1)

        @pl.when(i == 0)
        def _():
            dk_ref[...] = jnp.zeros_like(dk_ref)
            dv_ref[...] = jnp.zeros_like(dv_ref)
        qs = [q_ref[h] for h in range(hg)]
        dos = [do_ref[h].astype(MXU) for h in range(hg)]
        tri, past = _sb_masks(tq)
        row = lax.broadcasted_iota(jnp.int32, (tq, tq), 0)
        col = lax.broadcasted_iota(jnp.int32, (tq, tq), 1)
        tri_lt = jnp.where(row < col, 1.0, 0.0).astype(MXU)
        lane = lax.broadcasted_iota(jnp.int32, (tq, LANES), 1)

        def block(j, carry, masked):
            dqs, cgs = carry
            start = pl.multiple_of(j * tq, tq)
            heads = range(hg)
            kblk = pl.ds(start, tq)
            zs = [_dot(qs[h], k_ref[h, kblk, :], NT) for h in heads]
            dws = [_dot(dos[h], v_ref[h, kblk, :], NT) for h in heads]
            sc = [_sb_logs(zs[h], past if masked else None) for h in heads]
            cs = [jnp.sum(jnp.where(lane == j, cb_ref[h], 0.0), axis=-1, keepdims=True) for h in heads]
            bincs = [_dot_split(sc[h][1], tri) for h in heads]
            ws = [jnp.exp(sc[h][0] + cs[h] + bincs[h] - sc[h][1]) for h in heads]
            if masked:
                ws = [jnp.where(past, w, 0.0) for w in ws]
            wbs = [w.astype(MXU) for w in ws]
            gs = [ws[h] * dws[h] for h in heads]
            gpres = [cgs[h] + _dot_split(gs[h], tri_lt) for h in heads]
            sigs = [jnp.exp(sc[h][0]) for h in heads]
            dzs = [gs[h] * (1.0 - sigs[h]) - gpres[h] * sigs[h] for h in heads]
            if masked:
                dzs = [jnp.where(past, dz, 0.0) for dz in dzs]
            dzbs = [dz.astype(MXU) for dz in dzs]
            new_dq = [dqs[h] + _dot(dzbs[h], k_ref[h, kblk, :]) for h in heads]
            for h in heads:
                dk_ref[h, kblk, :] += _dot(dzbs[h], qs[h], TN)
            for h in heads:
                dv_ref[h, kblk, :] += _dot(wbs[h], dos[h], TN)
            new_cg = [cgs[h] + jnp.sum(gs[h], axis=-1, keepdims=True) for h in heads]
            return tuple(new_dq), tuple(new_cg)

        carry = ((jnp.zeros((tq, SB_D), F32),) * hg, (jnp.zeros((tq, 1), F32),) * hg)
        carry = lax.fori_loop(0, i, lambda j, car: block(j, car, False), carry)
        dqs, _ = block(i, carry, True)
        for h in range(hg):
            dq_ref[h] = dqs[h]
        x_end(x_refs)

    blk = lambda g, i: (g, i, 0)
    full = lambda g, i: (g, 0, 0)
    return _call(body, name=name, grid=(ng, ni),
                 ins=[(qn, (hg, tq, SB_D), blk), (kn, (hg, T, SB_D), full), (v, (hg, T, SB_D), full),
                      (cb, (hg, tq, LANES), blk), (do, (hg, tq, SB_D), blk)] + x_ins,
                 outs=[((H, T, SB_D), F32, (hg, tq, SB_D), blk), ((H, T, SB_D), F32, (hg, T, SB_D), full),
                       ((H, T, SB_D), F32, (hg, T, SB_D), full)] + x_outs,
                 scratch=x_scratch)


def heads_to_cols(o, *, name, tm=256):
    H, T, _ = o.shape
    tm = _rows(T, tm)

    def body(o_ref, out_ref):
        out_ref[...] = jnp.concatenate([o_ref[h] for h in range(H)], axis=-1).astype(MXU)

    return _call(body, name=name, grid=(T // tm,),
                 ins=[(o, (H, tm, SB_D), lambda i: (0, i, 0))],
                 outs=[((T, H * SB_D), MXU, (tm, H * SB_D), lambda i: (i, 0))])[0]


def cols_to_heads(dmix, col_block, *, name, tm=256):
    T = dmix.shape[0]
    tm = _rows(T, tm)
    W = SB_H * SB_D

    def body(d_ref, out_ref):
        for h in range(SB_H):
            out_ref[h] = d_ref[:, h * SB_D:(h + 1) * SB_D]

    return _call(body, name=name, grid=(T // tm,),
                 ins=[(dmix, (tm, W), lambda i: (i, col_block))],
                 outs=[((SB_H, T, SB_D), F32, (SB_H, tm, SB_D), lambda i: (0, i, 0))])[0]


def sb_post_bwd(dqn, dkn, dv, ps, gq, gk, *, name, tm=256):
    T = ps.shape[0]
    tm = _rows(T, tm)
    W = SB_H * SB_D

    def body(dq_ref, dk_ref, dv_ref, q_ref, k_ref, gq_ref, gk_ref, out_ref, dgq_ref, dgk_ref):
        @pl.when(pl.program_id(0) == 0)
        def _():
            dgq_ref[...] = jnp.zeros_like(dgq_ref)
            dgk_ref[...] = jnp.zeros_like(dgk_ref)

        def norm_bwd(x, d, g):
            r = lax.rsqrt(jnp.mean(x * x, axis=-1, keepdims=True) + EPS)
            dg = jnp.sum(d * x * r, axis=0, keepdims=True)
            t = d * g
            return t * r - x * (r * r * r) * jnp.mean(t * x, axis=-1, keepdims=True), dg

        dqs, dks, dvs = [], [], []
        dgq = jnp.zeros((1, SB_D), F32)
        dgk = jnp.zeros((1, SB_D), F32)
        for h in range(SB_H):
            sl = slice(h * SB_D, (h + 1) * SB_D)
            a, ga = norm_bwd(q_ref[:, sl], dq_ref[h] * (SB_D ** -0.5), gq_ref[...])
            b, gb = norm_bwd(k_ref[:, sl], dk_ref[h], gk_ref[...])
            dqs.append(a)
            dks.append(b)
            dvs.append(dv_ref[h])
            dgq = dgq + ga
            dgk = dgk + gb
        out_ref[...] = jnp.concatenate(dqs + dks + dvs, axis=-1).astype(MXU)
        dgq_ref[...] += dgq
        dgk_ref[...] += dgk

    hm = lambda i: (0, i, 0)
    fix = lambda i: (0, 0)
    return _call(body, name=name, grid=(T // tm,),
                 ins=[(dqn, (SB_H, tm, SB_D), hm), (dkn, (SB_H, tm, SB_D), hm), (dv, (SB_H, tm, SB_D), hm),
                      (ps, (tm, W), lambda i: (i, 0)), (ps, (tm, W), lambda i: (i, 1)),
                      (gq, (1, SB_D), fix), (gk, (1, SB_D), fix)],
                 outs=[((T, 3 * W), MXU, (tm, 3 * W), lambda i: (i, 0)),
                       ((1, SB_D), F32, (1, SB_D), fix), ((1, SB_D), F32, (1, SB_D), fix)])


GLA_TM = 512
GLA_DK = GLA_H * GLA_K
GLA_DV = GLA_H * GLA_V


def _gla_masks(tm):
    row = lax.broadcasted_iota(jnp.int32, (tm, tm), 0)
    col = lax.broadcasted_iota(jnp.int32, (tm, tm), 1)
    same = (row // CHUNK) == (col // CHUNK)
    return row, col, same


def _gla_gate(glr, w2, b, tm):
    pre = _dot(glr, w2) + b
    la = (jnp.minimum(pre, 0.0) - _softplus_neg_abs(pre)) * (1.0 / 16.0)
    row, col, same = _gla_masks(tm)
    m_incl = jnp.where(same & (col <= row), 1.0, 0.0).astype(MXU)
    m_full = jnp.where(same, 1.0, 0.0).astype(MXU)
    bc = _dot_split(m_incl, la, a_split=False)
    tot = _dot_split(m_full, la, a_split=False)
    return pre, bc, tot


def gla_fwd(pg, glr, w2, b_gate, g_out, *, name):
    T = pg.shape[0]
    tm = _rows(T, GLA_TM)
    ncb = tm // CHUNK
    NC = T // CHUNK

    def body(q_ref, k_ref, v_ref, r_ref, l_ref, w2_ref, b_ref, g_ref, o_ref, st_ref, S):
        @pl.when(pl.program_id(0) == 0)
        def _():
            S[...] = jnp.zeros_like(S)
        _, bc, tot = _gla_gate(l_ref[...], w2_ref[...], b_ref[...], tm)
        kend = k_ref[...] * jnp.exp(tot - bc)
        qs = q_ref[...] * (GLA_K ** -0.5)
        a_all = jnp.exp(tot)
        v = v_ref[...]
        r = r_ref[...]
        for c in range(ncb):
            rows = slice(c * CHUNK, (c + 1) * CHUNK)
            outs = []
            for h in range(GLA_H):
                hk = slice(h * GLA_K, (h + 1) * GLA_K)
                hv = slice(h * GLA_V, (h + 1) * GLA_V)
                ut = _dot(v[rows, hv], kend[rows, hk], TN)
                s_new = S[h] * a_all[c * CHUNK:c * CHUNK + 1, hk] + ut
                S[h] = s_new
                st_ref[c, h] = s_new
                o = _dot(qs[rows, hk], s_new, NT)
                rinv = lax.rsqrt(jnp.mean(o * o, axis=-1, keepdims=True) + EPS)
                rr = r[rows, hv]
                outs.append(o * rinv * g_ref[...] * (rr * _sigmoid(rr)))
            o_ref[pl.ds(c * CHUNK, CHUNK), :] = jnp.concatenate(outs, axis=-1).astype(MXU)

    fix = lambda i: (0, 0)
    return _call(body, name=name, grid=(T // tm,),
                 ins=[(pg, (tm, GLA_DK), lambda i: (i, 0)), (pg, (tm, GLA_DK), lambda i: (i, 1)),
                      (pg, (tm, GLA_DV), lambda i: (i, 1)), (pg, (tm, GLA_DV), lambda i: (i, 2)),
                      (glr, (tm, LANES), lambda i: (i, 0)), (w2, (LANES, GLA_DK), fix),
                      (b_gate, (1, GLA_DK), fix), (g_out, (1, GLA_V), fix)],
                 outs=[((T, GLA_DV), MXU, (tm, GLA_DV), lambda i: (i, 0)),
                       ((NC, GLA_H, GLA_V, GLA_K), F32, (ncb, GLA_H, GLA_V, GLA_K), lambda i: (i, 0, 0, 0))],
                 scratch=[pltpu.VMEM((GLA_H, GLA_V, GLA_K), F32)])


def gla_bwd(pg, glr, w2, b_gate, g_out, st, dmix, *, name):
    T = pg.shape[0]
    tm = _rows(T, GLA_TM)
    ncb = tm // CHUNK
    n = T // tm

    def body(q_ref, k_ref, v_ref, r_ref, l_ref, w2_ref, b_ref, g_ref, st_ref, sp_ref, d_ref,
             dpg_ref, dl_ref, dw2_ref, db_ref, dg_ref, dS, dkend, extra):
        i = pl.program_id(0)

        @pl.when(i == 0)
        def _():
            dS[...] = jnp.zeros_like(dS)
            dw2_ref[...] = jnp.zeros_like(dw2_ref)
            db_ref[...] = jnp.zeros_like(db_ref)
            dg_ref[...] = jnp.zeros_like(dg_ref)
        first_tile = i == n - 1
        glr_v = l_ref[...]
        pre, bc, tot = _gla_gate(glr_v, w2_ref[...], b_ref[...], tm)
        dec = jnp.exp(tot - bc)
        k = k_ref[...]
        kend = k * dec
        qs = q_ref[...] * (GLA_K ** -0.5)
        a_all = jnp.exp(tot)
        v = v_ref[...]
        r = r_ref[...]
        g = g_ref[...]
        dgg = jnp.zeros((1, GLA_V), F32)
        for c in reversed(range(ncb)):
            rows = slice(c * CHUNK, (c + 1) * CHUNK)
            dq_l, dv_l, dr_l, dk_l, ex_l = [], [], [], [], []
            for h in range(GLA_H):
                hk = slice(h * GLA_K, (h + 1) * GLA_K)
                hv = slice(h * GLA_V, (h + 1) * GLA_V)
                s_c = st_ref[c, h]
                if c > 0:
                    s_p = st_ref[c - 1, h]
                else:
                    s_p = jnp.where(first_tile, 0.0, sp_ref[0, h])
                qh = qs[rows, hk]
                o = _dot(qh, s_c, NT)
                rinv = lax.rsqrt(jnp.mean(o * o, axis=-1, keepdims=True) + EPS)
                nrm = o * rinv
                rr = r[rows, hv]
                sg = _sigmoid(rr)
                sil = rr * sg
                d = d_ref[pl.ds(c * CHUNK, CHUNK), pl.ds(h * GLA_V, GLA_V)]
                dr_l.append(d * nrm * g * (sg * (1.0 + rr * (1.0 - sg))))
                dgg = dgg + jnp.sum(d * sil * nrm, axis=0, keepdims=True)
                dn = d * sil * g
                do = rinv * (dn - nrm * jnp.mean(dn * nrm, axis=-1, keepdims=True))
                dq_l.append(_dot(do, s_c) * (GLA_K ** -0.5))
                dst = dS[h] + _dot(do, qh, TN)
                a = a_all[c * CHUNK:c * CHUNK + 1, hk]
                da = jnp.sum(dst * s_p, axis=0, keepdims=True)
                dS[h] = dst * a
                dk_l.append(_dot(v[rows, hv], dst))
                dv_l.append(_dot(kend[rows, hk], dst, NT))
                ex_l.append(jnp.broadcast_to(da * a, (CHUNK, GLA_K)))
            cr = pl.ds(c * CHUNK, CHUNK)
            dpg_ref[cr, pl.ds(0, GLA_DK)] = jnp.concatenate(dq_l, axis=-1).astype(MXU)
            dpg_ref[cr, pl.ds(2 * GLA_DK, GLA_DV)] = jnp.concatenate(dv_l, axis=-1).astype(MXU)
            dpg_ref[cr, pl.ds(2 * GLA_DK + GLA_DV, GLA_DV)] = jnp.concatenate(dr_l, axis=-1).astype(MXU)
            dkend[cr, :] = jnp.concatenate(dk_l, axis=-1)
            extra[cr, :] = jnp.concatenate(ex_l, axis=-1)
        dke = dkend[...]
        dpg_ref[:, pl.ds(GLA_DK, GLA_DK)] = (dke * dec).astype(MXU)
        e = dke * kend
        row, col, same = _gla_masks(tm)
        m_lt = jnp.where(same & (col < row), 1.0, 0.0).astype(MXU)
        dla = _dot_split(m_lt, e, a_split=False) + extra[...]
        sp = _softplus_neg_abs(pre)
        one_m_sig = jnp.exp(-jnp.maximum(pre, 0.0) - sp)
        dpre = dla * (1.0 / 16.0) * one_m_sig
        dl_ref[...] = _dot(dpre, w2_ref[...], NT).astype(MXU)
        dw2_ref[...] += _dot(glr_v, dpre, TN)
        db_ref[...] += jnp.sum(dpre, axis=0, keepdims=True)
        dg_ref[...] += dgg

    fix = lambda i: (0, 0)
    rev = lambda i: n - 1 - i
    return _call(body, name=name, grid=(n,),
                 ins=[(pg, (tm, GLA_DK), lambda i: (rev(i), 0)), (pg, (tm, GLA_DK), lambda i: (rev(i), 1)),
                      (pg, (tm, GLA_DV), lambda i: (rev(i), 1)), (pg, (tm, GLA_DV), lambda i: (rev(i), 2)),
                      (glr, (tm, LANES), lambda i: (rev(i), 0)), (w2, (LANES, GLA_DK), fix),
                      (b_gate, (1, GLA_DK), fix), (g_out, (1, GLA_V), fix),
                      (st, (ncb, GLA_H, GLA_V, GLA_K), lambda i: (rev(i), 0, 0, 0)),
                      (st, (1, GLA_H, GLA_V, GLA_K), lambda i: (jnp.maximum(rev(i) * ncb - 1, 0), 0, 0, 0)),
                      (dmix, (tm, GLA_DV), lambda i: (rev(i), 0))],
                 outs=[((T, 2 * GLA_DK + 2 * GLA_DV), MXU, (tm, 2 * GLA_DK + 2 * GLA_DV), lambda i: (rev(i), 0)),
                       ((T, LANES), MXU, (tm, LANES), lambda i: (rev(i), 0)),
                       ((LANES, GLA_DK), F32, (LANES, GLA_DK), fix),
                       ((1, GLA_DK), F32, (1, GLA_DK), fix), ((1, GLA_V), F32, (1, GLA_V), fix)],
                 scratch=[pltpu.VMEM((GLA_H, GLA_V, GLA_K), F32), pltpu.VMEM((tm, GLA_DK), F32),
                          pltpu.VMEM((tm, GLA_DK), F32)])


def local_step(x, tgt, W, late=None):
    row1 = lambda a, l: a[l:l + 1]
    hn0 = rms_fwd(x, row1(W["mix_norm"], 0), name="l0_norm")
    pg = mm([(hn0, W["wi_g"])], name="l0_proj_gla", tn=512)
    ps = mm([(hn0, W["wi_s"])], name="l0_proj_sb", tn=512)
    glr = mm([(hn0, W["wi_l"])], name="l0_proj_gate", out_dtype=MXU)
    og, st = gla_fwd(pg, glr, W["w2"], W["b_gate"], W["g_gla"], name="gla_fwd")
    qn, kn, vh = sb_prep(ps, W["g_q"], W["g_k"], name="sb_prep")
    if late is None:
        osb_h, sb_cb = sb_fwd(qn, kn, vh, name="sb_fwd")
    else:
        osb_h, sb_cb, gathered = sb_fwd(qn, kn, vh, name="sb_fwd", fused=(late["src"], N_CHIPS, GATHER))
        W = {**W, **late["unpack"](gathered)}
    osb = heads_to_cols(osb_h, name="sb_out")
    h1 = mm([(og, W["wo_g"]), (osb, W["wo_s"])], res=x, name="l0_out")
    h2, ffn0 = ffn_fwd(h1, row1(W["ffn_norm"], 0), W["wg0"], W["wu0"], W["wd0"], 0)
    hn1 = rms_fwd(h2, row1(W["mix_norm"], 1), name="l1_norm")
    A = mm([(hn1, W["pw1"])], bias=W["b_pw1"], name="l1_pw1")
    s, cconv = conv_fwd(A, W["w_dw"], W["b_dw"], W["ln_g"], W["ln_b"], name="conv_fwd")
    h3 = mm([(s, W["pw2"])], bias=W["b_pw2"], res=h2, name="l1_pw2")
    y, ffn1 = ffn_fwd(h3, row1(W["ffn_norm"], 1), W["wg1"], W["wu1"], W["wd1"], 1)
    dy, loss_lanes = loss_grad(y, tgt, name="loss")
    G = {}
    dh3, g_fn1, G["wg1"], G["wu1"], G["wd1"], cs3 = ffn_bwd(
        dy, h3, row1(W["ffn_norm"], 1), W["wg1"], W["wu1"], W["wd1"], ffn1, 1)
    G["b_pw2"] = cs3
    ds = mm([(dh3, W["pw2"])], trans_b=True, name="l1_ds")
    G["pw2"] = mm_tn(s, dh3, name="l1_dpw2", tm=1024, tn=1024)
    dc, G["ln_g"], G["ln_b"] = ln_swish_bwd(cconv, ds, W["ln_g"], W["ln_b"], name="ln_bwd")
    dA, G["w_dw"], G["b_dw"], G["b_pw1"] = conv_bwd(dc, A, W["w_dw"], name="conv_bwd")
    G["pw1"] = mm_tn(hn1, dA, name="l1_dpw1", tm=1024, tn=1024)
    dhn1 = mm([(dA, W["pw1"])], trans_b=True, name="l1_dhn")
    dh2, g_mn1, _ = rms_bwd(h2, row1(W["mix_norm"], 1), dhn1, dh3, name="l1_norm_bwd")
    dh1, g_fn0, G["wg0"], G["wu0"], G["wd0"], _ = ffn_bwd(
        dh2, h1, row1(W["ffn_norm"], 0), W["wg0"], W["wu0"], W["wd0"], ffn0, 0)
    dmix = mm([(dh1, W["wo_gs"])], trans_b=True, name="l0_dmix")
    G["wo_g"] = mm_tn(og, dh1, name="l0_dwo_g", tm=512, tn=1024)
    G["wo_s"] = mm_tn(osb, dh1, name="l0_dwo_s", tm=512, tn=1024)
    do_h = cols_to_heads(dmix, 1, name="sb_dout")
    slots = None
    if late is None:
        dqn, dkn, dvh = sb_bwd(qn, kn, vh, sb_cb, do_h, name="sb_bwd")
    else:
        dqn, dkn, dvh, slots = sb_bwd(qn, kn, vh, sb_cb, do_h, name="sb_bwd", fused=(late["pack"](G), N_DEV, SCATTER))
    dps, G["g_q"], G["g_k"] = sb_post_bwd(dqn, dkn, dvh, ps, W["g_q"], W["g_k"], name="sb_post_bwd")
    dpg, dglr, G["w2"], G["b_gate"], G["g_gla"] = gla_bwd(
        pg, glr, W["w2"], W["b_gate"], W["g_gla"], st, dmix, name="gla_bwd")
    G["wi_g"] = mm_tn(hn0, dpg, name="l0_dwi_g", tm=1024, tn=512)
    G["wi_s"] = mm_tn(hn0, dps, name="l0_dwi_s", tm=1024, tn=512)
    G["wi_l"] = mm_tn(hn0, dglr, name="l0_dwi_l", tm=1024, tn=LANES)
    dhn0 = mm([(dpg, W["wi_g"]), (dps, W["wi_s"]), (dglr, W["wi_l"])], trans_b=True, name="l0_dhn")
    dx, g_mn0, _ = rms_bwd(x, row1(W["mix_norm"], 0), dhn0, dh1, name="l0_norm_bwd")
    G["mix_norm"] = jnp.concatenate([g_mn0, g_mn1], axis=0)
    G["ffn_norm"] = jnp.concatenate([g_fn0, g_fn1], axis=0)
    return loss_lanes, dx, G, slots


_C_GLA = 2 * GLA_DK + 2 * GLA_DV
_C_SB0 = _C_GLA + GLA_RANK


def layout_first(full):
    w_in = full["hy_w_in"][0]
    W = {
        "wi_g": w_in[:, :_C_GLA].astype(MXU),
        "wi_s": w_in[:, _C_SB0:].astype(MXU),
        "wi_l": jnp.pad(w_in[:, _C_GLA:_C_SB0], ((0, 0), (0, LANES - GLA_RANK))).astype(MXU),
        "w2": jnp.pad(full["hy_w_gate2"][0], ((0, LANES - GLA_RANK), (0, 0))),
        "b_gate": full["hy_b_gate"], "g_gla": full["hy_gla_norm"],
        "g_q": full["hy_sb_q_norm"], "g_k": full["hy_sb_k_norm"],
        "wo_gs": full["hy_w_out"][0].astype(MXU),
        "b_pw1": full["cv_b_pw1"],
        "w_dw": jnp.pad(full["cv_w_dw"][0], ((0, HALO - CONV_W), (0, 0))),
        "b_dw": full["cv_b_dw"], "ln_g": full["cv_ln_g"], "ln_b": full["cv_ln_b"],
        "b_pw2": full["cv_b_pw2"],
        "mix_norm": full["mix_norm"], "ffn_norm": full["ffn_norm"],
    }
    W["wo_g"] = W["wo_gs"][:GLA_DV]
    W["wo_s"] = W["wo_gs"][GLA_DV:]
    return W


def layout_late(full):
    W = {"pw1": full["cv_w_pw1"][0].astype(MXU), "pw2": full["cv_w_pw2"][0].astype(MXU)}
    for l in range(2):
        W[f"wg{l}"] = full["ffn_w_gate"][l].astype(MXU)
        W[f"wu{l}"] = full["ffn_w_up"][l].astype(MXU)
        W[f"wd{l}"] = full["ffn_w_down"][l].astype(MXU)
    return W


def reference_first(G):
    return {
        "mix_norm": G["mix_norm"], "ffn_norm": G["ffn_norm"],
        "hy_w_in": jnp.concatenate([G["wi_g"], G["wi_l"][:, :GLA_RANK], G["wi_s"]], axis=1)[None],
        "hy_w_gate2": G["w2"][:GLA_RANK][None],
        "hy_b_gate": G["b_gate"], "hy_gla_norm": G["g_gla"],
        "hy_sb_q_norm": G["g_q"], "hy_sb_k_norm": G["g_k"],
        "hy_w_out": jnp.concatenate([G["wo_g"], G["wo_s"]], axis=0)[None],
        "cv_b_pw1": G["b_pw1"],
        "cv_w_dw": G["w_dw"][:CONV_W][None], "cv_b_dw": G["b_dw"],
        "cv_ln_g": G["ln_g"], "cv_ln_b": G["ln_b"], "cv_b_pw2": G["b_pw2"],
    }


def reference_late(G):
    return {
        "cv_w_pw1": G["pw1"][None], "cv_w_pw2": G["pw2"][None],
        "ffn_w_gate": jnp.stack([G["wg0"], G["wg1"]]),
        "ffn_w_up": jnp.stack([G["wu0"], G["wu1"]]),
        "ffn_w_down": jnp.stack([G["wd0"], G["wd1"]]),
    }


WEIGHTS = ["mix_norm", "ffn_norm", "hy_w_in", "hy_w_gate2", "hy_b_gate", "hy_gla_norm", "hy_sb_q_norm",
           "hy_sb_k_norm", "hy_w_out", "cv_w_pw1", "cv_b_pw1", "cv_w_dw", "cv_b_dw", "cv_ln_g", "cv_ln_b",
           "cv_w_pw2", "cv_b_pw2", "ffn_w_gate", "ffn_w_up", "ffn_w_down"]
SHARD_AXIS = {"hy_w_in": 2, "hy_w_out": 1, "cv_w_pw1": 2, "cv_w_pw2": 1, "ffn_w_gate": 2, "ffn_w_up": 2,
              "ffn_w_down": 1, "hy_w_gate2": 2, "cv_b_pw1": 1, "cv_w_dw": 2, "cv_b_dw": 1, "cv_ln_g": 1,
              "cv_ln_b": 1, "cv_b_pw2": 1}
FIRST_BIG = ["hy_w_in", "hy_w_out"]
SMALL = ["hy_w_gate2", "cv_b_pw1", "cv_w_dw", "cv_b_dw", "cv_ln_g", "cv_ln_b", "cv_b_pw2"]
LATE = ["cv_w_pw1", "cv_w_pw2", "ffn_w_gate", "ffn_w_up", "ffn_w_down"]
REPL = ["mix_norm", "ffn_norm", "hy_b_gate", "hy_gla_norm", "hy_sb_q_norm", "hy_sb_k_norm"]
N_CHIPS = 4
N_DEV = 8
GRAD_ROWS = 1024


def _pack(arrs, rows_multiple, dtype):
    flat = jnp.concatenate([a.reshape(-1).astype(dtype) for a in arrs])
    n = flat.shape[0]
    rows = -(-n // (LANES * rows_multiple)) * rows_multiple
    return jnp.pad(flat, (0, rows * LANES - n)).reshape(rows, LANES)


def _unpack(flat2d, shapes):
    flat = flat2d.reshape(-1)
    out, off = [], 0
    for s in shapes:
        n = math.prod(s)
        out.append(flat[off:off + n].reshape(s))
        off += n
    return out


def _coords():
    return lax.axis_index("x"), lax.axis_index("y"), lax.axis_index("c")


def _flip(pos, f):
    return tuple(1 - p if b else p for p, b in zip(pos, f))


def _exchange_copies(src_ref, dst_ref, send_sems, recv_sems, loc_sem, *, flips, src_idx, dst_idx, local_idx,
                     with_recvs=True):
    me = _coords()
    loc = None
    if local_idx is not None:
        si, di = local_idx(me)
        loc = pltpu.make_async_copy(src_ref.at[si], dst_ref.at[di], loc_sem)
    sends, recvs = [], []
    for k, f in enumerate(flips):
        peer = _flip(me, f)
        sends.append(pltpu.make_async_remote_copy(
            src_ref=src_ref.at[src_idx(me, peer)], dst_ref=dst_ref.at[dst_idx(me)],
            send_sem=send_sems.at[k], recv_sem=recv_sems.at[k],
            device_id=peer, device_id_type=pl.DeviceIdType.MESH))
        if with_recvs:
            recvs.append(pltpu.make_async_remote_copy(
                src_ref=src_ref.at[src_idx(peer, me)], dst_ref=dst_ref.at[dst_idx(peer)],
                send_sem=send_sems.at[k], recv_sem=recv_sems.at[k],
                device_id=peer, device_id_type=pl.DeviceIdType.MESH))
    return loc, sends, recvs


def _exchange_start(*refs, **spec):
    loc, sends, _ = _exchange_copies(*refs, with_recvs=False, **spec)
    if loc is not None:
        loc.start()
    for s in sends:
        s.start()


def _exchange_wait(*refs, **spec):
    loc, sends, recvs = _exchange_copies(*refs, **spec)
    for s in sends:
        s.wait_send()
    for r in recvs:
        r.wait_recv()
    if loc is not None:
        loc.wait()


def exchange(src, n_dst, spec, *, name):
    n = len(spec["flips"])

    def body(*refs):
        _exchange_start(*refs, **spec)
        _exchange_wait(*refs, **spec)

    return pl.pallas_call(
        body, name=name,
        out_shape=jax.ShapeDtypeStruct((n_dst,) + src.shape[1:], src.dtype),
        in_specs=[pl.BlockSpec(memory_space=pl.ANY)],
        out_specs=pl.BlockSpec(memory_space=pl.ANY),
        scratch_shapes=[pltpu.SemaphoreType.DMA((n,)), pltpu.SemaphoreType.DMA((n,)), pltpu.SemaphoreType.DMA(())],
    )(src)


CHIP_FLIPS = [(1, 0, 0), (0, 1, 0), (1, 1, 0)]
SIBLING = [(0, 0, 1)]
ALL_FLIPS = [(a, b, c) for a in (0, 1) for b in (0, 1) for c in (0, 1) if (a, b, c) != (0, 0, 0)]


def _chip(pos):
    return 2 * pos[0] + pos[1]


def _dev(pos):
    return 4 * pos[0] + 2 * pos[1] + pos[2]


GATHER = dict(flips=CHIP_FLIPS, src_idx=lambda me, peer: 0, dst_idx=_chip, local_idx=lambda me: (0, _chip(me)))
SCATTER = dict(flips=ALL_FLIPS, src_idx=lambda me, peer: 4 * peer[2] + _chip(peer), dst_idx=_dev,
               local_idx=lambda me: (4 * me[2] + _chip(me), _dev(me)))
SHARE = dict(flips=SIBLING, src_idx=lambda me, peer: 0, dst_idx=lambda me: me[2], local_idx=lambda me: (0, me[2]))
ALL_TO_ALL = dict(flips=ALL_FLIPS, src_idx=lambda me, peer: 0, dst_idx=_dev, local_idx=lambda me: (0, _dev(me)))


def sum_slots(x, *, name, tr=512):
    n, R, L = x.shape
    tr = _rows(R, tr)

    def body(x_ref, o_ref):
        acc = x_ref[0]
        for k in range(1, n):
            acc = acc + x_ref[k]
        o_ref[...] = acc

    return _call(body, name=name, grid=(R // tr,),
                 ins=[(x, (n, tr, L), lambda i: (0, i, 0))],
                 outs=[((R, L), F32, (tr, L), lambda i: (i, 0))])[0]


def adamw(w, g, m, v, *, name, tr=512):
    R, L = w.shape
    tr = _rows(R, tr)

    def body(w_ref, g_ref, m_ref, v_ref, d_ref, mo_ref, vo_ref):
        g = g_ref[...]
        m = ADAM_B1 * m_ref[...] + (1.0 - ADAM_B1) * g
        v = ADAM_B2 * v_ref[...] + (1.0 - ADAM_B2) * (g * g)
        m_hat = m / (1.0 - ADAM_B1 ** ADAM_STEP)
        v_hat = v / (1.0 - ADAM_B2 ** ADAM_STEP)
        d_ref[...] = -ADAM_LR * (m_hat / (jnp.sqrt(v_hat) + ADAM_EPS) + ADAM_WD * w_ref[...])
        mo_ref[...] = m
        vo_ref[...] = v

    row = lambda i: (i, 0)
    return _call(body, name=name, grid=(R // tr,),
                 ins=[(a, (tr, L), row) for a in (w, g, m, v)],
                 outs=[((R, L), F32, (tr, L), row)] * 3)


def kernel(x, mix_norm, ffn_norm, hy_w_in, hy_w_gate2, hy_b_gate, hy_gla_norm, hy_sb_q_norm, hy_sb_k_norm, hy_w_out, cv_w_pw1, cv_b_pw1, cv_w_dw, cv_b_dw, cv_ln_g, cv_ln_b, cv_w_pw2, cv_b_pw2, ffn_w_gate, ffn_w_up, ffn_w_down, loss_target, m_mix_norm, m_ffn_norm, m_hy_w_in, m_hy_w_gate2, m_hy_b_gate, m_hy_gla_norm, m_hy_sb_q_norm, m_hy_sb_k_norm, m_hy_w_out, m_cv_w_pw1, m_cv_b_pw1, m_cv_w_dw, m_cv_b_dw, m_cv_ln_g, m_cv_ln_b, m_cv_w_pw2, m_cv_b_pw2, m_ffn_w_gate, m_ffn_w_up, m_ffn_w_down, v_mix_norm, v_ffn_norm, v_hy_w_in, v_hy_w_gate2, v_hy_b_gate, v_hy_gla_norm, v_hy_sb_q_norm, v_hy_sb_k_norm, v_hy_w_out, v_cv_w_pw1, v_cv_b_pw1, v_cv_w_dw, v_cv_b_dw, v_cv_ln_g, v_cv_ln_b, v_cv_w_pw2, v_cv_b_pw2, v_ffn_w_gate, v_ffn_w_up, v_ffn_w_down):
    w = dict(zip(WEIGHTS, (mix_norm, ffn_norm, hy_w_in, hy_w_gate2, hy_b_gate, hy_gla_norm, hy_sb_q_norm, hy_sb_k_norm, hy_w_out, cv_w_pw1, cv_b_pw1, cv_w_dw, cv_b_dw, cv_ln_g, cv_ln_b, cv_w_pw2, cv_b_pw2, ffn_w_gate, ffn_w_up, ffn_w_down)))
    m = dict(zip(WEIGHTS, (m_mix_norm, m_ffn_norm, m_hy_w_in, m_hy_w_gate2, m_hy_b_gate, m_hy_gla_norm, m_hy_sb_q_norm, m_hy_sb_k_norm, m_hy_w_out, m_cv_w_pw1, m_cv_b_pw1, m_cv_w_dw, m_cv_b_dw, m_cv_ln_g, m_cv_ln_b, m_cv_w_pw2, m_cv_b_pw2, m_ffn_w_gate, m_ffn_w_up, m_ffn_w_down)))
    v = dict(zip(WEIGHTS, (v_mix_norm, v_ffn_norm, v_hy_w_in, v_hy_w_gate2, v_hy_b_gate, v_hy_gla_norm, v_hy_sb_q_norm, v_hy_sb_k_norm, v_hy_w_out, v_cv_w_pw1, v_cv_b_pw1, v_cv_w_dw, v_cv_b_dw, v_cv_ln_g, v_cv_ln_b, v_cv_w_pw2, v_cv_b_pw2, v_ffn_w_gate, v_ffn_w_up, v_ffn_w_down)))
    shard_shapes = {n: w[n].shape for n in SHARD_AXIS}

    def unpack_gathered(buf, names):
        parts = [_unpack(buf[k], [shard_shapes[n] for n in names]) for k in range(N_CHIPS)]
        return {n: jnp.concatenate([parts[k][j] for k in range(N_CHIPS)], axis=SHARD_AXIS[n])
                for j, n in enumerate(names)}

    def pack_grads(Gr, names):
        pieces = [jnp.split(Gr[n], N_CHIPS, axis=SHARD_AXIS[n]) for n in names]
        per_chip = [_pack([p[k] for p in pieces], GRAD_ROWS, F32) for k in range(N_CHIPS)]
        Rh = per_chip[0].shape[0] // 2
        return jnp.stack([pc.reshape(2, Rh, LANES) for pc in per_chip], axis=1).reshape(2 * N_CHIPS, Rh, LANES)

    def finish_reduce(slots, tag):
        half = sum_slots(slots, name=f"reduce_{tag}_sum")
        return exchange(half[None], 2, SHARE, name=f"reduce_{tag}_share").reshape(2 * half.shape[0], LANES)

    first = exchange(_pack([w[n] for n in FIRST_BIG], 32, MXU)[None], N_CHIPS, GATHER, name="gather_first")
    small = exchange(_pack([w[n] for n in SMALL], 8, F32)[None], N_CHIPS, GATHER, name="gather_small")
    full = {**{n: w[n] for n in REPL}, **unpack_gathered(first, FIRST_BIG), **unpack_gathered(small, SMALL)}
    late = {"src": _pack([w[n] for n in LATE], 32, MXU)[None],
            "unpack": lambda buf: layout_late(unpack_gathered(buf, LATE)),
            "pack": lambda G: pack_grads(reference_late(G), LATE)}

    loss_lanes, dx, Gk, late_slots = local_step(x[0], loss_target[0], layout_first(full), late)
    Gr = reference_first(Gk)

    first_names = FIRST_BIG + SMALL
    first_slots = exchange(pack_grads(Gr, first_names), N_DEV, SCATTER, name="reduce_first")
    groups = [(first_names, finish_reduce(first_slots, "first"), "first"), (LATE, finish_reduce(late_slots, "late"), "late")]

    loss_row = jnp.pad(jnp.sum(loss_lanes).reshape(1), (0, LANES - 1))
    rep = _pack([Gr[n] for n in REPL] + [loss_row], 8, F32)
    rep_all = exchange(rep[None], N_DEV, ALL_TO_ALL, name="reduce_small")
    rep_sum = sum_slots(rep_all, name="reduce_small_sum", tr=rep.shape[0])
    rep_shapes = [w[n].shape for n in REPL]
    rep_n = sum(math.prod(s) for s in rep_shapes)
    loss = rep_sum.reshape(-1)[rep_n]

    out = {"grad": {}, "delta": {}, "new_m": {}, "new_v": {}}
    for names, g_flat, tag in groups:
        w_flat, m_flat, v_flat = (_pack([t[n] for n in names], GRAD_ROWS, F32) for t in (w, m, v))
        d_flat, mo_flat, vo_flat = adamw(w_flat, g_flat, m_flat, v_flat, name=f"adamw_{tag}")
        shapes = [shard_shapes[n] for n in names]
        for key, buf in (("grad", g_flat), ("delta", d_flat), ("new_m", mo_flat), ("new_v", vo_flat)):
            out[key].update(dict(zip(names, _unpack(buf, shapes))))
    rw = _pack([w[n] for n in REPL], 8, F32)
    rm = _pack([m[n] for n in REPL], 8, F32)
    rv = _pack([v[n] for n in REPL], 8, F32)
    rg = _pack(_unpack(rep_sum, rep_shapes), 8, F32)
    rd, rmo, rvo = adamw(rw, rg, rm, rv, name="adamw_small", tr=rw.shape[0])
    for key, buf in (("grad", rg), ("delta", rd), ("new_m", rmo), ("new_v", rvo)):
        out[key].update(dict(zip(REPL, _unpack(buf, rep_shapes))))

    return (loss, dx[None], *[out["grad"][n] for n in WEIGHTS], *[out["delta"][n] for n in WEIGHTS],
            *[out["new_m"][n] for n in WEIGHTS], *[out["new_v"][n] for n in WEIGHTS])
```

```python
import functools
import math

import numpy as np
import jax
import jax.numpy as jnp
from jax import lax
from jax.experimental import pallas as pl
from jax.experimental.pallas import tpu as pltpu

F32 = jnp.float32
MXU = jnp.bfloat16
EPS = 1e-6
LANES = 128
VMEM_LIMIT = 56 * 1024 * 1024

D = 1024
F = 2816
CHUNK = 64
GLA_H, GLA_K, GLA_V, GLA_RANK = 4, 64, 128, 16
SB_H, SB_D = 8, 64
CONV_W = 31
HALO = 32

ADAM_LR, ADAM_B1, ADAM_B2, ADAM_EPS, ADAM_WD, ADAM_STEP = 0.001, 0.9, 0.999, 1e-08, 0.01, 10

NN = (((1,), (0,)), ((), ()))
NT = (((1,), (1,)), ((), ()))
TN = (((0,), (0,)), ((), ()))


def _dot(a, b, dims=NN):
    return lax.dot_general(a.astype(MXU), b.astype(MXU), dims, preferred_element_type=F32)


def _dot_split(a, b, dims=NN, a_split=True):
    x = a if a_split else b
    hi = x.astype(MXU)
    lo = (x - hi.astype(F32)).astype(MXU)
    if a_split:
        return _dot(hi, b, dims) + _dot(lo, b, dims)
    return _dot(a, hi, dims) + _dot(a, lo, dims)


def _sigmoid(x):
    return 1.0 / (1.0 + jnp.exp(-x))


def _softplus_neg_abs(z):
    return jnp.log(1.0 + jnp.exp(-jnp.abs(z)))


def _call(body, *, name, grid, ins, outs, scratch=()):
    spec = lambda b, m: pl.BlockSpec(memory_space=pl.ANY) if b is None else pl.BlockSpec(b, m)
    res = pl.pallas_call(
        body,
        name=name,
        grid=grid,
        in_specs=[spec(b, m) for _, b, m in ins],
        out_specs=[spec(b, m) for _, _, b, m in outs],
        out_shape=[jax.ShapeDtypeStruct(s, d) for s, d, _, _ in outs],
        scratch_shapes=list(scratch),
        compiler_params=pltpu.CompilerParams(
            dimension_semantics=("arbitrary",) * len(grid), vmem_limit_bytes=VMEM_LIMIT),
    )(*[a for a, _, _ in ins])
    return res


def _rows(T, tm):
    tm = min(tm, T)
    assert T % tm == 0, (T, tm)
    return tm


def mm(pairs, *, name, trans_b=False, bias=None, res=None, out_dtype=F32, tm=512, tn=512):
    M = pairs[0][0].shape[0]
    N = pairs[0][1].shape[0] if trans_b else pairs[0][1].shape[1]
    tm = _rows(M, tm)
    tn = min(tn, N)
    assert N % tn == 0, (N, tn)
    np_ = len(pairs)

    def body(*refs):
        o_ref = refs[-1]
        acc = None
        for p in range(np_):
            d = _dot(refs[2 * p][...], refs[2 * p + 1][...], NT if trans_b else NN)
            acc = d if acc is None else acc + d
        k = 2 * np_
        if bias is not None:
            acc = acc + refs[k][...]
            k += 1
        if res is not None:
            acc = acc + refs[k][...]
        o_ref[...] = acc.astype(out_dtype)

    ins = []
    for a, b in pairs:
        K = a.shape[1]
        ins.append((a, (tm, K), lambda i, j: (i, 0)))
        if trans_b:
            ins.append((b, (tn, K), lambda i, j: (j, 0)))
        else:
            ins.append((b, (K, tn), lambda i, j: (0, j)))
    if bias is not None:
        ins.append((bias, (1, tn), lambda i, j: (0, j)))
    if res is not None:
        ins.append((res, (tm, tn), lambda i, j: (i, j)))
    return _call(body, name=name, grid=(M // tm, N // tn), ins=ins,
                 outs=[((M, N), out_dtype, (tm, tn), lambda i, j: (i, j))])[0]


def mm_tn(a, b, *, name, tm=512, tn=512, tk=512):
    T, M = a.shape
    N = b.shape[1]
    tm, tn, tk = min(tm, M), min(tn, N), min(tk, T)
    assert M % tm == 0 and N % tn == 0 and T % tk == 0, (M, N, T, tm, tn, tk)

    def body(a_ref, b_ref, o_ref):
        @pl.when(pl.program_id(2) == 0)
        def _():
            o_ref[...] = jnp.zeros_like(o_ref)
        o_ref[...] += _dot(a_ref[...], b_ref[...], TN)

    return _call(body, name=name, grid=(M // tm, N // tn, T // tk),
                 ins=[(a, (tk, tm), lambda i, j, k: (k, i)), (b, (tk, tn), lambda i, j, k: (k, j))],
                 outs=[((M, N), F32, (tm, tn), lambda i, j, k: (i, j))])[0]


def rms_fwd(x, g, *, name, tm=256):
    T = x.shape[0]
    tm = _rows(T, tm)

    def body(x_ref, g_ref, o_ref):
        x = x_ref[...]
        r = lax.rsqrt(jnp.mean(x * x, axis=-1, keepdims=True) + EPS)
        o_ref[...] = (x * r * g_ref[...]).astype(MXU)

    return _call(body, name=name, grid=(T // tm,),
                 ins=[(x, (tm, D), lambda i: (i, 0)), (g, (1, D), lambda i: (0, 0))],
                 outs=[((T, D), MXU, (tm, D), lambda i: (i, 0))])[0]


def rms_bwd(x, g, dhn, dres, *, name, tm=256):
    T = x.shape[0]
    tm = _rows(T, tm)

    def body(x_ref, g_ref, d_ref, r_ref, dx_ref, dg_ref, cs_ref):
        @pl.when(pl.program_id(0) == 0)
        def _():
            dg_ref[...] = jnp.zeros_like(dg_ref)
            cs_ref[...] = jnp.zeros_like(cs_ref)
        x = x_ref[...]
        d = d_ref[...]
        r = lax.rsqrt(jnp.mean(x * x, axis=-1, keepdims=True) + EPS)
        dg_ref[...] += jnp.sum(d * x * r, axis=0, keepdims=True)
        t = d * g_ref[...]
        m = jnp.mean(t * x, axis=-1, keepdims=True)
        dx = r_ref[...] + t * r - x * (r * r * r) * m
        dx_ref[...] = dx
        cs_ref[...] += jnp.sum(dx, axis=0, keepdims=True)

    row = lambda i: (i, 0)
    fix = lambda i: (0, 0)
    return _call(body, name=name, grid=(T // tm,),
                 ins=[(x, (tm, D), row), (g, (1, D), fix), (dhn, (tm, D), row), (dres, (tm, D), row)],
                 outs=[((T, D), F32, (tm, D), row), ((1, D), F32, (1, D), fix), ((1, D), F32, (1, D), fix)])


def loss_grad(y, t, *, name, tm=256):
    T = y.shape[0]
    tm = _rows(T, tm)

    def body(y_ref, t_ref, dy_ref, l_ref):
        @pl.when(pl.program_id(0) == 0)
        def _():
            l_ref[...] = jnp.zeros_like(l_ref)
        e = y_ref[...] - t_ref[...]
        dy_ref[...] = e * (1.0 / D)
        l_ref[...] += jnp.sum(e * e, axis=0, keepdims=True) * (0.5 / D)

    row = lambda i: (i, 0)
    return _call(body, name=name, grid=(T // tm,),
                 ins=[(y, (tm, D), row), (t, (tm, D), row)],
                 outs=[((T, D), F32, (tm, D), row), ((1, D), F32, (1, D), lambda i: (0, 0))])


def ffn_up(hn, wg, wu, *, name, tm=1024, tn=256):
    T = hn.shape[0]
    tm = _rows(T, tm)

    def body(h_ref, wg_ref, wu_ref, g_ref, u_ref, a_ref):
        h = h_ref[...]
        g = _dot(h, wg_ref[...])
        u = _dot(h, wu_ref[...])
        g_ref[...] = g.astype(MXU)
        u_ref[...] = u.astype(MXU)
        a_ref[...] = (g * _sigmoid(g) * u).astype(MXU)

    tile = lambda i, j: (i, j)
    return _call(body, name=name, grid=(T // tm, F // tn),
                 ins=[(hn, (tm, D), lambda i, j: (i, 0)), (wg, (D, tn), lambda i, j: (0, j)),
                      (wu, (D, tn), lambda i, j: (0, j))],
                 outs=[((T, F), MXU, (tm, tn), tile), ((T, F), MXU, (tm, tn), tile),
                       ((T, F), MXU, (tm, tn), tile)])


def ffn_bwd_act(dy, wd, G, U, *, name, tm=1024, tn=256):
    T = dy.shape[0]
    tm = _rows(T, tm)

    def body(dy_ref, wd_ref, g_ref, u_ref, dg_ref, du_ref, a_ref):
        da = _dot(dy_ref[...], wd_ref[...], NT)
        g = g_ref[...].astype(F32)
        u = u_ref[...].astype(F32)
        s = _sigmoid(g)
        sil = g * s
        a_ref[...] = (sil * u).astype(MXU)
        du_ref[...] = (da * sil).astype(MXU)
        dg_ref[...] = (da * u * (s * (1.0 + g * (1.0 - s)))).astype(MXU)

    tile = lambda i, j: (i, j)
    return _call(body, name=name, grid=(T // tm, F // tn),
                 ins=[(dy, (tm, D), lambda i, j: (i, 0)), (wd, (tn, D), lambda i, j: (j, 0)),
                      (G, (tm, tn), tile), (U, (tm, tn), tile)],
                 outs=[((T, F), MXU, (tm, tn), tile), ((T, F), MXU, (tm, tn), tile),
                       ((T, F), MXU, (tm, tn), tile)])


def ffn_fwd(h, g_norm, wg, wu, wd, tag):
    hn = rms_fwd(h, g_norm, name=f"ffn{tag}_norm")
    G, U, act = ffn_up(hn, wg, wu, name=f"ffn{tag}_up")
    h_out = mm([(act, wd)], res=h, name=f"ffn{tag}_down")
    return h_out, (hn, G, U)


def ffn_bwd(dy, h, g_norm, wg, wu, wd, saved, tag):
    hn, G, U = saved
    dG, dU, act = ffn_bwd_act(dy, wd, G, U, name=f"ffn{tag}_bwd_act")
    d_wd = mm_tn(act, dy, name=f"ffn{tag}_dwd", tm=1408, tn=512)
    d_wg = mm_tn(hn, dG, name=f"ffn{tag}_dwg", tm=512, tn=1408)
    d_wu = mm_tn(hn, dU, name=f"ffn{tag}_dwu", tm=512, tn=1408)
    dhn = mm([(dG, wg), (dU, wu)], trans_b=True, name=f"ffn{tag}_dhn")
    dh, d_g, cs = rms_bwd(h, g_norm, dhn, dy, name=f"ffn{tag}_norm_bwd")
    return dh, d_g, d_wg, d_wu, d_wd, cs


SUBLANES = 8


def _window_scratch(tm):
    return [pltpu.VMEM((tm + HALO + SUBLANES, D), F32), pltpu.VMEM((SUBLANES, tm + HALO, D), F32)]


def _shift_copies(win, sh):
    rows = sh.shape[1]
    win[pl.ds(rows, SUBLANES), :] = jnp.zeros((SUBLANES, D), F32)
    for s in range(SUBLANES):
        sh[s] = win[pl.ds(s, rows), :]


def _tap(sh, off, rb):
    s = off % SUBLANES
    return sh[s, pl.ds(off - s, rb), :]


def conv_fwd(A, w_dw, b_dw, ln_g, ln_b, *, name, tm=128, rb=32):
    T = A.shape[0]
    tm = _rows(T, tm)

    def body(a_ref, ap_ref, w_ref, b_ref, g_ref, bb_ref, s_ref, c_ref, win, sh):
        i = pl.program_id(0)
        a = a_ref[...]
        win[pl.ds(HALO, tm), :] = a[:, :D] * _sigmoid(a[:, D:])
        ap = ap_ref[pl.ds(tm - HALO, HALO), :]
        up = ap[:, :D] * _sigmoid(ap[:, D:])
        win[pl.ds(0, HALO), :] = jnp.where(i > 0, up, 0.0)
        _shift_copies(win, sh)
        for r0 in range(0, tm, rb):
            acc = jnp.broadcast_to(b_ref[...], (rb, D))
            for k in range(CONV_W):
                acc = acc + w_ref[pl.ds(k, 1), :] * _tap(sh, r0 + k + HALO - (CONV_W - 1), rb)
            c_ref[pl.ds(r0, rb), :] = acc
        c = c_ref[...]
        mu = jnp.mean(c, axis=-1, keepdims=True)
        cc = c - mu
        var = jnp.mean(cc * cc, axis=-1, keepdims=True)
        z = cc * lax.rsqrt(var + EPS) * g_ref[...] + bb_ref[...]
        s_ref[...] = (z * _sigmoid(z)).astype(MXU)

    row = lambda i: (i, 0)
    fix = lambda i: (0, 0)
    return _call(body, name=name, grid=(T // tm,),
                 ins=[(A, (tm, 2 * D), row), (A, (tm, 2 * D), lambda i: (jnp.maximum(i - 1, 0), 0)),
                      (w_dw, (HALO, D), fix), (b_dw, (1, D), fix), (ln_g, (1, D), fix), (ln_b, (1, D), fix)],
                 outs=[((T, D), MXU, (tm, D), row), ((T, D), F32, (tm, D), row)],
                 scratch=_window_scratch(tm))


def ln_swish_bwd(c, ds, ln_g, ln_b, *, name, tm=256):
    T = c.shape[0]
    tm = _rows(T, tm)

    def body(c_ref, ds_ref, g_ref, b_ref, dc_ref, dg_ref, db_ref):
        @pl.when(pl.program_id(0) == 0)
        def _():
            dg_ref[...] = jnp.zeros_like(dg_ref)
            db_ref[...] = jnp.zeros_like(db_ref)
        c = c_ref[...]
        mu = jnp.mean(c, axis=-1, keepdims=True)
        cc = c - mu
        rstd = lax.rsqrt(jnp.mean(cc * cc, axis=-1, keepdims=True) + EPS)
        n = cc * rstd
        z = n * g_ref[...] + b_ref[...]
        s = _sigmoid(z)
        dz = ds_ref[...] * (s * (1.0 + z * (1.0 - s)))
        dg_ref[...] += jnp.sum(dz * n, axis=0, keepdims=True)
        db_ref[...] += jnp.sum(dz, axis=0, keepdims=True)
        dn = dz * g_ref[...]
        dc_ref[...] = rstd * (dn - jnp.mean(dn, axis=-1, keepdims=True)
                              - n * jnp.mean(dn * n, axis=-1, keepdims=True))

    row = lambda i: (i, 0)
    fix = lambda i: (0, 0)
    return _call(body, name=name, grid=(T // tm,),
                 ins=[(c, (tm, D), row), (ds, (tm, D), row), (ln_g, (1, D), fix), (ln_b, (1, D), fix)],
                 outs=[((T, D), F32, (tm, D), row), ((1, D), F32, (1, D), fix), ((1, D), F32, (1, D), fix)])


def conv_bwd(dc, A, w_dw, *, name, tm=128, rb=32):
    T = A.shape[0]
    tm = _rows(T, tm)
    n = T // tm
    SUB = 8

    def body(dc_ref, dn_ref, a_ref, ap_ref, w_ref, da_ref, dw_ref, dbd_ref, dbp_ref, wdc, sdc, wu, su, accw):
        i = pl.program_id(0)

        @pl.when(i == 0)
        def _():
            accw[...] = jnp.zeros_like(accw)
            dbd_ref[...] = jnp.zeros_like(dbd_ref)
            dbp_ref[...] = jnp.zeros_like(dbp_ref)
        a = a_ref[...]
        a1, a2 = a[:, :D], a[:, D:]
        sg = _sigmoid(a2)
        wu[pl.ds(HALO, tm), :] = a1 * sg
        ap = ap_ref[pl.ds(tm - HALO, HALO), :]
        wu[pl.ds(0, HALO), :] = jnp.where(i > 0, ap[:, :D] * _sigmoid(ap[:, D:]), 0.0)
        dc = dc_ref[...]
        wdc[pl.ds(0, tm), :] = dc
        wdc[pl.ds(tm, HALO), :] = jnp.where(i < n - 1, dn_ref[pl.ds(0, HALO), :], 0.0)
        dbd_ref[...] += jnp.sum(dc, axis=0, keepdims=True)
        _shift_copies(wdc, sdc)
        _shift_copies(wu, su)
        for r0 in range(0, tm, rb):
            du = jnp.zeros((rb, D), F32)
            dcs = wdc[pl.ds(r0, rb), :]
            for k in range(CONV_W):
                du = du + w_ref[pl.ds(k, 1), :] * _tap(sdc, r0 + (CONV_W - 1) - k, rb)
                p = dcs * _tap(su, r0 + k + HALO - (CONV_W - 1), rb)
                accw[pl.ds(SUB * k, SUB), :] += jnp.sum(p.reshape(rb // SUB, SUB, D), axis=0)
            s = sg[r0:r0 + rb]
            da_ref[pl.ds(r0, rb), pl.ds(0, D)] = (du * s).astype(MXU)
            da_ref[pl.ds(r0, rb), pl.ds(D, D)] = (du * a1[r0:r0 + rb] * s * (1.0 - s)).astype(MXU)
        da = da_ref[...].astype(F32)
        dbp_ref[...] += jnp.sum(da, axis=0, keepdims=True)

        @pl.when(i == n - 1)
        def _():
            dw_ref[...] = jnp.zeros_like(dw_ref)
            for k in range(CONV_W):
                dw_ref[pl.ds(k, 1), :] = jnp.sum(accw[pl.ds(SUB * k, SUB), :], axis=0, keepdims=True)

    row = lambda i: (i, 0)
    fix = lambda i: (0, 0)
    return _call(body, name=name, grid=(n,),
                 ins=[(dc, (tm, D), row), (dc, (tm, D), lambda i: (jnp.minimum(i + 1, n - 1), 0)),
                      (A, (tm, 2 * D), row), (A, (tm, 2 * D), lambda i: (jnp.maximum(i - 1, 0), 0)),
                      (w_dw, (HALO, D), fix)],
                 outs=[((T, 2 * D), MXU, (tm, 2 * D), row), ((HALO, D), F32, (HALO, D), fix),
                       ((1, D), F32, (1, D), fix), ((1, 2 * D), F32, (1, 2 * D), fix)],
                 scratch=_window_scratch(tm) + _window_scratch(tm) + [pltpu.VMEM((SUB * HALO, D), F32)])


def sb_prep(ps, gq, gk, *, name, tm=256):
    T = ps.shape[0]
    tm = _rows(T, tm)
    W = SB_H * SB_D

    def body(q_ref, k_ref, v_ref, gq_ref, gk_ref, qo, ko, vo):
        qs, ks = [], []
        for h in range(SB_H):
            sl = slice(h * SB_D, (h + 1) * SB_D)
            q = q_ref[:, sl]
            k = k_ref[:, sl]
            rq = lax.rsqrt(jnp.mean(q * q, axis=-1, keepdims=True) + EPS)
            rk = lax.rsqrt(jnp.mean(k * k, axis=-1, keepdims=True) + EPS)
            qs.append(q * rq * gq_ref[...] * (SB_D ** -0.5))
            ks.append(k * rk * gk_ref[...])
        qo[...] = jnp.concatenate(qs, axis=-1).astype(MXU)
        ko[...] = jnp.concatenate(ks, axis=-1).astype(MXU)
        vo[...] = v_ref[...].astype(MXU)

    row = lambda i: (i, 0)
    fix = lambda i: (0, 0)
    return _call(body, name=name, grid=(T // tm,),
                 ins=[(ps, (tm, W), lambda i: (i, 0)), (ps, (tm, W), lambda i: (i, 1)),
                      (ps, (tm, W), lambda i: (i, 2)), (gq, (1, SB_D), fix), (gk, (1, SB_D), fix)],
                 outs=[((T, W), MXU, (tm, W), row)] * 3)


def _sb_masks(tq):
    row = lax.broadcasted_iota(jnp.int32, (tq, tq), 0)
    col = lax.broadcasted_iota(jnp.int32, (tq, tq), 1)
    tri = jnp.where(row >= col, 1.0, 0.0).astype(MXU)
    past = col < row
    return tri, past


def _sb_scores(q, k, past):
    return _sb_logs(_dot(q, k, NT), past)


def _sb_logs(z, past):
    ls = jnp.minimum(z, 0.0) - _softplus_neg_abs(z)
    lk = ls - z
    if past is not None:
        lk = jnp.where(past, lk, 0.0)
    return ls, lk


def _fused_exchange(fused, steps_pred):
    if fused is None:
        return [], [], [], lambda refs: None, lambda refs: None
    src, n_dst, spec = fused
    n = len(spec["flips"])
    ins = [(src, None, None)]
    outs = [((n_dst,) + src.shape[1:], src.dtype, None, None)]
    scratch = [pltpu.SemaphoreType.DMA((n,)), pltpu.SemaphoreType.DMA((n,)), pltpu.SemaphoreType.DMA(())]

    def at_start(refs):
        @pl.when(steps_pred()[0])
        def _():
            _exchange_start(*refs, **spec)

    def at_end(refs):
        @pl.when(steps_pred()[1])
        def _():
            _exchange_wait(*refs, **spec)

    return ins, outs, scratch, at_start, at_end


PAIR = 2 * SB_D


def _split_pair(x2, lo):
    zero = jnp.zeros_like(x2)
    return [jnp.where(lo, x2, zero), jnp.where(lo, zero, x2)]


def sb_fwd(qn, kn, v, *, name, tq=256, pairs=2, fused=None):
    T, W = qn.shape
    tq = _rows(T, tq)
    wb, hg = pairs * PAIR, 2 * pairs
    assert T // tq <= LANES and W % wb == 0
    ng, ni = W // wb, T // tq
    pred = lambda: ((pl.program_id(0) == 0) & (pl.program_id(1) == 0),
                    (pl.program_id(0) == ng - 1) & (pl.program_id(1) == ni - 1))
    x_ins, x_outs, x_scratch, x_start, x_end = _fused_exchange(fused, pred)
    nx = len(x_ins)
    heads = range(hg)
    pcols = [slice(p * PAIR, (p + 1) * PAIR) for p in range(pairs)]

    def body(*refs):
        q_ref, k_ref, v_ref = refs[:3]
        o_ref, cb_ref = refs[3 + nx:5 + nx]
        x_refs = refs[3:3 + nx] + refs[5 + nx:]
        x_start(x_refs)
        i = pl.program_id(1)
        tri, past = _sb_masks(tq)
        lane = lax.broadcasted_iota(jnp.int32, (tq, LANES), 1)
        lo = lane < SB_D
        cb_ref[...] = jnp.zeros((hg, tq, LANES), F32)
        qs = [m for p in range(pairs) for m in _split_pair(q_ref[:, pcols[p]], lo)]

        def block(j, carry, masked):
            accs, cs = carry
            kblk = pl.ds(pl.multiple_of(j * tq, tq), tq)
            for h in heads:
                cb_ref[h] = jnp.where(lane == j, cs[h], cb_ref[h])
            zs = [_dot(qs[h], k_ref[kblk, pcols[h // 2]], NT) for h in heads]
            sc = [_sb_logs(zs[h], past if masked else None) for h in heads]
            bincs = [_dot_split(sc[h][1], tri) for h in heads]
            ws = [jnp.exp(sc[h][0] + cs[h] + bincs[h] - sc[h][1]) for h in heads]
            if masked:
                ws = [jnp.where(past, w, 0.0) for w in ws]
            pv = [_dot(ws[h], v_ref[kblk, pcols[h // 2]]) for h in heads]
            new_a = [accs[p] + jnp.where(lo, pv[2 * p], pv[2 * p + 1]) for p in range(pairs)]
            new_c = [cs[h] + jnp.sum(sc[h][1], axis=-1, keepdims=True) for h in heads]
            return tuple(new_a), tuple(new_c)

        carry = ((jnp.zeros((tq, PAIR), F32),) * pairs, (jnp.zeros((tq, 1), F32),) * hg)
        carry = block(i, carry, True)
        accs, _ = lax.fori_loop(0, i, lambda jj, car: block(i - 1 - jj, car, False), carry)
        for p in range(pairs):
            o_ref[:, pcols[p]] = accs[p]
        x_end(x_refs)

    blk = lambda g, i: (i, g)
    full = lambda g, i: (0, g)
    return _call(body, name=name, grid=(ng, ni),
                 ins=[(qn, (tq, wb), blk), (kn, (T, wb), full), (v, (T, wb), full)] + x_ins,
                 outs=[((T, W), F32, (tq, wb), blk),
                       ((W // SB_D, T, LANES), F32, (hg, tq, LANES), lambda g, i: (g, i, 0))] + x_outs,
                 scratch=x_scratch)


def sb_bwd(qn, kn, v, cb, dmix, do_col, *, name, tq=256, pairs=2, fused=None):
    T, W = qn.shape
    tq = _rows(T, tq)
    wb, hg = pairs * PAIR, 2 * pairs
    assert W % wb == 0 and do_col % wb == 0
    ng, ni = W // wb, T // tq
    pred = lambda: ((pl.program_id(0) == 0) & (pl.program_id(1) == 0),
                    (pl.program_id(0) == ng - 1) & (pl.program_id(1) == ni - 1))
    x_ins, x_outs, x_scratch, x_start, x_end = _fused_exchange(fused, pred)
    nx = len(x_ins)
    heads = range(hg)
    pcols = [slice(p * PAIR, (p + 1) * PAIR) for p in range(pairs)]

    def body(*refs):
        q_ref, k_ref, v_ref, cb_ref, do_ref = refs[:5]
        dq_ref, dk_hbm, dv_hbm = refs[5 + nx:8 + nx]
        dk_ref, dv_ref = refs[8 + 2 * nx:10 + 2 * nx]
        x_refs = refs[5:5 + nx] + refs[8 + nx:8 + 2 * nx] + refs[10 + 2 * nx:]
        x_start(x_refs)
        i = pl.program_id(1)

        @pl.when(i == 0)
        def _():
            dk_ref[...] = jnp.zeros_like(dk_ref)
            dv_ref[...] = jnp.zeros_like(dv_ref)
        tri, past = _sb_masks(tq)
        row = lax.broadcasted_iota(jnp.int32, (tq, tq), 0)
        col = lax.broadcasted_iota(jnp.int32, (tq, tq), 1)
        tri_lt = jnp.where(row < col, 1.0, 0.0).astype(MXU)
        lane = lax.broadcasted_iota(jnp.int32, (tq, LANES), 1)
        lo = lane < SB_D
        qs = [m for p in range(pairs) for m in _split_pair(q_ref[:, pcols[p]], lo)]
        dos = [m for p in range(pairs) for m in _split_pair(do_ref[:, pcols[p]].astype(MXU), lo)]

        def block(j, carry, masked):
            dqs, cgs = carry
            kblk = pl.ds(pl.multiple_of(j * tq, tq), tq)
            k2 = [k_ref[kblk, pcols[p]] for p in range(pairs)]
            zs = [_dot(qs[h], k2[h // 2], NT) for h in heads]
            dws = [_dot(dos[h], v_ref[kblk, pcols[h // 2]], NT) for h in heads]
            sc = [_sb_logs(zs[h], past if masked else None) for h in heads]
            cs = [jnp.sum(jnp.where(lane == j, cb_ref[h], 0.0), axis=-1, keepdims=True) for h in heads]
            bincs = [_dot_split(sc[h][1], tri) for h in heads]
            ws = [jnp.exp(sc[h][0] + cs[h] + bincs[h] - sc[h][1]) for h in heads]
            if masked:
                ws = [jnp.where(past, w, 0.0) for w in ws]
            wbs = [w.astype(MXU) for w in ws]
            gs = [ws[h] * dws[h] for h in heads]
            gpres = [cgs[h] + _dot(gs[h], tri_lt) for h in heads]
            sigs = [jnp.exp(sc[h][0]) for h in heads]
            dzs = [gs[h] - sigs[h] * (gs[h] + gpres[h]) for h in heads]
            if masked:
                dzs = [jnp.where(past, dz, 0.0) for dz in dzs]
            dzbs = [dz.astype(MXU) for dz in dzs]
            dqp = [_dot(dzbs[h], k2[h // 2]) for h in heads]
            new_dq = [dqs[p] + jnp.where(lo, dqp[2 * p], dqp[2 * p + 1]) for p in range(pairs)]
            for p in range(pairs):
                dk_ref[kblk, pcols[p]] += _dot(dzbs[2 * p], qs[2 * p], TN) + _dot(dzbs[2 * p + 1], qs[2 * p + 1], TN)
            for p in range(pairs):
                dv_ref[kblk, pcols[p]] += _dot(wbs[2 * p], dos[2 * p], TN) + _dot(wbs[2 * p + 1], dos[2 * p + 1], TN)
            new_cg = [cgs[h] + jnp.sum(gs[h], axis=-1, keepdims=True) for h in heads]
            return tuple(new_dq), tuple(new_cg)

        carry = ((jnp.zeros((tq, PAIR), F32),) * pairs, (jnp.zeros((tq, 1), F32),) * hg)
        carry = lax.fori_loop(0, i, lambda j, car: block(j, car, False), carry)
        dqs, _ = block(i, carry, True)
        for p in range(pairs):
            dq_ref[:, pcols[p]] = dqs[p]

        cols = pl.ds(pl.multiple_of(pl.program_id(0) * wb, wb), wb)

        @pl.when(i == ni - 1)
        def _():
            pltpu.sync_copy(dk_ref, dk_hbm.at[:, cols])
            pltpu.sync_copy(dv_ref, dv_hbm.at[:, cols])
        x_end(x_refs)

    blk = lambda g, i: (i, g)
    full = lambda g, i: (0, g)
    return _call(body, name=name, grid=(ng, ni),
                 ins=[(qn, (tq, wb), blk), (kn, (T, wb), full), (v, (T, wb), full),
                      (cb, (hg, tq, LANES), lambda g, i: (g, i, 0)),
                      (dmix, (tq, wb), lambda g, i: (i, do_col // wb + g))] + x_ins,
                 outs=[((T, W), F32, (tq, wb), blk), ((T, W), F32, None, None), ((T, W), F32, None, None)] + x_outs,
                 scratch=[pltpu.VMEM((T, wb), F32), pltpu.VMEM((T, wb), F32)] + x_scratch)


def sb_post_bwd(dqn, dkn, dv, ps, gq, gk, *, name, tm=256):
    T = ps.shape[0]
    tm = _rows(T, tm)
    W = SB_H * SB_D

    def body(dq_ref, dk_ref, dv_ref, q_ref, k_ref, gq_ref, gk_ref, out_ref, dgq_ref, dgk_ref):
        @pl.when(pl.program_id(0) == 0)
        def _():
            dgq_ref[...] = jnp.zeros_like(dgq_ref)
            dgk_ref[...] = jnp.zeros_like(dgk_ref)

        def norm_bwd(x, d, g):
            r = lax.rsqrt(jnp.mean(x * x, axis=-1, keepdims=True) + EPS)
            dg = jnp.sum(d * x * r, axis=0, keepdims=True)
            t = d * g
            return t * r - x * (r * r * r) * jnp.mean(t * x, axis=-1, keepdims=True), dg

        dqs, dks = [], []
        dgq = jnp.zeros((1, SB_D), F32)
        dgk = jnp.zeros((1, SB_D), F32)
        for h in range(SB_H):
            sl = slice(h * SB_D, (h + 1) * SB_D)
            a, ga = norm_bwd(q_ref[:, sl], dq_ref[:, sl] * (SB_D ** -0.5), gq_ref[...])
            b, gb = norm_bwd(k_ref[:, sl], dk_ref[:, sl], gk_ref[...])
            dqs.append(a)
            dks.append(b)
            dgq = dgq + ga
            dgk = dgk + gb
        out_ref[...] = jnp.concatenate(dqs + dks + [dv_ref[...]], axis=-1).astype(MXU)
        dgq_ref[...] += dgq
        dgk_ref[...] += dgk

    row = lambda i: (i, 0)
    fix = lambda i: (0, 0)
    return _call(body, name=name, grid=(T // tm,),
                 ins=[(dqn, (tm, W), row), (dkn, (tm, W), row), (dv, (tm, W), row),
                      (ps, (tm, W), lambda i: (i, 0)), (ps, (tm, W), lambda i: (i, 1)),
                      (gq, (1, SB_D), fix), (gk, (1, SB_D), fix)],
                 outs=[((T, 3 * W), MXU, (tm, 3 * W), lambda i: (i, 0)),
                       ((1, SB_D), F32, (1, SB_D), fix), ((1, SB_D), F32, (1, SB_D), fix)])


GLA_TM = 512
GLA_DK = GLA_H * GLA_K
GLA_DV = GLA_H * GLA_V


def _gla_masks(tm):
    row = lax.broadcasted_iota(jnp.int32, (tm, tm), 0)
    col = lax.broadcasted_iota(jnp.int32, (tm, tm), 1)
    same = (row // CHUNK) == (col // CHUNK)
    return row, col, same


def _gla_gate(glr, w2, b, tm):
    pre = _dot(glr, w2) + b
    la = (jnp.minimum(pre, 0.0) - _softplus_neg_abs(pre)) * (1.0 / 16.0)
    row, col, same = _gla_masks(tm)
    m_incl = jnp.where(same & (col <= row), 1.0, 0.0).astype(MXU)
    m_full = jnp.where(same, 1.0, 0.0).astype(MXU)
    bc = _dot_split(m_incl, la, a_split=False)
    tot = _dot_split(m_full, la, a_split=False)
    return pre, bc, tot


def gla_fwd(pg, glr, w2, b_gate, g_out, *, name):
    T = pg.shape[0]
    tm = _rows(T, GLA_TM)
    ncb = tm // CHUNK
    NC = T // CHUNK

    def body(q_ref, k_ref, v_ref, r_ref, l_ref, w2_ref, b_ref, g_ref, o_ref, st_ref, S):
        @pl.when(pl.program_id(0) == 0)
        def _():
            S[...] = jnp.zeros_like(S)
        _, bc, tot = _gla_gate(l_ref[...], w2_ref[...], b_ref[...], tm)
        kend = k_ref[...] * jnp.exp(tot - bc)
        qs = q_ref[...] * (GLA_K ** -0.5)
        a_all = jnp.exp(tot)
        v = v_ref[...]
        r = r_ref[...]
        for c in range(ncb):
            rows = slice(c * CHUNK, (c + 1) * CHUNK)
            outs = []
            for h in range(GLA_H):
                hk = slice(h * GLA_K, (h + 1) * GLA_K)
                hv = slice(h * GLA_V, (h + 1) * GLA_V)
                ut = _dot(v[rows, hv], kend[rows, hk], TN)
                s_new = S[h] * a_all[c * CHUNK:c * CHUNK + 1, hk] + ut
                S[h] = s_new
                st_ref[c, h] = s_new
                o = _dot(qs[rows, hk], s_new, NT)
                rinv = lax.rsqrt(jnp.mean(o * o, axis=-1, keepdims=True) + EPS)
                rr = r[rows, hv]
                outs.append(o * rinv * g_ref[...] * (rr * _sigmoid(rr)))
            o_ref[pl.ds(c * CHUNK, CHUNK), :] = jnp.concatenate(outs, axis=-1).astype(MXU)

    fix = lambda i: (0, 0)
    return _call(body, name=name, grid=(T // tm,),
                 ins=[(pg, (tm, GLA_DK), lambda i: (i, 0)), (pg, (tm, GLA_DK), lambda i: (i, 1)),
                      (pg, (tm, GLA_DV), lambda i: (i, 1)), (pg, (tm, GLA_DV), lambda i: (i, 2)),
                      (glr, (tm, LANES), lambda i: (i, 0)), (w2, (LANES, GLA_DK), fix),
                      (b_gate, (1, GLA_DK), fix), (g_out, (1, GLA_V), fix)],
                 outs=[((T, GLA_DV), MXU, (tm, GLA_DV), lambda i: (i, 0)),
                       ((NC, GLA_H, GLA_V, GLA_K), F32, (ncb, GLA_H, GLA_V, GLA_K), lambda i: (i, 0, 0, 0))],
                 scratch=[pltpu.VMEM((GLA_H, GLA_V, GLA_K), F32)])


def gla_bwd(pg, glr, w2, b_gate, g_out, st, dmix, *, name):
    T = pg.shape[0]
    tm = _rows(T, GLA_TM)
    ncb = tm // CHUNK
    n = T // tm

    def body(q_ref, k_ref, v_ref, r_ref, l_ref, w2_ref, b_ref, g_ref, st_ref, sp_ref, d_ref,
             dpg_ref, dl_ref, dw2_ref, db_ref, dg_ref, dS, dkend, extra):
        i = pl.program_id(0)

        @pl.when(i == 0)
        def _():
            dS[...] = jnp.zeros_like(dS)
            dw2_ref[...] = jnp.zeros_like(dw2_ref)
            db_ref[...] = jnp.zeros_like(db_ref)
            dg_ref[...] = jnp.zeros_like(dg_ref)
        first_tile = i == n - 1
        glr_v = l_ref[...]
        pre, bc, tot = _gla_gate(glr_v, w2_ref[...], b_ref[...], tm)
        dec = jnp.exp(tot - bc)
        k = k_ref[...]
        kend = k * dec
        qs = q_ref[...] * (GLA_K ** -0.5)
        a_all = jnp.exp(tot)
        v = v_ref[...]
        r = r_ref[...]
        g = g_ref[...]
        dgg = jnp.zeros((1, GLA_V), F32)
        for c in reversed(range(ncb)):
            rows = slice(c * CHUNK, (c + 1) * CHUNK)
            dq_l, dv_l, dr_l, dk_l, ex_l = [], [], [], [], []
            for h in range(GLA_H):
                hk = slice(h * GLA_K, (h + 1) * GLA_K)
                hv = slice(h * GLA_V, (h + 1) * GLA_V)
                s_c = st_ref[c, h]
                if c > 0:
                    s_p = st_ref[c - 1, h]
                else:
                    s_p = jnp.where(first_tile, 0.0, sp_ref[0, h])
                qh = qs[rows, hk]
                o = _dot(qh, s_c, NT)
                rinv = lax.rsqrt(jnp.mean(o * o, axis=-1, keepdims=True) + EPS)
                nrm = o * rinv
                rr = r[rows, hv]
                sg = _sigmoid(rr)
                sil = rr * sg
                d = d_ref[pl.ds(c * CHUNK, CHUNK), pl.ds(h * GLA_V, GLA_V)]
                dr_l.append(d * nrm * g * (sg * (1.0 + rr * (1.0 - sg))))
                dgg = dgg + jnp.sum(d * sil * nrm, axis=0, keepdims=True)
                dn = d * sil * g
                do = rinv * (dn - nrm * jnp.mean(dn * nrm, axis=-1, keepdims=True))
                dq_l.append(_dot(do, s_c) * (GLA_K ** -0.5))
                dst = dS[h] + _dot(do, qh, TN)
                a = a_all[c * CHUNK:c * CHUNK + 1, hk]
                da = jnp.sum(dst * s_p, axis=0, keepdims=True)
                dS[h] = dst * a
                dk_l.append(_dot(v[rows, hv], dst))
                dv_l.append(_dot(kend[rows, hk], dst, NT))
                ex_l.append(jnp.broadcast_to(da * a, (CHUNK, GLA_K)))
            cr = pl.ds(c * CHUNK, CHUNK)
            dpg_ref[cr, pl.ds(0, GLA_DK)] = jnp.concatenate(dq_l, axis=-1).astype(MXU)
            dpg_ref[cr, pl.ds(2 * GLA_DK, GLA_DV)] = jnp.concatenate(dv_l, axis=-1).astype(MXU)
            dpg_ref[cr, pl.ds(2 * GLA_DK + GLA_DV, GLA_DV)] = jnp.concatenate(dr_l, axis=-1).astype(MXU)
            dkend[cr, :] = jnp.concatenate(dk_l, axis=-1)
            extra[cr, :] = jnp.concatenate(ex_l, axis=-1)
        dke = dkend[...]
        dpg_ref[:, pl.ds(GLA_DK, GLA_DK)] = (dke * dec).astype(MXU)
        e = dke * kend
        row, col, same = _gla_masks(tm)
        m_lt = jnp.where(same & (col < row), 1.0, 0.0).astype(MXU)
        dla = _dot_split(m_lt, e, a_split=False) + extra[...]
        sp = _softplus_neg_abs(pre)
        one_m_sig = jnp.exp(-jnp.maximum(pre, 0.0) - sp)
        dpre = dla * (1.0 / 16.0) * one_m_sig
        dl_ref[...] = _dot(dpre, w2_ref[...], NT).astype(MXU)
        dw2_ref[...] += _dot(glr_v, dpre, TN)
        db_ref[...] += jnp.sum(dpre, axis=0, keepdims=True)
        dg_ref[...] += dgg

    fix = lambda i: (0, 0)
    rev = lambda i: n - 1 - i
    return _call(body, name=name, grid=(n,),
                 ins=[(pg, (tm, GLA_DK), lambda i: (rev(i), 0)), (pg, (tm, GLA_DK), lambda i: (rev(i), 1)),
                      (pg, (tm, GLA_DV), lambda i: (rev(i), 1)), (pg, (tm, GLA_DV), lambda i: (rev(i), 2)),
                      (glr, (tm, LANES), lambda i: (rev(i), 0)), (w2, (LANES, GLA_DK), fix),
                      (b_gate, (1, GLA_DK), fix), (g_out, (1, GLA_V), fix),
                      (st, (ncb, GLA_H, GLA_V, GLA_K), lambda i: (rev(i), 0, 0, 0)),
                      (st, (1, GLA_H, GLA_V, GLA_K), lambda i: (jnp.maximum(rev(i) * ncb - 1, 0), 0, 0, 0)),
                      (dmix, (tm, GLA_DV), lambda i: (rev(i), 0))],
                 outs=[((T, 2 * GLA_DK + 2 * GLA_DV), MXU, (tm, 2 * GLA_DK + 2 * GLA_DV), lambda i: (rev(i), 0)),
                       ((T, LANES), MXU, (tm, LANES), lambda i: (rev(i), 0)),
                       ((LANES, GLA_DK), F32, (LANES, GLA_DK), fix),
                       ((1, GLA_DK), F32, (1, GLA_DK), fix), ((1, GLA_V), F32, (1, GLA_V), fix)],
                 scratch=[pltpu.VMEM((GLA_H, GLA_V, GLA_K), F32), pltpu.VMEM((tm, GLA_DK), F32),
                          pltpu.VMEM((tm, GLA_DK), F32)])


def local_step(x, tgt, W, late=None):
    row1 = lambda a, l: a[l:l + 1]
    hn0 = rms_fwd(x, row1(W["mix_norm"], 0), name="l0_norm")
    pg = mm([(hn0, W["wi_g"])], name="l0_proj_gla", tn=512)
    ps = mm([(hn0, W["wi_s"])], name="l0_proj_sb", tn=512)
    glr = mm([(hn0, W["wi_l"])], name="l0_proj_gate", out_dtype=MXU)
    og, st = gla_fwd(pg, glr, W["w2"], W["b_gate"], W["g_gla"], name="gla_fwd")
    qn, kn, vh = sb_prep(ps, W["g_q"], W["g_k"], name="sb_prep")
    if late is None:
        osb, sb_cb = sb_fwd(qn, kn, vh, name="sb_fwd")
    else:
        osb, sb_cb, gathered = sb_fwd(qn, kn, vh, name="sb_fwd", fused=(late["src"], N_CHIPS, GATHER))
        W = {**W, **late["unpack"](gathered)}
    h1 = mm([(og, W["wo_g"]), (osb, W["wo_s"])], res=x, name="l0_out")
    h2, ffn0 = ffn_fwd(h1, row1(W["ffn_norm"], 0), W["wg0"], W["wu0"], W["wd0"], 0)
    hn1 = rms_fwd(h2, row1(W["mix_norm"], 1), name="l1_norm")
    A = mm([(hn1, W["pw1"])], bias=W["b_pw1"], name="l1_pw1")
    s, cconv = conv_fwd(A, W["w_dw"], W["b_dw"], W["ln_g"], W["ln_b"], name="conv_fwd")
    h3 = mm([(s, W["pw2"])], bias=W["b_pw2"], res=h2, name="l1_pw2")
    y, ffn1 = ffn_fwd(h3, row1(W["ffn_norm"], 1), W["wg1"], W["wu1"], W["wd1"], 1)
    dy, loss_lanes = loss_grad(y, tgt, name="loss")
    G = {}
    dh3, g_fn1, G["wg1"], G["wu1"], G["wd1"], cs3 = ffn_bwd(
        dy, h3, row1(W["ffn_norm"], 1), W["wg1"], W["wu1"], W["wd1"], ffn1, 1)
    G["b_pw2"] = cs3
    ds = mm([(dh3, W["pw2"])], trans_b=True, name="l1_ds")
    G["pw2"] = mm_tn(s, dh3, name="l1_dpw2", tm=1024, tn=1024)
    dc, G["ln_g"], G["ln_b"] = ln_swish_bwd(cconv, ds, W["ln_g"], W["ln_b"], name="ln_bwd")
    dA, G["w_dw"], G["b_dw"], G["b_pw1"] = conv_bwd(dc, A, W["w_dw"], name="conv_bwd")
    G["pw1"] = mm_tn(hn1, dA, name="l1_dpw1", tm=1024, tn=1024)
    dhn1 = mm([(dA, W["pw1"])], trans_b=True, name="l1_dhn")
    dh2, g_mn1, _ = rms_bwd(h2, row1(W["mix_norm"], 1), dhn1, dh3, name="l1_norm_bwd")
    dh1, g_fn0, G["wg0"], G["wu0"], G["wd0"], _ = ffn_bwd(
        dh2, h1, row1(W["ffn_norm"], 0), W["wg0"], W["wu0"], W["wd0"], ffn0, 0)
    dmix = mm([(dh1, W["wo_gs"])], trans_b=True, name="l0_dmix")
    G["wo_g"] = mm_tn(og, dh1, name="l0_dwo_g", tm=512, tn=1024)
    G["wo_s"] = mm_tn(osb, dh1, name="l0_dwo_s", tm=512, tn=1024)
    slots = None
    if late is None:
        dqn, dkn, dvh = sb_bwd(qn, kn, vh, sb_cb, dmix, GLA_DV, name="sb_bwd")
    else:
        dqn, dkn, dvh, slots = sb_bwd(qn, kn, vh, sb_cb, dmix, GLA_DV, name="sb_bwd",
                                      fused=(late["pack"](G), N_DEV, SCATTER))
    dps, G["g_q"], G["g_k"] = sb_post_bwd(dqn, dkn, dvh, ps, W["g_q"], W["g_k"], name="sb_post_bwd")
    dpg, dglr, G["w2"], G["b_gate"], G["g_gla"] = gla_bwd(
        pg, glr, W["w2"], W["b_gate"], W["g_gla"], st, dmix, name="gla_bwd")
    G["wi_g"] = mm_tn(hn0, dpg, name="l0_dwi_g", tm=1024, tn=512)
    G["wi_s"] = mm_tn(hn0, dps, name="l0_dwi_s", tm=1024, tn=512)
    G["wi_l"] = mm_tn(hn0, dglr, name="l0_dwi_l", tm=1024, tn=LANES)
    dhn0 = mm([(dpg, W["wi_g"]), (dps, W["wi_s"]), (dglr, W["wi_l"])], trans_b=True, name="l0_dhn")
    dx, g_mn0, _ = rms_bwd(x, row1(W["mix_norm"], 0), dhn0, dh1, name="l0_norm_bwd")
    G["mix_norm"] = jnp.concatenate([g_mn0, g_mn1], axis=0)
    G["ffn_norm"] = jnp.concatenate([g_fn0, g_fn1], axis=0)
    return loss_lanes, dx, G, slots


_C_GLA = 2 * GLA_DK + 2 * GLA_DV
_C_SB0 = _C_GLA + GLA_RANK


def layout_first(full):
    w_in = full["hy_w_in"][0]
    W = {
        "wi_g": w_in[:, :_C_GLA].astype(MXU),
        "wi_s": w_in[:, _C_SB0:].astype(MXU),
        "wi_l": jnp.pad(w_in[:, _C_GLA:_C_SB0], ((0, 0), (0, LANES - GLA_RANK))).astype(MXU),
        "w2": jnp.pad(full["hy_w_gate2"][0], ((0, LANES - GLA_RANK), (0, 0))),
        "b_gate": full["hy_b_gate"], "g_gla": full["hy_gla_norm"],
        "g_q": full["hy_sb_q_norm"], "g_k": full["hy_sb_k_norm"],
        "wo_gs": full["hy_w_out"][0].astype(MXU),
        "b_pw1": full["cv_b_pw1"],
        "w_dw": jnp.pad(full["cv_w_dw"][0], ((0, HALO - CONV_W), (0, 0))),
        "b_dw": full["cv_b_dw"], "ln_g": full["cv_ln_g"], "ln_b": full["cv_ln_b"],
        "b_pw2": full["cv_b_pw2"],
        "mix_norm": full["mix_norm"], "ffn_norm": full["ffn_norm"],
    }
    W["wo_g"] = W["wo_gs"][:GLA_DV]
    W["wo_s"] = W["wo_gs"][GLA_DV:]
    return W


def layout_late(full):
    W = {"pw1": full["cv_w_pw1"][0].astype(MXU), "pw2": full["cv_w_pw2"][0].astype(MXU)}
    for l in range(2):
        W[f"wg{l}"] = full["ffn_w_gate"][l].astype(MXU)
        W[f"wu{l}"] = full["ffn_w_up"][l].astype(MXU)
        W[f"wd{l}"] = full["ffn_w_down"][l].astype(MXU)
    return W


def reference_first(G):
    return {
        "mix_norm": G["mix_norm"], "ffn_norm": G["ffn_norm"],
        "hy_w_in": jnp.concatenate([G["wi_g"], G["wi_l"][:, :GLA_RANK], G["wi_s"]], axis=1)[None],
        "hy_w_gate2": G["w2"][:GLA_RANK][None],
        "hy_b_gate": G["b_gate"], "hy_gla_norm": G["g_gla"],
        "hy_sb_q_norm": G["g_q"], "hy_sb_k_norm": G["g_k"],
        "hy_w_out": jnp.concatenate([G["wo_g"], G["wo_s"]], axis=0)[None],
        "cv_b_pw1": G["b_pw1"],
        "cv_w_dw": G["w_dw"][:CONV_W][None], "cv_b_dw": G["b_dw"],
        "cv_ln_g": G["ln_g"], "cv_ln_b": G["ln_b"], "cv_b_pw2": G["b_pw2"],
    }


def reference_late(G):
    return {
        "cv_w_pw1": G["pw1"][None], "cv_w_pw2": G["pw2"][None],
        "ffn_w_gate": jnp.stack([G["wg0"], G["wg1"]]),
        "ffn_w_up": jnp.stack([G["wu0"], G["wu1"]]),
        "ffn_w_down": jnp.stack([G["wd0"], G["wd1"]]),
    }


WEIGHTS = ["mix_norm", "ffn_norm", "hy_w_in", "hy_w_gate2", "hy_b_gate", "hy_gla_norm", "hy_sb_q_norm",
           "hy_sb_k_norm", "hy_w_out", "cv_w_pw1", "cv_b_pw1", "cv_w_dw", "cv_b_dw", "cv_ln_g", "cv_ln_b",
           "cv_w_pw2", "cv_b_pw2", "ffn_w_gate", "ffn_w_up", "ffn_w_down"]
SHARD_AXIS = {"hy_w_in": 2, "hy_w_out": 1, "cv_w_pw1": 2, "cv_w_pw2": 1, "ffn_w_gate": 2, "ffn_w_up": 2,
              "ffn_w_down": 1, "hy_w_gate2": 2, "cv_b_pw1": 1, "cv_w_dw": 2, "cv_b_dw": 1, "cv_ln_g": 1,
              "cv_ln_b": 1, "cv_b_pw2": 1}
FIRST_BIG = ["hy_w_in", "hy_w_out"]
SMALL = ["hy_w_gate2", "cv_b_pw1", "cv_w_dw", "cv_b_dw", "cv_ln_g", "cv_ln_b", "cv_b_pw2"]
LATE = ["cv_w_pw1", "cv_w_pw2", "ffn_w_gate", "ffn_w_up", "ffn_w_down"]
REPL = ["mix_norm", "ffn_norm", "hy_b_gate", "hy_gla_norm", "hy_sb_q_norm", "hy_sb_k_norm"]
N_CHIPS = 4
N_DEV = 8
GRAD_ROWS = 1024


def _pack(arrs, rows_multiple, dtype):
    flat = jnp.concatenate([a.reshape(-1).astype(dtype) for a in arrs])
    n = flat.shape[0]
    rows = -(-n // (LANES * rows_multiple)) * rows_multiple
    return jnp.pad(flat, (0, rows * LANES - n)).reshape(rows, LANES)


def _unpack(flat2d, shapes):
    flat = flat2d.reshape(-1)
    out, off = [], 0
    for s in shapes:
        n = math.prod(s)
        out.append(flat[off:off + n].reshape(s))
        off += n
    return out


def _coords():
    return lax.axis_index("x"), lax.axis_index("y"), lax.axis_index("c")


def _flip(pos, f):
    return tuple(1 - p if b else p for p, b in zip(pos, f))


def _exchange_copies(src_ref, dst_ref, send_sems, recv_sems, loc_sem, *, flips, src_idx, dst_idx, local_idx,
                     with_recvs=True):
    me = _coords()
    loc = None
    if local_idx is not None:
        si, di = local_idx(me)
        loc = pltpu.make_async_copy(src_ref.at[si], dst_ref.at[di], loc_sem)
    sends, recvs = [], []
    for k, f in enumerate(flips):
        peer = _flip(me, f)
        sends.append(pltpu.make_async_remote_copy(
            src_ref=src_ref.at[src_idx(me, peer)], dst_ref=dst_ref.at[dst_idx(me)],
            send_sem=send_sems.at[k], recv_sem=recv_sems.at[k],
            device_id=peer, device_id_type=pl.DeviceIdType.MESH))
        if with_recvs:
            recvs.append(pltpu.make_async_remote_copy(
                src_ref=src_ref.at[src_idx(peer, me)], dst_ref=dst_ref.at[dst_idx(peer)],
                send_sem=send_sems.at[k], recv_sem=recv_sems.at[k],
                device_id=peer, device_id_type=pl.DeviceIdType.MESH))
    return loc, sends, recvs


def _exchange_start(*refs, **spec):
    loc, sends, _ = _exchange_copies(*refs, with_recvs=False, **spec)
    if loc is not None:
        loc.start()
    for s in sends:
        s.start()


def _exchange_wait(*refs, **spec):
    loc, sends, recvs = _exchange_copies(*refs, **spec)
    for s in sends:
        s.wait_send()
    for r in recvs:
        r.wait_recv()
    if loc is not None:
        loc.wait()


def exchange(src, n_dst, spec, *, name, in_place=False):
    n = len(spec["flips"])
    assert not in_place or (n_dst == src.shape[0] and spec["local_idx"] is None)

    def body(*refs):
        _exchange_start(*refs, **spec)
        _exchange_wait(*refs, **spec)

    return pl.pallas_call(
        body, name=name,
        out_shape=jax.ShapeDtypeStruct((n_dst,) + src.shape[1:], src.dtype),
        in_specs=[pl.BlockSpec(memory_space=pl.ANY)],
        out_specs=pl.BlockSpec(memory_space=pl.ANY),
        scratch_shapes=[pltpu.SemaphoreType.DMA((n,)), pltpu.SemaphoreType.DMA((n,)), pltpu.SemaphoreType.DMA(())],
        input_output_aliases={0: 0} if in_place else {},
    )(src)


CHIP_FLIPS = [(1, 0, 0), (0, 1, 0), (1, 1, 0)]
SIBLING = [(0, 0, 1)]
ALL_FLIPS = [(a, b, c) for a in (0, 1) for b in (0, 1) for c in (0, 1) if (a, b, c) != (0, 0, 0)]


def _chip(pos):
    return 2 * pos[0] + pos[1]


def _dev(pos):
    return 4 * pos[0] + 2 * pos[1] + pos[2]


GATHER = dict(flips=CHIP_FLIPS, src_idx=lambda me, peer: 0, dst_idx=_chip, local_idx=lambda me: (0, _chip(me)))
SCATTER = dict(flips=ALL_FLIPS, src_idx=lambda me, peer: 4 * peer[2] + _chip(peer), dst_idx=_dev,
               local_idx=lambda me: (4 * me[2] + _chip(me), _dev(me)))
SHARE = dict(flips=SIBLING, src_idx=lambda me, peer: me[2], dst_idx=lambda me: me[2], local_idx=None)
ALL_TO_ALL = dict(flips=ALL_FLIPS, src_idx=lambda me, peer: 0, dst_idx=_dev, local_idx=lambda me: (0, _dev(me)))


def sum_slots(x, *, name, tr=512):
    n, R, L = x.shape
    tr = _rows(R, tr)

    def body(x_ref, o_ref):
        acc = x_ref[0]
        for k in range(1, n):
            acc = acc + x_ref[k]
        o_ref[...] = acc

    return _call(body, name=name, grid=(R // tr,),
                 ins=[(x, (n, tr, L), lambda i: (0, i, 0))],
                 outs=[((R, L), F32, (tr, L), lambda i: (i, 0))])[0]


def sum_slots_into_half(x, half, *, name, tr=512):
    n, R, L = x.shape
    tr = _rows(R, tr)

    def body(h_ref, x_ref, o_ref):
        acc = x_ref[0]
        for k in range(1, n):
            acc = acc + x_ref[k]
        o_ref[0] = acc

    return pl.pallas_call(
        body, name=name,
        grid_spec=pltpu.PrefetchScalarGridSpec(
            num_scalar_prefetch=1, grid=(R // tr,),
            in_specs=[pl.BlockSpec((n, tr, L), lambda i, h: (0, i, 0))],
            out_specs=pl.BlockSpec((1, tr, L), lambda i, h: (h[0], i, 0))),
        out_shape=jax.ShapeDtypeStruct((2, R, L), F32),
        compiler_params=pltpu.CompilerParams(dimension_semantics=("arbitrary",), vmem_limit_bytes=VMEM_LIMIT),
    )(jnp.reshape(half, (1,)).astype(jnp.int32), x)


def adamw(w, g, m, v, *, name, tr=512):
    R, L = w.shape
    tr = _rows(R, tr)

    def body(w_ref, g_ref, m_ref, v_ref, d_ref, mo_ref, vo_ref):
        g = g_ref[...]
        m = ADAM_B1 * m_ref[...] + (1.0 - ADAM_B1) * g
        v = ADAM_B2 * v_ref[...] + (1.0 - ADAM_B2) * (g * g)
        m_hat = m / (1.0 - ADAM_B1 ** ADAM_STEP)
        v_hat = v / (1.0 - ADAM_B2 ** ADAM_STEP)
        d_ref[...] = -ADAM_LR * (m_hat / (jnp.sqrt(v_hat) + ADAM_EPS) + ADAM_WD * w_ref[...])
        mo_ref[...] = m
        vo_ref[...] = v

    row = lambda i: (i, 0)
    return _call(body, name=name, grid=(R // tr,),
                 ins=[(a, (tr, L), row) for a in (w, g, m, v)],
                 outs=[((R, L), F32, (tr, L), row)] * 3)


def kernel(x, mix_norm, ffn_norm, hy_w_in, hy_w_gate2, hy_b_gate, hy_gla_norm, hy_sb_q_norm, hy_sb_k_norm, hy_w_out, cv_w_pw1, cv_b_pw1, cv_w_dw, cv_b_dw, cv_ln_g, cv_ln_b, cv_w_pw2, cv_b_pw2, ffn_w_gate, ffn_w_up, ffn_w_down, loss_target, m_mix_norm, m_ffn_norm, m_hy_w_in, m_hy_w_gate2, m_hy_b_gate, m_hy_gla_norm, m_hy_sb_q_norm, m_hy_sb_k_norm, m_hy_w_out, m_cv_w_pw1, m_cv_b_pw1, m_cv_w_dw, m_cv_b_dw, m_cv_ln_g, m_cv_ln_b, m_cv_w_pw2, m_cv_b_pw2, m_ffn_w_gate, m_ffn_w_up, m_ffn_w_down, v_mix_norm, v_ffn_norm, v_hy_w_in, v_hy_w_gate2, v_hy_b_gate, v_hy_gla_norm, v_hy_sb_q_norm, v_hy_sb_k_norm, v_hy_w_out, v_cv_w_pw1, v_cv_b_pw1, v_cv_w_dw, v_cv_b_dw, v_cv_ln_g, v_cv_ln_b, v_cv_w_pw2, v_cv_b_pw2, v_ffn_w_gate, v_ffn_w_up, v_ffn_w_down):
    w = dict(zip(WEIGHTS, (mix_norm, ffn_norm, hy_w_in, hy_w_gate2, hy_b_gate, hy_gla_norm, hy_sb_q_norm, hy_sb_k_norm, hy_w_out, cv_w_pw1, cv_b_pw1, cv_w_dw, cv_b_dw, cv_ln_g, cv_ln_b, cv_w_pw2, cv_b_pw2, ffn_w_gate, ffn_w_up, ffn_w_down)))
    m = dict(zip(WEIGHTS, (m_mix_norm, m_ffn_norm, m_hy_w_in, m_hy_w_gate2, m_hy_b_gate, m_hy_gla_norm, m_hy_sb_q_norm, m_hy_sb_k_norm, m_hy_w_out, m_cv_w_pw1, m_cv_b_pw1, m_cv_w_dw, m_cv_b_dw, m_cv_ln_g, m_cv_ln_b, m_cv_w_pw2, m_cv_b_pw2, m_ffn_w_gate, m_ffn_w_up, m_ffn_w_down)))
    v = dict(zip(WEIGHTS, (v_mix_norm, v_ffn_norm, v_hy_w_in, v_hy_w_gate2, v_hy_b_gate, v_hy_gla_norm, v_hy_sb_q_norm, v_hy_sb_k_norm, v_hy_w_out, v_cv_w_pw1, v_cv_b_pw1, v_cv_w_dw, v_cv_b_dw, v_cv_ln_g, v_cv_ln_b, v_cv_w_pw2, v_cv_b_pw2, v_ffn_w_gate, v_ffn_w_up, v_ffn_w_down)))
    shard_shapes = {n: w[n].shape for n in SHARD_AXIS}

    def unpack_gathered(buf, names):
        parts = [_unpack(buf[k], [shard_shapes[n] for n in names]) for k in range(N_CHIPS)]
        return {n: jnp.concatenate([parts[k][j] for k in range(N_CHIPS)], axis=SHARD_AXIS[n])
                for j, n in enumerate(names)}

    def pack_grads(Gr, names):
        pieces = [jnp.split(Gr[n], N_CHIPS, axis=SHARD_AXIS[n]) for n in names]
        per_chip = [_pack([p[k] for p in pieces], GRAD_ROWS, F32) for k in range(N_CHIPS)]
        Rh = per_chip[0].shape[0] // 2
        return jnp.stack([pc.reshape(2, Rh, LANES) for pc in per_chip], axis=1).reshape(2 * N_CHIPS, Rh, LANES)

    def finish_reduce(slots, tag):
        halves = sum_slots_into_half(slots, lax.axis_index("c"), name=f"reduce_{tag}_sum")
        return exchange(halves, 2, SHARE, name=f"reduce_{tag}_share", in_place=True).reshape(2 * slots.shape[1], LANES)

    first = exchange(_pack([w[n] for n in FIRST_BIG], 32, MXU)[None], N_CHIPS, GATHER, name="gather_first")
    small = exchange(_pack([w[n] for n in SMALL], 8, F32)[None], N_CHIPS, GATHER, name="gather_small")
    full = {**{n: w[n] for n in REPL}, **unpack_gathered(first, FIRST_BIG), **unpack_gathered(small, SMALL)}
    late = {"src": _pack([w[n] for n in LATE], 32, MXU)[None],
            "unpack": lambda buf: layout_late(unpack_gathered(buf, LATE)),
            "pack": lambda G: pack_grads(reference_late(G), LATE)}

    loss_lanes, dx, Gk, late_slots = local_step(x[0], loss_target[0], layout_first(full), late)
    Gr = reference_first(Gk)

    first_names = FIRST_BIG + SMALL
    first_slots = exchange(pack_grads(Gr, first_names), N_DEV, SCATTER, name="reduce_first")
    groups = [(first_names, finish_reduce(first_slots, "first"), "first"), (LATE, finish_reduce(late_slots, "late"), "late")]

    loss_row = jnp.pad(jnp.sum(loss_lanes).reshape(1), (0, LANES - 1))
    rep = _pack([Gr[n] for n in REPL] + [loss_row], 8, F32)
    rep_all = exchange(rep[None], N_DEV, ALL_TO_ALL, name="reduce_small")
    rep_sum = sum_slots(rep_all, name="reduce_small_sum", tr=rep.shape[0])
    rep_shapes = [w[n].shape for n in REPL]
    rep_n = sum(math.prod(s) for s in rep_shapes)
    loss = rep_sum.reshape(-1)[rep_n]

    out = {"grad": {}, "delta": {}, "new_m": {}, "new_v": {}}
    for names, g_flat, tag in groups:
        w_flat, m_flat, v_flat = (_pack([t[n] for n in names], GRAD_ROWS, F32) for t in (w, m, v))
        d_flat, mo_flat, vo_flat = adamw(w_flat, g_flat, m_flat, v_flat, name=f"adamw_{tag}")
        shapes = [shard_shapes[n] for n in names]
        for key, buf in (("grad", g_flat), ("delta", d_flat), ("new_m", mo_flat), ("new_v", vo_flat)):
            out[key].update(dict(zip(names, _unpack(buf, shapes))))
    rw = _pack([w[n] for n in REPL], 8, F32)
    rm = _pack([m[n] for n in REPL], 8, F32)
    rv = _pack([v[n] for n in REPL], 8, F32)
    rg = _pack(_unpack(rep_sum, rep_shapes), 8, F32)
    rd, rmo, rvo = adamw(rw, rg, rm, rv, name="adamw_small", tr=rw.shape[0])
    for key, buf in (("grad", rg), ("delta", rd), ("new_m", rmo), ("new_v", rvo)):
        out[key].update(dict(zip(REPL, _unpack(buf, rep_shapes))))

    return (loss, dx[None], *[out["grad"][n] for n in WEIGHTS], *[out["delta"][n] for n in WEIGHTS],
            *[out["new_m"][n] for n in WEIGHTS], *[out["new_v"][n] for n in WEIGHTS])
```

```python
import functools
import math

import numpy as np
import jax
import jax.numpy as jnp
from jax import lax
from jax.experimental import pallas as pl
from jax.experimental.pallas import tpu as pltpu

F32 = jnp.float32
MXU = jnp.bfloat16
EPS = 1e-6
LANES = 128
VMEM_LIMIT = 56 * 1024 * 1024

D = 1024
F = 2816
CHUNK = 64
GLA_H, GLA_K, GLA_V, GLA_RANK = 4, 64, 128, 16
SB_H, SB_D = 8, 64
CONV_W = 31
HALO = 32

ADAM_LR, ADAM_B1, ADAM_B2, ADAM_EPS, ADAM_WD, ADAM_STEP = 0.001, 0.9, 0.999, 1e-08, 0.01, 10

NN = (((1,), (0,)), ((), ()))
NT = (((1,), (1,)), ((), ()))
TN = (((0,), (0,)), ((), ()))


def _dot(a, b, dims=NN):
    return lax.dot_general(a.astype(MXU), b.astype(MXU), dims, preferred_element_type=F32)


def _dot_split(a, b, dims=NN, a_split=True):
    x = a if a_split else b
    hi = x.astype(MXU)
    lo = (x - hi.astype(F32)).astype(MXU)
    if a_split:
        return _dot(hi, b, dims) + _dot(lo, b, dims)
    return _dot(a, hi, dims) + _dot(a, lo, dims)


def _sigmoid(x):
    return 1.0 / (1.0 + jnp.exp(-x))


def _softplus_neg_abs(z):
    return jnp.log(1.0 + jnp.exp(-jnp.abs(z)))


def _call(body, *, name, grid, ins, outs, scratch=()):
    spec = lambda b, m: pl.BlockSpec(memory_space=pl.ANY) if b is None else pl.BlockSpec(b, m)
    res = pl.pallas_call(
        body,
        name=name,
        grid=grid,
        in_specs=[spec(b, m) for _, b, m in ins],
        out_specs=[spec(b, m) for _, _, b, m in outs],
        out_shape=[jax.ShapeDtypeStruct(s, d) for s, d, _, _ in outs],
        scratch_shapes=list(scratch),
        compiler_params=pltpu.CompilerParams(
            dimension_semantics=("arbitrary",) * len(grid), vmem_limit_bytes=VMEM_LIMIT),
    )(*[a for a, _, _ in ins])
    return res


def _rows(T, tm):
    tm = min(tm, T)
    assert T % tm == 0, (T, tm)
    return tm


def mm(pairs, *, name, trans_b=False, bias=None, res=None, out_dtype=F32, tm=512, tn=512):
    M = pairs[0][0].shape[0]
    N = pairs[0][1].shape[0] if trans_b else pairs[0][1].shape[1]
    tm = _rows(M, tm)
    tn = min(tn, N)
    assert N % tn == 0, (N, tn)
    np_ = len(pairs)

    def body(*refs):
        o_ref = refs[-1]
        acc = None
        for p in range(np_):
            d = _dot(refs[2 * p][...], refs[2 * p + 1][...], NT if trans_b else NN)
            acc = d if acc is None else acc + d
        k = 2 * np_
        if bias is not None:
            acc = acc + refs[k][...]
            k += 1
        if res is not None:
            acc = acc + refs[k][...]
        o_ref[...] = acc.astype(out_dtype)

    ins = []
    for a, b in pairs:
        K = a.shape[1]
        ins.append((a, (tm, K), lambda i, j: (i, 0)))
        if trans_b:
            ins.append((b, (tn, K), lambda i, j: (j, 0)))
        else:
            ins.append((b, (K, tn), lambda i, j: (0, j)))
    if bias is not None:
        ins.append((bias, (1, tn), lambda i, j: (0, j)))
    if res is not None:
        ins.append((res, (tm, tn), lambda i, j: (i, j)))
    return _call(body, name=name, grid=(M // tm, N // tn), ins=ins,
                 outs=[((M, N), out_dtype, (tm, tn), lambda i, j: (i, j))])[0]


def mm_tn(a, b, *, name, tm=512, tn=512, tk=1024):
    T, M = a.shape
    N = b.shape[1]
    tm, tn, tk = min(tm, M), min(tn, N), min(tk, T)
    assert M % tm == 0 and N % tn == 0 and T % tk == 0, (M, N, T, tm, tn, tk)

    def body(a_ref, b_ref, o_ref):
        @pl.when(pl.program_id(2) == 0)
        def _():
            o_ref[...] = jnp.zeros_like(o_ref)
        o_ref[...] += _dot(a_ref[...], b_ref[...], TN)

    return _call(body, name=name, grid=(M // tm, N // tn, T // tk),
                 ins=[(a, (tk, tm), lambda i, j, k: (k, i)), (b, (tk, tn), lambda i, j, k: (k, j))],
                 outs=[((M, N), F32, (tm, tn), lambda i, j, k: (i, j))])[0]


def rms_fwd(x, g, *, name, tm=256):
    T = x.shape[0]
    tm = _rows(T, tm)

    def body(x_ref, g_ref, o_ref):
        x = x_ref[...]
        r = lax.rsqrt(jnp.mean(x * x, axis=-1, keepdims=True) + EPS)
        o_ref[...] = (x * r * g_ref[...]).astype(MXU)

    return _call(body, name=name, grid=(T // tm,),
                 ins=[(x, (tm, D), lambda i: (i, 0)), (g, (1, D), lambda i: (0, 0))],
                 outs=[((T, D), MXU, (tm, D), lambda i: (i, 0))])[0]


def rms_bwd(x, g, dhn, dres, *, name, tm=256):
    T = x.shape[0]
    tm = _rows(T, tm)

    def body(x_ref, g_ref, d_ref, r_ref, dx_ref, dg_ref, cs_ref):
        @pl.when(pl.program_id(0) == 0)
        def _():
            dg_ref[...] = jnp.zeros_like(dg_ref)
            cs_ref[...] = jnp.zeros_like(cs_ref)
        x = x_ref[...]
        d = d_ref[...]
        r = lax.rsqrt(jnp.mean(x * x, axis=-1, keepdims=True) + EPS)
        dg_ref[...] += jnp.sum(d * x * r, axis=0, keepdims=True)
        t = d * g_ref[...]
        m = jnp.mean(t * x, axis=-1, keepdims=True)
        dx = r_ref[...] + t * r - x * (r * r * r) * m
        dx_ref[...] = dx
        cs_ref[...] += jnp.sum(dx, axis=0, keepdims=True)

    row = lambda i: (i, 0)
    fix = lambda i: (0, 0)
    return _call(body, name=name, grid=(T // tm,),
                 ins=[(x, (tm, D), row), (g, (1, D), fix), (dhn, (tm, D), row), (dres, (tm, D), row)],
                 outs=[((T, D), F32, (tm, D), row), ((1, D), F32, (1, D), fix), ((1, D), F32, (1, D), fix)])


def loss_grad(y, t, *, name, tm=256):
    T = y.shape[0]
    tm = _rows(T, tm)

    def body(y_ref, t_ref, dy_ref, l_ref):
        @pl.when(pl.program_id(0) == 0)
        def _():
            l_ref[...] = jnp.zeros_like(l_ref)
        e = y_ref[...] - t_ref[...]
        dy_ref[...] = e * (1.0 / D)
        l_ref[...] += jnp.sum(e * e, axis=0, keepdims=True) * (0.5 / D)

    row = lambda i: (i, 0)
    return _call(body, name=name, grid=(T // tm,),
                 ins=[(y, (tm, D), row), (t, (tm, D), row)],
                 outs=[((T, D), F32, (tm, D), row), ((1, D), F32, (1, D), lambda i: (0, 0))])


def ffn_up(hn, wg, wu, *, name, tm=1024, tn=256):
    T = hn.shape[0]
    tm = _rows(T, tm)

    def body(h_ref, wg_ref, wu_ref, g_ref, u_ref, a_ref):
        h = h_ref[...]
        g = _dot(h, wg_ref[...])
        u = _dot(h, wu_ref[...])
        g_ref[...] = g.astype(MXU)
        u_ref[...] = u.astype(MXU)
        a_ref[...] = (g * _sigmoid(g) * u).astype(MXU)

    tile = lambda i, j: (i, j)
    return _call(body, name=name, grid=(T // tm, F // tn),
                 ins=[(hn, (tm, D), lambda i, j: (i, 0)), (wg, (D, tn), lambda i, j: (0, j)),
                      (wu, (D, tn), lambda i, j: (0, j))],
                 outs=[((T, F), MXU, (tm, tn), tile), ((T, F), MXU, (tm, tn), tile),
                       ((T, F), MXU, (tm, tn), tile)])


def ffn_bwd_act(dy, wd, G, U, *, name, tm=1024, tn=256):
    T = dy.shape[0]
    tm = _rows(T, tm)

    def body(dy_ref, wd_ref, g_ref, u_ref, dg_ref, du_ref, a_ref):
        da = _dot(dy_ref[...], wd_ref[...], NT)
        g = g_ref[...].astype(F32)
        u = u_ref[...].astype(F32)
        s = _sigmoid(g)
        sil = g * s
        a_ref[...] = (sil * u).astype(MXU)
        du_ref[...] = (da * sil).astype(MXU)
        dg_ref[...] = (da * u * (s * (1.0 + g * (1.0 - s)))).astype(MXU)

    tile = lambda i, j: (i, j)
    return _call(body, name=name, grid=(T // tm, F // tn),
                 ins=[(dy, (tm, D), lambda i, j: (i, 0)), (wd, (tn, D), lambda i, j: (j, 0)),
                      (G, (tm, tn), tile), (U, (tm, tn), tile)],
                 outs=[((T, F), MXU, (tm, tn), tile), ((T, F), MXU, (tm, tn), tile),
                       ((T, F), MXU, (tm, tn), tile)])


def ffn_fwd(h, g_norm, wg, wu, wd, tag):
    hn = rms_fwd(h, g_norm, name=f"ffn{tag}_norm")
    G, U, act = ffn_up(hn, wg, wu, name=f"ffn{tag}_up")
    h_out = mm([(act, wd)], res=h, name=f"ffn{tag}_down")
    return h_out, (hn, G, U)


def ffn_bwd(dy, h, g_norm, wg, wu, wd, saved, tag):
    hn, G, U = saved
    dG, dU, act = ffn_bwd_act(dy, wd, G, U, name=f"ffn{tag}_bwd_act")
    d_wd = mm_tn(act, dy, name=f"ffn{tag}_dwd", tm=1408, tn=512)
    d_wg = mm_tn(hn, dG, name=f"ffn{tag}_dwg", tm=512, tn=1408)
    d_wu = mm_tn(hn, dU, name=f"ffn{tag}_dwu", tm=512, tn=1408)
    dhn = mm([(dG, wg), (dU, wu)], trans_b=True, name=f"ffn{tag}_dhn")
    dh, d_g, cs = rms_bwd(h, g_norm, dhn, dy, name=f"ffn{tag}_norm_bwd")
    return dh, d_g, d_wg, d_wu, d_wd, cs


SUBLANES = 8


def _window_scratch(tm):
    return [pltpu.VMEM((tm + HALO + SUBLANES, D), F32), pltpu.VMEM((SUBLANES, tm + HALO, D), F32)]


def _shift_copies(win, sh):
    rows = sh.shape[1]
    win[pl.ds(rows, SUBLANES), :] = jnp.zeros((SUBLANES, D), F32)
    for s in range(SUBLANES):
        sh[s] = win[pl.ds(s, rows), :]


def _tap(sh, off, rb):
    s = off % SUBLANES
    return sh[s, pl.ds(off - s, rb), :]


def conv_fwd(A, w_dw, b_dw, ln_g, ln_b, *, name, tm=128, rb=32):
    T = A.shape[0]
    tm = _rows(T, tm)

    def body(a_ref, ap_ref, w_ref, b_ref, g_ref, bb_ref, s_ref, c_ref, win, sh):
        i = pl.program_id(0)
        a = a_ref[...]
        win[pl.ds(HALO, tm), :] = a[:, :D] * _sigmoid(a[:, D:])
        ap = ap_ref[pl.ds(tm - HALO, HALO), :]
        up = ap[:, :D] * _sigmoid(ap[:, D:])
        win[pl.ds(0, HALO), :] = jnp.where(i > 0, up, 0.0)
        _shift_copies(win, sh)
        for r0 in range(0, tm, rb):
            acc = jnp.broadcast_to(b_ref[...], (rb, D))
            for k in range(CONV_W):
                acc = acc + w_ref[pl.ds(k, 1), :] * _tap(sh, r0 + k + HALO - (CONV_W - 1), rb)
            c_ref[pl.ds(r0, rb), :] = acc
        c = c_ref[...]
        mu = jnp.mean(c, axis=-1, keepdims=True)
        cc = c - mu
        var = jnp.mean(cc * cc, axis=-1, keepdims=True)
        z = cc * lax.rsqrt(var + EPS) * g_ref[...] + bb_ref[...]
        s_ref[...] = (z * _sigmoid(z)).astype(MXU)

    row = lambda i: (i, 0)
    fix = lambda i: (0, 0)
    return _call(body, name=name, grid=(T // tm,),
                 ins=[(A, (tm, 2 * D), row), (A, (tm, 2 * D), lambda i: (jnp.maximum(i - 1, 0), 0)),
                      (w_dw, (HALO, D), fix), (b_dw, (1, D), fix), (ln_g, (1, D), fix), (ln_b, (1, D), fix)],
                 outs=[((T, D), MXU, (tm, D), row), ((T, D), F32, (tm, D), row)],
                 scratch=_window_scratch(tm))


def ln_swish_bwd(c, ds, ln_g, ln_b, *, name, tm=256):
    T = c.shape[0]
    tm = _rows(T, tm)

    def body(c_ref, ds_ref, g_ref, b_ref, dc_ref, dg_ref, db_ref):
        @pl.when(pl.program_id(0) == 0)
        def _():
            dg_ref[...] = jnp.zeros_like(dg_ref)
            db_ref[...] = jnp.zeros_like(db_ref)
        c = c_ref[...]
        mu = jnp.mean(c, axis=-1, keepdims=True)
        cc = c - mu
        rstd = lax.rsqrt(jnp.mean(cc * cc, axis=-1, keepdims=True) + EPS)
        n = cc * rstd
        z = n * g_ref[...] + b_ref[...]
        s = _sigmoid(z)
        dz = ds_ref[...] * (s * (1.0 + z * (1.0 - s)))
        dg_ref[...] += jnp.sum(dz * n, axis=0, keepdims=True)
        db_ref[...] += jnp.sum(dz, axis=0, keepdims=True)
        dn = dz * g_ref[...]
        dc_ref[...] = rstd * (dn - jnp.mean(dn, axis=-1, keepdims=True)
                              - n * jnp.mean(dn * n, axis=-1, keepdims=True))

    row = lambda i: (i, 0)
    fix = lambda i: (0, 0)
    return _call(body, name=name, grid=(T // tm,),
                 ins=[(c, (tm, D), row), (ds, (tm, D), row), (ln_g, (1, D), fix), (ln_b, (1, D), fix)],
                 outs=[((T, D), F32, (tm, D), row), ((1, D), F32, (1, D), fix), ((1, D), F32, (1, D), fix)])


def conv_bwd(dc, A, w_dw, *, name, tm=128, rb=32):
    T = A.shape[0]
    tm = _rows(T, tm)
    n = T // tm
    SUB = 8

    def body(dc_ref, dn_ref, a_ref, ap_ref, w_ref, da_ref, dw_ref, dbd_ref, dbp_ref, wdc, sdc, wu, su, accw):
        i = pl.program_id(0)

        @pl.when(i == 0)
        def _():
            accw[...] = jnp.zeros_like(accw)
            dbd_ref[...] = jnp.zeros_like(dbd_ref)
            dbp_ref[...] = jnp.zeros_like(dbp_ref)
        a = a_ref[...]
        a1, a2 = a[:, :D], a[:, D:]
        sg = _sigmoid(a2)
        wu[pl.ds(HALO, tm), :] = a1 * sg
        ap = ap_ref[pl.ds(tm - HALO, HALO), :]
        wu[pl.ds(0, HALO), :] = jnp.where(i > 0, ap[:, :D] * _sigmoid(ap[:, D:]), 0.0)
        dc = dc_ref[...]
        wdc[pl.ds(0, tm), :] = dc
        wdc[pl.ds(tm, HALO), :] = jnp.where(i < n - 1, dn_ref[pl.ds(0, HALO), :], 0.0)
        dbd_ref[...] += jnp.sum(dc, axis=0, keepdims=True)
        _shift_copies(wdc, sdc)
        _shift_copies(wu, su)
        for r0 in range(0, tm, rb):
            du = jnp.zeros((rb, D), F32)
            dcs = wdc[pl.ds(r0, rb), :]
            for k in range(CONV_W):
                du = du + w_ref[pl.ds(k, 1), :] * _tap(sdc, r0 + (CONV_W - 1) - k, rb)
                p = dcs * _tap(su, r0 + k + HALO - (CONV_W - 1), rb)
                accw[pl.ds(SUB * k, SUB), :] += jnp.sum(p.reshape(rb // SUB, SUB, D), axis=0)
            s = sg[r0:r0 + rb]
            da_ref[pl.ds(r0, rb), pl.ds(0, D)] = (du * s).astype(MXU)
            da_ref[pl.ds(r0, rb), pl.ds(D, D)] = (du * a1[r0:r0 + rb] * s * (1.0 - s)).astype(MXU)
        da = da_ref[...].astype(F32)
        dbp_ref[...] += jnp.sum(da, axis=0, keepdims=True)

        @pl.when(i == n - 1)
        def _():
            dw_ref[...] = jnp.zeros_like(dw_ref)
            for k in range(CONV_W):
                dw_ref[pl.ds(k, 1), :] = jnp.sum(accw[pl.ds(SUB * k, SUB), :], axis=0, keepdims=True)

    row = lambda i: (i, 0)
    fix = lambda i: (0, 0)
    return _call(body, name=name, grid=(n,),
                 ins=[(dc, (tm, D), row), (dc, (tm, D), lambda i: (jnp.minimum(i + 1, n - 1), 0)),
                      (A, (tm, 2 * D), row), (A, (tm, 2 * D), lambda i: (jnp.maximum(i - 1, 0), 0)),
                      (w_dw, (HALO, D), fix)],
                 outs=[((T, 2 * D), MXU, (tm, 2 * D), row), ((HALO, D), F32, (HALO, D), fix),
                       ((1, D), F32, (1, D), fix), ((1, 2 * D), F32, (1, 2 * D), fix)],
                 scratch=_window_scratch(tm) + _window_scratch(tm) + [pltpu.VMEM((SUB * HALO, D), F32)])


def sb_prep(ps, gq, gk, *, name, tm=256):
    T = ps.shape[0]
    tm = _rows(T, tm)
    W = SB_H * SB_D

    def body(q_ref, k_ref, v_ref, gq_ref, gk_ref, qo, ko, vo):
        qs, ks = [], []
        for h in range(SB_H):
            sl = slice(h * SB_D, (h + 1) * SB_D)
            q = q_ref[:, sl]
            k = k_ref[:, sl]
            rq = lax.rsqrt(jnp.mean(q * q, axis=-1, keepdims=True) + EPS)
            rk = lax.rsqrt(jnp.mean(k * k, axis=-1, keepdims=True) + EPS)
            qs.append(q * rq * gq_ref[...] * (SB_D ** -0.5))
            ks.append(k * rk * gk_ref[...])
        qo[...] = jnp.concatenate(qs, axis=-1).astype(MXU)
        ko[...] = jnp.concatenate(ks, axis=-1).astype(MXU)
        vo[...] = v_ref[...].astype(MXU)

    row = lambda i: (i, 0)
    fix = lambda i: (0, 0)
    return _call(body, name=name, grid=(T // tm,),
                 ins=[(ps, (tm, W), lambda i: (i, 0)), (ps, (tm, W), lambda i: (i, 1)),
                      (ps, (tm, W), lambda i: (i, 2)), (gq, (1, SB_D), fix), (gk, (1, SB_D), fix)],
                 outs=[((T, W), MXU, (tm, W), row)] * 3)


def _sb_masks(tq):
    row = lax.broadcasted_iota(jnp.int32, (tq, tq), 0)
    col = lax.broadcasted_iota(jnp.int32, (tq, tq), 1)
    tri = jnp.where(row >= col, 1.0, 0.0).astype(MXU)
    past = col < row
    return tri, past


def _sb_scores(q, k, past):
    return _sb_logs(_dot(q, k, NT), past)


def _sb_logs(z, past):
    ls = jnp.minimum(z, 0.0) - _softplus_neg_abs(z)
    lk = ls - z
    if past is not None:
        lk = jnp.where(past, lk, 0.0)
    return ls, lk


def _fused_exchange(fused, steps_pred):
    if fused is None:
        return [], [], [], lambda refs: None, lambda refs: None
    src, n_dst, spec = fused
    n = len(spec["flips"])
    ins = [(src, None, None)]
    outs = [((n_dst,) + src.shape[1:], src.dtype, None, None)]
    scratch = [pltpu.SemaphoreType.DMA((n,)), pltpu.SemaphoreType.DMA((n,)), pltpu.SemaphoreType.DMA(())]

    def at_start(refs):
        @pl.when(steps_pred()[0])
        def _():
            _exchange_start(*refs, **spec)

    def at_end(refs):
        @pl.when(steps_pred()[1])
        def _():
            _exchange_wait(*refs, **spec)

    return ins, outs, scratch, at_start, at_end


PAIR = 2 * SB_D


def _split_pair(x2, lo):
    zero = jnp.zeros_like(x2)
    return [jnp.where(lo, x2, zero), jnp.where(lo, zero, x2)]


def sb_fwd(qn, kn, v, *, name, tq=256, pairs=2, fused=None):
    T, W = qn.shape
    tq = _rows(T, tq)
    wb, hg = pairs * PAIR, 2 * pairs
    assert T // tq <= LANES and W % wb == 0
    ng, ni = W // wb, T // tq
    pred = lambda: ((pl.program_id(0) == 0) & (pl.program_id(1) == 0),
                    (pl.program_id(0) == ng - 1) & (pl.program_id(1) == ni - 1))
    x_ins, x_outs, x_scratch, x_start, x_end = _fused_exchange(fused, pred)
    nx = len(x_ins)
    heads = range(hg)
    pcols = [slice(p * PAIR, (p + 1) * PAIR) for p in range(pairs)]

    def body(*refs):
        q_ref, k_ref, v_ref = refs[:3]
        o_ref, cb_ref = refs[3 + nx:5 + nx]
        x_refs = refs[3:3 + nx] + refs[5 + nx:]
        x_start(x_refs)
        i = pl.program_id(1)
        tri, past = _sb_masks(tq)
        lane = lax.broadcasted_iota(jnp.int32, (tq, LANES), 1)
        lo = lane < SB_D
        cb_ref[...] = jnp.zeros((hg, tq, LANES), F32)
        qs = [m for p in range(pairs) for m in _split_pair(q_ref[:, pcols[p]], lo)]

        def block(j, carry, masked):
            accs, cs = carry
            kblk = pl.ds(pl.multiple_of(j * tq, tq), tq)
            for h in heads:
                cb_ref[h] = jnp.where(lane == j, cs[h], cb_ref[h])
            zs = [_dot(qs[h], k_ref[kblk, pcols[h // 2]], NT) for h in heads]
            sc = [_sb_logs(zs[h], past if masked else None) for h in heads]
            bincs = [_dot_split(sc[h][1], tri) for h in heads]
            ws = [jnp.exp(sc[h][0] + cs[h] + bincs[h] - sc[h][1]) for h in heads]
            if masked:
                ws = [jnp.where(past, w, 0.0) for w in ws]
            pv = [_dot(ws[h], v_ref[kblk, pcols[h // 2]]) for h in heads]
            new_a = [accs[p] + jnp.where(lo, pv[2 * p], pv[2 * p + 1]) for p in range(pairs)]
            new_c = [cs[h] + jnp.sum(sc[h][1], axis=-1, keepdims=True) for h in heads]
            return tuple(new_a), tuple(new_c)

        carry = ((jnp.zeros((tq, PAIR), F32),) * pairs, (jnp.zeros((tq, 1), F32),) * hg)
        carry = block(i, carry, True)
        accs, _ = lax.fori_loop(0, i, lambda jj, car: block(i - 1 - jj, car, False), carry)
        for p in range(pairs):
            o_ref[:, pcols[p]] = accs[p]
        x_end(x_refs)

    blk = lambda g, i: (i, g)
    full = lambda g, i: (0, g)
    return _call(body, name=name, grid=(ng, ni),
                 ins=[(qn, (tq, wb), blk), (kn, (T, wb), full), (v, (T, wb), full)] + x_ins,
                 outs=[((T, W), F32, (tq, wb), blk),
                       ((W // SB_D, T, LANES), F32, (hg, tq, LANES), lambda g, i: (g, i, 0))] + x_outs,
                 scratch=x_scratch)


def sb_bwd(qn, kn, v, cb, dmix, do_col, *, name, tq=256, pairs=2, fused=None):
    T, W = qn.shape
    tq = _rows(T, tq)
    wb, hg = pairs * PAIR, 2 * pairs
    assert W % wb == 0 and do_col % wb == 0
    ng, ni = W // wb, T // tq
    pred = lambda: ((pl.program_id(0) == 0) & (pl.program_id(1) == 0),
                    (pl.program_id(0) == ng - 1) & (pl.program_id(1) == ni - 1))
    x_ins, x_outs, x_scratch, x_start, x_end = _fused_exchange(fused, pred)
    nx = len(x_ins)
    heads = range(hg)
    pcols = [slice(p * PAIR, (p + 1) * PAIR) for p in range(pairs)]

    def body(*refs):
        q_ref, k_ref, v_ref, cb_ref, do_ref = refs[:5]
        dq_ref, dk_hbm, dv_hbm = refs[5 + nx:8 + nx]
        dk_ref, dv_ref = refs[8 + 2 * nx:10 + 2 * nx]
        x_refs = refs[5:5 + nx] + refs[8 + nx:8 + 2 * nx] + refs[10 + 2 * nx:]
        x_start(x_refs)
        i = pl.program_id(1)

        @pl.when(i == 0)
        def _():
            dk_ref[...] = jnp.zeros_like(dk_ref)
            dv_ref[...] = jnp.zeros_like(dv_ref)
        tri, past = _sb_masks(tq)
        row = lax.broadcasted_iota(jnp.int32, (tq, tq), 0)
        col = lax.broadcasted_iota(jnp.int32, (tq, tq), 1)
        tri_lt = jnp.where(row < col, 1.0, 0.0).astype(MXU)
        lane = lax.broadcasted_iota(jnp.int32, (tq, LANES), 1)
        lo = lane < SB_D
        qs = [m for p in range(pairs) for m in _split_pair(q_ref[:, pcols[p]], lo)]
        dos = [m for p in range(pairs) for m in _split_pair(do_ref[:, pcols[p]].astype(MXU), lo)]

        def block(j, carry, masked):
            dqs, cgs = carry
            kblk = pl.ds(pl.multiple_of(j * tq, tq), tq)
            k2 = [k_ref[kblk, pcols[p]] for p in range(pairs)]
            zs = [_dot(qs[h], k2[h // 2], NT) for h in heads]
            dws = [_dot(dos[h], v_ref[kblk, pcols[h // 2]], NT) for h in heads]
            sc = [_sb_logs(zs[h], past if masked else None) for h in heads]
            cs = [jnp.sum(jnp.where(lane == j, cb_ref[h], 0.0), axis=-1, keepdims=True) for h in heads]
            bincs = [_dot_split(sc[h][1], tri) for h in heads]
            ws = [jnp.exp(sc[h][0] + cs[h] + bincs[h] - sc[h][1]) for h in heads]
            if masked:
                ws = [jnp.where(past, w, 0.0) for w in ws]
            wbs = [w.astype(MXU) for w in ws]
            gs = [ws[h] * dws[h] for h in heads]
            gpres = [cgs[h] + _dot(gs[h], tri_lt) for h in heads]
            sigs = [jnp.exp(sc[h][0]) for h in heads]
            dzs = [gs[h] - sigs[h] * (gs[h] + gpres[h]) for h in heads]
            if masked:
                dzs = [jnp.where(past, dz, 0.0) for dz in dzs]
            dzbs = [dz.astype(MXU) for dz in dzs]
            dqp = [_dot(dzbs[h], k2[h // 2]) for h in heads]
            new_dq = [dqs[p] + jnp.where(lo, dqp[2 * p], dqp[2 * p + 1]) for p in range(pairs)]
            for p in range(pairs):
                dk_ref[kblk, pcols[p]] += _dot(dzbs[2 * p], qs[2 * p], TN) + _dot(dzbs[2 * p + 1], qs[2 * p + 1], TN)
            for p in range(pairs):
                dv_ref[kblk, pcols[p]] += _dot(wbs[2 * p], dos[2 * p], TN) + _dot(wbs[2 * p + 1], dos[2 * p + 1], TN)
            new_cg = [cgs[h] + jnp.sum(gs[h], axis=-1, keepdims=True) for h in heads]
            return tuple(new_dq), tuple(new_cg)

        carry = ((jnp.zeros((tq, PAIR), F32),) * pairs, (jnp.zeros((tq, 1), F32),) * hg)
        carry = lax.fori_loop(0, i, lambda j, car: block(j, car, False), carry)
        dqs, _ = block(i, carry, True)
        for p in range(pairs):
            dq_ref[:, pcols[p]] = dqs[p]

        cols = pl.ds(pl.multiple_of(pl.program_id(0) * wb, wb), wb)

        @pl.when(i == ni - 1)
        def _():
            pltpu.sync_copy(dk_ref, dk_hbm.at[:, cols])
            pltpu.sync_copy(dv_ref, dv_hbm.at[:, cols])
        x_end(x_refs)

    blk = lambda g, i: (i, g)
    full = lambda g, i: (0, g)
    return _call(body, name=name, grid=(ng, ni),
                 ins=[(qn, (tq, wb), blk), (kn, (T, wb), full), (v, (T, wb), full),
                      (cb, (hg, tq, LANES), lambda g, i: (g, i, 0)),
                      (dmix, (tq, wb), lambda g, i: (i, do_col // wb + g))] + x_ins,
                 outs=[((T, W), F32, (tq, wb), blk), ((T, W), F32, None, None), ((T, W), F32, None, None)] + x_outs,
                 scratch=[pltpu.VMEM((T, wb), F32), pltpu.VMEM((T, wb), F32)] + x_scratch)


def sb_post_bwd(dqn, dkn, dv, ps, gq, gk, *, name, tm=256):
    T = ps.shape[0]
    tm = _rows(T, tm)
    W = SB_H * SB_D

    def body(dq_ref, dk_ref, dv_ref, q_ref, k_ref, gq_ref, gk_ref, out_ref, dgq_ref, dgk_ref):
        @pl.when(pl.program_id(0) == 0)
        def _():
            dgq_ref[...] = jnp.zeros_like(dgq_ref)
            dgk_ref[...] = jnp.zeros_like(dgk_ref)

        def norm_bwd(x, d, g):
            r = lax.rsqrt(jnp.mean(x * x, axis=-1, keepdims=True) + EPS)
            dg = jnp.sum(d * x * r, axis=0, keepdims=True)
            t = d * g
            return t * r - x * (r * r * r) * jnp.mean(t * x, axis=-1, keepdims=True), dg

        dqs, dks = [], []
        dgq = jnp.zeros((1, SB_D), F32)
        dgk = jnp.zeros((1, SB_D), F32)
        for h in range(SB_H):
            sl = slice(h * SB_D, (h + 1) * SB_D)
            a, ga = norm_bwd(q_ref[:, sl], dq_ref[:, sl] * (SB_D ** -0.5), gq_ref[...])
            b, gb = norm_bwd(k_ref[:, sl], dk_ref[:, sl], gk_ref[...])
            dqs.append(a)
            dks.append(b)
            dgq = dgq + ga
            dgk = dgk + gb
        out_ref[...] = jnp.concatenate(dqs + dks + [dv_ref[...]], axis=-1).astype(MXU)
        dgq_ref[...] += dgq
        dgk_ref[...] += dgk

    row = lambda i: (i, 0)
    fix = lambda i: (0, 0)
    return _call(body, name=name, grid=(T // tm,),
                 ins=[(dqn, (tm, W), row), (dkn, (tm, W), row), (dv, (tm, W), row),
                      (ps, (tm, W), lambda i: (i, 0)), (ps, (tm, W), lambda i: (i, 1)),
                      (gq, (1, SB_D), fix), (gk, (1, SB_D), fix)],
                 outs=[((T, 3 * W), MXU, (tm, 3 * W), lambda i: (i, 0)),
                       ((1, SB_D), F32, (1, SB_D), fix), ((1, SB_D), F32, (1, SB_D), fix)])


GLA_TM = 512
GLA_DK = GLA_H * GLA_K
GLA_DV = GLA_H * GLA_V


def _gla_masks(tm):
    row = lax.broadcasted_iota(jnp.int32, (tm, tm), 0)
    col = lax.broadcasted_iota(jnp.int32, (tm, tm), 1)
    same = (row // CHUNK) == (col // CHUNK)
    return row, col, same


def _gla_gate(glr, w2, b, tm):
    pre = _dot(glr, w2) + b
    la = (jnp.minimum(pre, 0.0) - _softplus_neg_abs(pre)) * (1.0 / 16.0)
    row, col, same = _gla_masks(tm)
    m_incl = jnp.where(same & (col <= row), 1.0, 0.0).astype(MXU)
    m_full = jnp.where(same, 1.0, 0.0).astype(MXU)
    bc = _dot_split(m_incl, la, a_split=False)
    tot = _dot_split(m_full, la, a_split=False)
    return pre, bc, tot


def gla_fwd(pg, glr, w2, b_gate, g_out, *, name):
    T = pg.shape[0]
    tm = _rows(T, GLA_TM)
    ncb = tm // CHUNK
    NC = T // CHUNK

    def body(q_ref, k_ref, v_ref, r_ref, l_ref, w2_ref, b_ref, g_ref, o_ref, st_ref, S):
        @pl.when(pl.program_id(0) == 0)
        def _():
            S[...] = jnp.zeros_like(S)
        _, bc, tot = _gla_gate(l_ref[...], w2_ref[...], b_ref[...], tm)
        kend = k_ref[...] * jnp.exp(tot - bc)
        qs = q_ref[...] * (GLA_K ** -0.5)
        a_all = jnp.exp(tot)
        v = v_ref[...]
        r = r_ref[...]
        rows = [slice(c * CHUNK, (c + 1) * CHUNK) for c in range(ncb)]
        hks = [slice(h * GLA_K, (h + 1) * GLA_K) for h in range(GLA_H)]
        hvs = [slice(h * GLA_V, (h + 1) * GLA_V) for h in range(GLA_H)]
        uts = [[_dot(v[rows[c], hvs[h]], kend[rows[c], hks[h]], TN) for h in range(GLA_H)] for c in range(ncb)]
        for h in range(GLA_H):
            s = S[h]
            for c in range(ncb):
                s = s * a_all[c * CHUNK:c * CHUNK + 1, hks[h]] + uts[c][h]
                st_ref[c, h] = s
            S[h] = s
        for c in range(ncb):
            outs = []
            for h in range(GLA_H):
                o = _dot(qs[rows[c], hks[h]], st_ref[c, h], NT)
                rinv = lax.rsqrt(jnp.mean(o * o, axis=-1, keepdims=True) + EPS)
                rr = r[rows[c], hvs[h]]
                outs.append(o * rinv * g_ref[...] * (rr * _sigmoid(rr)))
            o_ref[pl.ds(c * CHUNK, CHUNK), :] = jnp.concatenate(outs, axis=-1).astype(MXU)

    fix = lambda i: (0, 0)
    return _call(body, name=name, grid=(T // tm,),
                 ins=[(pg, (tm, GLA_DK), lambda i: (i, 0)), (pg, (tm, GLA_DK), lambda i: (i, 1)),
                      (pg, (tm, GLA_DV), lambda i: (i, 1)), (pg, (tm, GLA_DV), lambda i: (i, 2)),
                      (glr, (tm, LANES), lambda i: (i, 0)), (w2, (LANES, GLA_DK), fix),
                      (b_gate, (1, GLA_DK), fix), (g_out, (1, GLA_V), fix)],
                 outs=[((T, GLA_DV), MXU, (tm, GLA_DV), lambda i: (i, 0)),
                       ((NC, GLA_H, GLA_V, GLA_K), F32, (ncb, GLA_H, GLA_V, GLA_K), lambda i: (i, 0, 0, 0))],
                 scratch=[pltpu.VMEM((GLA_H, GLA_V, GLA_K), F32)])


def gla_bwd(pg, glr, w2, b_gate, g_out, st, dmix, *, name):
    T = pg.shape[0]
    tm = _rows(T, GLA_TM)
    ncb = tm // CHUNK
    n = T // tm

    def body(q_ref, k_ref, v_ref, r_ref, l_ref, w2_ref, b_ref, g_ref, st_ref, sp_ref, d_ref,
             dpg_ref, dl_ref, dw2_ref, db_ref, dg_ref, dS, dkend, extra, dst_ref):
        i = pl.program_id(0)

        @pl.when(i == 0)
        def _():
            dS[...] = jnp.zeros_like(dS)
            dw2_ref[...] = jnp.zeros_like(dw2_ref)
            db_ref[...] = jnp.zeros_like(db_ref)
            dg_ref[...] = jnp.zeros_like(dg_ref)
        first_tile = i == n - 1
        glr_v = l_ref[...]
        pre, bc, tot = _gla_gate(glr_v, w2_ref[...], b_ref[...], tm)
        dec = jnp.exp(tot - bc)
        k = k_ref[...]
        kend = k * dec
        qs = q_ref[...] * (GLA_K ** -0.5)
        a_all = jnp.exp(tot)
        v = v_ref[...]
        r = r_ref[...]
        g = g_ref[...]
        dgg = jnp.zeros((1, GLA_V), F32)
        rows = [slice(c * CHUNK, (c + 1) * CHUNK) for c in range(ncb)]
        hks = [slice(h * GLA_K, (h + 1) * GLA_K) for h in range(GLA_H)]
        hvs = [slice(h * GLA_V, (h + 1) * GLA_V) for h in range(GLA_H)]
        heads = range(GLA_H)
        qdo = [[None] * GLA_H for _ in range(ncb)]
        for c in range(ncb):
            cr = pl.ds(c * CHUNK, CHUNK)
            dq_l, dr_l = [], []
            for h in heads:
                s_c = st_ref[c, h]
                qh = qs[rows[c], hks[h]]
                o = _dot(qh, s_c, NT)
                rinv = lax.rsqrt(jnp.mean(o * o, axis=-1, keepdims=True) + EPS)
                nrm = o * rinv
                rr = r[rows[c], hvs[h]]
                sg = _sigmoid(rr)
                sil = rr * sg
                d = d_ref[cr, pl.ds(h * GLA_V, GLA_V)]
                dr_l.append(d * nrm * g * (sg * (1.0 + rr * (1.0 - sg))))
                dgg = dgg + jnp.sum(d * sil * nrm, axis=0, keepdims=True)
                dn = d * sil * g
                do = rinv * (dn - nrm * jnp.mean(dn * nrm, axis=-1, keepdims=True))
                dq_l.append(_dot(do, s_c) * (GLA_K ** -0.5))
                qdo[c][h] = _dot(do, qh, TN)
            dpg_ref[cr, pl.ds(0, GLA_DK)] = jnp.concatenate(dq_l, axis=-1).astype(MXU)
            dpg_ref[cr, pl.ds(2 * GLA_DK + GLA_DV, GLA_DV)] = jnp.concatenate(dr_l, axis=-1).astype(MXU)
        ex = [[None] * GLA_H for _ in range(ncb)]
        for h in heads:
            ds_h = dS[h]
            for c in reversed(range(ncb)):
                dst = ds_h + qdo[c][h]
                dst_ref[c, h] = dst
                s_p = st_ref[c - 1, h] if c > 0 else jnp.where(first_tile, 0.0, sp_ref[0, h])
                a = a_all[c * CHUNK:c * CHUNK + 1, hks[h]]
                da = jnp.sum(dst * s_p, axis=0, keepdims=True)
                ds_h = dst * a
                ex[c][h] = jnp.broadcast_to(da * a, (CHUNK, GLA_K))
            dS[h] = ds_h
        for c in range(ncb):
            extra[pl.ds(c * CHUNK, CHUNK), :] = jnp.concatenate(ex[c], axis=-1)
        for c in range(ncb):
            cr = pl.ds(c * CHUNK, CHUNK)
            dk_l = [_dot(v[rows[c], hvs[h]], dst_ref[c, h]) for h in heads]
            dv_l = [_dot(kend[rows[c], hks[h]], dst_ref[c, h], NT) for h in heads]
            dpg_ref[cr, pl.ds(2 * GLA_DK, GLA_DV)] = jnp.concatenate(dv_l, axis=-1).astype(MXU)
            dkend[cr, :] = jnp.concatenate(dk_l, axis=-1)
        dke = dkend[...]
        dpg_ref[:, pl.ds(GLA_DK, GLA_DK)] = (dke * dec).astype(MXU)
        e = dke * kend
        row, col, same = _gla_masks(tm)
        m_lt = jnp.where(same & (col < row), 1.0, 0.0).astype(MXU)
        dla = _dot_split(m_lt, e, a_split=False) + extra[...]
        sp = _softplus_neg_abs(pre)
        one_m_sig = jnp.exp(-jnp.maximum(pre, 0.0) - sp)
        dpre = dla * (1.0 / 16.0) * one_m_sig
        dl_ref[...] = _dot(dpre, w2_ref[...], NT).astype(MXU)
        dw2_ref[...] += _dot(glr_v, dpre, TN)
        db_ref[...] += jnp.sum(dpre, axis=0, keepdims=True)
        dg_ref[...] += dgg

    fix = lambda i: (0, 0)
    rev = lambda i: n - 1 - i
    return _call(body, name=name, grid=(n,),
                 ins=[(pg, (tm, GLA_DK), lambda i: (rev(i), 0)), (pg, (tm, GLA_DK), lambda i: (rev(i), 1)),
                      (pg, (tm, GLA_DV), lambda i: (rev(i), 1)), (pg, (tm, GLA_DV), lambda i: (rev(i), 2)),
                      (glr, (tm, LANES), lambda i: (rev(i), 0)), (w2, (LANES, GLA_DK), fix),
                      (b_gate, (1, GLA_DK), fix), (g_out, (1, GLA_V), fix),
                      (st, (ncb, GLA_H, GLA_V, GLA_K), lambda i: (rev(i), 0, 0, 0)),
                      (st, (1, GLA_H, GLA_V, GLA_K), lambda i: (jnp.maximum(rev(i) * ncb - 1, 0), 0, 0, 0)),
                      (dmix, (tm, GLA_DV), lambda i: (rev(i), 0))],
                 outs=[((T, 2 * GLA_DK + 2 * GLA_DV), MXU, (tm, 2 * GLA_DK + 2 * GLA_DV), lambda i: (rev(i), 0)),
                       ((T, LANES), MXU, (tm, LANES), lambda i: (rev(i), 0)),
                       ((LANES, GLA_DK), F32, (LANES, GLA_DK), fix),
                       ((1, GLA_DK), F32, (1, GLA_DK), fix), ((1, GLA_V), F32, (1, GLA_V), fix)],
                 scratch=[pltpu.VMEM((GLA_H, GLA_V, GLA_K), F32), pltpu.VMEM((tm, GLA_DK), F32),
                          pltpu.VMEM((tm, GLA_DK), F32), pltpu.VMEM((ncb, GLA_H, GLA_V, GLA_K), F32)])


def local_step(x, tgt, W, late=None):
    row1 = lambda a, l: a[l:l + 1]
    hn0 = rms_fwd(x, row1(W["mix_norm"], 0), name="l0_norm")
    pg = mm([(hn0, W["wi_g"])], name="l0_proj_gla", tn=512)
    ps = mm([(hn0, W["wi_s"])], name="l0_proj_sb", tn=512)
    glr = mm([(hn0, W["wi_l"])], name="l0_proj_gate", out_dtype=MXU)
    og, st = gla_fwd(pg, glr, W["w2"], W["b_gate"], W["g_gla"], name="gla_fwd")
    qn, kn, vh = sb_prep(ps, W["g_q"], W["g_k"], name="sb_prep")
    if late is None:
        osb, sb_cb = sb_fwd(qn, kn, vh, name="sb_fwd")
    else:
        osb, sb_cb, gathered = sb_fwd(qn, kn, vh, name="sb_fwd", fused=(late["src"], N_CHIPS, GATHER))
        W = {**W, **late["unpack"](gathered)}
    h1 = mm([(og, W["wo_g"]), (osb, W["wo_s"])], res=x, name="l0_out")
    h2, ffn0 = ffn_fwd(h1, row1(W["ffn_norm"], 0), W["wg0"], W["wu0"], W["wd0"], 0)
    hn1 = rms_fwd(h2, row1(W["mix_norm"], 1), name="l1_norm")
    A = mm([(hn1, W["pw1"])], bias=W["b_pw1"], name="l1_pw1")
    s, cconv = conv_fwd(A, W["w_dw"], W["b_dw"], W["ln_g"], W["ln_b"], name="conv_fwd")
    h3 = mm([(s, W["pw2"])], bias=W["b_pw2"], res=h2, name="l1_pw2")
    y, ffn1 = ffn_fwd(h3, row1(W["ffn_norm"], 1), W["wg1"], W["wu1"], W["wd1"], 1)
    dy, loss_lanes = loss_grad(y, tgt, name="loss")
    G = {}
    dh3, g_fn1, G["wg1"], G["wu1"], G["wd1"], cs3 = ffn_bwd(
        dy, h3, row1(W["ffn_norm"], 1), W["wg1"], W["wu1"], W["wd1"], ffn1, 1)
    G["b_pw2"] = cs3
    ds = mm([(dh3, W["pw2"])], trans_b=True, name="l1_ds")
    G["pw2"] = mm_tn(s, dh3, name="l1_dpw2", tm=1024, tn=1024)
    dc, G["ln_g"], G["ln_b"] = ln_swish_bwd(cconv, ds, W["ln_g"], W["ln_b"], name="ln_bwd")
    dA, G["w_dw"], G["b_dw"], G["b_pw1"] = conv_bwd(dc, A, W["w_dw"], name="conv_bwd")
    G["pw1"] = mm_tn(hn1, dA, name="l1_dpw1", tm=1024, tn=1024)
    dhn1 = mm([(dA, W["pw1"])], trans_b=True, name="l1_dhn")
    dh2, g_mn1, _ = rms_bwd(h2, row1(W["mix_norm"], 1), dhn1, dh3, name="l1_norm_bwd")
    dh1, g_fn0, G["wg0"], G["wu0"], G["wd0"], _ = ffn_bwd(
        dh2, h1, row1(W["ffn_norm"], 0), W["wg0"], W["wu0"], W["wd0"], ffn0, 0)
    dmix = mm([(dh1, W["wo_gs"])], trans_b=True, name="l0_dmix")
    G["wo_g"] = mm_tn(og, dh1, name="l0_dwo_g", tm=512, tn=1024)
    G["wo_s"] = mm_tn(osb, dh1, name="l0_dwo_s", tm=512, tn=1024)
    slots = None
    if late is None:
        dqn, dkn, dvh = sb_bwd(qn, kn, vh, sb_cb, dmix, GLA_DV, name="sb_bwd")
    else:
        dqn, dkn, dvh, slots = sb_bwd(qn, kn, vh, sb_cb, dmix, GLA_DV, name="sb_bwd",
                                      fused=(late["pack"](G), N_DEV, SCATTER))
    dps, G["g_q"], G["g_k"] = sb_post_bwd(dqn, dkn, dvh, ps, W["g_q"], W["g_k"], name="sb_post_bwd")
    dpg, dglr, G["w2"], G["b_gate"], G["g_gla"] = gla_bwd(
        pg, glr, W["w2"], W["b_gate"], W["g_gla"], st, dmix, name="gla_bwd")
    G["wi_g"] = mm_tn(hn0, dpg, name="l0_dwi_g", tm=1024, tn=512)
    G["wi_s"] = mm_tn(hn0, dps, name="l0_dwi_s", tm=1024, tn=512)
    G["wi_l"] = mm_tn(hn0, dglr, name="l0_dwi_l", tm=1024, tn=LANES)
    dhn0 = mm([(dpg, W["wi_g"]), (dps, W["wi_s"]), (dglr, W["wi_l"])], trans_b=True, name="l0_dhn")
    dx, g_mn0, _ = rms_bwd(x, row1(W["mix_norm"], 0), dhn0, dh1, name="l0_norm_bwd")
    G["mix_norm"] = jnp.concatenate([g_mn0, g_mn1], axis=0)
    G["ffn_norm"] = jnp.concatenate([g_fn0, g_fn1], axis=0)
    return loss_lanes, dx, G, slots


_C_GLA = 2 * GLA_DK + 2 * GLA_DV
_C_SB0 = _C_GLA + GLA_RANK


ITEMS = {
    "w_in": ("hy_w_in", 0, (D, _C_SB0 + 3 * SB_H * SB_D), 1), "wo_gs": ("hy_w_out", 0, (GLA_DV + SB_H * SB_D, D), 0),
    "w2r": ("hy_w_gate2", 0, (GLA_RANK, GLA_DK), 1), "b_pw1": ("cv_b_pw1", None, (1, 2 * D), 1),
    "w_dwr": ("cv_w_dw", 0, (CONV_W, D), 1), "b_dw": ("cv_b_dw", None, (1, D), 1),
    "ln_g": ("cv_ln_g", None, (1, D), 1), "ln_b": ("cv_ln_b", None, (1, D), 1), "b_pw2": ("cv_b_pw2", None, (1, D), 1),
    "pw1": ("cv_w_pw1", 0, (D, 2 * D), 1), "pw2": ("cv_w_pw2", 0, (D, D), 0),
    "wg0": ("ffn_w_gate", 0, (D, F), 1), "wg1": ("ffn_w_gate", 1, (D, F), 1),
    "wu0": ("ffn_w_up", 0, (D, F), 1), "wu1": ("ffn_w_up", 1, (D, F), 1),
    "wd0": ("ffn_w_down", 0, (F, D), 0), "wd1": ("ffn_w_down", 1, (F, D), 0),
}
FIRST_BIG_ITEMS = ["w_in", "wo_gs"]
SMALL_ITEMS = ["w2r", "b_pw1", "w_dwr", "b_dw", "ln_g", "ln_b", "b_pw2"]
LATE_ITEMS = ["pw1", "pw2", "wg0", "wg1", "wu0", "wu1", "wd0", "wd1"]


def layout_first(raw, repl):
    w_in = raw["w_in"]
    W = {
        "wi_g": w_in[:, :_C_GLA].astype(MXU),
        "wi_s": w_in[:, _C_SB0:].astype(MXU),
        "wi_l": jnp.pad(w_in[:, _C_GLA:_C_SB0], ((0, 0), (0, LANES - GLA_RANK))).astype(MXU),
        "w2": jnp.pad(raw["w2r"], ((0, LANES - GLA_RANK), (0, 0))),
        "b_gate": repl["hy_b_gate"], "g_gla": repl["hy_gla_norm"],
        "g_q": repl["hy_sb_q_norm"], "g_k": repl["hy_sb_k_norm"],
        "wo_gs": raw["wo_gs"].astype(MXU),
        "b_pw1": raw["b_pw1"], "w_dw": jnp.pad(raw["w_dwr"], ((0, HALO - CONV_W), (0, 0))),
        "b_dw": raw["b_dw"], "ln_g": raw["ln_g"], "ln_b": raw["ln_b"], "b_pw2": raw["b_pw2"],
        "mix_norm": repl["mix_norm"], "ffn_norm": repl["ffn_norm"],
    }
    W["wo_g"] = W["wo_gs"][:GLA_DV]
    W["wo_s"] = W["wo_gs"][GLA_DV:]
    return W


def layout_late(raw):
    return {k: raw[k].astype(MXU) for k in LATE_ITEMS}


def item_grads(G):
    out = {k: G[k] for k in ITEMS if k in G}
    if "wi_g" in G:
        out["w_in"] = jnp.concatenate([G["wi_g"], G["wi_l"][:, :GLA_RANK], G["wi_s"]], axis=1)
        out["wo_gs"] = jnp.concatenate([G["wo_g"], G["wo_s"]], axis=0)
        out["w2r"] = G["w2"][:GLA_RANK]
        out["w_dwr"] = G["w_dw"][:CONV_W]
    return out


def _chip_major(a, ax):
    r, c = a.shape
    if ax == 0:
        return a.reshape(N_CHIPS, -1)
    return a.reshape(r, N_CHIPS, c // N_CHIPS).transpose(1, 0, 2).reshape(N_CHIPS, -1)


def _from_chip_major(p, shape, ax):
    r, c = shape
    if ax == 0:
        return p.reshape(r, c)
    return p.reshape(N_CHIPS, r, c // N_CHIPS).transpose(1, 0, 2).reshape(r, c)


def unpack_items(buf, keys):
    flat = buf.reshape(N_CHIPS, -1)
    out, off = {}, 0
    for k in keys:
        _, _, shape, ax = ITEMS[k]
        n = math.prod(shape) // N_CHIPS
        out[k] = _from_chip_major(flat[:, off:off + n], shape, ax)
        off += n
    return out


def pack_item_grads(raw, keys):
    cat = jnp.concatenate([_chip_major(raw[k].astype(F32), ITEMS[k][3]) for k in keys], axis=1)
    n = cat.shape[1]
    rows = -(-n // (LANES * GRAD_ROWS)) * GRAD_ROWS
    return jnp.pad(cat, ((0, 0), (0, rows * LANES - n))).reshape(2 * N_CHIPS, rows // 2, LANES)


def _refs_of(keys):
    return list(dict.fromkeys(ITEMS[k][0] for k in keys))


WEIGHTS = ["mix_norm", "ffn_norm", "hy_w_in", "hy_w_gate2", "hy_b_gate", "hy_gla_norm", "hy_sb_q_norm",
           "hy_sb_k_norm", "hy_w_out", "cv_w_pw1", "cv_b_pw1", "cv_w_dw", "cv_b_dw", "cv_ln_g", "cv_ln_b",
           "cv_w_pw2", "cv_b_pw2", "ffn_w_gate", "ffn_w_up", "ffn_w_down"]
REPL =["mix_norm", "ffn_norm", "hy_b_gate", "hy_gla_norm", "hy_sb_q_norm", "hy_sb_k_norm"]
N_CHIPS = 4
N_DEV = 8
GRAD_ROWS = 1024


def _pack(arrs, rows_multiple, dtype):
    flat = jnp.concatenate([a.reshape(-1).astype(dtype) for a in arrs])
    n = flat.shape[0]
    rows = -(-n // (LANES * rows_multiple)) * rows_multiple
    return jnp.pad(flat, (0, rows * LANES - n)).reshape(rows, LANES)


def _unpack(flat2d, shapes):
    flat = flat2d.reshape(-1)
    out, off = [], 0
    for s in shapes:
        n = math.prod(s)
        out.append(flat[off:off + n].reshape(s))
        off += n
    return out


def _coords():
    return lax.axis_index("x"), lax.axis_index("y"), lax.axis_index("c")


def _flip(pos, f):
    return tuple(1 - p if b else p for p, b in zip(pos, f))


def _exchange_copies(src_ref, dst_ref, send_sems, recv_sems, loc_sem, *, flips, src_idx, dst_idx, local_idx,
                     with_recvs=True):
    me = _coords()
    loc = None
    if local_idx is not None:
        si, di = local_idx(me)
        loc = pltpu.make_async_copy(src_ref.at[si], dst_ref.at[di], loc_sem)
    sends, recvs = [], []
    for k, f in enumerate(flips):
        peer = _flip(me, f)
        sends.append(pltpu.make_async_remote_copy(
            src_ref=src_ref.at[src_idx(me, peer)], dst_ref=dst_ref.at[dst_idx(me)],
            send_sem=send_sems.at[k], recv_sem=recv_sems.at[k],
            device_id=peer, device_id_type=pl.DeviceIdType.MESH))
        if with_recvs:
            recvs.append(pltpu.make_async_remote_copy(
                src_ref=src_ref.at[src_idx(peer, me)], dst_ref=dst_ref.at[dst_idx(peer)],
                send_sem=send_sems.at[k], recv_sem=recv_sems.at[k],
                device_id=peer, device_id_type=pl.DeviceIdType.MESH))
    return loc, sends, recvs


def _exchange_start(*refs, **spec):
    loc, sends, _ = _exchange_copies(*refs, with_recvs=False, **spec)
    if loc is not None:
        loc.start()
    for s in sends:
        s.start()


def _exchange_wait(*refs, **spec):
    loc, sends, recvs = _exchange_copies(*refs, **spec)
    for s in sends:
        s.wait_send()
    for r in recvs:
        r.wait_recv()
    if loc is not None:
        loc.wait()


def exchange(src, n_dst, spec, *, name, in_place=False):
    n = len(spec["flips"])
    assert not in_place or (n_dst == src.shape[0] and spec["local_idx"] is None)

    def body(*refs):
        _exchange_start(*refs, **spec)
        _exchange_wait(*refs, **spec)

    return pl.pallas_call(
        body, name=name,
        out_shape=jax.ShapeDtypeStruct((n_dst,) + src.shape[1:], src.dtype),
        in_specs=[pl.BlockSpec(memory_space=pl.ANY)],
        out_specs=pl.BlockSpec(memory_space=pl.ANY),
        scratch_shapes=[pltpu.SemaphoreType.DMA((n,)), pltpu.SemaphoreType.DMA((n,)), pltpu.SemaphoreType.DMA(())],
        input_output_aliases={0: 0} if in_place else {},
    )(src)


CHIP_FLIPS = [(1, 0, 0), (0, 1, 0), (1, 1, 0)]
SIBLING = [(0, 0, 1)]
ALL_FLIPS = [(a, b, c) for a in (0, 1) for b in (0, 1) for c in (0, 1) if (a, b, c) != (0, 0, 0)]


def _chip(pos):
    return 2 * pos[0] + pos[1]


def _dev(pos):
    return 4 * pos[0] + 2 * pos[1] + pos[2]


GATHER = dict(flips=CHIP_FLIPS, src_idx=lambda me, peer: 0, dst_idx=_chip, local_idx=lambda me: (0, _chip(me)))
SCATTER = dict(flips=ALL_FLIPS, src_idx=lambda me, peer: 2 * _chip(peer) + peer[2], dst_idx=_dev,
               local_idx=lambda me: (2 * _chip(me) + me[2], _dev(me)))
SHARE = dict(flips=SIBLING, src_idx=lambda me, peer: me[2], dst_idx=lambda me: me[2], local_idx=None)
ALL_TO_ALL = dict(flips=ALL_FLIPS, src_idx=lambda me, peer: 0, dst_idx=_dev, local_idx=lambda me: (0, _dev(me)))


def sum_slots(x, *, name, tr=512):
    n, R, L = x.shape
    tr = _rows(R, tr)

    def body(x_ref, o_ref):
        acc = x_ref[0]
        for k in range(1, n):
            acc = acc + x_ref[k]
        o_ref[...] = acc

    return _call(body, name=name, grid=(R // tr,),
                 ins=[(x, (n, tr, L), lambda i: (0, i, 0))],
                 outs=[((R, L), F32, (tr, L), lambda i: (i, 0))])[0]


def sum_slots_into_half(x, half, *, name, tr=512):
    n, R, L = x.shape
    tr = _rows(R, tr)

    def body(h_ref, x_ref, o_ref):
        acc = x_ref[0]
        for k in range(1, n):
            acc = acc + x_ref[k]
        o_ref[0] = acc

    return pl.pallas_call(
        body, name=name,
        grid_spec=pltpu.PrefetchScalarGridSpec(
            num_scalar_prefetch=1, grid=(R // tr,),
            in_specs=[pl.BlockSpec((n, tr, L), lambda i, h: (0, i, 0))],
            out_specs=pl.BlockSpec((1, tr, L), lambda i, h: (h[0], i, 0))),
        out_shape=jax.ShapeDtypeStruct((2, R, L), F32),
        compiler_params=pltpu.CompilerParams(dimension_semantics=("arbitrary",), vmem_limit_bytes=VMEM_LIMIT),
    )(jnp.reshape(half, (1,)).astype(jnp.int32), x)


ADAM_ROWS = 256


def adamw(w, g, m, v, *, name):
    shape = w.shape
    w2, g2, m2, v2 = (a.reshape(-1, shape[-1]) for a in (w, g, m, v))
    R, L = w2.shape
    tr = ADAM_ROWS if R % ADAM_ROWS == 0 else next(
        (t for t in range(min(R, 2 * ADAM_ROWS) // SUBLANES * SUBLANES, 0, -SUBLANES) if R % t == 0), R)

    def body(w_ref, g_ref, m_ref, v_ref, d_ref, mo_ref, vo_ref):
        g = g_ref[...]
        m = ADAM_B1 * m_ref[...] + (1.0 - ADAM_B1) * g
        v = ADAM_B2 * v_ref[...] + (1.0 - ADAM_B2) * (g * g)
        m_hat = m / (1.0 - ADAM_B1 ** ADAM_STEP)
        v_hat = v / (1.0 - ADAM_B2 ** ADAM_STEP)
        d_ref[...] = -ADAM_LR * (m_hat / (jnp.sqrt(v_hat) + ADAM_EPS) + ADAM_WD * w_ref[...])
        mo_ref[...] = m
        vo_ref[...] = v

    row = lambda i: (i, 0)
    outs = _call(body, name=name, grid=(R // tr,),
                 ins=[(a, (tr, L), row) for a in (w2, g2, m2, v2)],
                 outs=[((R, L), F32, (tr, L), row)] * 3)
    return [o.reshape(shape) for o in outs]


def kernel(x, mix_norm, ffn_norm, hy_w_in, hy_w_gate2, hy_b_gate, hy_gla_norm, hy_sb_q_norm, hy_sb_k_norm, hy_w_out, cv_w_pw1, cv_b_pw1, cv_w_dw, cv_b_dw, cv_ln_g, cv_ln_b, cv_w_pw2, cv_b_pw2, ffn_w_gate, ffn_w_up, ffn_w_down, loss_target, m_mix_norm, m_ffn_norm, m_hy_w_in, m_hy_w_gate2, m_hy_b_gate, m_hy_gla_norm, m_hy_sb_q_norm, m_hy_sb_k_norm, m_hy_w_out, m_cv_w_pw1, m_cv_b_pw1, m_cv_w_dw, m_cv_b_dw, m_cv_ln_g, m_cv_ln_b, m_cv_w_pw2, m_cv_b_pw2, m_ffn_w_gate, m_ffn_w_up, m_ffn_w_down, v_mix_norm, v_ffn_norm, v_hy_w_in, v_hy_w_gate2, v_hy_b_gate, v_hy_gla_norm, v_hy_sb_q_norm, v_hy_sb_k_norm, v_hy_w_out, v_cv_w_pw1, v_cv_b_pw1, v_cv_w_dw, v_cv_b_dw, v_cv_ln_g, v_cv_ln_b, v_cv_w_pw2, v_cv_b_pw2, v_ffn_w_gate, v_ffn_w_up, v_ffn_w_down):
    w = dict(zip(WEIGHTS, (mix_norm, ffn_norm, hy_w_in, hy_w_gate2, hy_b_gate, hy_gla_norm, hy_sb_q_norm, hy_sb_k_norm, hy_w_out, cv_w_pw1, cv_b_pw1, cv_w_dw, cv_b_dw, cv_ln_g, cv_ln_b, cv_w_pw2, cv_b_pw2, ffn_w_gate, ffn_w_up, ffn_w_down)))
    m = dict(zip(WEIGHTS, (m_mix_norm, m_ffn_norm, m_hy_w_in, m_hy_w_gate2, m_hy_b_gate, m_hy_gla_norm, m_hy_sb_q_norm, m_hy_sb_k_norm, m_hy_w_out, m_cv_w_pw1, m_cv_b_pw1, m_cv_w_dw, m_cv_b_dw, m_cv_ln_g, m_cv_ln_b, m_cv_w_pw2, m_cv_b_pw2, m_ffn_w_gate, m_ffn_w_up, m_ffn_w_down)))
    v = dict(zip(WEIGHTS, (v_mix_norm, v_ffn_norm, v_hy_w_in, v_hy_w_gate2, v_hy_b_gate, v_hy_gla_norm, v_hy_sb_q_norm, v_hy_sb_k_norm, v_hy_w_out, v_cv_w_pw1, v_cv_b_pw1, v_cv_w_dw, v_cv_b_dw, v_cv_ln_g, v_cv_ln_b, v_cv_w_pw2, v_cv_b_pw2, v_ffn_w_gate, v_ffn_w_up, v_ffn_w_down)))
    def local_shards(keys, rows_multiple, dtype):
        return _pack([w[n] for n in _refs_of(keys)], rows_multiple, dtype)[None]

    def finish_reduce(slots, tag):
        halves = sum_slots_into_half(slots, lax.axis_index("c"), name=f"reduce_{tag}_sum")
        return exchange(halves, 2, SHARE, name=f"reduce_{tag}_share", in_place=True).reshape(2 * slots.shape[1], LANES)

    first = exchange(local_shards(FIRST_BIG_ITEMS, 32, MXU), N_CHIPS, GATHER, name="gather_first")
    small = exchange(local_shards(SMALL_ITEMS, 8, F32), N_CHIPS, GATHER, name="gather_small")
    raw = {**unpack_items(first, FIRST_BIG_ITEMS), **unpack_items(small, SMALL_ITEMS)}
    late = {"src": local_shards(LATE_ITEMS, 32, MXU),
            "unpack": lambda buf: layout_late(unpack_items(buf, LATE_ITEMS)),
            "pack": lambda G: pack_item_grads(item_grads(G), LATE_ITEMS)}

    loss_lanes, dx, Gk, late_slots = local_step(x[0], loss_target[0], layout_first(raw, {n: w[n] for n in REPL}), late)
    rep_names = {"mix_norm": "mix_norm", "ffn_norm": "ffn_norm", "hy_b_gate": "b_gate", "hy_gla_norm": "g_gla",
                 "hy_sb_q_norm": "g_q", "hy_sb_k_norm": "g_k"}
    Gr = {n: Gk[k] for n, k in rep_names.items()}

    first_items = FIRST_BIG_ITEMS + SMALL_ITEMS
    first_slots = exchange(pack_item_grads(item_grads(Gk), first_items), N_DEV, SCATTER, name="reduce_first")
    groups = [(_refs_of(first_items), finish_reduce(first_slots, "first")),
              (_refs_of(LATE_ITEMS), finish_reduce(late_slots, "late"))]

    loss_row = jnp.pad(jnp.sum(loss_lanes).reshape(1), (0, LANES - 1))
    rep = _pack([Gr[n] for n in REPL] + [loss_row], 8, F32)
    rep_all = exchange(rep[None], N_DEV, ALL_TO_ALL, name="reduce_small")
    rep_sum = sum_slots(rep_all, name="reduce_small_sum", tr=rep.shape[0])
    rep_shapes = [w[n].shape for n in REPL]
    rep_n = sum(math.prod(s) for s in rep_shapes)
    loss = rep_sum.reshape(-1)[rep_n]

    grads = dict(zip(REPL, _unpack(rep_sum, rep_shapes)))
    for names, g_flat in groups:
        grads.update(dict(zip(names, _unpack(g_flat, [w[n].shape for n in names]))))
    out = {"grad": grads, "delta": {}, "new_m": {}, "new_v": {}}
    small_refs = _refs_of(SMALL_ITEMS) + REPL
    for n in WEIGHTS:
        if n not in small_refs:
            out["delta"][n], out["new_m"][n], out["new_v"][n] = adamw(w[n], grads[n], m[n], v[n], name=f"adamw_{n}")
    packed = [_pack([t[n] for n in small_refs], 8, F32) for t in (w, grads, m, v)]
    small_shapes = [w[n].shape for n in small_refs]
    for key, buf in zip(("delta", "new_m", "new_v"), adamw(*packed, name="adamw_small")):
        out[key].update(dict(zip(small_refs, _unpack(buf, small_shapes))))

    return (loss, dx[None], *[out["grad"][n] for n in WEIGHTS], *[out["delta"][n] for n in WEIGHTS],
            *[out["new_m"][n] for n in WEIGHTS], *[out["new_v"][n] for n in WEIGHTS])
```

```python
import functools
import math

import numpy as np
import jax
import jax.numpy as jnp
from jax import lax
from jax.experimental import pallas as pl
from jax.experimental.pallas import tpu as pltpu

F32 = jnp.float32
MXU = jnp.bfloat16
EPS = 1e-6
LANES = 128
VMEM_LIMIT = 56 * 1024 * 1024

D = 1024
F = 2816
CHUNK = 64
GLA_H, GLA_K, GLA_V, GLA_RANK = 4, 64, 128, 16
SB_H, SB_D = 8, 64
CONV_W = 31
HALO = 32

ADAM_LR, ADAM_B1, ADAM_B2, ADAM_EPS, ADAM_WD, ADAM_STEP = 0.001, 0.9, 0.999, 1e-08, 0.01, 10

NN = (((1,), (0,)), ((), ()))
NT = (((1,), (1,)), ((), ()))
TN = (((0,), (0,)), ((), ()))


def _dot(a, b, dims=NN):
    return lax.dot_general(a.astype(MXU), b.astype(MXU), dims, preferred_element_type=F32)


def _dot_split(a, b, dims=NN, a_split=True):
    x = a if a_split else b
    hi = x.astype(MXU)
    lo = (x - hi.astype(F32)).astype(MXU)
    if a_split:
        return _dot(hi, b, dims) + _dot(lo, b, dims)
    return _dot(a, hi, dims) + _dot(a, lo, dims)


def _sigmoid(x):
    return 1.0 / (1.0 + jnp.exp(-x))


def _softplus_neg_abs(z):
    return jnp.log(1.0 + jnp.exp(-jnp.abs(z)))


def _call(body, *, name, grid, ins, outs, scratch=()):
    spec = lambda b, m: pl.BlockSpec(memory_space=pl.ANY) if b is None else pl.BlockSpec(b, m)
    res = pl.pallas_call(
        body,
        name=name,
        grid=grid,
        in_specs=[spec(b, m) for _, b, m in ins],
        out_specs=[spec(b, m) for _, _, b, m in outs],
        out_shape=[jax.ShapeDtypeStruct(s, d) for s, d, _, _ in outs],
        scratch_shapes=list(scratch),
        compiler_params=pltpu.CompilerParams(
            dimension_semantics=("arbitrary",) * len(grid), vmem_limit_bytes=VMEM_LIMIT),
    )(*[a for a, _, _ in ins])
    return res


def _rows(T, tm):
    tm = min(tm, T)
    assert T % tm == 0, (T, tm)
    return tm


def mm(pairs, *, name, trans_b=False, bias=None, res=None, out_dtype=F32, tm=512, tn=512, fused=None):
    M = pairs[0][0].shape[0]
    N = pairs[0][1].shape[0] if trans_b else pairs[0][1].shape[1]
    tm = _rows(M, tm)
    tn = min(tn, N)
    assert N % tn == 0, (N, tn)
    np_ = len(pairs)
    ni, nj = M // tm, N // tn
    pred = lambda: ((pl.program_id(0) == 0) & (pl.program_id(1) == 0),
                    (pl.program_id(0) == ni - 1) & (pl.program_id(1) == nj - 1))
    x_ins, x_outs, x_scratch, x_start, x_end = _fused_exchange(fused, pred)
    n_in = 2 * np_ + (bias is not None) + (res is not None)

    def body(*refs):
        o_ref = refs[n_in + len(x_ins)]
        x_refs = refs[n_in:n_in + len(x_ins)] + refs[n_in + len(x_ins) + 1:]
        x_start(x_refs)
        acc = None
        for p in range(np_):
            d = _dot(refs[2 * p][...], refs[2 * p + 1][...], NT if trans_b else NN)
            acc = d if acc is None else acc + d
        k = 2 * np_
        if bias is not None:
            acc = acc + refs[k][...]
            k += 1
        if res is not None:
            acc = acc + refs[k][...]
        o_ref[...] = acc.astype(out_dtype)
        x_end(x_refs)

    ins = []
    for a, b in pairs:
        K = a.shape[1]
        ins.append((a, (tm, K), lambda i, j: (i, 0)))
        if trans_b:
            ins.append((b, (tn, K), lambda i, j: (j, 0)))
        else:
            ins.append((b, (K, tn), lambda i, j: (0, j)))
    if bias is not None:
        ins.append((bias, (1, tn), lambda i, j: (0, j)))
    if res is not None:
        ins.append((res, (tm, tn), lambda i, j: (i, j)))
    outs = _call(body, name=name, grid=(ni, nj), ins=ins + x_ins,
                 outs=[((M, N), out_dtype, (tm, tn), lambda i, j: (i, j))] + x_outs, scratch=x_scratch)
    return outs[0] if fused is None else outs


def mm_tn(a, b, *, name, tm=512, tn=512, tk=1024):
    T, M = a.shape
    N = b.shape[1]
    tm, tn, tk = min(tm, M), min(tn, N), min(tk, T)
    assert M % tm == 0 and N % tn == 0 and T % tk == 0, (M, N, T, tm, tn, tk)

    def body(a_ref, b_ref, o_ref):
        @pl.when(pl.program_id(2) == 0)
        def _():
            o_ref[...] = jnp.zeros_like(o_ref)
        o_ref[...] += _dot(a_ref[...], b_ref[...], TN)

    return _call(body, name=name, grid=(M // tm, N // tn, T // tk),
                 ins=[(a, (tk, tm), lambda i, j, k: (k, i)), (b, (tk, tn), lambda i, j, k: (k, j))],
                 outs=[((M, N), F32, (tm, tn), lambda i, j, k: (i, j))])[0]


def rms_fwd(x, g, *, name, tm=256):
    T = x.shape[0]
    tm = _rows(T, tm)

    def body(x_ref, g_ref, o_ref):
        x = x_ref[...]
        r = lax.rsqrt(jnp.mean(x * x, axis=-1, keepdims=True) + EPS)
        o_ref[...] = (x * r * g_ref[...]).astype(MXU)

    return _call(body, name=name, grid=(T // tm,),
                 ins=[(x, (tm, D), lambda i: (i, 0)), (g, (1, D), lambda i: (0, 0))],
                 outs=[((T, D), MXU, (tm, D), lambda i: (i, 0))])[0]


def rms_bwd(x, g, dhn, dres, *, name, tm=256):
    T = x.shape[0]
    tm = _rows(T, tm)

    def body(x_ref, g_ref, d_ref, r_ref, dx_ref, dg_ref, cs_ref):
        @pl.when(pl.program_id(0) == 0)
        def _():
            dg_ref[...] = jnp.zeros_like(dg_ref)
            cs_ref[...] = jnp.zeros_like(cs_ref)
        x = x_ref[...]
        d = d_ref[...]
        r = lax.rsqrt(jnp.mean(x * x, axis=-1, keepdims=True) + EPS)
        dg_ref[...] += jnp.sum(d * x * r, axis=0, keepdims=True)
        t = d * g_ref[...]
        m = jnp.mean(t * x, axis=-1, keepdims=True)
        dx = r_ref[...] + t * r - x * (r * r * r) * m
        dx_ref[...] = dx
        cs_ref[...] += jnp.sum(dx, axis=0, keepdims=True)

    row = lambda i: (i, 0)
    fix = lambda i: (0, 0)
    return _call(body, name=name, grid=(T // tm,),
                 ins=[(x, (tm, D), row), (g, (1, D), fix), (dhn, (tm, D), row), (dres, (tm, D), row)],
                 outs=[((T, D), F32, (tm, D), row), ((1, D), F32, (1, D), fix), ((1, D), F32, (1, D), fix)])


def loss_grad(y, t, *, name, tm=256):
    T = y.shape[0]
    tm = _rows(T, tm)

    def body(y_ref, t_ref, dy_ref, l_ref):
        @pl.when(pl.program_id(0) == 0)
        def _():
            l_ref[...] = jnp.zeros_like(l_ref)
        e = y_ref[...] - t_ref[...]
        dy_ref[...] = e * (1.0 / D)
        l_ref[...] += jnp.sum(e * e, axis=0, keepdims=True) * (0.5 / D)

    row = lambda i: (i, 0)
    return _call(body, name=name, grid=(T // tm,),
                 ins=[(y, (tm, D), row), (t, (tm, D), row)],
                 outs=[((T, D), F32, (tm, D), row), ((1, D), F32, (1, D), lambda i: (0, 0))])


def ffn_up(hn, wg, wu, *, name, tm=1024, tn=256, fused=None):
    T = hn.shape[0]
    tm = _rows(T, tm)
    ni, nj = T // tm, F // tn
    pred = lambda: ((pl.program_id(0) == 0) & (pl.program_id(1) == 0),
                    (pl.program_id(0) == ni - 1) & (pl.program_id(1) == nj - 1))
    x_ins, x_outs, x_scratch, x_start, x_end = _fused_exchange(fused, pred)
    nx = len(x_ins)

    def body(*refs):
        h_ref, wg_ref, wu_ref = refs[:3]
        g_ref, u_ref, a_ref = refs[3 + nx:6 + nx]
        x_refs = refs[3:3 + nx] + refs[6 + nx:]
        x_start(x_refs)
        h = h_ref[...]
        g = _dot(h, wg_ref[...])
        u = _dot(h, wu_ref[...])
        g_ref[...] = g.astype(MXU)
        u_ref[...] = u.astype(MXU)
        a_ref[...] = (g * _sigmoid(g) * u).astype(MXU)
        x_end(x_refs)

    tile = lambda i, j: (i, j)
    return _call(body, name=name, grid=(ni, nj),
                 ins=[(hn, (tm, D), lambda i, j: (i, 0)), (wg, (D, tn), lambda i, j: (0, j)),
                      (wu, (D, tn), lambda i, j: (0, j))] + x_ins,
                 outs=[((T, F), MXU, (tm, tn), tile), ((T, F), MXU, (tm, tn), tile),
                       ((T, F), MXU, (tm, tn), tile)] + x_outs,
                 scratch=x_scratch)


def ffn_bwd_act(dy, wd, G, U, *, name, tm=1024, tn=256):
    T = dy.shape[0]
    tm = _rows(T, tm)

    def body(dy_ref, wd_ref, g_ref, u_ref, dg_ref, du_ref, a_ref):
        da = _dot(dy_ref[...], wd_ref[...], NT)
        g = g_ref[...].astype(F32)
        u = u_ref[...].astype(F32)
        s = _sigmoid(g)
        sil = g * s
        a_ref[...] = (sil * u).astype(MXU)
        du_ref[...] = (da * sil).astype(MXU)
        dg_ref[...] = (da * u * (s * (1.0 + g * (1.0 - s)))).astype(MXU)

    tile = lambda i, j: (i, j)
    return _call(body, name=name, grid=(T // tm, F // tn),
                 ins=[(dy, (tm, D), lambda i, j: (i, 0)), (wd, (tn, D), lambda i, j: (j, 0)),
                      (G, (tm, tn), tile), (U, (tm, tn), tile)],
                 outs=[((T, F), MXU, (tm, tn), tile), ((T, F), MXU, (tm, tn), tile),
                       ((T, F), MXU, (tm, tn), tile)])


def ffn_fwd(h, g_norm, wg, wu, wd, tag, fused=None):
    hn = rms_fwd(h, g_norm, name=f"ffn{tag}_norm")
    G, U, act, *landed = ffn_up(hn, wg, wu, name=f"ffn{tag}_up", fused=fused)
    h_out = mm([(act, wd)], res=h, name=f"ffn{tag}_down")
    return h_out, (hn, G, U), landed


def ffn_bwd(dy, h, g_norm, wg, wu, wd, saved, tag):
    hn, G, U = saved
    dG, dU, act = ffn_bwd_act(dy, wd, G, U, name=f"ffn{tag}_bwd_act")
    d_wd = mm_tn(act, dy, name=f"ffn{tag}_dwd", tm=1408, tn=512)
    d_wg = mm_tn(hn, dG, name=f"ffn{tag}_dwg", tm=512, tn=1408)
    d_wu = mm_tn(hn, dU, name=f"ffn{tag}_dwu", tm=512, tn=1408)
    dhn = mm([(dG, wg), (dU, wu)], trans_b=True, name=f"ffn{tag}_dhn")
    dh, d_g, cs = rms_bwd(h, g_norm, dhn, dy, name=f"ffn{tag}_norm_bwd")
    return dh, d_g, d_wg, d_wu, d_wd, cs


SUBLANES = 8


def _window_scratch(tm):
    return [pltpu.VMEM((tm + HALO + SUBLANES, D), F32), pltpu.VMEM((SUBLANES, tm + HALO, D), F32)]


def _shift_copies(win, sh):
    rows = sh.shape[1]
    win[pl.ds(rows, SUBLANES), :] = jnp.zeros((SUBLANES, D), F32)
    for s in range(SUBLANES):
        sh[s] = win[pl.ds(s, rows), :]


def _tap(sh, off, rb):
    s = off % SUBLANES
    return sh[s, pl.ds(off - s, rb), :]


def conv_fwd(A, w_dw, b_dw, ln_g, ln_b, *, name, tm=128, rb=32):
    T = A.shape[0]
    tm = _rows(T, tm)

    def body(a_ref, ap_ref, w_ref, b_ref, g_ref, bb_ref, s_ref, c_ref, win, sh):
        i = pl.program_id(0)
        a = a_ref[...]
        win[pl.ds(HALO, tm), :] = a[:, :D] * _sigmoid(a[:, D:])
        ap = ap_ref[pl.ds(tm - HALO, HALO), :]
        up = ap[:, :D] * _sigmoid(ap[:, D:])
        win[pl.ds(0, HALO), :] = jnp.where(i > 0, up, 0.0)
        _shift_copies(win, sh)
        for r0 in range(0, tm, rb):
            acc = jnp.broadcast_to(b_ref[...], (rb, D))
            for k in range(CONV_W):
                acc = acc + w_ref[pl.ds(k, 1), :] * _tap(sh, r0 + k + HALO - (CONV_W - 1), rb)
            c_ref[pl.ds(r0, rb), :] = acc
        c = c_ref[...]
        mu = jnp.mean(c, axis=-1, keepdims=True)
        cc = c - mu
        var = jnp.mean(cc * cc, axis=-1, keepdims=True)
        z = cc * lax.rsqrt(var + EPS) * g_ref[...] + bb_ref[...]
        s_ref[...] = (z * _sigmoid(z)).astype(MXU)

    row = lambda i: (i, 0)
    fix = lambda i: (0, 0)
    return _call(body, name=name, grid=(T // tm,),
                 ins=[(A, (tm, 2 * D), row), (A, (tm, 2 * D), lambda i: (jnp.maximum(i - 1, 0), 0)),
                      (w_dw, (HALO, D), fix), (b_dw, (1, D), fix), (ln_g, (1, D), fix), (ln_b, (1, D), fix)],
                 outs=[((T, D), MXU, (tm, D), row), ((T, D), F32, (tm, D), row)],
                 scratch=_window_scratch(tm))


def ln_swish_bwd(c, ds, ln_g, ln_b, *, name, tm=256):
    T = c.shape[0]
    tm = _rows(T, tm)

    def body(c_ref, ds_ref, g_ref, b_ref, dc_ref, dg_ref, db_ref):
        @pl.when(pl.program_id(0) == 0)
        def _():
            dg_ref[...] = jnp.zeros_like(dg_ref)
            db_ref[...] = jnp.zeros_like(db_ref)
        c = c_ref[...]
        mu = jnp.mean(c, axis=-1, keepdims=True)
        cc = c - mu
        rstd = lax.rsqrt(jnp.mean(cc * cc, axis=-1, keepdims=True) + EPS)
        n = cc * rstd
        z = n * g_ref[...] + b_ref[...]
        s = _sigmoid(z)
        dz = ds_ref[...] * (s * (1.0 + z * (1.0 - s)))
        dg_ref[...] += jnp.sum(dz * n, axis=0, keepdims=True)
        db_ref[...] += jnp.sum(dz, axis=0, keepdims=True)
        dn = dz * g_ref[...]
        dc_ref[...] = rstd * (dn - jnp.mean(dn, axis=-1, keepdims=True)
                              - n * jnp.mean(dn * n, axis=-1, keepdims=True))

    row = lambda i: (i, 0)
    fix = lambda i: (0, 0)
    return _call(body, name=name, grid=(T // tm,),
                 ins=[(c, (tm, D), row), (ds, (tm, D), row), (ln_g, (1, D), fix), (ln_b, (1, D), fix)],
                 outs=[((T, D), F32, (tm, D), row), ((1, D), F32, (1, D), fix), ((1, D), F32, (1, D), fix)])


def conv_bwd(dc, A, w_dw, *, name, tm=128, rb=32):
    T = A.shape[0]
    tm = _rows(T, tm)
    n = T // tm
    SUB = 8

    def body(dc_ref, dn_ref, a_ref, ap_ref, w_ref, da_ref, dw_ref, dbd_ref, dbp_ref, wdc, sdc, wu, su, accw):
        i = pl.program_id(0)

        @pl.when(i == 0)
        def _():
            accw[...] = jnp.zeros_like(accw)
            dbd_ref[...] = jnp.zeros_like(dbd_ref)
            dbp_ref[...] = jnp.zeros_like(dbp_ref)
        a = a_ref[...]
        a1, a2 = a[:, :D], a[:, D:]
        sg = _sigmoid(a2)
        wu[pl.ds(HALO, tm), :] = a1 * sg
        ap = ap_ref[pl.ds(tm - HALO, HALO), :]
        wu[pl.ds(0, HALO), :] = jnp.where(i > 0, ap[:, :D] * _sigmoid(ap[:, D:]), 0.0)
        dc = dc_ref[...]
        wdc[pl.ds(0, tm), :] = dc
        wdc[pl.ds(tm, HALO), :] = jnp.where(i < n - 1, dn_ref[pl.ds(0, HALO), :], 0.0)
        dbd_ref[...] += jnp.sum(dc, axis=0, keepdims=True)
        _shift_copies(wdc, sdc)
        _shift_copies(wu, su)
        for r0 in range(0, tm, rb):
            du = jnp.zeros((rb, D), F32)
            dcs = wdc[pl.ds(r0, rb), :]
            for k in range(CONV_W):
                du = du + w_ref[pl.ds(k, 1), :] * _tap(sdc, r0 + (CONV_W - 1) - k, rb)
                p = dcs * _tap(su, r0 + k + HALO - (CONV_W - 1), rb)
                accw[pl.ds(SUB * k, SUB), :] += jnp.sum(p.reshape(rb // SUB, SUB, D), axis=0)
            s = sg[r0:r0 + rb]
            da_ref[pl.ds(r0, rb), pl.ds(0, D)] = (du * s).astype(MXU)
            da_ref[pl.ds(r0, rb), pl.ds(D, D)] = (du * a1[r0:r0 + rb] * s * (1.0 - s)).astype(MXU)
        da = da_ref[...].astype(F32)
        dbp_ref[...] += jnp.sum(da, axis=0, keepdims=True)

        @pl.when(i == n - 1)
        def _():
            dw_ref[...] = jnp.zeros_like(dw_ref)
            for k in range(CONV_W):
                dw_ref[pl.ds(k, 1), :] = jnp.sum(accw[pl.ds(SUB * k, SUB), :], axis=0, keepdims=True)

    row = lambda i: (i, 0)
    fix = lambda i: (0, 0)
    return _call(body, name=name, grid=(n,),
                 ins=[(dc, (tm, D), row), (dc, (tm, D), lambda i: (jnp.minimum(i + 1, n - 1), 0)),
                      (A, (tm, 2 * D), row), (A, (tm, 2 * D), lambda i: (jnp.maximum(i - 1, 0), 0)),
                      (w_dw, (HALO, D), fix)],
                 outs=[((T, 2 * D), MXU, (tm, 2 * D), row), ((HALO, D), F32, (HALO, D), fix),
                       ((1, D), F32, (1, D), fix), ((1, 2 * D), F32, (1, 2 * D), fix)],
                 scratch=_window_scratch(tm) + _window_scratch(tm) + [pltpu.VMEM((SUB * HALO, D), F32)])


def sb_prep(ps, gq, gk, *, name, tm=256):
    T = ps.shape[0]
    tm = _rows(T, tm)
    W = SB_H * SB_D

    def body(q_ref, k_ref, v_ref, gq_ref, gk_ref, qo, ko, vo):
        qs, ks = [], []
        for h in range(SB_H):
            sl = slice(h * SB_D, (h + 1) * SB_D)
            q = q_ref[:, sl]
            k = k_ref[:, sl]
            rq = lax.rsqrt(jnp.mean(q * q, axis=-1, keepdims=True) + EPS)
            rk = lax.rsqrt(jnp.mean(k * k, axis=-1, keepdims=True) + EPS)
            qs.append(q * rq * gq_ref[...] * (SB_D ** -0.5))
            ks.append(k * rk * gk_ref[...])
        qo[...] = jnp.concatenate(qs, axis=-1).astype(MXU)
        ko[...] = jnp.concatenate(ks, axis=-1).astype(MXU)
        vo[...] = v_ref[...].astype(MXU)

    row = lambda i: (i, 0)
    fix = lambda i: (0, 0)
    return _call(body, name=name, grid=(T // tm,),
                 ins=[(ps, (tm, W), lambda i: (i, 0)), (ps, (tm, W), lambda i: (i, 1)),
                      (ps, (tm, W), lambda i: (i, 2)), (gq, (1, SB_D), fix), (gk, (1, SB_D), fix)],
                 outs=[((T, W), MXU, (tm, W), row)] * 3)


def _sb_masks(tq):
    row = lax.broadcasted_iota(jnp.int32, (tq, tq), 0)
    col = lax.broadcasted_iota(jnp.int32, (tq, tq), 1)
    tri = jnp.where(row >= col, 1.0, 0.0).astype(MXU)
    past = col < row
    return tri, past


def _sb_scores(q, k, past):
    return _sb_logs(_dot(q, k, NT), past)


def _sb_logs(z, past):
    ls = jnp.minimum(z, 0.0) - _softplus_neg_abs(z)
    lk = ls - z
    if past is not None:
        lk = jnp.where(past, lk, 0.0)
    return ls, lk


def _fused_exchange(fused, steps_pred):
    if fused is None:
        return [], [], [], lambda refs: None, lambda refs: None
    src, n_dst, spec = fused
    n = len(spec["flips"])
    ins = [(src, None, None)]
    outs = [((n_dst,) + src.shape[1:], src.dtype, None, None)]
    scratch = [pltpu.SemaphoreType.DMA((n,)), pltpu.SemaphoreType.DMA((n,)), pltpu.SemaphoreType.DMA(())]

    def at_start(refs):
        @pl.when(steps_pred()[0])
        def _():
            _exchange_start(*refs, **spec)

    def at_end(refs):
        @pl.when(steps_pred()[1])
        def _():
            _exchange_wait(*refs, **spec)

    return ins, outs, scratch, at_start, at_end


PAIR = 2 * SB_D
SB_DEAD = -120.0
SB_UNVISITED = -1e30


def _split_pair(x2, lo):
    zero = jnp.zeros_like(x2)
    return [jnp.where(lo, x2, zero), jnp.where(lo, zero, x2)]


def sb_fwd(qn, kn, v, *, name, tq=256, pairs=2, fused=None):
    T, W = qn.shape
    tq = _rows(T, tq)
    wb, hg = pairs * PAIR, 2 * pairs
    assert T // tq <= LANES and W % wb == 0
    ng, ni = W // wb, T // tq
    pred = lambda: ((pl.program_id(0) == 0) & (pl.program_id(1) == 0),
                    (pl.program_id(0) == ng - 1) & (pl.program_id(1) == ni - 1))
    x_ins, x_outs, x_scratch, x_start, x_end = _fused_exchange(fused, pred)
    nx = len(x_ins)
    heads = range(hg)
    pcols = [slice(p * PAIR, (p + 1) * PAIR) for p in range(pairs)]

    def body(*refs):
        q_ref, k_ref, v_ref = refs[:3]
        o_ref, cb_ref = refs[3 + nx:5 + nx]
        x_refs = refs[3:3 + nx] + refs[5 + nx:]
        x_start(x_refs)
        i = pl.program_id(1)
        tri, past = _sb_masks(tq)
        lane = lax.broadcasted_iota(jnp.int32, (tq, LANES), 1)
        lo = lane < SB_D
        cb_ref[...] = jnp.full((hg, tq, LANES), SB_UNVISITED, F32)
        qs = [m for p in range(pairs) for m in _split_pair(q_ref[:, pcols[p]], lo)]

        def block(j, carry, masked):
            accs, cs = carry
            kblk = pl.ds(pl.multiple_of(j * tq, tq), tq)
            for h in heads:
                cb_ref[h] = jnp.where(lane == j, cs[h], cb_ref[h])
            zs = [_dot(qs[h], k_ref[kblk, pcols[h // 2]], NT) for h in heads]
            sc = [_sb_logs(zs[h], past if masked else None) for h in heads]
            bincs = [_dot_split(sc[h][1], tri) for h in heads]
            ws = [jnp.exp(sc[h][0] + cs[h] + bincs[h] - sc[h][1]) for h in heads]
            if masked:
                ws = [jnp.where(past, w, 0.0) for w in ws]
            pv = [_dot(ws[h], v_ref[kblk, pcols[h // 2]]) for h in heads]
            new_a = [accs[p] + jnp.where(lo, pv[2 * p], pv[2 * p + 1]) for p in range(pairs)]
            new_c = [cs[h] + jnp.sum(sc[h][1], axis=-1, keepdims=True) for h in heads]
            return tuple(new_a), tuple(new_c)

        carry = ((jnp.zeros((tq, PAIR), F32),) * pairs, (jnp.zeros((tq, 1), F32),) * hg)
        carry = block(i, carry, True)

        def live(state):
            jj, _, cs = state
            return (jj < i) & (jnp.max(functools.reduce(jnp.maximum, cs)) > SB_DEAD)

        def step(state):
            jj, accs, cs = state
            accs, cs = block(i - 1 - jj, (accs, cs), False)
            return jj + 1, accs, cs

        _, accs, _ = lax.while_loop(live, step, (jnp.int32(0),) + carry)
        for p in range(pairs):
            o_ref[:, pcols[p]] = accs[p]
        x_end(x_refs)

    blk = lambda g, i: (i, g)
    full = lambda g, i: (0, g)
    return _call(body, name=name, grid=(ng, ni),
                 ins=[(qn, (tq, wb), blk), (kn, (T, wb), full), (v, (T, wb), full)] + x_ins,
                 outs=[((T, W), F32, (tq, wb), blk),
                       ((W // SB_D, T, LANES), F32, (hg, tq, LANES), lambda g, i: (g, i, 0))] + x_outs,
                 scratch=x_scratch)


def sb_bwd(qn, kn, v, cb, dmix, do_col, *, name, tq=256, pairs=2, fused=None):
    T, W = qn.shape
    tq = _rows(T, tq)
    wb, hg = pairs * PAIR, 2 * pairs
    assert W % wb == 0 and do_col % wb == 0
    ng, ni = W // wb, T // tq
    pred = lambda: ((pl.program_id(0) == 0) & (pl.program_id(1) == 0),
                    (pl.program_id(0) == ng - 1) & (pl.program_id(1) == ni - 1))
    x_ins, x_outs, x_scratch, x_start, x_end = _fused_exchange(fused, pred)
    nx = len(x_ins)
    heads = range(hg)
    pcols = [slice(p * PAIR, (p + 1) * PAIR) for p in range(pairs)]

    def body(*refs):
        q_ref, k_ref, v_ref, cb_ref, do_ref = refs[:5]
        dq_ref, dk_hbm, dv_hbm = refs[5 + nx:8 + nx]
        dk_ref, dv_ref = refs[8 + 2 * nx:10 + 2 * nx]
        x_refs = refs[5:5 + nx] + refs[8 + nx:8 + 2 * nx] + refs[10 + 2 * nx:]
        x_start(x_refs)
        i = pl.program_id(1)

        @pl.when(i == 0)
        def _():
            dk_ref[...] = jnp.zeros_like(dk_ref)
            dv_ref[...] = jnp.zeros_like(dv_ref)
        tri, past = _sb_masks(tq)
        row = lax.broadcasted_iota(jnp.int32, (tq, tq), 0)
        col = lax.broadcasted_iota(jnp.int32, (tq, tq), 1)
        tri_lt = jnp.where(row < col, 1.0, 0.0).astype(MXU)
        lane = lax.broadcasted_iota(jnp.int32, (tq, LANES), 1)
        lo = lane < SB_D
        qs = [m for p in range(pairs) for m in _split_pair(q_ref[:, pcols[p]], lo)]
        dos = [m for p in range(pairs) for m in _split_pair(do_ref[:, pcols[p]].astype(MXU), lo)]

        def block(j, carry, masked):
            dqs, cgs = carry
            kblk = pl.ds(pl.multiple_of(j * tq, tq), tq)
            k2 = [k_ref[kblk, pcols[p]] for p in range(pairs)]
            zs = [_dot(qs[h], k2[h // 2], NT) for h in heads]
            dws = [_dot(dos[h], v_ref[kblk, pcols[h // 2]], NT) for h in heads]
            sc = [_sb_logs(zs[h], past if masked else None) for h in heads]
            cs = [jnp.sum(jnp.where(lane == j, cb_ref[h], 0.0), axis=-1, keepdims=True) for h in heads]
            bincs = [_dot_split(sc[h][1], tri) for h in heads]
            ws = [jnp.exp(sc[h][0] + cs[h] + bincs[h] - sc[h][1]) for h in heads]
            if masked:
                ws = [jnp.where(past, w, 0.0) for w in ws]
            wbs = [w.astype(MXU) for w in ws]
            gs = [ws[h] * dws[h] for h in heads]
            gpres = [cgs[h] + _dot(gs[h], tri_lt) for h in heads]
            sigs = [jnp.exp(sc[h][0]) for h in heads]
            dzs = [gs[h] - sigs[h] * (gs[h] + gpres[h]) for h in heads]
            if masked:
                dzs = [jnp.where(past, dz, 0.0) for dz in dzs]
            dzbs = [dz.astype(MXU) for dz in dzs]
            dqp = [_dot(dzbs[h], k2[h // 2]) for h in heads]
            new_dq = [dqs[p] + jnp.where(lo, dqp[2 * p], dqp[2 * p + 1]) for p in range(pairs)]
            for p in range(pairs):
                dk_ref[kblk, pcols[p]] += _dot(dzbs[2 * p], qs[2 * p], TN) + _dot(dzbs[2 * p + 1], qs[2 * p + 1], TN)
            for p in range(pairs):
                dv_ref[kblk, pcols[p]] += _dot(wbs[2 * p], dos[2 * p], TN) + _dot(wbs[2 * p + 1], dos[2 * p + 1], TN)
            new_cg = [cgs[h] + jnp.sum(gs[h], axis=-1, keepdims=True) for h in heads]
            return tuple(new_dq), tuple(new_cg)

        colmax = functools.reduce(jnp.maximum, [jnp.max(cb_ref[h], axis=0, keepdims=True) for h in heads])
        lane_row = lax.broadcasted_iota(jnp.int32, (1, LANES), 1)
        is_live = (colmax > SB_DEAD) & (lane_row < i)
        first = jnp.min(jnp.where(is_live, lane_row, i).astype(F32)).astype(jnp.int32)
        carry = ((jnp.zeros((tq, PAIR), F32),) * pairs, (jnp.zeros((tq, 1), F32),) * hg)
        carry = lax.fori_loop(first, i, lambda j, car: block(j, car, False), carry)
        dqs, _ = block(i, carry, True)
        for p in range(pairs):
            dq_ref[:, pcols[p]] = dqs[p]

        cols = pl.ds(pl.multiple_of(pl.program_id(0) * wb, wb), wb)

        @pl.when(i == ni - 1)
        def _():
            pltpu.sync_copy(dk_ref, dk_hbm.at[:, cols])
            pltpu.sync_copy(dv_ref, dv_hbm.at[:, cols])
        x_end(x_refs)

    blk = lambda g, i: (i, g)
    full = lambda g, i: (0, g)
    return _call(body, name=name, grid=(ng, ni),
                 ins=[(qn, (tq, wb), blk), (kn, (T, wb), full), (v, (T, wb), full),
                      (cb, (hg, tq, LANES), lambda g, i: (g, i, 0)),
                      (dmix, (tq, wb), lambda g, i: (i, do_col // wb + g))] + x_ins,
                 outs=[((T, W), F32, (tq, wb), blk), ((T, W), F32, None, None), ((T, W), F32, None, None)] + x_outs,
                 scratch=[pltpu.VMEM((T, wb), F32), pltpu.VMEM((T, wb), F32)] + x_scratch)


def sb_post_bwd(dqn, dkn, dv, ps, gq, gk, *, name, tm=256):
    T = ps.shape[0]
    tm = _rows(T, tm)
    W = SB_H * SB_D

    def body(dq_ref, dk_ref, dv_ref, q_ref, k_ref, gq_ref, gk_ref, out_ref, dgq_ref, dgk_ref):
        @pl.when(pl.program_id(0) == 0)
        def _():
            dgq_ref[...] = jnp.zeros_like(dgq_ref)
            dgk_ref[...] = jnp.zeros_like(dgk_ref)

        def norm_bwd(x, d, g):
            r = lax.rsqrt(jnp.mean(x * x, axis=-1, keepdims=True) + EPS)
            dg = jnp.sum(d * x * r, axis=0, keepdims=True)
            t = d * g
            return t * r - x * (r * r * r) * jnp.mean(t * x, axis=-1, keepdims=True), dg

        dqs, dks = [], []
        dgq = jnp.zeros((1, SB_D), F32)
        dgk = jnp.zeros((1, SB_D), F32)
        for h in range(SB_H):
            sl = slice(h * SB_D, (h + 1) * SB_D)
            a, ga = norm_bwd(q_ref[:, sl], dq_ref[:, sl] * (SB_D ** -0.5), gq_ref[...])
            b, gb = norm_bwd(k_ref[:, sl], dk_ref[:, sl], gk_ref[...])
            dqs.append(a)
            dks.append(b)
            dgq = dgq + ga
            dgk = dgk + gb
        out_ref[...] = jnp.concatenate(dqs + dks + [dv_ref[...]], axis=-1).astype(MXU)
        dgq_ref[...] += dgq
        dgk_ref[...] += dgk

    row = lambda i: (i, 0)
    fix = lambda i: (0, 0)
    return _call(body, name=name, grid=(T // tm,),
                 ins=[(dqn, (tm, W), row), (dkn, (tm, W), row), (dv, (tm, W), row),
                      (ps, (tm, W), lambda i: (i, 0)), (ps, (tm, W), lambda i: (i, 1)),
                      (gq, (1, SB_D), fix), (gk, (1, SB_D), fix)],
                 outs=[((T, 3 * W), MXU, (tm, 3 * W), lambda i: (i, 0)),
                       ((1, SB_D), F32, (1, SB_D), fix), ((1, SB_D), F32, (1, SB_D), fix)])


GLA_TM = 512
GLA_DK = GLA_H * GLA_K
GLA_DV = GLA_H * GLA_V


def _gla_masks(tm):
    row = lax.broadcasted_iota(jnp.int32, (tm, tm), 0)
    col = lax.broadcasted_iota(jnp.int32, (tm, tm), 1)
    same = (row // CHUNK) == (col // CHUNK)
    return row, col, same


def _gla_gate(glr, w2, b, tm):
    pre = _dot(glr, w2) + b
    la = (jnp.minimum(pre, 0.0) - _softplus_neg_abs(pre)) * (1.0 / 16.0)
    row, col, same = _gla_masks(tm)
    m_incl = jnp.where(same & (col <= row), 1.0, 0.0).astype(MXU)
    m_full = jnp.where(same, 1.0, 0.0).astype(MXU)
    bc = _dot_split(m_incl, la, a_split=False)
    tot = _dot_split(m_full, la, a_split=False)
    return pre, bc, tot


def gla_fwd(pg, glr, w2, b_gate, g_out, *, name):
    T = pg.shape[0]
    tm = _rows(T, GLA_TM)
    ncb = tm // CHUNK
    NC = T // CHUNK

    def body(q_ref, k_ref, v_ref, r_ref, l_ref, w2_ref, b_ref, g_ref, o_ref, st_ref, S):
        @pl.when(pl.program_id(0) == 0)
        def _():
            S[...] = jnp.zeros_like(S)
        _, bc, tot = _gla_gate(l_ref[...], w2_ref[...], b_ref[...], tm)
        kend = k_ref[...] * jnp.exp(tot - bc)
        qs = q_ref[...] * (GLA_K ** -0.5)
        a_all = jnp.exp(tot)
        v = v_ref[...]
        r = r_ref[...]
        rows = [slice(c * CHUNK, (c + 1) * CHUNK) for c in range(ncb)]
        hks = [slice(h * GLA_K, (h + 1) * GLA_K) for h in range(GLA_H)]
        hvs = [slice(h * GLA_V, (h + 1) * GLA_V) for h in range(GLA_H)]
        uts = [[_dot(v[rows[c], hvs[h]], kend[rows[c], hks[h]], TN) for h in range(GLA_H)] for c in range(ncb)]
        for h in range(GLA_H):
            s = S[h]
            for c in range(ncb):
                s = s * a_all[c * CHUNK:c * CHUNK + 1, hks[h]] + uts[c][h]
                st_ref[c, h] = s
            S[h] = s
        for c in range(ncb):
            outs = []
            for h in range(GLA_H):
                o = _dot(qs[rows[c], hks[h]], st_ref[c, h], NT)
                rinv = lax.rsqrt(jnp.mean(o * o, axis=-1, keepdims=True) + EPS)
                rr = r[rows[c], hvs[h]]
                outs.append(o * rinv * g_ref[...] * (rr * _sigmoid(rr)))
            o_ref[pl.ds(c * CHUNK, CHUNK), :] = jnp.concatenate(outs, axis=-1).astype(MXU)

    fix = lambda i: (0, 0)
    return _call(body, name=name, grid=(T // tm,),
                 ins=[(pg, (tm, GLA_DK), lambda i: (i, 0)), (pg, (tm, GLA_DK), lambda i: (i, 1)),
                      (pg, (tm, GLA_DV), lambda i: (i, 1)), (pg, (tm, GLA_DV), lambda i: (i, 2)),
                      (glr, (tm, LANES), lambda i: (i, 0)), (w2, (LANES, GLA_DK), fix),
                      (b_gate, (1, GLA_DK), fix), (g_out, (1, GLA_V), fix)],
                 outs=[((T, GLA_DV), MXU, (tm, GLA_DV), lambda i: (i, 0)),
                       ((NC, GLA_H, GLA_V, GLA_K), F32, (ncb, GLA_H, GLA_V, GLA_K), lambda i: (i, 0, 0, 0))],
                 scratch=[pltpu.VMEM((GLA_H, GLA_V, GLA_K), F32)])


def gla_bwd(pg, glr, w2, b_gate, g_out, st, dmix, *, name):
    T = pg.shape[0]
    tm = _rows(T, GLA_TM)
    ncb = tm // CHUNK
    n = T // tm

    def body(q_ref, k_ref, v_ref, r_ref, l_ref, w2_ref, b_ref, g_ref, st_ref, sp_ref, d_ref,
             dpg_ref, dl_ref, dw2_ref, db_ref, dg_ref, dS, dkend, extra, dst_ref):
        i = pl.program_id(0)

        @pl.when(i == 0)
        def _():
            dS[...] = jnp.zeros_like(dS)
            dw2_ref[...] = jnp.zeros_like(dw2_ref)
            db_ref[...] = jnp.zeros_like(db_ref)
            dg_ref[...] = jnp.zeros_like(dg_ref)
        first_tile = i == n - 1
        glr_v = l_ref[...]
        pre, bc, tot = _gla_gate(glr_v, w2_ref[...], b_ref[...], tm)
        dec = jnp.exp(tot - bc)
        k = k_ref[...]
        kend = k * dec
        qs = q_ref[...] * (GLA_K ** -0.5)
        a_all = jnp.exp(tot)
        v = v_ref[...]
        r = r_ref[...]
        g = g_ref[...]
        dgg = jnp.zeros((1, GLA_V), F32)
        rows = [slice(c * CHUNK, (c + 1) * CHUNK) for c in range(ncb)]
        hks = [slice(h * GLA_K, (h + 1) * GLA_K) for h in range(GLA_H)]
        hvs = [slice(h * GLA_V, (h + 1) * GLA_V) for h in range(GLA_H)]
        heads = range(GLA_H)
        qdo = [[None] * GLA_H for _ in range(ncb)]
        for c in range(ncb):
            cr = pl.ds(c * CHUNK, CHUNK)
            dq_l, dr_l = [], []
            for h in heads:
                s_c = st_ref[c, h]
                qh = qs[rows[c], hks[h]]
                o = _dot(qh, s_c, NT)
                rinv = lax.rsqrt(jnp.mean(o * o, axis=-1, keepdims=True) + EPS)
                nrm = o * rinv
                rr = r[rows[c], hvs[h]]
                sg = _sigmoid(rr)
                sil = rr * sg
                d = d_ref[cr, pl.ds(h * GLA_V, GLA_V)]
                dr_l.append(d * nrm * g * (sg * (1.0 + rr * (1.0 - sg))))
                dgg = dgg + jnp.sum(d * sil * nrm, axis=0, keepdims=True)
                dn = d * sil * g
                do = rinv * (dn - nrm * jnp.mean(dn * nrm, axis=-1, keepdims=True))
                dq_l.append(_dot(do, s_c) * (GLA_K ** -0.5))
                qdo[c][h] = _dot(do, qh, TN)
            dpg_ref[cr, pl.ds(0, GLA_DK)] = jnp.concatenate(dq_l, axis=-1).astype(MXU)
            dpg_ref[cr, pl.ds(2 * GLA_DK + GLA_DV, GLA_DV)] = jnp.concatenate(dr_l, axis=-1).astype(MXU)
        ex = [[None] * GLA_H for _ in range(ncb)]
        for h in heads:
            ds_h = dS[h]
            for c in reversed(range(ncb)):
                dst = ds_h + qdo[c][h]
                dst_ref[c, h] = dst
                s_p = st_ref[c - 1, h] if c > 0 else jnp.where(first_tile, 0.0, sp_ref[0, h])
                a = a_all[c * CHUNK:c * CHUNK + 1, hks[h]]
                da = jnp.sum(dst * s_p, axis=0, keepdims=True)
                ds_h = dst * a
                ex[c][h] = jnp.broadcast_to(da * a, (CHUNK, GLA_K))
            dS[h] = ds_h
        for c in range(ncb):
            extra[pl.ds(c * CHUNK, CHUNK), :] = jnp.concatenate(ex[c], axis=-1)
        for c in range(ncb):
            cr = pl.ds(c * CHUNK, CHUNK)
            dk_l = [_dot(v[rows[c], hvs[h]], dst_ref[c, h]) for h in heads]
            dv_l = [_dot(kend[rows[c], hks[h]], dst_ref[c, h], NT) for h in heads]
            dpg_ref[cr, pl.ds(2 * GLA_DK, GLA_DV)] = jnp.concatenate(dv_l, axis=-1).astype(MXU)
            dkend[cr, :] = jnp.concatenate(dk_l, axis=-1)
        dke = dkend[...]
        dpg_ref[:, pl.ds(GLA_DK, GLA_DK)] = (dke * dec).astype(MXU)
        e = dke * kend
        row, col, same = _gla_masks(tm)
        m_lt = jnp.where(same & (col < row), 1.0, 0.0).astype(MXU)
        dla = _dot_split(m_lt, e, a_split=False) + extra[...]
        sp = _softplus_neg_abs(pre)
        one_m_sig = jnp.exp(-jnp.maximum(pre, 0.0) - sp)
        dpre = dla * (1.0 / 16.0) * one_m_sig
        dl_ref[...] = _dot(dpre, w2_ref[...], NT).astype(MXU)
        dw2_ref[...] += _dot(glr_v, dpre, TN)
        db_ref[...] += jnp.sum(dpre, axis=0, keepdims=True)
        dg_ref[...] += dgg

    fix = lambda i: (0, 0)
    rev = lambda i: n - 1 - i
    return _call(body, name=name, grid=(n,),
                 ins=[(pg, (tm, GLA_DK), lambda i: (rev(i), 0)), (pg, (tm, GLA_DK), lambda i: (rev(i), 1)),
                      (pg, (tm, GLA_DV), lambda i: (rev(i), 1)), (pg, (tm, GLA_DV), lambda i: (rev(i), 2)),
                      (glr, (tm, LANES), lambda i: (rev(i), 0)), (w2, (LANES, GLA_DK), fix),
                      (b_gate, (1, GLA_DK), fix), (g_out, (1, GLA_V), fix),
                      (st, (ncb, GLA_H, GLA_V, GLA_K), lambda i: (rev(i), 0, 0, 0)),
                      (st, (1, GLA_H, GLA_V, GLA_K), lambda i: (jnp.maximum(rev(i) * ncb - 1, 0), 0, 0, 0)),
                      (dmix, (tm, GLA_DV), lambda i: (rev(i), 0))],
                 outs=[((T, 2 * GLA_DK + 2 * GLA_DV), MXU, (tm, 2 * GLA_DK + 2 * GLA_DV), lambda i: (rev(i), 0)),
                       ((T, LANES), MXU, (tm, LANES), lambda i: (rev(i), 0)),
                       ((LANES, GLA_DK), F32, (LANES, GLA_DK), fix),
                       ((1, GLA_DK), F32, (1, GLA_DK), fix), ((1, GLA_V), F32, (1, GLA_V), fix)],
                 scratch=[pltpu.VMEM((GLA_H, GLA_V, GLA_K), F32), pltpu.VMEM((tm, GLA_DK), F32),
                          pltpu.VMEM((tm, GLA_DK), F32), pltpu.VMEM((ncb, GLA_H, GLA_V, GLA_K), F32)])


def local_step(x, tgt, W, late=None):
    row1 = lambda a, l: a[l:l + 1]
    hn0 = rms_fwd(x, row1(W["mix_norm"], 0), name="l0_norm")
    pg = mm([(hn0, W["wi_g"])], name="l0_proj_gla", tn=512)
    ps = mm([(hn0, W["wi_s"])], name="l0_proj_sb", tn=512)
    glr = mm([(hn0, W["wi_l"])], name="l0_proj_gate", out_dtype=MXU)
    og, st = gla_fwd(pg, glr, W["w2"], W["b_gate"], W["g_gla"], name="gla_fwd")
    qn, kn, vh = sb_prep(ps, W["g_q"], W["g_k"], name="sb_prep")
    if late is None:
        osb, sb_cb = sb_fwd(qn, kn, vh, name="sb_fwd")
    else:
        osb, sb_cb, gathered = sb_fwd(qn, kn, vh, name="sb_fwd", fused=(late["src"][0], N_CHIPS, GATHER))
        W = {**W, **late["unpack"][0](gathered)}
    h1 = mm([(og, W["wo_g"]), (osb, W["wo_s"])], res=x, name="l0_out")
    h2, ffn0, landed = ffn_fwd(h1, row1(W["ffn_norm"], 0), W["wg0"], W["wu0"], W["wd0"], 0,
                               fused=None if late is None else (late["src"][1], N_CHIPS, GATHER))
    if late is not None:
        W = {**W, **late["unpack"][1](landed[0])}
    hn1 = rms_fwd(h2, row1(W["mix_norm"], 1), name="l1_norm")
    A = mm([(hn1, W["pw1"])], bias=W["b_pw1"], name="l1_pw1")
    s, cconv = conv_fwd(A, W["w_dw"], W["b_dw"], W["ln_g"], W["ln_b"], name="conv_fwd")
    h3 = mm([(s, W["pw2"])], bias=W["b_pw2"], res=h2, name="l1_pw2")
    y, ffn1, _ = ffn_fwd(h3, row1(W["ffn_norm"], 1), W["wg1"], W["wu1"], W["wd1"], 1)
    dy, loss_lanes = loss_grad(y, tgt, name="loss")
    G = {}
    dh3, g_fn1, G["wg1"], G["wu1"], G["wd1"], cs3 = ffn_bwd(
        dy, h3, row1(W["ffn_norm"], 1), W["wg1"], W["wu1"], W["wd1"], ffn1, 1)
    G["b_pw2"] = cs3
    ds = mm([(dh3, W["pw2"])], trans_b=True, name="l1_ds")
    G["pw2"] = mm_tn(s, dh3, name="l1_dpw2", tm=1024, tn=1024)
    dc, G["ln_g"], G["ln_b"] = ln_swish_bwd(cconv, ds, W["ln_g"], W["ln_b"], name="ln_bwd")
    dA, G["w_dw"], G["b_dw"], G["b_pw1"] = conv_bwd(dc, A, W["w_dw"], name="conv_bwd")
    G["pw1"] = mm_tn(hn1, dA, name="l1_dpw1", tm=1024, tn=1024)
    dhn1 = mm([(dA, W["pw1"])], trans_b=True, name="l1_dhn")
    dh2, g_mn1, _ = rms_bwd(h2, row1(W["mix_norm"], 1), dhn1, dh3, name="l1_norm_bwd")
    dh1, g_fn0, G["wg0"], G["wu0"], G["wd0"], _ = ffn_bwd(
        dh2, h1, row1(W["ffn_norm"], 0), W["wg0"], W["wu0"], W["wd0"], ffn0, 0)
    dmix = mm([(dh1, W["wo_gs"])], trans_b=True, name="l0_dmix")
    G["wo_g"] = mm_tn(og, dh1, name="l0_dwo_g", tm=512, tn=1024)
    G["wo_s"] = mm_tn(osb, dh1, name="l0_dwo_s", tm=512, tn=1024)
    slots = None
    if late is None:
        dqn, dkn, dvh = sb_bwd(qn, kn, vh, sb_cb, dmix, GLA_DV, name="sb_bwd")
    else:
        dqn, dkn, dvh, slots = sb_bwd(qn, kn, vh, sb_cb, dmix, GLA_DV, name="sb_bwd",
                                      fused=(late["pack"](G), N_DEV, SCATTER))
    dps, G["g_q"], G["g_k"] = sb_post_bwd(dqn, dkn, dvh, ps, W["g_q"], W["g_k"], name="sb_post_bwd")
    dpg, dglr, G["w2"], G["b_gate"], G["g_gla"] = gla_bwd(
        pg, glr, W["w2"], W["b_gate"], W["g_gla"], st, dmix, name="gla_bwd")
    G["wi_g"] = mm_tn(hn0, dpg, name="l0_dwi_g", tm=1024, tn=512)
    G["wi_s"] = mm_tn(hn0, dps, name="l0_dwi_s", tm=1024, tn=512)
    G["wi_l"] = mm_tn(hn0, dglr, name="l0_dwi_l", tm=1024, tn=LANES)
    first_slots = None
    dhn_pairs = [(dpg, W["wi_g"]), (dps, W["wi_s"]), (dglr, W["wi_l"])]
    if late is None:
        dhn0 = mm(dhn_pairs, trans_b=True, name="l0_dhn")
    else:
        dhn0, first_slots = mm(dhn_pairs, trans_b=True, name="l0_dhn", fused=(late["pack_first"](G), N_DEV, SCATTER))
    dx, g_mn0, _ = rms_bwd(x, row1(W["mix_norm"], 0), dhn0, dh1, name="l0_norm_bwd")
    G["mix_norm"] = jnp.concatenate([g_mn0, g_mn1], axis=0)
    G["ffn_norm"] = jnp.concatenate([g_fn0, g_fn1], axis=0)
    return loss_lanes, dx, G, slots, first_slots


_C_GLA = 2 * GLA_DK + 2 * GLA_DV
_C_SB0 = _C_GLA + GLA_RANK


ITEMS = {
    "w_in": ("hy_w_in", 0, (D, _C_SB0 + 3 * SB_H * SB_D), 1), "wo_gs": ("hy_w_out", 0, (GLA_DV + SB_H * SB_D, D), 0),
    "w2r": ("hy_w_gate2", 0, (GLA_RANK, GLA_DK), 1), "b_pw1": ("cv_b_pw1", None, (1, 2 * D), 1),
    "w_dwr": ("cv_w_dw", 0, (CONV_W, D), 1), "b_dw": ("cv_b_dw", None, (1, D), 1),
    "ln_g": ("cv_ln_g", None, (1, D), 1), "ln_b": ("cv_ln_b", None, (1, D), 1), "b_pw2": ("cv_b_pw2", None, (1, D), 1),
    "pw1": ("cv_w_pw1", 0, (D, 2 * D), 1), "pw2": ("cv_w_pw2", 0, (D, D), 0),
    "wg0": ("ffn_w_gate", 0, (D, F), 1), "wg1": ("ffn_w_gate", 1, (D, F), 1),
    "wu0": ("ffn_w_up", 0, (D, F), 1), "wu1": ("ffn_w_up", 1, (D, F), 1),
    "wd0": ("ffn_w_down", 0, (F, D), 0), "wd1": ("ffn_w_down", 1, (F, D), 0),
}
FIRST_BIG_ITEMS = ["w_in", "wo_gs"]
SMALL_ITEMS = ["w2r", "b_pw1", "w_dwr", "b_dw", "ln_g", "ln_b", "b_pw2"]
LATE_ITEMS = ["pw1", "pw2", "wg0", "wg1", "wu0", "wu1", "wd0", "wd1"]


def layout_first(raw, repl):
    w_in = raw["w_in"]
    W = {
        "wi_g": w_in[:, :_C_GLA].astype(MXU),
        "wi_s": w_in[:, _C_SB0:].astype(MXU),
        "wi_l": jnp.pad(w_in[:, _C_GLA:_C_SB0], ((0, 0), (0, LANES - GLA_RANK))).astype(MXU),
        "w2": jnp.pad(raw["w2r"], ((0, LANES - GLA_RANK), (0, 0))),
        "b_gate": repl["hy_b_gate"], "g_gla": repl["hy_gla_norm"],
        "g_q": repl["hy_sb_q_norm"], "g_k": repl["hy_sb_k_norm"],
        "wo_gs": raw["wo_gs"].astype(MXU),
        "b_pw1": raw["b_pw1"], "w_dw": jnp.pad(raw["w_dwr"], ((0, HALO - CONV_W), (0, 0))),
        "b_dw": raw["b_dw"], "ln_g": raw["ln_g"], "ln_b": raw["ln_b"], "b_pw2": raw["b_pw2"],
        "mix_norm": repl["mix_norm"], "ffn_norm": repl["ffn_norm"],
    }
    W["wo_g"] = W["wo_gs"][:GLA_DV]
    W["wo_s"] = W["wo_gs"][GLA_DV:]
    return W


def layout_late(raw):
    return {k: raw[k].astype(MXU) for k in LATE_ITEMS}


def item_grads(G):
    out = {k: G[k] for k in ITEMS if k in G}
    if "wi_g" in G:
        out["w_in"] = jnp.concatenate([G["wi_g"], G["wi_l"][:, :GLA_RANK], G["wi_s"]], axis=1)
        out["wo_gs"] = jnp.concatenate([G["wo_g"], G["wo_s"]], axis=0)
        out["w2r"] = G["w2"][:GLA_RANK]
        out["w_dwr"] = G["w_dw"][:CONV_W]
    return out


def _chip_major(a, ax):
    r, c = a.shape
    if ax == 1:
        a = a.reshape(r, N_CHIPS, c // N_CHIPS).transpose(1, 0, 2)
    return a.reshape(N_CHIPS, -1, LANES)


def _from_chip_major(p, shape, ax):
    r, c = shape
    if ax == 0:
        return p.reshape(r, c)
    return p.reshape(N_CHIPS, r, c // N_CHIPS).transpose(1, 0, 2).reshape(r, c)


def _item_rows(k):
    n = math.prod(ITEMS[k][2]) // N_CHIPS
    assert n % LANES == 0, k
    return n // LANES


def unpack_items(buf, keys):
    out, off = {}, 0
    for k in keys:
        _, _, shape, ax = ITEMS[k]
        out[k] = _from_chip_major(buf[:, off:off + _item_rows(k)], shape, ax)
        off += _item_rows(k)
    return out


def pack_item_grads(raw, keys, dtype=F32):
    cat = jnp.concatenate([_chip_major(raw[k].astype(dtype), ITEMS[k][3]) for k in keys], axis=1)
    n = cat.shape[1]
    rows = -(-n // GRAD_ROWS) * GRAD_ROWS
    return jnp.pad(cat, ((0, 0), (0, rows - n), (0, 0))).reshape(2 * N_CHIPS, rows // 2, LANES)


def _refs_of(keys):
    return list(dict.fromkeys(ITEMS[k][0] for k in keys))


WEIGHTS = ["mix_norm", "ffn_norm", "hy_w_in", "hy_w_gate2", "hy_b_gate", "hy_gla_norm", "hy_sb_q_norm",
           "hy_sb_k_norm", "hy_w_out", "cv_w_pw1", "cv_b_pw1", "cv_w_dw", "cv_b_dw", "cv_ln_g", "cv_ln_b",
           "cv_w_pw2", "cv_b_pw2", "ffn_w_gate", "ffn_w_up", "ffn_w_down"]
REPL =["mix_norm", "ffn_norm", "hy_b_gate", "hy_gla_norm", "hy_sb_q_norm", "hy_sb_k_norm"]
N_CHIPS = 4
N_DEV = 8
GRAD_ROWS = 1024


def _pack(arrs, rows_multiple, dtype):
    flat = jnp.concatenate([a.reshape(-1).astype(dtype) for a in arrs])
    n = flat.shape[0]
    rows = -(-n // (LANES * rows_multiple)) * rows_multiple
    return jnp.pad(flat, (0, rows * LANES - n)).reshape(rows, LANES)


def _unpack(flat2d, shapes):
    flat = flat2d.reshape(-1)
    out, off = [], 0
    for s in shapes:
        n = math.prod(s)
        out.append(flat[off:off + n].reshape(s))
        off += n
    return out


def _coords():
    return lax.axis_index("x"), lax.axis_index("y"), lax.axis_index("c")


def _flip(pos, f):
    return tuple(1 - p if b else p for p, b in zip(pos, f))


def _exchange_copies(src_ref, dst_ref, send_sems, recv_sems, loc_sem, *, flips, src_idx, dst_idx, local_idx,
                     with_recvs=True):
    me = _coords()
    loc = None
    if local_idx is not None:
        si, di = local_idx(me)
        loc = pltpu.make_async_copy(src_ref.at[si], dst_ref.at[di], loc_sem)
    sends, recvs = [], []
    for k, f in enumerate(flips):
        peer = _flip(me, f)
        sends.append(pltpu.make_async_remote_copy(
            src_ref=src_ref.at[src_idx(me, peer)], dst_ref=dst_ref.at[dst_idx(me)],
            send_sem=send_sems.at[k], recv_sem=recv_sems.at[k],
            device_id=peer, device_id_type=pl.DeviceIdType.MESH))
        if with_recvs:
            recvs.append(pltpu.make_async_remote_copy(
                src_ref=src_ref.at[src_idx(peer, me)], dst_ref=dst_ref.at[dst_idx(peer)],
                send_sem=send_sems.at[k], recv_sem=recv_sems.at[k],
                device_id=peer, device_id_type=pl.DeviceIdType.MESH))
    return loc, sends, recvs


def _exchange_start(*refs, **spec):
    loc, sends, _ = _exchange_copies(*refs, with_recvs=False, **spec)
    if loc is not None:
        loc.start()
    for s in sends:
        s.start()


def _exchange_wait(*refs, **spec):
    loc, sends, recvs = _exchange_copies(*refs, **spec)
    for s in sends:
        s.wait_send()
    for r in recvs:
        r.wait_recv()
    if loc is not None:
        loc.wait()


def exchange(src, n_dst, spec, *, name, in_place=False):
    n = len(spec["flips"])
    assert not in_place or (n_dst == src.shape[0] and spec["local_idx"] is None)

    def body(*refs):
        _exchange_start(*refs, **spec)
        _exchange_wait(*refs, **spec)

    return pl.pallas_call(
        body, name=name,
        out_shape=jax.ShapeDtypeStruct((n_dst,) + src.shape[1:], src.dtype),
        in_specs=[pl.BlockSpec(memory_space=pl.ANY)],
        out_specs=pl.BlockSpec(memory_space=pl.ANY),
        scratch_shapes=[pltpu.SemaphoreType.DMA((n,)), pltpu.SemaphoreType.DMA((n,)), pltpu.SemaphoreType.DMA(())],
        input_output_aliases={0: 0} if in_place else {},
    )(src)


CHIP_FLIPS = [(1, 0, 0), (0, 1, 0), (1, 1, 0)]
SIBLING = [(0, 0, 1)]
ALL_FLIPS = [(a, b, c) for a in (0, 1) for b in (0, 1) for c in (0, 1) if (a, b, c) != (0, 0, 0)]


def _chip(pos):
    return 2 * pos[0] + pos[1]


def _dev(pos):
    return 4 * pos[0] + 2 * pos[1] + pos[2]


GATHER = dict(flips=CHIP_FLIPS, src_idx=lambda me, peer: 0, dst_idx=_chip, local_idx=lambda me: (0, _chip(me)))
SCATTER = dict(flips=ALL_FLIPS, src_idx=lambda me, peer: 2 * _chip(peer) + peer[2], dst_idx=_dev,
               local_idx=lambda me: (2 * _chip(me) + me[2], _dev(me)))
SHARE = dict(flips=SIBLING, src_idx=lambda me, peer: me[2], dst_idx=lambda me: me[2], local_idx=None)
ALL_TO_ALL = dict(flips=ALL_FLIPS, src_idx=lambda me, peer: 0, dst_idx=_dev, local_idx=lambda me: (0, _dev(me)))


def sum_slots(x, *, name, tr=512):
    n, R, L = x.shape
    tr = _rows(R, tr)

    def body(x_ref, o_ref):
        acc = x_ref[0]
        for k in range(1, n):
            acc = acc + x_ref[k]
        o_ref[...] = acc

    return _call(body, name=name, grid=(R // tr,),
                 ins=[(x, (n, tr, L), lambda i: (0, i, 0))],
                 outs=[((R, L), F32, (tr, L), lambda i: (i, 0))])[0]


def sum_slots_into_half(x, half, *, name, tr=512):
    n, R, L = x.shape
    tr = _rows(R, tr)

    def body(h_ref, x_ref, o_ref):
        acc = x_ref[0].astype(F32)
        for k in range(1, n):
            acc = acc + x_ref[k].astype(F32)
        o_ref[0] = acc

    return pl.pallas_call(
        body, name=name,
        grid_spec=pltpu.PrefetchScalarGridSpec(
            num_scalar_prefetch=1, grid=(R // tr,),
            in_specs=[pl.BlockSpec((n, tr, L), lambda i, h: (0, i, 0))],
            out_specs=pl.BlockSpec((1, tr, L), lambda i, h: (h[0], i, 0))),
        out_shape=jax.ShapeDtypeStruct((2, R, L), F32),
        compiler_params=pltpu.CompilerParams(dimension_semantics=("arbitrary",), vmem_limit_bytes=VMEM_LIMIT),
    )(jnp.reshape(half, (1,)).astype(jnp.int32), x)


ADAM_ROWS = 256


def adamw(w, g, m, v, *, name):
    shape = w.shape
    w2, g2, m2, v2 = (a.reshape(-1, shape[-1]) for a in (w, g, m, v))
    R, L = w2.shape
    tr = ADAM_ROWS if R % ADAM_ROWS == 0 else next(
        (t for t in range(min(R, 2 * ADAM_ROWS) // SUBLANES * SUBLANES, 0, -SUBLANES) if R % t == 0), R)

    def body(w_ref, g_ref, m_ref, v_ref, d_ref, mo_ref, vo_ref):
        g = g_ref[...]
        m = ADAM_B1 * m_ref[...] + (1.0 - ADAM_B1) * g
        v = ADAM_B2 * v_ref[...] + (1.0 - ADAM_B2) * (g * g)
        m_hat = m / (1.0 - ADAM_B1 ** ADAM_STEP)
        v_hat = v / (1.0 - ADAM_B2 ** ADAM_STEP)
        d_ref[...] = -ADAM_LR * (m_hat / (jnp.sqrt(v_hat) + ADAM_EPS) + ADAM_WD * w_ref[...])
        mo_ref[...] = m
        vo_ref[...] = v

    row = lambda i: (i, 0)
    outs = _call(body, name=name, grid=(R // tr,),
                 ins=[(a, (tr, L), row) for a in (w2, g2, m2, v2)],
                 outs=[((R, L), F32, (tr, L), row)] * 3)
    return [o.reshape(shape) for o in outs]


def kernel(x, mix_norm, ffn_norm, hy_w_in, hy_w_gate2, hy_b_gate, hy_gla_norm, hy_sb_q_norm, hy_sb_k_norm, hy_w_out, cv_w_pw1, cv_b_pw1, cv_w_dw, cv_b_dw, cv_ln_g, cv_ln_b, cv_w_pw2, cv_b_pw2, ffn_w_gate, ffn_w_up, ffn_w_down, loss_target, m_mix_norm, m_ffn_norm, m_hy_w_in, m_hy_w_gate2, m_hy_b_gate, m_hy_gla_norm, m_hy_sb_q_norm, m_hy_sb_k_norm, m_hy_w_out, m_cv_w_pw1, m_cv_b_pw1, m_cv_w_dw, m_cv_b_dw, m_cv_ln_g, m_cv_ln_b, m_cv_w_pw2, m_cv_b_pw2, m_ffn_w_gate, m_ffn_w_up, m_ffn_w_down, v_mix_norm, v_ffn_norm, v_hy_w_in, v_hy_w_gate2, v_hy_b_gate, v_hy_gla_norm, v_hy_sb_q_norm, v_hy_sb_k_norm, v_hy_w_out, v_cv_w_pw1, v_cv_b_pw1, v_cv_w_dw, v_cv_b_dw, v_cv_ln_g, v_cv_ln_b, v_cv_w_pw2, v_cv_b_pw2, v_ffn_w_gate, v_ffn_w_up, v_ffn_w_down):
    w = dict(zip(WEIGHTS, (mix_norm, ffn_norm, hy_w_in, hy_w_gate2, hy_b_gate, hy_gla_norm, hy_sb_q_norm, hy_sb_k_norm, hy_w_out, cv_w_pw1, cv_b_pw1, cv_w_dw, cv_b_dw, cv_ln_g, cv_ln_b, cv_w_pw2, cv_b_pw2, ffn_w_gate, ffn_w_up, ffn_w_down)))
    m = dict(zip(WEIGHTS, (m_mix_norm, m_ffn_norm, m_hy_w_in, m_hy_w_gate2, m_hy_b_gate, m_hy_gla_norm, m_hy_sb_q_norm, m_hy_sb_k_norm, m_hy_w_out, m_cv_w_pw1, m_cv_b_pw1, m_cv_w_dw, m_cv_b_dw, m_cv_ln_g, m_cv_ln_b, m_cv_w_pw2, m_cv_b_pw2, m_ffn_w_gate, m_ffn_w_up, m_ffn_w_down)))
    v = dict(zip(WEIGHTS, (v_mix_norm, v_ffn_norm, v_hy_w_in, v_hy_w_gate2, v_hy_b_gate, v_hy_gla_norm, v_hy_sb_q_norm, v_hy_sb_k_norm, v_hy_w_out, v_cv_w_pw1, v_cv_b_pw1, v_cv_w_dw, v_cv_b_dw, v_cv_ln_g, v_cv_ln_b, v_cv_w_pw2, v_cv_b_pw2, v_ffn_w_gate, v_ffn_w_up, v_ffn_w_down)))
    def local_shards(keys, rows_multiple, dtype):
        item = lambda ref, idx: w[ref] if idx is None else w[ref][idx]
        return _pack([item(*ITEMS[k][:2]) for k in keys], rows_multiple, dtype)[None]

    def finish_reduce(slots, tag):
        halves = sum_slots_into_half(slots, lax.axis_index("c"), name=f"reduce_{tag}_sum")
        return exchange(halves, 2, SHARE, name=f"reduce_{tag}_share", in_place=True).reshape(2 * slots.shape[1], LANES)

    first = exchange(local_shards(FIRST_BIG_ITEMS, 32, MXU), N_CHIPS, GATHER, name="gather_first")
    small = exchange(local_shards(SMALL_ITEMS, 8, F32), N_CHIPS, GATHER, name="gather_small")
    raw = {**unpack_items(first, FIRST_BIG_ITEMS), **unpack_items(small, SMALL_ITEMS)}
    stages = (["wg0", "wu0", "wd0"], ["pw1", "pw2", "wg1", "wu1", "wd1"])
    late = {"src": [local_shards(keys, 32, MXU) for keys in stages],
            "unpack": [functools.partial(unpack_items, keys=keys) for keys in stages],
            "pack": lambda G: pack_item_grads(item_grads(G), LATE_ITEMS, MXU),
            "pack_first": lambda G: pack_item_grads(item_grads(G), FIRST_BIG_ITEMS + SMALL_ITEMS)}

    loss_lanes, dx, Gk, late_slots, first_slots = local_step(
        x[0], loss_target[0], layout_first(raw, {n: w[n] for n in REPL}), late)
    rep_names = {"mix_norm": "mix_norm", "ffn_norm": "ffn_norm", "hy_b_gate": "b_gate", "hy_gla_norm": "g_gla",
                 "hy_sb_q_norm": "g_q", "hy_sb_k_norm": "g_k"}
    Gr = {n: Gk[k] for n, k in rep_names.items()}

    groups = [(_refs_of(FIRST_BIG_ITEMS + SMALL_ITEMS), finish_reduce(first_slots, "first")),
              (_refs_of(LATE_ITEMS), finish_reduce(late_slots, "late"))]

    loss_row = jnp.pad(jnp.sum(loss_lanes).reshape(1), (0, LANES - 1))
    rep = _pack([Gr[n] for n in REPL] + [loss_row], 8, F32)
    rep_all = exchange(rep[None], N_DEV, ALL_TO_ALL, name="reduce_small")
    rep_sum = sum_slots(rep_all, name="reduce_small_sum", tr=rep.shape[0])
    rep_shapes = [w[n].shape for n in REPL]
    rep_n = sum(math.prod(s) for s in rep_shapes)
    loss = rep_sum.reshape(-1)[rep_n]

    grads = dict(zip(REPL, _unpack(rep_sum, rep_shapes)))
    for names, g_flat in groups:
        grads.update(dict(zip(names, _unpack(g_flat, [w[n].shape for n in names]))))
    out = {"grad": grads, "delta": {}, "new_m": {}, "new_v": {}}
    small_refs = _refs_of(SMALL_ITEMS) + REPL
    for n in WEIGHTS:
        if n not in small_refs:
            out["delta"][n], out["new_m"][n], out["new_v"][n] = adamw(w[n], grads[n], m[n], v[n], name=f"adamw_{n}")
    packed = [_pack([t[n] for n in small_refs], 8, F32) for t in (w, grads, m, v)]
    small_shapes = [w[n].shape for n in small_refs]
    for key, buf in zip(("delta", "new_m", "new_v"), adamw(*packed, name="adamw_small")):
        out[key].update(dict(zip(small_refs, _unpack(buf, small_shapes))))

    return (loss, dx[None], *[out["grad"][n] for n in WEIGHTS], *[out["delta"][n] for n in WEIGHTS],
            *[out["new_m"][n] for n in WEIGHTS], *[out["new_v"][n] for n in WEIGHTS])
```

```python
import functools
import math

import numpy as np
import jax
import jax.numpy as jnp
from jax import lax
from jax.experimental import pallas as pl
from jax.experimental.pallas import tpu as pltpu

F32 = jnp.float32
MXU = jnp.bfloat16
EPS = 1e-6
LANES = 128
VMEM_LIMIT = 56 * 1024 * 1024

D = 1024
F = 2816
CHUNK = 64
GLA_H, GLA_K, GLA_V, GLA_RANK = 4, 64, 128, 16
SB_H, SB_D = 8, 64
CONV_W = 31
HALO = 32

ADAM_LR, ADAM_B1, ADAM_B2, ADAM_EPS, ADAM_WD, ADAM_STEP = 0.001, 0.9, 0.999, 1e-08, 0.01, 10

NN = (((1,), (0,)), ((), ()))
NT = (((1,), (1,)), ((), ()))
TN = (((0,), (0,)), ((), ()))


def _dot(a, b, dims=NN):
    return lax.dot_general(a.astype(MXU), b.astype(MXU), dims, preferred_element_type=F32)


def _dot_split(a, b, dims=NN, a_split=True):
    x = a if a_split else b
    hi = x.astype(MXU)
    lo = (x - hi.astype(F32)).astype(MXU)
    if a_split:
        return _dot(hi, b, dims) + _dot(lo, b, dims)
    return _dot(a, hi, dims) + _dot(a, lo, dims)


def _sigmoid(x):
    return 1.0 / (1.0 + jnp.exp(-x))


def _softplus_neg_abs(z):
    return jnp.log(1.0 + jnp.exp(-jnp.abs(z)))


def _call(body, *, name, grid, ins, outs, scratch=()):
    spec = lambda b, m: pl.BlockSpec(memory_space=pl.ANY) if b is None else pl.BlockSpec(b, m)
    res = pl.pallas_call(
        body,
        name=name,
        grid=grid,
        in_specs=[spec(b, m) for _, b, m in ins],
        out_specs=[spec(b, m) for _, _, b, m in outs],
        out_shape=[jax.ShapeDtypeStruct(s, d) for s, d, _, _ in outs],
        scratch_shapes=list(scratch),
        compiler_params=pltpu.CompilerParams(
            dimension_semantics=("arbitrary",) * len(grid), vmem_limit_bytes=VMEM_LIMIT),
    )(*[a for a, _, _ in ins])
    return res


def _rows(T, tm):
    tm = min(tm, T)
    assert T % tm == 0, (T, tm)
    return tm


def mm(pairs, *, name, trans_b=False, bias=None, res=None, out_dtype=F32, tm=512, tn=512, fused=None):
    M = pairs[0][0].shape[0]
    N = pairs[0][1].shape[0] if trans_b else pairs[0][1].shape[1]
    tm = _rows(M, tm)
    tn = min(tn, N)
    assert N % tn == 0, (N, tn)
    np_ = len(pairs)
    ni, nj = M // tm, N // tn
    pred = lambda: ((pl.program_id(0) == 0) & (pl.program_id(1) == 0),
                    (pl.program_id(0) == ni - 1) & (pl.program_id(1) == nj - 1))
    x_ins, x_outs, x_scratch, x_start, x_end = _fused_exchange(fused, pred)
    n_in = 2 * np_ + (bias is not None) + (res is not None)

    def body(*refs):
        o_ref = refs[n_in + len(x_ins)]
        x_refs = refs[n_in:n_in + len(x_ins)] + refs[n_in + len(x_ins) + 1:]
        x_start(x_refs)
        acc = None
        for p in range(np_):
            d = _dot(refs[2 * p][...], refs[2 * p + 1][...], NT if trans_b else NN)
            acc = d if acc is None else acc + d
        k = 2 * np_
        if bias is not None:
            acc = acc + refs[k][...]
            k += 1
        if res is not None:
            acc = acc + refs[k][...]
        o_ref[...] = acc.astype(out_dtype)
        x_end(x_refs)

    ins = []
    for a, b in pairs:
        K = a.shape[1]
        ins.append((a, (tm, K), lambda i, j: (i, 0)))
        if trans_b:
            ins.append((b, (tn, K), lambda i, j: (j, 0)))
        else:
            ins.append((b, (K, tn), lambda i, j: (0, j)))
    if bias is not None:
        ins.append((bias, (1, tn), lambda i, j: (0, j)))
    if res is not None:
        ins.append((res, (tm, tn), lambda i, j: (i, j)))
    outs = _call(body, name=name, grid=(ni, nj), ins=ins + x_ins,
                 outs=[((M, N), out_dtype, (tm, tn), lambda i, j: (i, j))] + x_outs, scratch=x_scratch)
    return outs[0] if fused is None else outs


def mm_tn(a, b, *, name, tm=512, tn=512, tk=1024):
    T, M = a.shape
    N = b.shape[1]
    tm, tn, tk = min(tm, M), min(tn, N), min(tk, T)
    assert M % tm == 0 and N % tn == 0 and T % tk == 0, (M, N, T, tm, tn, tk)

    def body(a_ref, b_ref, o_ref):
        @pl.when(pl.program_id(2) == 0)
        def _():
            o_ref[...] = jnp.zeros_like(o_ref)
        o_ref[...] += _dot(a_ref[...], b_ref[...], TN)

    return _call(body, name=name, grid=(M // tm, N // tn, T // tk),
                 ins=[(a, (tk, tm), lambda i, j, k: (k, i)), (b, (tk, tn), lambda i, j, k: (k, j))],
                 outs=[((M, N), F32, (tm, tn), lambda i, j, k: (i, j))])[0]


def rms_fwd(x, g, *, name, tm=256):
    T = x.shape[0]
    tm = _rows(T, tm)

    def body(x_ref, g_ref, o_ref):
        x = x_ref[...]
        r = lax.rsqrt(jnp.mean(x * x, axis=-1, keepdims=True) + EPS)
        o_ref[...] = (x * r * g_ref[...]).astype(MXU)

    return _call(body, name=name, grid=(T // tm,),
                 ins=[(x, (tm, D), lambda i: (i, 0)), (g, (1, D), lambda i: (0, 0))],
                 outs=[((T, D), MXU, (tm, D), lambda i: (i, 0))])[0]


def rms_bwd(x, g, dhn, dres, *, name, tm=256):
    T = x.shape[0]
    tm = _rows(T, tm)

    def body(x_ref, g_ref, d_ref, r_ref, dx_ref, dg_ref, cs_ref):
        @pl.when(pl.program_id(0) == 0)
        def _():
            dg_ref[...] = jnp.zeros_like(dg_ref)
            cs_ref[...] = jnp.zeros_like(cs_ref)
        x = x_ref[...]
        d = d_ref[...]
        r = lax.rsqrt(jnp.mean(x * x, axis=-1, keepdims=True) + EPS)
        dg_ref[...] += jnp.sum(d * x * r, axis=0, keepdims=True)
        t = d * g_ref[...]
        m = jnp.mean(t * x, axis=-1, keepdims=True)
        dx = r_ref[...] + t * r - x * (r * r * r) * m
        dx_ref[...] = dx
        cs_ref[...] += jnp.sum(dx, axis=0, keepdims=True)

    row = lambda i: (i, 0)
    fix = lambda i: (0, 0)
    return _call(body, name=name, grid=(T // tm,),
                 ins=[(x, (tm, D), row), (g, (1, D), fix), (dhn, (tm, D), row), (dres, (tm, D), row)],
                 outs=[((T, D), F32, (tm, D), row), ((1, D), F32, (1, D), fix), ((1, D), F32, (1, D), fix)])


def loss_grad(y, t, *, name, tm=256):
    T = y.shape[0]
    tm = _rows(T, tm)

    def body(y_ref, t_ref, dy_ref, l_ref):
        @pl.when(pl.program_id(0) == 0)
        def _():
            l_ref[...] = jnp.zeros_like(l_ref)
        e = y_ref[...] - t_ref[...]
        dy_ref[...] = e * (1.0 / D)
        l_ref[...] += jnp.sum(e * e, axis=0, keepdims=True) * (0.5 / D)

    row = lambda i: (i, 0)
    return _call(body, name=name, grid=(T // tm,),
                 ins=[(y, (tm, D), row), (t, (tm, D), row)],
                 outs=[((T, D), F32, (tm, D), row), ((1, D), F32, (1, D), lambda i: (0, 0))])


def ffn_up(hn, wg, wu, *, name, tm=1024, tn=256, fused=None):
    T = hn.shape[0]
    tm = _rows(T, tm)
    ni, nj = T // tm, F // tn
    pred = lambda: ((pl.program_id(0) == 0) & (pl.program_id(1) == 0),
                    (pl.program_id(0) == ni - 1) & (pl.program_id(1) == nj - 1))
    x_ins, x_outs, x_scratch, x_start, x_end = _fused_exchange(fused, pred)
    nx = len(x_ins)

    def body(*refs):
        h_ref, wg_ref, wu_ref = refs[:3]
        g_ref, u_ref, a_ref = refs[3 + nx:6 + nx]
        x_refs = refs[3:3 + nx] + refs[6 + nx:]
        x_start(x_refs)
        h = h_ref[...]
        g = _dot(h, wg_ref[...])
        u = _dot(h, wu_ref[...])
        g_ref[...] = g.astype(MXU)
        u_ref[...] = u.astype(MXU)
        a_ref[...] = (g * _sigmoid(g) * u).astype(MXU)
        x_end(x_refs)

    tile = lambda i, j: (i, j)
    return _call(body, name=name, grid=(ni, nj),
                 ins=[(hn, (tm, D), lambda i, j: (i, 0)), (wg, (D, tn), lambda i, j: (0, j)),
                      (wu, (D, tn), lambda i, j: (0, j))] + x_ins,
                 outs=[((T, F), MXU, (tm, tn), tile), ((T, F), MXU, (tm, tn), tile),
                       ((T, F), MXU, (tm, tn), tile)] + x_outs,
                 scratch=x_scratch)


def ffn_bwd_act(dy, wd, G, U, *, name, tm=1024, tn=256):
    T = dy.shape[0]
    tm = _rows(T, tm)

    def body(dy_ref, wd_ref, g_ref, u_ref, dg_ref, du_ref, a_ref):
        da = _dot(dy_ref[...], wd_ref[...], NT)
        g = g_ref[...].astype(F32)
        u = u_ref[...].astype(F32)
        s = _sigmoid(g)
        sil = g * s
        a_ref[...] = (sil * u).astype(MXU)
        du_ref[...] = (da * sil).astype(MXU)
        dg_ref[...] = (da * u * (s * (1.0 + g * (1.0 - s)))).astype(MXU)

    tile = lambda i, j: (i, j)
    return _call(body, name=name, grid=(T // tm, F // tn),
                 ins=[(dy, (tm, D), lambda i, j: (i, 0)), (wd, (tn, D), lambda i, j: (j, 0)),
                      (G, (tm, tn), tile), (U, (tm, tn), tile)],
                 outs=[((T, F), MXU, (tm, tn), tile), ((T, F), MXU, (tm, tn), tile),
                       ((T, F), MXU, (tm, tn), tile)])


def ffn_fwd(h, g_norm, wg, wu, wd, tag, fused=None):
    hn = rms_fwd(h, g_norm, name=f"ffn{tag}_norm")
    G, U, act, *landed = ffn_up(hn, wg, wu, name=f"ffn{tag}_up", fused=fused)
    h_out = mm([(act, wd)], res=h, name=f"ffn{tag}_down")
    return h_out, (hn, G, U), landed


def ffn_bwd(dy, h, g_norm, wg, wu, wd, saved, tag):
    hn, G, U = saved
    dG, dU, act = ffn_bwd_act(dy, wd, G, U, name=f"ffn{tag}_bwd_act")
    d_wd = mm_tn(act, dy, name=f"ffn{tag}_dwd", tm=1408, tn=512)
    d_wg = mm_tn(hn, dG, name=f"ffn{tag}_dwg", tm=512, tn=1408)
    d_wu = mm_tn(hn, dU, name=f"ffn{tag}_dwu", tm=512, tn=1408)
    dhn = mm([(dG, wg), (dU, wu)], trans_b=True, name=f"ffn{tag}_dhn")
    dh, d_g, cs = rms_bwd(h, g_norm, dhn, dy, name=f"ffn{tag}_norm_bwd")
    return dh, d_g, d_wg, d_wu, d_wd, cs


SUBLANES = 8


def _window_scratch(tm):
    return [pltpu.VMEM((tm + HALO + SUBLANES, D), F32), pltpu.VMEM((SUBLANES, tm + HALO, D), F32)]


def _shift_copies(win, sh):
    rows = sh.shape[1]
    win[pl.ds(rows, SUBLANES), :] = jnp.zeros((SUBLANES, D), F32)
    for s in range(SUBLANES):
        sh[s] = win[pl.ds(s, rows), :]


def _tap(sh, off, rb):
    s = off % SUBLANES
    return sh[s, pl.ds(off - s, rb), :]


def conv_fwd(A, w_dw, b_dw, ln_g, ln_b, *, name, tm=128, rb=32, fused=None):
    T = A.shape[0]
    tm = _rows(T, tm)
    n = T // tm
    pred = lambda: (pl.program_id(0) == 0, pl.program_id(0) == n - 1)
    x_ins, x_outs, x_scratch, x_start, x_end = _fused_exchange(fused, pred)
    nx = len(x_ins)

    def body(*refs):
        a_ref, ap_ref, w_ref, b_ref, g_ref, bb_ref = refs[:6]
        s_ref, c_ref = refs[6 + nx:8 + nx]
        win, sh = refs[8 + 2 * nx:10 + 2 * nx]
        x_refs = refs[6:6 + nx] + refs[8 + nx:8 + 2 * nx] + refs[10 + 2 * nx:]
        x_start(x_refs)
        i = pl.program_id(0)
        a = a_ref[...]
        win[pl.ds(HALO, tm), :] = a[:, :D] * _sigmoid(a[:, D:])
        ap = ap_ref[pl.ds(tm - HALO, HALO), :]
        up = ap[:, :D] * _sigmoid(ap[:, D:])
        win[pl.ds(0, HALO), :] = jnp.where(i > 0, up, 0.0)
        _shift_copies(win, sh)
        for r0 in range(0, tm, rb):
            acc = jnp.broadcast_to(b_ref[...], (rb, D))
            for k in range(CONV_W):
                acc = acc + w_ref[pl.ds(k, 1), :] * _tap(sh, r0 + k + HALO - (CONV_W - 1), rb)
            c_ref[pl.ds(r0, rb), :] = acc
        c = c_ref[...]
        mu = jnp.mean(c, axis=-1, keepdims=True)
        cc = c - mu
        var = jnp.mean(cc * cc, axis=-1, keepdims=True)
        z = cc * lax.rsqrt(var + EPS) * g_ref[...] + bb_ref[...]
        s_ref[...] = (z * _sigmoid(z)).astype(MXU)
        x_end(x_refs)

    row = lambda i: (i, 0)
    fix = lambda i: (0, 0)
    return _call(body, name=name, grid=(n,),
                 ins=[(A, (tm, 2 * D), row), (A, (tm, 2 * D), lambda i: (jnp.maximum(i - 1, 0), 0)),
                      (w_dw, (HALO, D), fix), (b_dw, (1, D), fix), (ln_g, (1, D), fix), (ln_b, (1, D), fix)] + x_ins,
                 outs=[((T, D), MXU, (tm, D), row), ((T, D), F32, (tm, D), row)] + x_outs,
                 scratch=_window_scratch(tm) + x_scratch)


def ln_swish_bwd(c, ds, ln_g, ln_b, *, name, tm=256):
    T = c.shape[0]
    tm = _rows(T, tm)

    def body(c_ref, ds_ref, g_ref, b_ref, dc_ref, dg_ref, db_ref):
        @pl.when(pl.program_id(0) == 0)
        def _():
            dg_ref[...] = jnp.zeros_like(dg_ref)
            db_ref[...] = jnp.zeros_like(db_ref)
        c = c_ref[...]
        mu = jnp.mean(c, axis=-1, keepdims=True)
        cc = c - mu
        rstd = lax.rsqrt(jnp.mean(cc * cc, axis=-1, keepdims=True) + EPS)
        n = cc * rstd
        z = n * g_ref[...] + b_ref[...]
        s = _sigmoid(z)
        dz = ds_ref[...] * (s * (1.0 + z * (1.0 - s)))
        dg_ref[...] += jnp.sum(dz * n, axis=0, keepdims=True)
        db_ref[...] += jnp.sum(dz, axis=0, keepdims=True)
        dn = dz * g_ref[...]
        dc_ref[...] = rstd * (dn - jnp.mean(dn, axis=-1, keepdims=True)
                              - n * jnp.mean(dn * n, axis=-1, keepdims=True))

    row = lambda i: (i, 0)
    fix = lambda i: (0, 0)
    return _call(body, name=name, grid=(T // tm,),
                 ins=[(c, (tm, D), row), (ds, (tm, D), row), (ln_g, (1, D), fix), (ln_b, (1, D), fix)],
                 outs=[((T, D), F32, (tm, D), row), ((1, D), F32, (1, D), fix), ((1, D), F32, (1, D), fix)])


def conv_bwd(dc, A, w_dw, *, name, tm=128, rb=32, fused=None):
    T = A.shape[0]
    tm = _rows(T, tm)
    n = T // tm
    SUB = 8
    pred = lambda: (pl.program_id(0) == 0, pl.program_id(0) == n - 1)
    x_ins, x_outs, x_scratch, x_start, x_end = _fused_exchange(fused, pred)
    nx = len(x_ins)

    def body(*refs):
        dc_ref, dn_ref, a_ref, ap_ref, w_ref = refs[:5]
        da_ref, dw_ref, dbd_ref, dbp_ref = refs[5 + nx:9 + nx]
        wdc, sdc, wu, su, accw = refs[9 + 2 * nx:14 + 2 * nx]
        x_refs = refs[5:5 + nx] + refs[9 + nx:9 + 2 * nx] + refs[14 + 2 * nx:]
        x_start(x_refs)
        i = pl.program_id(0)

        @pl.when(i == 0)
        def _():
            accw[...] = jnp.zeros_like(accw)
            dbd_ref[...] = jnp.zeros_like(dbd_ref)
            dbp_ref[...] = jnp.zeros_like(dbp_ref)
        a = a_ref[...]
        a1, a2 = a[:, :D], a[:, D:]
        sg = _sigmoid(a2)
        wu[pl.ds(HALO, tm), :] = a1 * sg
        ap = ap_ref[pl.ds(tm - HALO, HALO), :]
        wu[pl.ds(0, HALO), :] = jnp.where(i > 0, ap[:, :D] * _sigmoid(ap[:, D:]), 0.0)
        dc = dc_ref[...]
        wdc[pl.ds(0, tm), :] = dc
        wdc[pl.ds(tm, HALO), :] = jnp.where(i < n - 1, dn_ref[pl.ds(0, HALO), :], 0.0)
        dbd_ref[...] += jnp.sum(dc, axis=0, keepdims=True)
        _shift_copies(wdc, sdc)
        _shift_copies(wu, su)
        for r0 in range(0, tm, rb):
            du = jnp.zeros((rb, D), F32)
            dcs = wdc[pl.ds(r0, rb), :]
            for k in range(CONV_W):
                du = du + w_ref[pl.ds(k, 1), :] * _tap(sdc, r0 + (CONV_W - 1) - k, rb)
                p = dcs * _tap(su, r0 + k + HALO - (CONV_W - 1), rb)
                accw[pl.ds(SUB * k, SUB), :] += jnp.sum(p.reshape(rb // SUB, SUB, D), axis=0)
            s = sg[r0:r0 + rb]
            da_ref[pl.ds(r0, rb), pl.ds(0, D)] = (du * s).astype(MXU)
            da_ref[pl.ds(r0, rb), pl.ds(D, D)] = (du * a1[r0:r0 + rb] * s * (1.0 - s)).astype(MXU)
        da = da_ref[...].astype(F32)
        dbp_ref[...] += jnp.sum(da, axis=0, keepdims=True)

        @pl.when(i == n - 1)
        def _():
            dw_ref[...] = jnp.zeros_like(dw_ref)
            for k in range(CONV_W):
                dw_ref[pl.ds(k, 1), :] = jnp.sum(accw[pl.ds(SUB * k, SUB), :], axis=0, keepdims=True)
        x_end(x_refs)

    row = lambda i: (i, 0)
    fix = lambda i: (0, 0)
    return _call(body, name=name, grid=(n,),
                 ins=[(dc, (tm, D), row), (dc, (tm, D), lambda i: (jnp.minimum(i + 1, n - 1), 0)),
                      (A, (tm, 2 * D), row), (A, (tm, 2 * D), lambda i: (jnp.maximum(i - 1, 0), 0)),
                      (w_dw, (HALO, D), fix)] + x_ins,
                 outs=[((T, 2 * D), MXU, (tm, 2 * D), row), ((HALO, D), F32, (HALO, D), fix),
                       ((1, D), F32, (1, D), fix), ((1, 2 * D), F32, (1, 2 * D), fix)] + x_outs,
                 scratch=_window_scratch(tm) + _window_scratch(tm) + [pltpu.VMEM((SUB * HALO, D), F32)] + x_scratch)


def sb_prep(ps, gq, gk, *, name, tm=256):
    T = ps.shape[0]
    tm = _rows(T, tm)
    W = SB_H * SB_D

    def body(q_ref, k_ref, v_ref, gq_ref, gk_ref, qo, ko, vo):
        qs, ks = [], []
        for h in range(SB_H):
            sl = slice(h * SB_D, (h + 1) * SB_D)
            q = q_ref[:, sl]
            k = k_ref[:, sl]
            rq = lax.rsqrt(jnp.mean(q * q, axis=-1, keepdims=True) + EPS)
            rk = lax.rsqrt(jnp.mean(k * k, axis=-1, keepdims=True) + EPS)
            qs.append(q * rq * gq_ref[...] * (SB_D ** -0.5))
            ks.append(k * rk * gk_ref[...])
        qo[...] = jnp.concatenate(qs, axis=-1).astype(MXU)
        ko[...] = jnp.concatenate(ks, axis=-1).astype(MXU)
        vo[...] = v_ref[...].astype(MXU)

    row = lambda i: (i, 0)
    fix = lambda i: (0, 0)
    return _call(body, name=name, grid=(T // tm,),
                 ins=[(ps, (tm, W), lambda i: (i, 0)), (ps, (tm, W), lambda i: (i, 1)),
                      (ps, (tm, W), lambda i: (i, 2)), (gq, (1, SB_D), fix), (gk, (1, SB_D), fix)],
                 outs=[((T, W), MXU, (tm, W), row)] * 3)


def _sb_masks(tq):
    row = lax.broadcasted_iota(jnp.int32, (tq, tq), 0)
    col = lax.broadcasted_iota(jnp.int32, (tq, tq), 1)
    tri = jnp.where(row >= col, 1.0, 0.0).astype(MXU)
    past = col < row
    return tri, past


def _sb_scores(q, k, past):
    return _sb_logs(_dot(q, k, NT), past)


def _sb_logs(z, past):
    ls = jnp.minimum(z, 0.0) - _softplus_neg_abs(z)
    lk = ls - z
    if past is not None:
        lk = jnp.where(past, lk, 0.0)
    return ls, lk


def _fused_exchange(fused, steps_pred):
    if fused is None:
        return [], [], [], lambda refs: None, lambda refs: None
    src, n_dst, spec = fused
    n = len(spec["flips"])
    ins = [(src, None, None)]
    outs = [((n_dst,) + src.shape[1:], src.dtype, None, None)]
    scratch = [pltpu.SemaphoreType.DMA((n,)), pltpu.SemaphoreType.DMA((n,)), pltpu.SemaphoreType.DMA(())]

    def at_start(refs):
        @pl.when(steps_pred()[0])
        def _():
            _exchange_start(*refs, **spec)

    def at_end(refs):
        @pl.when(steps_pred()[1])
        def _():
            _exchange_wait(*refs, **spec)

    return ins, outs, scratch, at_start, at_end


PAIR = 2 * SB_D
SB_DEAD = -120.0
SB_UNVISITED = -1e30


def _split_pair(x2, lo):
    zero = jnp.zeros_like(x2)
    return [jnp.where(lo, x2, zero), jnp.where(lo, zero, x2)]


def sb_fwd(qn, kn, v, *, name, tq=256, pairs=2, fused=None):
    T, W = qn.shape
    tq = _rows(T, tq)
    wb, hg = pairs * PAIR, 2 * pairs
    assert T // tq <= LANES and W % wb == 0
    ng, ni = W // wb, T // tq
    pred = lambda: ((pl.program_id(0) == 0) & (pl.program_id(1) == 0),
                    (pl.program_id(0) == ng - 1) & (pl.program_id(1) == ni - 1))
    x_ins, x_outs, x_scratch, x_start, x_end = _fused_exchange(fused, pred)
    nx = len(x_ins)
    heads = range(hg)
    pcols = [slice(p * PAIR, (p + 1) * PAIR) for p in range(pairs)]

    def body(*refs):
        q_ref, k_ref, v_ref = refs[:3]
        o_ref, cb_ref = refs[3 + nx:5 + nx]
        x_refs = refs[3:3 + nx] + refs[5 + nx:]
        x_start(x_refs)
        i = pl.program_id(1)
        tri, past = _sb_masks(tq)
        lane = lax.broadcasted_iota(jnp.int32, (tq, LANES), 1)
        lo = lane < SB_D
        cb_ref[...] = jnp.full((hg, tq, LANES), SB_UNVISITED, F32)
        qs = [m for p in range(pairs) for m in _split_pair(q_ref[:, pcols[p]], lo)]

        def block(j, carry, masked):
            accs, cs = carry
            kblk = pl.ds(pl.multiple_of(j * tq, tq), tq)
            for h in heads:
                cb_ref[h] = jnp.where(lane == j, cs[h], cb_ref[h])
            zs = [_dot(qs[h], k_ref[kblk, pcols[h // 2]], NT) for h in heads]
            sc = [_sb_logs(zs[h], past if masked else None) for h in heads]
            bincs = [_dot_split(sc[h][1], tri) for h in heads]
            ws = [jnp.exp(sc[h][0] + cs[h] + bincs[h] - sc[h][1]) for h in heads]
            if masked:
                ws = [jnp.where(past, w, 0.0) for w in ws]
            pv = [_dot(ws[h], v_ref[kblk, pcols[h // 2]]) for h in heads]
            new_a = [accs[p] + jnp.where(lo, pv[2 * p], pv[2 * p + 1]) for p in range(pairs)]
            new_c = [cs[h] + jnp.sum(sc[h][1], axis=-1, keepdims=True) for h in heads]
            return tuple(new_a), tuple(new_c)

        carry = ((jnp.zeros((tq, PAIR), F32),) * pairs, (jnp.zeros((tq, 1), F32),) * hg)
        carry = block(i, carry, True)

        def live(state):
            jj, _, cs = state
            return (jj < i) & (jnp.max(functools.reduce(jnp.maximum, cs)) > SB_DEAD)

        def step(state):
            jj, accs, cs = state
            accs, cs = block(i - 1 - jj, (accs, cs), False)
            return jj + 1, accs, cs

        _, accs, _ = lax.while_loop(live, step, (jnp.int32(0),) + carry)
        for p in range(pairs):
            o_ref[:, pcols[p]] = accs[p]
        x_end(x_refs)

    blk = lambda g, i: (i, g)
    full = lambda g, i: (0, g)
    return _call(body, name=name, grid=(ng, ni),
                 ins=[(qn, (tq, wb), blk), (kn, (T, wb), full), (v, (T, wb), full)] + x_ins,
                 outs=[((T, W), F32, (tq, wb), blk),
                       ((W // SB_D, T, LANES), F32, (hg, tq, LANES), lambda g, i: (g, i, 0))] + x_outs,
                 scratch=x_scratch)


def sb_bwd(qn, kn, v, cb, dmix, do_col, *, name, tq=256, pairs=2, fused=None):
    T, W = qn.shape
    tq = _rows(T, tq)
    wb, hg = pairs * PAIR, 2 * pairs
    assert W % wb == 0 and do_col % wb == 0
    ng, ni = W // wb, T // tq
    pred = lambda: ((pl.program_id(0) == 0) & (pl.program_id(1) == 0),
                    (pl.program_id(0) == ng - 1) & (pl.program_id(1) == ni - 1))
    x_ins, x_outs, x_scratch, x_start, x_end = _fused_exchange(fused, pred)
    nx = len(x_ins)
    heads = range(hg)
    pcols = [slice(p * PAIR, (p + 1) * PAIR) for p in range(pairs)]

    def body(*refs):
        q_ref, k_ref, v_ref, cb_ref, do_ref = refs[:5]
        dq_ref, dk_hbm, dv_hbm = refs[5 + nx:8 + nx]
        dk_ref, dv_ref = refs[8 + 2 * nx:10 + 2 * nx]
        x_refs = refs[5:5 + nx] + refs[8 + nx:8 + 2 * nx] + refs[10 + 2 * nx:]
        x_start(x_refs)
        i = pl.program_id(1)

        @pl.when(i == 0)
        def _():
            dk_ref[...] = jnp.zeros_like(dk_ref)
            dv_ref[...] = jnp.zeros_like(dv_ref)
        tri, past = _sb_masks(tq)
        row = lax.broadcasted_iota(jnp.int32, (tq, tq), 0)
        col = lax.broadcasted_iota(jnp.int32, (tq, tq), 1)
        tri_lt = jnp.where(row < col, 1.0, 0.0).astype(MXU)
        lane = lax.broadcasted_iota(jnp.int32, (tq, LANES), 1)
        lo = lane < SB_D
        qs = [m for p in range(pairs) for m in _split_pair(q_ref[:, pcols[p]], lo)]
        dos = [m for p in range(pairs) for m in _split_pair(do_ref[:, pcols[p]].astype(MXU), lo)]

        def block(j, carry, masked):
            dqs, cgs = carry
            kblk = pl.ds(pl.multiple_of(j * tq, tq), tq)
            k2 = [k_ref[kblk, pcols[p]] for p in range(pairs)]
            zs = [_dot(qs[h], k2[h // 2], NT) for h in heads]
            dws = [_dot(dos[h], v_ref[kblk, pcols[h // 2]], NT) for h in heads]
            sc = [_sb_logs(zs[h], past if masked else None) for h in heads]
            cs = [jnp.sum(jnp.where(lane == j, cb_ref[h], 0.0), axis=-1, keepdims=True) for h in heads]
            bincs = [_dot_split(sc[h][1], tri) for h in heads]
            ws = [jnp.exp(sc[h][0] + cs[h] + bincs[h] - sc[h][1]) for h in heads]
            if masked:
                ws = [jnp.where(past, w, 0.0) for w in ws]
            wbs = [w.astype(MXU) for w in ws]
            gs = [ws[h] * dws[h] for h in heads]
            gpres = [cgs[h] + _dot(gs[h], tri_lt) for h in heads]
            sigs = [jnp.exp(sc[h][0]) for h in heads]
            dzs = [gs[h] - sigs[h] * (gs[h] + gpres[h]) for h in heads]
            if masked:
                dzs = [jnp.where(past, dz, 0.0) for dz in dzs]
            dzbs = [dz.astype(MXU) for dz in dzs]
            dqp = [_dot(dzbs[h], k2[h // 2]) for h in heads]
            new_dq = [dqs[p] + jnp.where(lo, dqp[2 * p], dqp[2 * p + 1]) for p in range(pairs)]
            for p in range(pairs):
                dk_ref[kblk, pcols[p]] += _dot(dzbs[2 * p], qs[2 * p], TN) + _dot(dzbs[2 * p + 1], qs[2 * p + 1], TN)
            for p in range(pairs):
                dv_ref[kblk, pcols[p]] += _dot(wbs[2 * p], dos[2 * p], TN) + _dot(wbs[2 * p + 1], dos[2 * p + 1], TN)
            new_cg = [cgs[h] + jnp.sum(gs[h], axis=-1, keepdims=True) for h in heads]
            return tuple(new_dq), tuple(new_cg)

        colmax = functools.reduce(jnp.maximum, [jnp.max(cb_ref[h], axis=0, keepdims=True) for h in heads])
        lane_row = lax.broadcasted_iota(jnp.int32, (1, LANES), 1)
        is_live = (colmax > SB_DEAD) & (lane_row < i)
        first = jnp.min(jnp.where(is_live, lane_row, i).astype(F32)).astype(jnp.int32)
        carry = ((jnp.zeros((tq, PAIR), F32),) * pairs, (jnp.zeros((tq, 1), F32),) * hg)
        carry = lax.fori_loop(first, i, lambda j, car: block(j, car, False), carry)
        dqs, _ = block(i, carry, True)
        for p in range(pairs):
            dq_ref[:, pcols[p]] = dqs[p]

        cols = pl.ds(pl.multiple_of(pl.program_id(0) * wb, wb), wb)

        @pl.when(i == ni - 1)
        def _():
            pltpu.sync_copy(dk_ref, dk_hbm.at[:, cols])
            pltpu.sync_copy(dv_ref, dv_hbm.at[:, cols])
        x_end(x_refs)

    blk = lambda g, i: (i, g)
    full = lambda g, i: (0, g)
    return _call(body, name=name, grid=(ng, ni),
                 ins=[(qn, (tq, wb), blk), (kn, (T, wb), full), (v, (T, wb), full),
                      (cb, (hg, tq, LANES), lambda g, i: (g, i, 0)),
                      (dmix, (tq, wb), lambda g, i: (i, do_col // wb + g))] + x_ins,
                 outs=[((T, W), F32, (tq, wb), blk), ((T, W), F32, None, None), ((T, W), F32, None, None)] + x_outs,
                 scratch=[pltpu.VMEM((T, wb), F32), pltpu.VMEM((T, wb), F32)] + x_scratch)


def sb_post_bwd(dqn, dkn, dv, ps, gq, gk, *, name, tm=256):
    T = ps.shape[0]
    tm = _rows(T, tm)
    W = SB_H * SB_D

    def body(dq_ref, dk_ref, dv_ref, q_ref, k_ref, gq_ref, gk_ref, out_ref, dgq_ref, dgk_ref):
        @pl.when(pl.program_id(0) == 0)
        def _():
            dgq_ref[...] = jnp.zeros_like(dgq_ref)
            dgk_ref[...] = jnp.zeros_like(dgk_ref)

        def norm_bwd(x, d, g):
            r = lax.rsqrt(jnp.mean(x * x, axis=-1, keepdims=True) + EPS)
            dg = jnp.sum(d * x * r, axis=0, keepdims=True)
            t = d * g
            return t * r - x * (r * r * r) * jnp.mean(t * x, axis=-1, keepdims=True), dg

        dqs, dks = [], []
        dgq = jnp.zeros((1, SB_D), F32)
        dgk = jnp.zeros((1, SB_D), F32)
        for h in range(SB_H):
            sl = slice(h * SB_D, (h + 1) * SB_D)
            a, ga = norm_bwd(q_ref[:, sl], dq_ref[:, sl] * (SB_D ** -0.5), gq_ref[...])
            b, gb = norm_bwd(k_ref[:, sl], dk_ref[:, sl], gk_ref[...])
            dqs.append(a)
            dks.append(b)
            dgq = dgq + ga
            dgk = dgk + gb
        out_ref[...] = jnp.concatenate(dqs + dks + [dv_ref[...]], axis=-1).astype(MXU)
        dgq_ref[...] += dgq
        dgk_ref[...] += dgk

    row = lambda i: (i, 0)
    fix = lambda i: (0, 0)
    return _call(body, name=name, grid=(T // tm,),
                 ins=[(dqn, (tm, W), row), (dkn, (tm, W), row), (dv, (tm, W), row),
                      (ps, (tm, W), lambda i: (i, 0)), (ps, (tm, W), lambda i: (i, 1)),
                      (gq, (1, SB_D), fix), (gk, (1, SB_D), fix)],
                 outs=[((T, 3 * W), MXU, (tm, 3 * W), lambda i: (i, 0)),
                       ((1, SB_D), F32, (1, SB_D), fix), ((1, SB_D), F32, (1, SB_D), fix)])


GLA_TM = 512
GLA_DK = GLA_H * GLA_K
GLA_DV = GLA_H * GLA_V


def _gla_masks(tm):
    row = lax.broadcasted_iota(jnp.int32, (tm, tm), 0)
    col = lax.broadcasted_iota(jnp.int32, (tm, tm), 1)
    same = (row // CHUNK) == (col // CHUNK)
    return row, col, same


def _gla_gate(glr, w2, b, tm):
    pre = _dot(glr, w2) + b
    la = (jnp.minimum(pre, 0.0) - _softplus_neg_abs(pre)) * (1.0 / 16.0)
    row, col, same = _gla_masks(tm)
    m_incl = jnp.where(same & (col <= row), 1.0, 0.0).astype(MXU)
    m_full = jnp.where(same, 1.0, 0.0).astype(MXU)
    bc = _dot_split(m_incl, la, a_split=False)
    tot = _dot_split(m_full, la, a_split=False)
    return pre, bc, tot


def gla_fwd(pg, glr, w2, b_gate, g_out, *, name):
    T = pg.shape[0]
    tm = _rows(T, GLA_TM)
    ncb = tm // CHUNK
    NC = T // CHUNK

    def body(q_ref, k_ref, v_ref, r_ref, l_ref, w2_ref, b_ref, g_ref, o_ref, st_ref, S):
        @pl.when(pl.program_id(0) == 0)
        def _():
            S[...] = jnp.zeros_like(S)
        _, bc, tot = _gla_gate(l_ref[...], w2_ref[...], b_ref[...], tm)
        kend = k_ref[...] * jnp.exp(tot - bc)
        qs = q_ref[...] * (GLA_K ** -0.5)
        a_all = jnp.exp(tot)
        v = v_ref[...]
        r = r_ref[...]
        rows = [slice(c * CHUNK, (c + 1) * CHUNK) for c in range(ncb)]
        hks = [slice(h * GLA_K, (h + 1) * GLA_K) for h in range(GLA_H)]
        hvs = [slice(h * GLA_V, (h + 1) * GLA_V) for h in range(GLA_H)]
        uts = [[_dot(v[rows[c], hvs[h]], kend[rows[c], hks[h]], TN) for h in range(GLA_H)] for c in range(ncb)]
        for h in range(GLA_H):
            s = S[h]
            for c in range(ncb):
                s = s * a_all[c * CHUNK:c * CHUNK + 1, hks[h]] + uts[c][h]
                st_ref[c, h] = s
            S[h] = s
        for c in range(ncb):
            outs = []
            for h in range(GLA_H):
                o = _dot(qs[rows[c], hks[h]], st_ref[c, h], NT)
                rinv = lax.rsqrt(jnp.mean(o * o, axis=-1, keepdims=True) + EPS)
                rr = r[rows[c], hvs[h]]
                outs.append(o * rinv * g_ref[...] * (rr * _sigmoid(rr)))
            o_ref[pl.ds(c * CHUNK, CHUNK), :] = jnp.concatenate(outs, axis=-1).astype(MXU)

    fix = lambda i: (0, 0)
    return _call(body, name=name, grid=(T // tm,),
                 ins=[(pg, (tm, GLA_DK), lambda i: (i, 0)), (pg, (tm, GLA_DK), lambda i: (i, 1)),
                      (pg, (tm, GLA_DV), lambda i: (i, 1)), (pg, (tm, GLA_DV), lambda i: (i, 2)),
                      (glr, (tm, LANES), lambda i: (i, 0)), (w2, (LANES, GLA_DK), fix),
                      (b_gate, (1, GLA_DK), fix), (g_out, (1, GLA_V), fix)],
                 outs=[((T, GLA_DV), MXU, (tm, GLA_DV), lambda i: (i, 0)),
                       ((NC, GLA_H, GLA_V, GLA_K), F32, (ncb, GLA_H, GLA_V, GLA_K), lambda i: (i, 0, 0, 0))],
                 scratch=[pltpu.VMEM((GLA_H, GLA_V, GLA_K), F32)])


def gla_bwd(pg, glr, w2, b_gate, g_out, st, dmix, *, name):
    T = pg.shape[0]
    tm = _rows(T, GLA_TM)
    ncb = tm // CHUNK
    n = T // tm

    def body(q_ref, k_ref, v_ref, r_ref, l_ref, w2_ref, b_ref, g_ref, st_ref, sp_ref, d_ref,
             dpg_ref, dl_ref, dw2_ref, db_ref, dg_ref, dS, dkend, extra, dst_ref):
        i = pl.program_id(0)

        @pl.when(i == 0)
        def _():
            dS[...] = jnp.zeros_like(dS)
            dw2_ref[...] = jnp.zeros_like(dw2_ref)
            db_ref[...] = jnp.zeros_like(db_ref)
            dg_ref[...] = jnp.zeros_like(dg_ref)
        first_tile = i == n - 1
        glr_v = l_ref[...]
        pre, bc, tot = _gla_gate(glr_v, w2_ref[...], b_ref[...], tm)
        dec = jnp.exp(tot - bc)
        k = k_ref[...]
        kend = k * dec
        qs = q_ref[...] * (GLA_K ** -0.5)
        a_all = jnp.exp(tot)
        v = v_ref[...]
        r = r_ref[...]
        g = g_ref[...]
        dgg = jnp.zeros((1, GLA_V), F32)
        rows = [slice(c * CHUNK, (c + 1) * CHUNK) for c in range(ncb)]
        hks = [slice(h * GLA_K, (h + 1) * GLA_K) for h in range(GLA_H)]
        hvs = [slice(h * GLA_V, (h + 1) * GLA_V) for h in range(GLA_H)]
        heads = range(GLA_H)
        qdo = [[None] * GLA_H for _ in range(ncb)]
        for c in range(ncb):
            cr = pl.ds(c * CHUNK, CHUNK)
            dq_l, dr_l = [], []
            for h in heads:
                s_c = st_ref[c, h]
                qh = qs[rows[c], hks[h]]
                o = _dot(qh, s_c, NT)
                rinv = lax.rsqrt(jnp.mean(o * o, axis=-1, keepdims=True) + EPS)
                nrm = o * rinv
                rr = r[rows[c], hvs[h]]
                sg = _sigmoid(rr)
                sil = rr * sg
                d = d_ref[cr, pl.ds(h * GLA_V, GLA_V)]
                dr_l.append(d * nrm * g * (sg * (1.0 + rr * (1.0 - sg))))
                dgg = dgg + jnp.sum(d * sil * nrm, axis=0, keepdims=True)
                dn = d * sil * g
                do = rinv * (dn - nrm * jnp.mean(dn * nrm, axis=-1, keepdims=True))
                dq_l.append(_dot(do, s_c) * (GLA_K ** -0.5))
                qdo[c][h] = _dot(do, qh, TN)
            dpg_ref[cr, pl.ds(0, GLA_DK)] = jnp.concatenate(dq_l, axis=-1).astype(MXU)
            dpg_ref[cr, pl.ds(2 * GLA_DK + GLA_DV, GLA_DV)] = jnp.concatenate(dr_l, axis=-1).astype(MXU)
        ex = [[None] * GLA_H for _ in range(ncb)]
        for h in heads:
            ds_h = dS[h]
            for c in reversed(range(ncb)):
                dst = ds_h + qdo[c][h]
                dst_ref[c, h] = dst
                s_p = st_ref[c - 1, h] if c > 0 else jnp.where(first_tile, 0.0, sp_ref[0, h])
                a = a_all[c * CHUNK:c * CHUNK + 1, hks[h]]
                da = jnp.sum(dst * s_p, axis=0, keepdims=True)
                ds_h = dst * a
                ex[c][h] = jnp.broadcast_to(da * a, (CHUNK, GLA_K))
            dS[h] = ds_h
        for c in range(ncb):
            extra[pl.ds(c * CHUNK, CHUNK), :] = jnp.concatenate(ex[c], axis=-1)
        for c in range(ncb):
            cr = pl.ds(c * CHUNK, CHUNK)
            dk_l = [_dot(v[rows[c], hvs[h]], dst_ref[c, h]) for h in heads]
            dv_l = [_dot(kend[rows[c], hks[h]], dst_ref[c, h], NT) for h in heads]
            dpg_ref[cr, pl.ds(2 * GLA_DK, GLA_DV)] = jnp.concatenate(dv_l, axis=-1).astype(MXU)
            dkend[cr, :] = jnp.concatenate(dk_l, axis=-1)
        dke = dkend[...]
        dpg_ref[:, pl.ds(GLA_DK, GLA_DK)] = (dke * dec).astype(MXU)
        e = dke * kend
        row, col, same = _gla_masks(tm)
        m_lt = jnp.where(same & (col < row), 1.0, 0.0).astype(MXU)
        dla = _dot_split(m_lt, e, a_split=False) + extra[...]
        sp = _softplus_neg_abs(pre)
        one_m_sig = jnp.exp(-jnp.maximum(pre, 0.0) - sp)
        dpre = dla * (1.0 / 16.0) * one_m_sig
        dl_ref[...] = _dot(dpre, w2_ref[...], NT).astype(MXU)
        dw2_ref[...] += _dot(glr_v, dpre, TN)
        db_ref[...] += jnp.sum(dpre, axis=0, keepdims=True)
        dg_ref[...] += dgg

    fix = lambda i: (0, 0)
    rev = lambda i: n - 1 - i
    return _call(body, name=name, grid=(n,),
                 ins=[(pg, (tm, GLA_DK), lambda i: (rev(i), 0)), (pg, (tm, GLA_DK), lambda i: (rev(i), 1)),
                      (pg, (tm, GLA_DV), lambda i: (rev(i), 1)), (pg, (tm, GLA_DV), lambda i: (rev(i), 2)),
                      (glr, (tm, LANES), lambda i: (rev(i), 0)), (w2, (LANES, GLA_DK), fix),
                      (b_gate, (1, GLA_DK), fix), (g_out, (1, GLA_V), fix),
                      (st, (ncb, GLA_H, GLA_V, GLA_K), lambda i: (rev(i), 0, 0, 0)),
                      (st, (1, GLA_H, GLA_V, GLA_K), lambda i: (jnp.maximum(rev(i) * ncb - 1, 0), 0, 0, 0)),
                      (dmix, (tm, GLA_DV), lambda i: (rev(i), 0))],
                 outs=[((T, 2 * GLA_DK + 2 * GLA_DV), MXU, (tm, 2 * GLA_DK + 2 * GLA_DV), lambda i: (rev(i), 0)),
                       ((T, LANES), MXU, (tm, LANES), lambda i: (rev(i), 0)),
                       ((LANES, GLA_DK), F32, (LANES, GLA_DK), fix),
                       ((1, GLA_DK), F32, (1, GLA_DK), fix), ((1, GLA_V), F32, (1, GLA_V), fix)],
                 scratch=[pltpu.VMEM((GLA_H, GLA_V, GLA_K), F32), pltpu.VMEM((tm, GLA_DK), F32),
                          pltpu.VMEM((tm, GLA_DK), F32), pltpu.VMEM((ncb, GLA_H, GLA_V, GLA_K), F32)])


def local_step(x, tgt, W, late=None):
    row1 = lambda a, l: a[l:l + 1]
    hn0 = rms_fwd(x, row1(W["mix_norm"], 0), name="l0_norm")
    pg = mm([(hn0, W["wi_g"])], name="l0_proj_gla", tn=512)
    ps = mm([(hn0, W["wi_s"])], name="l0_proj_sb", tn=512)
    glr = mm([(hn0, W["wi_l"])], name="l0_proj_gate", out_dtype=MXU)
    og, st = gla_fwd(pg, glr, W["w2"], W["b_gate"], W["g_gla"], name="gla_fwd")
    qn, kn, vh = sb_prep(ps, W["g_q"], W["g_k"], name="sb_prep")
    if late is None:
        osb, sb_cb = sb_fwd(qn, kn, vh, name="sb_fwd")
    else:
        osb, sb_cb, gathered = sb_fwd(qn, kn, vh, name="sb_fwd", fused=(late["src"][0], N_CHIPS, GATHER))
        W = {**W, **late["unpack"][0](gathered)}
    h1 = mm([(og, W["wo_g"]), (osb, W["wo_s"])], res=x, name="l0_out")
    h2, ffn0, landed = ffn_fwd(h1, row1(W["ffn_norm"], 0), W["wg0"], W["wu0"], W["wd0"], 0,
                               fused=None if late is None else (late["src"][1], N_CHIPS, GATHER))
    if late is not None:
        W = {**W, **late["unpack"][1](landed[0])}
    hn1 = rms_fwd(h2, row1(W["mix_norm"], 1), name="l1_norm")
    A = mm([(hn1, W["pw1"])], bias=W["b_pw1"], name="l1_pw1")
    s, cconv, *landed = conv_fwd(A, W["w_dw"], W["b_dw"], W["ln_g"], W["ln_b"], name="conv_fwd",
                                 fused=None if late is None else (late["src"][2], N_CHIPS, GATHER))
    if late is not None:
        W = {**W, **late["unpack"][2](landed[0])}
    h3 = mm([(s, W["pw2"])], bias=W["b_pw2"], res=h2, name="l1_pw2")
    y, ffn1, _ = ffn_fwd(h3, row1(W["ffn_norm"], 1), W["wg1"], W["wu1"], W["wd1"], 1)
    dy, loss_lanes = loss_grad(y, tgt, name="loss")
    G = {}
    dh3, g_fn1, G["wg1"], G["wu1"], G["wd1"], cs3 = ffn_bwd(
        dy, h3, row1(W["ffn_norm"], 1), W["wg1"], W["wu1"], W["wd1"], ffn1, 1)
    G["b_pw2"] = cs3
    ds = mm([(dh3, W["pw2"])], trans_b=True, name="l1_ds")
    G["pw2"] = mm_tn(s, dh3, name="l1_dpw2", tm=1024, tn=1024)
    dc, G["ln_g"], G["ln_b"] = ln_swish_bwd(cconv, ds, W["ln_g"], W["ln_b"], name="ln_bwd")
    dA, G["w_dw"], G["b_dw"], G["b_pw1"], *slots_a = conv_bwd(
        dc, A, W["w_dw"], name="conv_bwd", fused=None if late is None else (late["pack"][0](G), N_DEV, SCATTER))
    G["pw1"] = mm_tn(hn1, dA, name="l1_dpw1", tm=1024, tn=1024)
    dhn1 = mm([(dA, W["pw1"])], trans_b=True, name="l1_dhn")
    dh2, g_mn1, _ = rms_bwd(h2, row1(W["mix_norm"], 1), dhn1, dh3, name="l1_norm_bwd")
    dh1, g_fn0, G["wg0"], G["wu0"], G["wd0"], _ = ffn_bwd(
        dh2, h1, row1(W["ffn_norm"], 0), W["wg0"], W["wu0"], W["wd0"], ffn0, 0)
    dmix = mm([(dh1, W["wo_gs"])], trans_b=True, name="l0_dmix")
    G["wo_g"] = mm_tn(og, dh1, name="l0_dwo_g", tm=512, tn=1024)
    G["wo_s"] = mm_tn(osb, dh1, name="l0_dwo_s", tm=512, tn=1024)
    slots = None
    if late is None:
        dqn, dkn, dvh = sb_bwd(qn, kn, vh, sb_cb, dmix, GLA_DV, name="sb_bwd")
    else:
        dqn, dkn, dvh, slots_b = sb_bwd(qn, kn, vh, sb_cb, dmix, GLA_DV, name="sb_bwd",
                                        fused=(late["pack"][1](G), N_DEV, SCATTER))
        slots = [slots_a[0], slots_b]
    dps, G["g_q"], G["g_k"] = sb_post_bwd(dqn, dkn, dvh, ps, W["g_q"], W["g_k"], name="sb_post_bwd")
    dpg, dglr, G["w2"], G["b_gate"], G["g_gla"] = gla_bwd(
        pg, glr, W["w2"], W["b_gate"], W["g_gla"], st, dmix, name="gla_bwd")
    G["wi_g"] = mm_tn(hn0, dpg, name="l0_dwi_g", tm=1024, tn=512)
    G["wi_s"] = mm_tn(hn0, dps, name="l0_dwi_s", tm=1024, tn=512)
    G["wi_l"] = mm_tn(hn0, dglr, name="l0_dwi_l", tm=1024, tn=LANES)
    first_slots = None
    dhn_pairs = [(dpg, W["wi_g"]), (dps, W["wi_s"]), (dglr, W["wi_l"])]
    if late is None:
        dhn0 = mm(dhn_pairs, trans_b=True, name="l0_dhn")
    else:
        dhn0, first_slots = mm(dhn_pairs, trans_b=True, name="l0_dhn", fused=(late["pack_first"](G), N_DEV, SCATTER))
    dx, g_mn0, _ = rms_bwd(x, row1(W["mix_norm"], 0), dhn0, dh1, name="l0_norm_bwd")
    G["mix_norm"] = jnp.concatenate([g_mn0, g_mn1], axis=0)
    G["ffn_norm"] = jnp.concatenate([g_fn0, g_fn1], axis=0)
    return loss_lanes, dx, G, slots, first_slots


_C_GLA = 2 * GLA_DK + 2 * GLA_DV
_C_SB0 = _C_GLA + GLA_RANK


ITEMS = {
    "w_in": ("hy_w_in", 0, (D, _C_SB0 + 3 * SB_H * SB_D), 1), "wo_gs": ("hy_w_out", 0, (GLA_DV + SB_H * SB_D, D), 0),
    "w2r": ("hy_w_gate2", 0, (GLA_RANK, GLA_DK), 1), "b_pw1": ("cv_b_pw1", None, (1, 2 * D), 1),
    "w_dwr": ("cv_w_dw", 0, (CONV_W, D), 1), "b_dw": ("cv_b_dw", None, (1, D), 1),
    "ln_g": ("cv_ln_g", None, (1, D), 1), "ln_b": ("cv_ln_b", None, (1, D), 1), "b_pw2": ("cv_b_pw2", None, (1, D), 1),
    "pw1": ("cv_w_pw1", 0, (D, 2 * D), 1), "pw2": ("cv_w_pw2", 0, (D, D), 0),
    "wg0": ("ffn_w_gate", 0, (D, F), 1), "wg1": ("ffn_w_gate", 1, (D, F), 1),
    "wu0": ("ffn_w_up", 0, (D, F), 1), "wu1": ("ffn_w_up", 1, (D, F), 1),
    "wd0": ("ffn_w_down", 0, (F, D), 0), "wd1": ("ffn_w_down", 1, (F, D), 0),
}
FIRST_BIG_ITEMS = ["w_in", "wo_gs"]
SMALL_ITEMS = ["w2r", "b_pw1", "w_dwr", "b_dw", "ln_g", "ln_b", "b_pw2"]
LATE_ITEMS = ["pw1", "pw2", "wg0", "wg1", "wu0", "wu1", "wd0", "wd1"]
LATE_SCATTERS = (["wg1", "wu1", "wd1"], ["pw1", "pw2", "wg0", "wu0", "wd0"])


def layout_first(raw, repl):
    w_in = raw["w_in"]
    W = {
        "wi_g": w_in[:, :_C_GLA].astype(MXU),
        "wi_s": w_in[:, _C_SB0:].astype(MXU),
        "wi_l": jnp.pad(w_in[:, _C_GLA:_C_SB0], ((0, 0), (0, LANES - GLA_RANK))).astype(MXU),
        "w2": jnp.pad(raw["w2r"], ((0, LANES - GLA_RANK), (0, 0))),
        "b_gate": repl["hy_b_gate"], "g_gla": repl["hy_gla_norm"],
        "g_q": repl["hy_sb_q_norm"], "g_k": repl["hy_sb_k_norm"],
        "wo_gs": raw["wo_gs"].astype(MXU),
        "b_pw1": raw["b_pw1"], "w_dw": jnp.pad(raw["w_dwr"], ((0, HALO - CONV_W), (0, 0))),
        "b_dw": raw["b_dw"], "ln_g": raw["ln_g"], "ln_b": raw["ln_b"], "b_pw2": raw["b_pw2"],
        "mix_norm": repl["mix_norm"], "ffn_norm": repl["ffn_norm"],
    }
    W["wo_g"] = W["wo_gs"][:GLA_DV]
    W["wo_s"] = W["wo_gs"][GLA_DV:]
    return W


def layout_late(raw):
    return {k: raw[k].astype(MXU) for k in LATE_ITEMS}


def item_grads(G):
    out = {k: G[k] for k in ITEMS if k in G}
    if "wi_g" in G:
        out["w_in"] = jnp.concatenate([G["wi_g"], G["wi_l"][:, :GLA_RANK], G["wi_s"]], axis=1)
        out["wo_gs"] = jnp.concatenate([G["wo_g"], G["wo_s"]], axis=0)
        out["w2r"] = G["w2"][:GLA_RANK]
        out["w_dwr"] = G["w_dw"][:CONV_W]
    return out


def _chip_major(a, ax):
    r, c = a.shape
    if ax == 1:
        a = a.reshape(r, N_CHIPS, c // N_CHIPS).transpose(1, 0, 2)
    return a.reshape(N_CHIPS, -1, LANES)


def _from_chip_major(p, shape, ax):
    r, c = shape
    if ax == 0:
        return p.reshape(r, c)
    return p.reshape(N_CHIPS, r, c // N_CHIPS).transpose(1, 0, 2).reshape(r, c)


def _item_rows(k):
    n = math.prod(ITEMS[k][2]) // N_CHIPS
    assert n % LANES == 0, k
    return n // LANES


def unpack_items(buf, keys):
    out, off = {}, 0
    for k in keys:
        _, _, shape, ax = ITEMS[k]
        out[k] = _from_chip_major(buf[:, off:off + _item_rows(k)], shape, ax)
        off += _item_rows(k)
    return out


def pack_item_grads(raw, keys, dtype=F32):
    cat = jnp.concatenate([_chip_major(raw[k].astype(dtype), ITEMS[k][3]) for k in keys], axis=1)
    n = cat.shape[1]
    rows = -(-n // GRAD_ROWS) * GRAD_ROWS
    return jnp.pad(cat, ((0, 0), (0, rows - n), (0, 0))).reshape(2 * N_CHIPS, rows // 2, LANES)


def _refs_of(keys):
    return list(dict.fromkeys(ITEMS[k][0] for k in keys))


WEIGHTS = ["mix_norm", "ffn_norm", "hy_w_in", "hy_w_gate2", "hy_b_gate", "hy_gla_norm", "hy_sb_q_norm",
           "hy_sb_k_norm", "hy_w_out", "cv_w_pw1", "cv_b_pw1", "cv_w_dw", "cv_b_dw", "cv_ln_g", "cv_ln_b",
           "cv_w_pw2", "cv_b_pw2", "ffn_w_gate", "ffn_w_up", "ffn_w_down"]
REPL =["mix_norm", "ffn_norm", "hy_b_gate", "hy_gla_norm", "hy_sb_q_norm", "hy_sb_k_norm"]
N_CHIPS = 4
N_DEV = 8
GRAD_ROWS = 1024


def _pack(arrs, rows_multiple, dtype):
    flat = jnp.concatenate([a.reshape(-1).astype(dtype) for a in arrs])
    n = flat.shape[0]
    rows = -(-n // (LANES * rows_multiple)) * rows_multiple
    return jnp.pad(flat, (0, rows * LANES - n)).reshape(rows, LANES)


def _unpack(flat2d, shapes):
    flat = flat2d.reshape(-1)
    out, off = [], 0
    for s in shapes:
        n = math.prod(s)
        out.append(flat[off:off + n].reshape(s))
        off += n
    return out


def _coords():
    return lax.axis_index("x"), lax.axis_index("y"), lax.axis_index("c")


def _flip(pos, f):
    return tuple(1 - p if b else p for p, b in zip(pos, f))


def _exchange_copies(src_ref, dst_ref, send_sems, recv_sems, loc_sem, *, flips, src_idx, dst_idx, local_idx,
                     with_recvs=True):
    me = _coords()
    loc = None
    if local_idx is not None:
        si, di = local_idx(me)
        loc = pltpu.make_async_copy(src_ref.at[si], dst_ref.at[di], loc_sem)
    sends, recvs = [], []
    for k, f in enumerate(flips):
        peer = _flip(me, f)
        sends.append(pltpu.make_async_remote_copy(
            src_ref=src_ref.at[src_idx(me, peer)], dst_ref=dst_ref.at[dst_idx(me)],
            send_sem=send_sems.at[k], recv_sem=recv_sems.at[k],
            device_id=peer, device_id_type=pl.DeviceIdType.MESH))
        if with_recvs:
            recvs.append(pltpu.make_async_remote_copy(
                src_ref=src_ref.at[src_idx(peer, me)], dst_ref=dst_ref.at[dst_idx(peer)],
                send_sem=send_sems.at[k], recv_sem=recv_sems.at[k],
                device_id=peer, device_id_type=pl.DeviceIdType.MESH))
    return loc, sends, recvs


def _exchange_start(*refs, **spec):
    loc, sends, _ = _exchange_copies(*refs, with_recvs=False, **spec)
    if loc is not None:
        loc.start()
    for s in sends:
        s.start()


def _exchange_wait(*refs, **spec):
    loc, sends, recvs = _exchange_copies(*refs, **spec)
    for s in sends:
        s.wait_send()
    for r in recvs:
        r.wait_recv()
    if loc is not None:
        loc.wait()


def exchange(src, n_dst, spec, *, name, in_place=False):
    n = len(spec["flips"])
    assert not in_place or (n_dst == src.shape[0] and spec["local_idx"] is None)

    def body(*refs):
        _exchange_start(*refs, **spec)
        _exchange_wait(*refs, **spec)

    return pl.pallas_call(
        body, name=name,
        out_shape=jax.ShapeDtypeStruct((n_dst,) + src.shape[1:], src.dtype),
        in_specs=[pl.BlockSpec(memory_space=pl.ANY)],
        out_specs=pl.BlockSpec(memory_space=pl.ANY),
        scratch_shapes=[pltpu.SemaphoreType.DMA((n,)), pltpu.SemaphoreType.DMA((n,)), pltpu.SemaphoreType.DMA(())],
        input_output_aliases={0: 0} if in_place else {},
    )(src)


CHIP_FLIPS = [(1, 0, 0), (0, 1, 0), (1, 1, 0)]
SIBLING = [(0, 0, 1)]
ALL_FLIPS = [(a, b, c) for a in (0, 1) for b in (0, 1) for c in (0, 1) if (a, b, c) != (0, 0, 0)]


def _chip(pos):
    return 2 * pos[0] + pos[1]


def _dev(pos):
    return 4 * pos[0] + 2 * pos[1] + pos[2]


GATHER = dict(flips=CHIP_FLIPS, src_idx=lambda me, peer: 0, dst_idx=_chip, local_idx=lambda me: (0, _chip(me)))
SCATTER = dict(flips=ALL_FLIPS, src_idx=lambda me, peer: 2 * _chip(peer) + peer[2], dst_idx=_dev,
               local_idx=lambda me: (2 * _chip(me) + me[2], _dev(me)))
SHARE = dict(flips=SIBLING, src_idx=lambda me, peer: me[2], dst_idx=lambda me: me[2], local_idx=None)
ALL_TO_ALL = dict(flips=ALL_FLIPS, src_idx=lambda me, peer: 0, dst_idx=_dev, local_idx=lambda me: (0, _dev(me)))


def sum_slots(x, *, name, tr=512):
    n, R, L = x.shape
    tr = _rows(R, tr)

    def body(x_ref, o_ref):
        acc = x_ref[0]
        for k in range(1, n):
            acc = acc + x_ref[k]
        o_ref[...] = acc

    return _call(body, name=name, grid=(R // tr,),
                 ins=[(x, (n, tr, L), lambda i: (0, i, 0))],
                 outs=[((R, L), F32, (tr, L), lambda i: (i, 0))])[0]


def sum_slots_into_half(x, half, *, name, tr=512):
    n, R, L = x.shape
    tr = _rows(R, tr)

    def body(h_ref, x_ref, o_ref):
        acc = x_ref[0].astype(F32)
        for k in range(1, n):
            acc = acc + x_ref[k].astype(F32)
        o_ref[0] = acc

    return pl.pallas_call(
        body, name=name,
        grid_spec=pltpu.PrefetchScalarGridSpec(
            num_scalar_prefetch=1, grid=(R // tr,),
            in_specs=[pl.BlockSpec((n, tr, L), lambda i, h: (0, i, 0))],
            out_specs=pl.BlockSpec((1, tr, L), lambda i, h: (h[0], i, 0))),
        out_shape=jax.ShapeDtypeStruct((2, R, L), F32),
        compiler_params=pltpu.CompilerParams(dimension_semantics=("arbitrary",), vmem_limit_bytes=VMEM_LIMIT),
    )(jnp.reshape(half, (1,)).astype(jnp.int32), x)


ADAM_ROWS = 256


def adamw(w, g, m, v, *, name):
    shape = w.shape
    w2, g2, m2, v2 = (a.reshape(-1, shape[-1]) for a in (w, g, m, v))
    R, L = w2.shape
    tr = ADAM_ROWS if R % ADAM_ROWS == 0 else next(
        (t for t in range(min(R, 2 * ADAM_ROWS) // SUBLANES * SUBLANES, 0, -SUBLANES) if R % t == 0), R)

    def body(w_ref, g_ref, m_ref, v_ref, d_ref, mo_ref, vo_ref):
        g = g_ref[...]
        m = ADAM_B1 * m_ref[...] + (1.0 - ADAM_B1) * g
        v = ADAM_B2 * v_ref[...] + (1.0 - ADAM_B2) * (g * g)
        m_hat = m / (1.0 - ADAM_B1 ** ADAM_STEP)
        v_hat = v / (1.0 - ADAM_B2 ** ADAM_STEP)
        d_ref[...] = -ADAM_LR * (m_hat / (jnp.sqrt(v_hat) + ADAM_EPS) + ADAM_WD * w_ref[...])
        mo_ref[...] = m
        vo_ref[...] = v

    row = lambda i: (i, 0)
    outs = _call(body, name=name, grid=(R // tr,),
                 ins=[(a, (tr, L), row) for a in (w2, g2, m2, v2)],
                 outs=[((R, L), F32, (tr, L), row)] * 3)
    return [o.reshape(shape) for o in outs]


def kernel(x, mix_norm, ffn_norm, hy_w_in, hy_w_gate2, hy_b_gate, hy_gla_norm, hy_sb_q_norm, hy_sb_k_norm, hy_w_out, cv_w_pw1, cv_b_pw1, cv_w_dw, cv_b_dw, cv_ln_g, cv_ln_b, cv_w_pw2, cv_b_pw2, ffn_w_gate, ffn_w_up, ffn_w_down, loss_target, m_mix_norm, m_ffn_norm, m_hy_w_in, m_hy_w_gate2, m_hy_b_gate, m_hy_gla_norm, m_hy_sb_q_norm, m_hy_sb_k_norm, m_hy_w_out, m_cv_w_pw1, m_cv_b_pw1, m_cv_w_dw, m_cv_b_dw, m_cv_ln_g, m_cv_ln_b, m_cv_w_pw2, m_cv_b_pw2, m_ffn_w_gate, m_ffn_w_up, m_ffn_w_down, v_mix_norm, v_ffn_norm, v_hy_w_in, v_hy_w_gate2, v_hy_b_gate, v_hy_gla_norm, v_hy_sb_q_norm, v_hy_sb_k_norm, v_hy_w_out, v_cv_w_pw1, v_cv_b_pw1, v_cv_w_dw, v_cv_b_dw, v_cv_ln_g, v_cv_ln_b, v_cv_w_pw2, v_cv_b_pw2, v_ffn_w_gate, v_ffn_w_up, v_ffn_w_down):
    w = dict(zip(WEIGHTS, (mix_norm, ffn_norm, hy_w_in, hy_w_gate2, hy_b_gate, hy_gla_norm, hy_sb_q_norm, hy_sb_k_norm, hy_w_out, cv_w_pw1, cv_b_pw1, cv_w_dw, cv_b_dw, cv_ln_g, cv_ln_b, cv_w_pw2, cv_b_pw2, ffn_w_gate, ffn_w_up, ffn_w_down)))
    m = dict(zip(WEIGHTS, (m_mix_norm, m_ffn_norm, m_hy_w_in, m_hy_w_gate2, m_hy_b_gate, m_hy_gla_norm, m_hy_sb_q_norm, m_hy_sb_k_norm, m_hy_w_out, m_cv_w_pw1, m_cv_b_pw1, m_cv_w_dw, m_cv_b_dw, m_cv_ln_g, m_cv_ln_b, m_cv_w_pw2, m_cv_b_pw2, m_ffn_w_gate, m_ffn_w_up, m_ffn_w_down)))
    v = dict(zip(WEIGHTS, (v_mix_norm, v_ffn_norm, v_hy_w_in, v_hy_w_gate2, v_hy_b_gate, v_hy_gla_norm, v_hy_sb_q_norm, v_hy_sb_k_norm, v_hy_w_out, v_cv_w_pw1, v_cv_b_pw1, v_cv_w_dw, v_cv_b_dw, v_cv_ln_g, v_cv_ln_b, v_cv_w_pw2, v_cv_b_pw2, v_ffn_w_gate, v_ffn_w_up, v_ffn_w_down)))
    def local_shards(keys, rows_multiple, dtype):
        item = lambda ref, idx: w[ref] if idx is None else w[ref][idx]
        return _pack([item(*ITEMS[k][:2]) for k in keys], rows_multiple, dtype)[None]

    def finish_reduce(slots, tag):
        halves = sum_slots_into_half(slots, lax.axis_index("c"), name=f"reduce_{tag}_sum")
        return exchange(halves, 2, SHARE, name=f"reduce_{tag}_share", in_place=True).reshape(2 * slots.shape[1], LANES)

    first = exchange(local_shards(FIRST_BIG_ITEMS, 32, MXU), N_CHIPS, GATHER, name="gather_first")
    small = exchange(local_shards(SMALL_ITEMS, 8, F32), N_CHIPS, GATHER, name="gather_small")
    raw = {**unpack_items(first, FIRST_BIG_ITEMS), **unpack_items(small, SMALL_ITEMS)}
    stages = (["wg0", "wu0", "wd0"], ["pw1", "pw2"], ["wg1", "wu1", "wd1"])
    late = {"src": [local_shards(keys, 32, MXU) for keys in stages],
            "unpack": [functools.partial(unpack_items, keys=keys) for keys in stages],
            "pack": [lambda G, keys=keys: pack_item_grads(item_grads(G), keys, MXU) for keys in LATE_SCATTERS],
            "pack_first": lambda G: pack_item_grads(item_grads(G), FIRST_BIG_ITEMS, MXU)}

    loss_lanes, dx, Gk, late_slots, first_slots = local_step(
        x[0], loss_target[0], layout_first(raw, {n: w[n] for n in REPL}), late)
    rep_names = {"mix_norm": "mix_norm", "ffn_norm": "ffn_norm", "hy_b_gate": "b_gate", "hy_gla_norm": "g_gla",
                 "hy_sb_q_norm": "g_q", "hy_sb_k_norm": "g_k"}
    Gr = {n: Gk[k] for n, k in rep_names.items()}

    groups = [(FIRST_BIG_ITEMS, finish_reduce(first_slots, "first")),
              (LATE_SCATTERS[0], finish_reduce(late_slots[0], "late0")),
              (LATE_SCATTERS[1], finish_reduce(late_slots[1], "late1"))]

    loss_row = jnp.pad(jnp.sum(loss_lanes).reshape(1), (0, LANES - 1))
    small_g = item_grads(Gk)
    rep = _pack([Gr[n] for n in REPL] + [small_g[k] for k in SMALL_ITEMS] + [loss_row], 8, F32)
    rep_all = exchange(rep[None], N_DEV, ALL_TO_ALL, name="reduce_small")
    rep_sum = sum_slots(rep_all, name="reduce_small_sum", tr=rep.shape[0])
    rep_shapes = [w[n].shape for n in REPL] + [ITEMS[k][2] for k in SMALL_ITEMS]
    rep_parts = _unpack(rep_sum, rep_shapes)
    loss = rep_sum.reshape(-1)[sum(math.prod(s) for s in rep_shapes)]
    grads = dict(zip(REPL, rep_parts))
    chip = 2 * lax.axis_index("x") + lax.axis_index("y")
    for k, full in zip(SMALL_ITEMS, rep_parts[len(REPL):]):
        ref, _, (_, c), ax = ITEMS[k]
        assert ax == 1
        grads[ref] = lax.dynamic_slice_in_dim(full, chip * (c // N_CHIPS), c // N_CHIPS, axis=1).reshape(w[ref].shape)

    item_g = {}
    for keys, g_flat in groups:
        off = 0
        for k in keys:
            ref, idx = ITEMS[k][:2]
            item_g[k] = g_flat[off:off + _item_rows(k)].reshape(w[ref].shape if idx is None else w[ref].shape[1:])
            off += _item_rows(k)
    for ref in _refs_of(list(item_g)):
        layers = sorted((ITEMS[k][1], k) for k in item_g if ITEMS[k][0] == ref)
        grads[ref] = jnp.stack([item_g[k] for _, k in layers])
    out = {"grad": grads, "delta": {}, "new_m": {}, "new_v": {}}
    small_refs = _refs_of(SMALL_ITEMS) + REPL
    for n in WEIGHTS:
        if n not in small_refs:
            out["delta"][n], out["new_m"][n], out["new_v"][n] = adamw(w[n], grads[n], m[n], v[n], name=f"adamw_{n}")
    packed = [_pack([t[n] for n in small_refs], 8, F32) for t in (w, grads, m, v)]
    small_shapes = [w[n].shape for n in small_refs]
    for key, buf in zip(("delta", "new_m", "new_v"), adamw(*packed, name="adamw_small")):
        out[key].update(dict(zip(small_refs, _unpack(buf, small_shapes))))

    return (loss, dx[None], *[out["grad"][n] for n in WEIGHTS], *[out["delta"][n] for n in WEIGHTS],
            *[out["new_m"][n] for n in WEIGHTS], *[out["new_v"][n] for n in WEIGHTS])
```

```python
import functools
import math

import numpy as np
import jax
import jax.numpy as jnp
from jax import lax
from jax.experimental import pallas as pl
from jax.experimental.pallas import tpu as pltpu

F32 = jnp.float32
MXU = jnp.bfloat16
EPS = 1e-6
LANES = 128
VMEM_LIMIT = 56 * 1024 * 1024

D = 1024
F = 2816
CHUNK = 64
GLA_H, GLA_K, GLA_V, GLA_RANK = 4, 64, 128, 16
SB_H, SB_D = 8, 64
CONV_W = 31
HALO = 32

ADAM_LR, ADAM_B1, ADAM_B2, ADAM_EPS, ADAM_WD, ADAM_STEP = 0.001, 0.9, 0.999, 1e-08, 0.01, 10

NN = (((1,), (0,)), ((), ()))
NT = (((1,), (1,)), ((), ()))
TN = (((0,), (0,)), ((), ()))


def _dot(a, b, dims=NN):
    return lax.dot_general(a.astype(MXU), b.astype(MXU), dims, preferred_element_type=F32)


def _dot_split(a, b, dims=NN, a_split=True):
    x = a if a_split else b
    hi = x.astype(MXU)
    lo = (x - hi.astype(F32)).astype(MXU)
    if a_split:
        return _dot(hi, b, dims) + _dot(lo, b, dims)
    return _dot(a, hi, dims) + _dot(a, lo, dims)


def _sigmoid(x):
    return 1.0 / (1.0 + jnp.exp(-x))


def _softplus_neg_abs(z):
    return jnp.log(1.0 + jnp.exp(-jnp.abs(z)))


def _call(body, *, name, grid, ins, outs, scratch=()):
    spec = lambda b, m: pl.BlockSpec(memory_space=pl.ANY) if b is None else pl.BlockSpec(b, m)
    res = pl.pallas_call(
        body,
        name=name,
        grid=grid,
        in_specs=[spec(b, m) for _, b, m in ins],
        out_specs=[spec(b, m) for _, _, b, m in outs],
        out_shape=[jax.ShapeDtypeStruct(s, d) for s, d, _, _ in outs],
        scratch_shapes=list(scratch),
        compiler_params=pltpu.CompilerParams(
            dimension_semantics=("arbitrary",) * len(grid), vmem_limit_bytes=VMEM_LIMIT),
    )(*[a for a, _, _ in ins])
    return res


def _rows(T, tm):
    tm = min(tm, T)
    assert T % tm == 0, (T, tm)
    return tm


def mm(pairs, *, name, trans_b=False, bias=None, res=None, out_dtype=F32, tm=512, tn=512, fused=None):
    M = pairs[0][0].shape[0]
    N = pairs[0][1].shape[0] if trans_b else pairs[0][1].shape[1]
    tm = _rows(M, tm)
    tn = min(tn, N)
    assert N % tn == 0, (N, tn)
    np_ = len(pairs)
    ni, nj = M // tm, N // tn
    pred = lambda: ((pl.program_id(0) == 0) & (pl.program_id(1) == 0),
                    (pl.program_id(0) == ni - 1) & (pl.program_id(1) == nj - 1))
    x_ins, x_outs, x_scratch, x_start, x_end = _fused_exchange(fused, pred)
    n_in = 2 * np_ + (bias is not None) + (res is not None)

    def body(*refs):
        o_ref = refs[n_in + len(x_ins)]
        x_refs = refs[n_in:n_in + len(x_ins)] + refs[n_in + len(x_ins) + 1:]
        x_start(x_refs)
        acc = None
        for p in range(np_):
            d = _dot(refs[2 * p][...], refs[2 * p + 1][...], NT if trans_b else NN)
            acc = d if acc is None else acc + d
        k = 2 * np_
        if bias is not None:
            acc = acc + refs[k][...]
            k += 1
        if res is not None:
            acc = acc + refs[k][...]
        o_ref[...] = acc.astype(out_dtype)
        x_end(x_refs)

    ins = []
    for a, b in pairs:
        K = a.shape[1]
        ins.append((a, (tm, K), lambda i, j: (i, 0)))
        if trans_b:
            ins.append((b, (tn, K), lambda i, j: (j, 0)))
        else:
            ins.append((b, (K, tn), lambda i, j: (0, j)))
    if bias is not None:
        ins.append((bias, (1, tn), lambda i, j: (0, j)))
    if res is not None:
        ins.append((res, (tm, tn), lambda i, j: (i, j)))
    outs = _call(body, name=name, grid=(ni, nj), ins=ins + x_ins,
                 outs=[((M, N), out_dtype, (tm, tn), lambda i, j: (i, j))] + x_outs, scratch=x_scratch)
    return outs[0] if fused is None else outs


def mm_tn(a, b, *, name, tm=512, tn=512, tk=1024):
    T, M = a.shape
    N = b.shape[1]
    tm, tn, tk = min(tm, M), min(tn, N), min(tk, T)
    assert M % tm == 0 and N % tn == 0 and T % tk == 0, (M, N, T, tm, tn, tk)

    def body(a_ref, b_ref, o_ref):
        @pl.when(pl.program_id(2) == 0)
        def _():
            o_ref[...] = jnp.zeros_like(o_ref)
        o_ref[...] += _dot(a_ref[...], b_ref[...], TN)

    return _call(body, name=name, grid=(M // tm, N // tn, T // tk),
                 ins=[(a, (tk, tm), lambda i, j, k: (k, i)), (b, (tk, tn), lambda i, j, k: (k, j))],
                 outs=[((M, N), F32, (tm, tn), lambda i, j, k: (i, j))])[0]


def rms_fwd(x, g, *, name, tm=256, fused=None):
    T = x.shape[0]
    tm = _rows(T, tm)
    n = T // tm
    pred = lambda: (pl.program_id(0) == 0, pl.program_id(0) == n - 1)
    x_ins, x_outs, x_scratch, x_start, x_end = _fused_exchange(fused, pred)
    nx = len(x_ins)

    def body(*refs):
        x_ref, g_ref = refs[:2]
        o_ref = refs[2 + nx]
        x_refs = refs[2:2 + nx] + refs[3 + nx:]
        x_start(x_refs)
        x = x_ref[...]
        r = lax.rsqrt(jnp.mean(x * x, axis=-1, keepdims=True) + EPS)
        o_ref[...] = (x * r * g_ref[...]).astype(MXU)
        x_end(x_refs)

    outs = _call(body, name=name, grid=(n,),
                 ins=[(x, (tm, D), lambda i: (i, 0)), (g, (1, D), lambda i: (0, 0))] + x_ins,
                 outs=[((T, D), MXU, (tm, D), lambda i: (i, 0))] + x_outs, scratch=x_scratch)
    return outs[0] if fused is None else outs


def rms_bwd(x, g, dhn, dres, *, name, tm=256):
    T = x.shape[0]
    tm = _rows(T, tm)

    def body(x_ref, g_ref, d_ref, r_ref, dx_ref, dg_ref, cs_ref):
        @pl.when(pl.program_id(0) == 0)
        def _():
            dg_ref[...] = jnp.zeros_like(dg_ref)
            cs_ref[...] = jnp.zeros_like(cs_ref)
        x = x_ref[...]
        d = d_ref[...]
        r = lax.rsqrt(jnp.mean(x * x, axis=-1, keepdims=True) + EPS)
        dg_ref[...] += jnp.sum(d * x * r, axis=0, keepdims=True)
        t = d * g_ref[...]
        m = jnp.mean(t * x, axis=-1, keepdims=True)
        dx = r_ref[...] + t * r - x * (r * r * r) * m
        dx_ref[...] = dx
        cs_ref[...] += jnp.sum(dx, axis=0, keepdims=True)

    row = lambda i: (i, 0)
    fix = lambda i: (0, 0)
    return _call(body, name=name, grid=(T // tm,),
                 ins=[(x, (tm, D), row), (g, (1, D), fix), (dhn, (tm, D), row), (dres, (tm, D), row)],
                 outs=[((T, D), F32, (tm, D), row), ((1, D), F32, (1, D), fix), ((1, D), F32, (1, D), fix)])


def loss_grad(y, t, *, name, tm=256):
    T = y.shape[0]
    tm = _rows(T, tm)

    def body(y_ref, t_ref, dy_ref, l_ref):
        @pl.when(pl.program_id(0) == 0)
        def _():
            l_ref[...] = jnp.zeros_like(l_ref)
        e = y_ref[...] - t_ref[...]
        dy_ref[...] = e * (1.0 / D)
        l_ref[...] += jnp.sum(e * e, axis=0, keepdims=True) * (0.5 / D)

    row = lambda i: (i, 0)
    return _call(body, name=name, grid=(T // tm,),
                 ins=[(y, (tm, D), row), (t, (tm, D), row)],
                 outs=[((T, D), F32, (tm, D), row), ((1, D), F32, (1, D), lambda i: (0, 0))])


def ffn_up(hn, wg, wu, *, name, tm=1024, tn=256, fused=None):
    T = hn.shape[0]
    tm = _rows(T, tm)
    ni, nj = T // tm, F // tn
    pred = lambda: ((pl.program_id(0) == 0) & (pl.program_id(1) == 0),
                    (pl.program_id(0) == ni - 1) & (pl.program_id(1) == nj - 1))
    x_ins, x_outs, x_scratch, x_start, x_end = _fused_exchange(fused, pred)
    nx = len(x_ins)

    def body(*refs):
        h_ref, wg_ref, wu_ref = refs[:3]
        g_ref, u_ref, a_ref = refs[3 + nx:6 + nx]
        x_refs = refs[3:3 + nx] + refs[6 + nx:]
        x_start(x_refs)
        h = h_ref[...]
        g = _dot(h, wg_ref[...])
        u = _dot(h, wu_ref[...])
        g_ref[...] = g.astype(MXU)
        u_ref[...] = u.astype(MXU)
        a_ref[...] = (g * _sigmoid(g) * u).astype(MXU)
        x_end(x_refs)

    tile = lambda i, j: (i, j)
    return _call(body, name=name, grid=(ni, nj),
                 ins=[(hn, (tm, D), lambda i, j: (i, 0)), (wg, (D, tn), lambda i, j: (0, j)),
                      (wu, (D, tn), lambda i, j: (0, j))] + x_ins,
                 outs=[((T, F), MXU, (tm, tn), tile), ((T, F), MXU, (tm, tn), tile),
                       ((T, F), MXU, (tm, tn), tile)] + x_outs,
                 scratch=x_scratch)


def ffn_bwd_act(dy, wd, G, U, *, name, tm=1024, tn=256):
    T = dy.shape[0]
    tm = _rows(T, tm)

    def body(dy_ref, wd_ref, g_ref, u_ref, dg_ref, du_ref):
        da = _dot(dy_ref[...], wd_ref[...], NT)
        g = g_ref[...].astype(F32)
        u = u_ref[...].astype(F32)
        s = _sigmoid(g)
        sil = g * s
        du_ref[...] = (da * sil).astype(MXU)
        dg_ref[...] = (da * u * (s * (1.0 + g * (1.0 - s)))).astype(MXU)

    tile = lambda i, j: (i, j)
    return _call(body, name=name, grid=(T // tm, F // tn),
                 ins=[(dy, (tm, D), lambda i, j: (i, 0)), (wd, (tn, D), lambda i, j: (j, 0)),
                      (G, (tm, tn), tile), (U, (tm, tn), tile)],
                 outs=[((T, F), MXU, (tm, tn), tile), ((T, F), MXU, (tm, tn), tile)])


def ffn_fwd(h, g_norm, wg, wu, wd, tag, fused=None):
    hn = rms_fwd(h, g_norm, name=f"ffn{tag}_norm")
    G, U, act, *landed = ffn_up(hn, wg, wu, name=f"ffn{tag}_up", fused=fused)
    h_out = mm([(act, wd)], res=h, name=f"ffn{tag}_down")
    return h_out, (hn, G, U, act), landed


def ffn_bwd(dy, h, g_norm, wg, wu, wd, saved, tag):
    hn, G, U, act = saved
    dG, dU = ffn_bwd_act(dy, wd, G, U, name=f"ffn{tag}_bwd_act")
    d_wd = mm_tn(act, dy, name=f"ffn{tag}_dwd", tm=1408, tn=512)
    d_wg = mm_tn(hn, dG, name=f"ffn{tag}_dwg", tm=512, tn=1408)
    d_wu = mm_tn(hn, dU, name=f"ffn{tag}_dwu", tm=512, tn=1408)
    dhn = mm([(dG, wg), (dU, wu)], trans_b=True, name=f"ffn{tag}_dhn")
    dh, d_g, cs = rms_bwd(h, g_norm, dhn, dy, name=f"ffn{tag}_norm_bwd")
    return dh, d_g, d_wg, d_wu, d_wd, cs


SUBLANES = 8


def _window_scratch(tm):
    return [pltpu.VMEM((tm + HALO + SUBLANES, D), F32), pltpu.VMEM((SUBLANES, tm + HALO, D), F32)]


def _shift_copies(win, sh):
    rows = sh.shape[1]
    win[pl.ds(rows, SUBLANES), :] = jnp.zeros((SUBLANES, D), F32)
    for s in range(SUBLANES):
        sh[s] = win[pl.ds(s, rows), :]


def _tap(sh, off, rb):
    s = off % SUBLANES
    return sh[s, pl.ds(off - s, rb), :]


def conv_fwd(A, w_dw, b_dw, ln_g, ln_b, *, name, tm=128, rb=32, fused=None):
    T = A.shape[0]
    tm = _rows(T, tm)
    n = T // tm
    pred = lambda: (pl.program_id(0) == 0, pl.program_id(0) == n - 1)
    x_ins, x_outs, x_scratch, x_start, x_end = _fused_exchange(fused, pred)
    nx = len(x_ins)

    def body(*refs):
        a_ref, ap_ref, w_ref, b_ref, g_ref, bb_ref = refs[:6]
        s_ref, c_ref = refs[6 + nx:8 + nx]
        win, sh = refs[8 + 2 * nx:10 + 2 * nx]
        x_refs = refs[6:6 + nx] + refs[8 + nx:8 + 2 * nx] + refs[10 + 2 * nx:]
        x_start(x_refs)
        i = pl.program_id(0)
        a = a_ref[...]
        win[pl.ds(HALO, tm), :] = a[:, :D] * _sigmoid(a[:, D:])
        ap = ap_ref[pl.ds(tm - HALO, HALO), :]
        up = ap[:, :D] * _sigmoid(ap[:, D:])
        win[pl.ds(0, HALO), :] = jnp.where(i > 0, up, 0.0)
        _shift_copies(win, sh)
        for r0 in range(0, tm, rb):
            acc = jnp.broadcast_to(b_ref[...], (rb, D))
            for k in range(CONV_W):
                acc = acc + w_ref[pl.ds(k, 1), :] * _tap(sh, r0 + k + HALO - (CONV_W - 1), rb)
            c_ref[pl.ds(r0, rb), :] = acc
        c = c_ref[...]
        mu = jnp.mean(c, axis=-1, keepdims=True)
        cc = c - mu
        var = jnp.mean(cc * cc, axis=-1, keepdims=True)
        z = cc * lax.rsqrt(var + EPS) * g_ref[...] + bb_ref[...]
        s_ref[...] = (z * _sigmoid(z)).astype(MXU)
        x_end(x_refs)

    row = lambda i: (i, 0)
    fix = lambda i: (0, 0)
    return _call(body, name=name, grid=(n,),
                 ins=[(A, (tm, 2 * D), row), (A, (tm, 2 * D), lambda i: (jnp.maximum(i - 1, 0), 0)),
                      (w_dw, (HALO, D), fix), (b_dw, (1, D), fix), (ln_g, (1, D), fix), (ln_b, (1, D), fix)] + x_ins,
                 outs=[((T, D), MXU, (tm, D), row), ((T, D), F32, (tm, D), row)] + x_outs,
                 scratch=_window_scratch(tm) + x_scratch)


def ln_swish_bwd(c, ds, ln_g, ln_b, *, name, tm=256):
    T = c.shape[0]
    tm = _rows(T, tm)

    def body(c_ref, ds_ref, g_ref, b_ref, dc_ref, dg_ref, db_ref):
        @pl.when(pl.program_id(0) == 0)
        def _():
            dg_ref[...] = jnp.zeros_like(dg_ref)
            db_ref[...] = jnp.zeros_like(db_ref)
        c = c_ref[...]
        mu = jnp.mean(c, axis=-1, keepdims=True)
        cc = c - mu
        rstd = lax.rsqrt(jnp.mean(cc * cc, axis=-1, keepdims=True) + EPS)
        n = cc * rstd
        z = n * g_ref[...] + b_ref[...]
        s = _sigmoid(z)
        dz = ds_ref[...] * (s * (1.0 + z * (1.0 - s)))
        dg_ref[...] += jnp.sum(dz * n, axis=0, keepdims=True)
        db_ref[...] += jnp.sum(dz, axis=0, keepdims=True)
        dn = dz * g_ref[...]
        dc_ref[...] = rstd * (dn - jnp.mean(dn, axis=-1, keepdims=True)
                              - n * jnp.mean(dn * n, axis=-1, keepdims=True))

    row = lambda i: (i, 0)
    fix = lambda i: (0, 0)
    return _call(body, name=name, grid=(T // tm,),
                 ins=[(c, (tm, D), row), (ds, (tm, D), row), (ln_g, (1, D), fix), (ln_b, (1, D), fix)],
                 outs=[((T, D), F32, (tm, D), row), ((1, D), F32, (1, D), fix), ((1, D), F32, (1, D), fix)])


def conv_bwd(dc, A, w_dw, *, name, tm=128, rb=32, fused=None):
    T = A.shape[0]
    tm = _rows(T, tm)
    n = T // tm
    SUB = 8
    pred = lambda: (pl.program_id(0) == 0, pl.program_id(0) == n - 1)
    x_ins, x_outs, x_scratch, x_start, x_end = _fused_exchange(fused, pred)
    nx = len(x_ins)

    def body(*refs):
        dc_ref, dn_ref, a_ref, ap_ref, w_ref = refs[:5]
        da_ref, dw_ref, dbd_ref, dbp_ref = refs[5 + nx:9 + nx]
        wdc, sdc, wu, su, accw = refs[9 + 2 * nx:14 + 2 * nx]
        x_refs = refs[5:5 + nx] + refs[9 + nx:9 + 2 * nx] + refs[14 + 2 * nx:]
        x_start(x_refs)
        i = pl.program_id(0)

        @pl.when(i == 0)
        def _():
            accw[...] = jnp.zeros_like(accw)
            dbd_ref[...] = jnp.zeros_like(dbd_ref)
            dbp_ref[...] = jnp.zeros_like(dbp_ref)
        a = a_ref[...]
        a1, a2 = a[:, :D], a[:, D:]
        sg = _sigmoid(a2)
        wu[pl.ds(HALO, tm), :] = a1 * sg
        ap = ap_ref[pl.ds(tm - HALO, HALO), :]
        wu[pl.ds(0, HALO), :] = jnp.where(i > 0, ap[:, :D] * _sigmoid(ap[:, D:]), 0.0)
        dc = dc_ref[...]
        wdc[pl.ds(0, tm), :] = dc
        wdc[pl.ds(tm, HALO), :] = jnp.where(i < n - 1, dn_ref[pl.ds(0, HALO), :], 0.0)
        dbd_ref[...] += jnp.sum(dc, axis=0, keepdims=True)
        _shift_copies(wdc, sdc)
        _shift_copies(wu, su)
        for r0 in range(0, tm, rb):
            du = jnp.zeros((rb, D), F32)
            dcs = wdc[pl.ds(r0, rb), :]
            for k in range(CONV_W):
                du = du + w_ref[pl.ds(k, 1), :] * _tap(sdc, r0 + (CONV_W - 1) - k, rb)
                p = dcs * _tap(su, r0 + k + HALO - (CONV_W - 1), rb)
                accw[pl.ds(SUB * k, SUB), :] += jnp.sum(p.reshape(rb // SUB, SUB, D), axis=0)
            s = sg[r0:r0 + rb]
            da_ref[pl.ds(r0, rb), pl.ds(0, D)] = (du * s).astype(MXU)
            da_ref[pl.ds(r0, rb), pl.ds(D, D)] = (du * a1[r0:r0 + rb] * s * (1.0 - s)).astype(MXU)
        da = da_ref[...].astype(F32)
        dbp_ref[...] += jnp.sum(da, axis=0, keepdims=True)

        @pl.when(i == n - 1)
        def _():
            dw_ref[...] = jnp.zeros_like(dw_ref)
            for k in range(CONV_W):
                dw_ref[pl.ds(k, 1), :] = jnp.sum(accw[pl.ds(SUB * k, SUB), :], axis=0, keepdims=True)
        x_end(x_refs)

    row = lambda i: (i, 0)
    fix = lambda i: (0, 0)
    return _call(body, name=name, grid=(n,),
                 ins=[(dc, (tm, D), row), (dc, (tm, D), lambda i: (jnp.minimum(i + 1, n - 1), 0)),
                      (A, (tm, 2 * D), row), (A, (tm, 2 * D), lambda i: (jnp.maximum(i - 1, 0), 0)),
                      (w_dw, (HALO, D), fix)] + x_ins,
                 outs=[((T, 2 * D), MXU, (tm, 2 * D), row), ((HALO, D), F32, (HALO, D), fix),
                       ((1, D), F32, (1, D), fix), ((1, 2 * D), F32, (1, 2 * D), fix)] + x_outs,
                 scratch=_window_scratch(tm) + _window_scratch(tm) + [pltpu.VMEM((SUB * HALO, D), F32)] + x_scratch)


def sb_prep(ps, gq, gk, *, name, tm=256):
    T = ps.shape[0]
    tm = _rows(T, tm)
    W = SB_H * SB_D

    def body(q_ref, k_ref, v_ref, gq_ref, gk_ref, qo, ko, vo):
        qs, ks = [], []
        for h in range(SB_H):
            sl = slice(h * SB_D, (h + 1) * SB_D)
            q = q_ref[:, sl]
            k = k_ref[:, sl]
            rq = lax.rsqrt(jnp.mean(q * q, axis=-1, keepdims=True) + EPS)
            rk = lax.rsqrt(jnp.mean(k * k, axis=-1, keepdims=True) + EPS)
            qs.append(q * rq * gq_ref[...] * (SB_D ** -0.5))
            ks.append(k * rk * gk_ref[...])
        qo[...] = jnp.concatenate(qs, axis=-1).astype(MXU)
        ko[...] = jnp.concatenate(ks, axis=-1).astype(MXU)
        vo[...] = v_ref[...].astype(MXU)

    row = lambda i: (i, 0)
    fix = lambda i: (0, 0)
    return _call(body, name=name, grid=(T // tm,),
                 ins=[(ps, (tm, W), lambda i: (i, 0)), (ps, (tm, W), lambda i: (i, 1)),
                      (ps, (tm, W), lambda i: (i, 2)), (gq, (1, SB_D), fix), (gk, (1, SB_D), fix)],
                 outs=[((T, W), MXU, (tm, W), row)] * 3)


def _sb_masks(tq):
    row = lax.broadcasted_iota(jnp.int32, (tq, tq), 0)
    col = lax.broadcasted_iota(jnp.int32, (tq, tq), 1)
    tri = jnp.where(row >= col, 1.0, 0.0).astype(MXU)
    past = col < row
    return tri, past


def _sb_scores(q, k, past):
    return _sb_logs(_dot(q, k, NT), past)


def _sb_logs(z, past):
    ls = jnp.minimum(z, 0.0) - _softplus_neg_abs(z)
    lk = ls - z
    if past is not None:
        lk = jnp.where(past, lk, 0.0)
    return ls, lk


def _fused_exchange(fused, steps_pred):
    if fused is None:
        return [], [], [], lambda refs: None, lambda refs: None
    src, n_dst, spec = fused
    n = len(spec["flips"])
    ins = [(src, None, None)]
    outs = [((n_dst,) + src.shape[1:], src.dtype, None, None)]
    scratch = [pltpu.SemaphoreType.DMA((n,)), pltpu.SemaphoreType.DMA((n,)), pltpu.SemaphoreType.DMA(())]

    def at_start(refs):
        @pl.when(steps_pred()[0])
        def _():
            _exchange_start(*refs, **spec)

    def at_end(refs):
        @pl.when(steps_pred()[1])
        def _():
            _exchange_wait(*refs, **spec)

    return ins, outs, scratch, at_start, at_end


PAIR = 2 * SB_D
SB_DEAD = -120.0
SB_UNVISITED = -1e30


def _split_pair(x2, lo):
    zero = jnp.zeros_like(x2)
    return [jnp.where(lo, x2, zero), jnp.where(lo, zero, x2)]


def sb_fwd(qn, kn, v, *, name, tq=256, pairs=2, fused=None):
    T, W = qn.shape
    tq = _rows(T, tq)
    wb, hg = pairs * PAIR, 2 * pairs
    assert T // tq <= LANES and W % wb == 0
    ng, ni = W // wb, T // tq
    pred = lambda: ((pl.program_id(0) == 0) & (pl.program_id(1) == 0),
                    (pl.program_id(0) == ng - 1) & (pl.program_id(1) == ni - 1))
    x_ins, x_outs, x_scratch, x_start, x_end = _fused_exchange(fused, pred)
    nx = len(x_ins)
    heads = range(hg)
    pcols = [slice(p * PAIR, (p + 1) * PAIR) for p in range(pairs)]

    def body(*refs):
        q_ref, k_ref, v_ref = refs[:3]
        o_ref, cb_ref = refs[3 + nx:5 + nx]
        x_refs = refs[3:3 + nx] + refs[5 + nx:]
        x_start(x_refs)
        i = pl.program_id(1)
        tri, past = _sb_masks(tq)
        lane = lax.broadcasted_iota(jnp.int32, (tq, LANES), 1)
        lo = lane < SB_D
        cb_ref[...] = jnp.full((hg, tq, LANES), SB_UNVISITED, F32)
        qs = [m for p in range(pairs) for m in _split_pair(q_ref[:, pcols[p]], lo)]

        def block(j, carry, masked):
            accs, cs = carry
            kblk = pl.ds(pl.multiple_of(j * tq, tq), tq)
            for h in heads:
                cb_ref[h] = jnp.where(lane == j, cs[h], cb_ref[h])
            zs = [_dot(qs[h], k_ref[kblk, pcols[h // 2]], NT) for h in heads]
            sc = [_sb_logs(zs[h], past if masked else None) for h in heads]
            bincs = [_dot_split(sc[h][1], tri) for h in heads]
            ws = [jnp.exp(sc[h][0] + cs[h] + bincs[h] - sc[h][1]) for h in heads]
            if masked:
                ws = [jnp.where(past, w, 0.0) for w in ws]
            pv = [_dot(ws[h], v_ref[kblk, pcols[h // 2]]) for h in heads]
            new_a = [accs[p] + jnp.where(lo, pv[2 * p], pv[2 * p + 1]) for p in range(pairs)]
            new_c = [cs[h] + jnp.sum(sc[h][1], axis=-1, keepdims=True) for h in heads]
            return tuple(new_a), tuple(new_c)

        carry = ((jnp.zeros((tq, PAIR), F32),) * pairs, (jnp.zeros((tq, 1), F32),) * hg)
        carry = block(i, carry, True)

        def live(state):
            jj, _, cs = state
            return (jj < i) & (jnp.max(functools.reduce(jnp.maximum, cs)) > SB_DEAD)

        def step(state):
            jj, accs, cs = state
            accs, cs = block(i - 1 - jj, (accs, cs), False)
            return jj + 1, accs, cs

        _, accs, _ = lax.while_loop(live, step, (jnp.int32(0),) + carry)
        for p in range(pairs):
            o_ref[:, pcols[p]] = accs[p]
        x_end(x_refs)

    blk = lambda g, i: (i, g)
    full = lambda g, i: (0, g)
    return _call(body, name=name, grid=(ng, ni),
                 ins=[(qn, (tq, wb), blk), (kn, (T, wb), full), (v, (T, wb), full)] + x_ins,
                 outs=[((T, W), F32, (tq, wb), blk),
                       ((W // SB_D, T, LANES), F32, (hg, tq, LANES), lambda g, i: (g, i, 0))] + x_outs,
                 scratch=x_scratch)


def sb_bwd(qn, kn, v, cb, dmix, do_col, *, name, tq=256, pairs=2, fused=None):
    T, W = qn.shape
    tq = _rows(T, tq)
    wb, hg = pairs * PAIR, 2 * pairs
    assert W % wb == 0 and do_col % wb == 0
    ng, ni = W // wb, T // tq
    pred = lambda: ((pl.program_id(0) == 0) & (pl.program_id(1) == 0),
                    (pl.program_id(0) == ng - 1) & (pl.program_id(1) == ni - 1))
    x_ins, x_outs, x_scratch, x_start, x_end = _fused_exchange(fused, pred)
    nx = len(x_ins)
    heads = range(hg)
    pcols = [slice(p * PAIR, (p + 1) * PAIR) for p in range(pairs)]

    def body(*refs):
        q_ref, k_ref, v_ref, cb_ref, do_ref = refs[:5]
        dq_ref, dk_hbm, dv_hbm = refs[5 + nx:8 + nx]
        dk_ref, dv_ref = refs[8 + 2 * nx:10 + 2 * nx]
        x_refs = refs[5:5 + nx] + refs[8 + nx:8 + 2 * nx] + refs[10 + 2 * nx:]
        x_start(x_refs)
        i = pl.program_id(1)

        @pl.when(i == 0)
        def _():
            dk_ref[...] = jnp.zeros_like(dk_ref)
            dv_ref[...] = jnp.zeros_like(dv_ref)
        tri, past = _sb_masks(tq)
        row = lax.broadcasted_iota(jnp.int32, (tq, tq), 0)
        col = lax.broadcasted_iota(jnp.int32, (tq, tq), 1)
        tri_lt = jnp.where(row < col, 1.0, 0.0).astype(MXU)
        lane = lax.broadcasted_iota(jnp.int32, (tq, LANES), 1)
        lo = lane < SB_D
        qs = [m for p in range(pairs) for m in _split_pair(q_ref[:, pcols[p]], lo)]
        dos = [m for p in range(pairs) for m in _split_pair(do_ref[:, pcols[p]].astype(MXU), lo)]

        def block(j, carry, masked):
            dqs, cgs = carry
            kblk = pl.ds(pl.multiple_of(j * tq, tq), tq)
            k2 = [k_ref[kblk, pcols[p]] for p in range(pairs)]
            zs = [_dot(qs[h], k2[h // 2], NT) for h in heads]
            dws = [_dot(dos[h], v_ref[kblk, pcols[h // 2]], NT) for h in heads]
            sc = [_sb_logs(zs[h], past if masked else None) for h in heads]
            cs = [jnp.sum(jnp.where(lane == j, cb_ref[h], 0.0), axis=-1, keepdims=True) for h in heads]
            bincs = [_dot_split(sc[h][1], tri) for h in heads]
            ws = [jnp.exp(sc[h][0] + cs[h] + bincs[h] - sc[h][1]) for h in heads]
            if masked:
                ws = [jnp.where(past, w, 0.0) for w in ws]
            wbs = [w.astype(MXU) for w in ws]
            gs = [ws[h] * dws[h] for h in heads]
            gpres = [cgs[h] + _dot(gs[h], tri_lt) for h in heads]
            sigs = [jnp.exp(sc[h][0]) for h in heads]
            dzs = [gs[h] - sigs[h] * (gs[h] + gpres[h]) for h in heads]
            if masked:
                dzs = [jnp.where(past, dz, 0.0) for dz in dzs]
            dzbs = [dz.astype(MXU) for dz in dzs]
            dqp = [_dot(dzbs[h], k2[h // 2]) for h in heads]
            new_dq = [dqs[p] + jnp.where(lo, dqp[2 * p], dqp[2 * p + 1]) for p in range(pairs)]
            for p in range(pairs):
                dk_ref[kblk, pcols[p]] += _dot(dzbs[2 * p], qs[2 * p], TN) + _dot(dzbs[2 * p + 1], qs[2 * p + 1], TN)
            for p in range(pairs):
                dv_ref[kblk, pcols[p]] += _dot(wbs[2 * p], dos[2 * p], TN) + _dot(wbs[2 * p + 1], dos[2 * p + 1], TN)
            new_cg = [cgs[h] + jnp.sum(gs[h], axis=-1, keepdims=True) for h in heads]
            return tuple(new_dq), tuple(new_cg)

        colmax = functools.reduce(jnp.maximum, [jnp.max(cb_ref[h], axis=0, keepdims=True) for h in heads])
        lane_row = lax.broadcasted_iota(jnp.int32, (1, LANES), 1)
        is_live = (colmax > SB_DEAD) & (lane_row < i)
        first = jnp.min(jnp.where(is_live, lane_row, i).astype(F32)).astype(jnp.int32)
        carry = ((jnp.zeros((tq, PAIR), F32),) * pairs, (jnp.zeros((tq, 1), F32),) * hg)
        carry = lax.fori_loop(first, i, lambda j, car: block(j, car, False), carry)
        dqs, _ = block(i, carry, True)
        for p in range(pairs):
            dq_ref[:, pcols[p]] = dqs[p]

        cols = pl.ds(pl.multiple_of(pl.program_id(0) * wb, wb), wb)

        @pl.when(i == ni - 1)
        def _():
            pltpu.sync_copy(dk_ref, dk_hbm.at[:, cols])
            pltpu.sync_copy(dv_ref, dv_hbm.at[:, cols])
        x_end(x_refs)

    blk = lambda g, i: (i, g)
    full = lambda g, i: (0, g)
    return _call(body, name=name, grid=(ng, ni),
                 ins=[(qn, (tq, wb), blk), (kn, (T, wb), full), (v, (T, wb), full),
                      (cb, (hg, tq, LANES), lambda g, i: (g, i, 0)),
                      (dmix, (tq, wb), lambda g, i: (i, do_col // wb + g))] + x_ins,
                 outs=[((T, W), F32, (tq, wb), blk), ((T, W), F32, None, None), ((T, W), F32, None, None)] + x_outs,
                 scratch=[pltpu.VMEM((T, wb), F32), pltpu.VMEM((T, wb), F32)] + x_scratch)


def sb_post_bwd(dqn, dkn, dv, ps, gq, gk, *, name, tm=256):
    T = ps.shape[0]
    tm = _rows(T, tm)
    W = SB_H * SB_D

    def body(dq_ref, dk_ref, dv_ref, q_ref, k_ref, gq_ref, gk_ref, out_ref, dgq_ref, dgk_ref):
        @pl.when(pl.program_id(0) == 0)
        def _():
            dgq_ref[...] = jnp.zeros_like(dgq_ref)
            dgk_ref[...] = jnp.zeros_like(dgk_ref)

        def norm_bwd(x, d, g):
            r = lax.rsqrt(jnp.mean(x * x, axis=-1, keepdims=True) + EPS)
            dg = jnp.sum(d * x * r, axis=0, keepdims=True)
            t = d * g
            return t * r - x * (r * r * r) * jnp.mean(t * x, axis=-1, keepdims=True), dg

        dqs, dks = [], []
        dgq = jnp.zeros((1, SB_D), F32)
        dgk = jnp.zeros((1, SB_D), F32)
        for h in range(SB_H):
            sl = slice(h * SB_D, (h + 1) * SB_D)
            a, ga = norm_bwd(q_ref[:, sl], dq_ref[:, sl] * (SB_D ** -0.5), gq_ref[...])
            b, gb = norm_bwd(k_ref[:, sl], dk_ref[:, sl], gk_ref[...])
            dqs.append(a)
            dks.append(b)
            dgq = dgq + ga
            dgk = dgk + gb
        out_ref[...] = jnp.concatenate(dqs + dks + [dv_ref[...]], axis=-1).astype(MXU)
        dgq_ref[...] += dgq
        dgk_ref[...] += dgk

    row = lambda i: (i, 0)
    fix = lambda i: (0, 0)
    return _call(body, name=name, grid=(T // tm,),
                 ins=[(dqn, (tm, W), row), (dkn, (tm, W), row), (dv, (tm, W), row),
                      (ps, (tm, W), lambda i: (i, 0)), (ps, (tm, W), lambda i: (i, 1)),
                      (gq, (1, SB_D), fix), (gk, (1, SB_D), fix)],
                 outs=[((T, 3 * W), MXU, (tm, 3 * W), lambda i: (i, 0)),
                       ((1, SB_D), F32, (1, SB_D), fix), ((1, SB_D), F32, (1, SB_D), fix)])


GLA_TM = 512
GLA_DK = GLA_H * GLA_K
GLA_DV = GLA_H * GLA_V


def _gla_masks(tm):
    row = lax.broadcasted_iota(jnp.int32, (tm, tm), 0)
    col = lax.broadcasted_iota(jnp.int32, (tm, tm), 1)
    same = (row // CHUNK) == (col // CHUNK)
    return row, col, same


def _gla_gate(glr, w2, b, tm):
    pre = _dot(glr, w2) + b
    la = (jnp.minimum(pre, 0.0) - _softplus_neg_abs(pre)) * (1.0 / 16.0)
    row, col, same = _gla_masks(tm)
    m_incl = jnp.where(same & (col <= row), 1.0, 0.0).astype(MXU)
    m_full = jnp.where(same, 1.0, 0.0).astype(MXU)
    bc = _dot_split(m_incl, la, a_split=False)
    tot = _dot_split(m_full, la, a_split=False)
    return pre, bc, tot


def gla_fwd(pg, glr, w2, b_gate, g_out, *, name):
    T = pg.shape[0]
    tm = _rows(T, GLA_TM)
    ncb = tm // CHUNK
    NC = T // CHUNK

    def body(q_ref, k_ref, v_ref, r_ref, l_ref, w2_ref, b_ref, g_ref, o_ref, st_ref, S):
        @pl.when(pl.program_id(0) == 0)
        def _():
            S[...] = jnp.zeros_like(S)
        _, bc, tot = _gla_gate(l_ref[...], w2_ref[...], b_ref[...], tm)
        kend = k_ref[...] * jnp.exp(tot - bc)
        qs = q_ref[...] * (GLA_K ** -0.5)
        a_all = jnp.exp(tot)
        v = v_ref[...]
        r = r_ref[...]
        rows = [slice(c * CHUNK, (c + 1) * CHUNK) for c in range(ncb)]
        hks = [slice(h * GLA_K, (h + 1) * GLA_K) for h in range(GLA_H)]
        hvs = [slice(h * GLA_V, (h + 1) * GLA_V) for h in range(GLA_H)]
        uts = [[_dot(v[rows[c], hvs[h]], kend[rows[c], hks[h]], TN) for h in range(GLA_H)] for c in range(ncb)]
        for h in range(GLA_H):
            s = S[h]
            for c in range(ncb):
                s = s * a_all[c * CHUNK:c * CHUNK + 1, hks[h]] + uts[c][h]
                st_ref[c, h] = s
            S[h] = s
        for c in range(ncb):
            outs = []
            for h in range(GLA_H):
                o = _dot(qs[rows[c], hks[h]], st_ref[c, h], NT)
                rinv = lax.rsqrt(jnp.mean(o * o, axis=-1, keepdims=True) + EPS)
                rr = r[rows[c], hvs[h]]
                outs.append(o * rinv * g_ref[...] * (rr * _sigmoid(rr)))
            o_ref[pl.ds(c * CHUNK, CHUNK), :] = jnp.concatenate(outs, axis=-1).astype(MXU)

    fix = lambda i: (0, 0)
    return _call(body, name=name, grid=(T // tm,),
                 ins=[(pg, (tm, GLA_DK), lambda i: (i, 0)), (pg, (tm, GLA_DK), lambda i: (i, 1)),
                      (pg, (tm, GLA_DV), lambda i: (i, 1)), (pg, (tm, GLA_DV), lambda i: (i, 2)),
                      (glr, (tm, LANES), lambda i: (i, 0)), (w2, (LANES, GLA_DK), fix),
                      (b_gate, (1, GLA_DK), fix), (g_out, (1, GLA_V), fix)],
                 outs=[((T, GLA_DV), MXU, (tm, GLA_DV), lambda i: (i, 0)),
                       ((NC, GLA_H, GLA_V, GLA_K), F32, (ncb, GLA_H, GLA_V, GLA_K), lambda i: (i, 0, 0, 0))],
                 scratch=[pltpu.VMEM((GLA_H, GLA_V, GLA_K), F32)])


def gla_bwd(pg, glr, w2, b_gate, g_out, st, dmix, *, name):
    T = pg.shape[0]
    tm = _rows(T, GLA_TM)
    ncb = tm // CHUNK
    n = T // tm

    def body(q_ref, k_ref, v_ref, r_ref, l_ref, w2_ref, b_ref, g_ref, st_ref, sp_ref, d_ref,
             dpg_ref, dl_ref, dw2_ref, db_ref, dg_ref, dS, dkend, extra, dst_ref):
        i = pl.program_id(0)

        @pl.when(i == 0)
        def _():
            dS[...] = jnp.zeros_like(dS)
            dw2_ref[...] = jnp.zeros_like(dw2_ref)
            db_ref[...] = jnp.zeros_like(db_ref)
            dg_ref[...] = jnp.zeros_like(dg_ref)
        first_tile = i == n - 1
        glr_v = l_ref[...]
        pre, bc, tot = _gla_gate(glr_v, w2_ref[...], b_ref[...], tm)
        dec = jnp.exp(tot - bc)
        k = k_ref[...]
        kend = k * dec
        qs = q_ref[...] * (GLA_K ** -0.5)
        a_all = jnp.exp(tot)
        v = v_ref[...]
        r = r_ref[...]
        g = g_ref[...]
        dgg = jnp.zeros((1, GLA_V), F32)
        rows = [slice(c * CHUNK, (c + 1) * CHUNK) for c in range(ncb)]
        hks = [slice(h * GLA_K, (h + 1) * GLA_K) for h in range(GLA_H)]
        hvs = [slice(h * GLA_V, (h + 1) * GLA_V) for h in range(GLA_H)]
        heads = range(GLA_H)
        qdo = [[None] * GLA_H for _ in range(ncb)]
        for c in range(ncb):
            cr = pl.ds(c * CHUNK, CHUNK)
            dq_l, dr_l = [], []
            for h in heads:
                s_c = st_ref[c, h]
                qh = qs[rows[c], hks[h]]
                o = _dot(qh, s_c, NT)
                rinv = lax.rsqrt(jnp.mean(o * o, axis=-1, keepdims=True) + EPS)
                nrm = o * rinv
                rr = r[rows[c], hvs[h]]
                sg = _sigmoid(rr)
                sil = rr * sg
                d = d_ref[cr, pl.ds(h * GLA_V, GLA_V)]
                dr_l.append(d * nrm * g * (sg * (1.0 + rr * (1.0 - sg))))
                dgg = dgg + jnp.sum(d * sil * nrm, axis=0, keepdims=True)
                dn = d * sil * g
                do = rinv * (dn - nrm * jnp.mean(dn * nrm, axis=-1, keepdims=True))
                dq_l.append(_dot(do, s_c) * (GLA_K ** -0.5))
                qdo[c][h] = _dot(do, qh, TN)
            dpg_ref[cr, pl.ds(0, GLA_DK)] = jnp.concatenate(dq_l, axis=-1).astype(MXU)
            dpg_ref[cr, pl.ds(2 * GLA_DK + GLA_DV, GLA_DV)] = jnp.concatenate(dr_l, axis=-1).astype(MXU)
        ex = [[None] * GLA_H for _ in range(ncb)]
        for h in heads:
            ds_h = dS[h]
            for c in reversed(range(ncb)):
                dst = ds_h + qdo[c][h]
                dst_ref[c, h] = dst
                s_p = st_ref[c - 1, h] if c > 0 else jnp.where(first_tile, 0.0, sp_ref[0, h])
                a = a_all[c * CHUNK:c * CHUNK + 1, hks[h]]
                da = jnp.sum(dst * s_p, axis=0, keepdims=True)
                ds_h = dst * a
                ex[c][h] = jnp.broadcast_to(da * a, (CHUNK, GLA_K))
            dS[h] = ds_h
        for c in range(ncb):
            extra[pl.ds(c * CHUNK, CHUNK), :] = jnp.concatenate(ex[c], axis=-1)
        for c in range(ncb):
            cr = pl.ds(c * CHUNK, CHUNK)
            dk_l = [_dot(v[rows[c], hvs[h]], dst_ref[c, h]) for h in heads]
            dv_l = [_dot(kend[rows[c], hks[h]], dst_ref[c, h], NT) for h in heads]
            dpg_ref[cr, pl.ds(2 * GLA_DK, GLA_DV)] = jnp.concatenate(dv_l, axis=-1).astype(MXU)
            dkend[cr, :] = jnp.concatenate(dk_l, axis=-1)
        dke = dkend[...]
        dpg_ref[:, pl.ds(GLA_DK, GLA_DK)] = (dke * dec).astype(MXU)
        e = dke * kend
        row, col, same = _gla_masks(tm)
        m_lt = jnp.where(same & (col < row), 1.0, 0.0).astype(MXU)
        dla = _dot_split(m_lt, e, a_split=False) + extra[...]
        sp = _softplus_neg_abs(pre)
        one_m_sig = jnp.exp(-jnp.maximum(pre, 0.0) - sp)
        dpre = dla * (1.0 / 16.0) * one_m_sig
        dl_ref[...] = _dot(dpre, w2_ref[...], NT).astype(MXU)
        dw2_ref[...] += _dot(glr_v, dpre, TN)
        db_ref[...] += jnp.sum(dpre, axis=0, keepdims=True)
        dg_ref[...] += dgg

    fix = lambda i: (0, 0)
    rev = lambda i: n - 1 - i
    return _call(body, name=name, grid=(n,),
                 ins=[(pg, (tm, GLA_DK), lambda i: (rev(i), 0)), (pg, (tm, GLA_DK), lambda i: (rev(i), 1)),
                      (pg, (tm, GLA_DV), lambda i: (rev(i), 1)), (pg, (tm, GLA_DV), lambda i: (rev(i), 2)),
                      (glr, (tm, LANES), lambda i: (rev(i), 0)), (w2, (LANES, GLA_DK), fix),
                      (b_gate, (1, GLA_DK), fix), (g_out, (1, GLA_V), fix),
                      (st, (ncb, GLA_H, GLA_V, GLA_K), lambda i: (rev(i), 0, 0, 0)),
                      (st, (1, GLA_H, GLA_V, GLA_K), lambda i: (jnp.maximum(rev(i) * ncb - 1, 0), 0, 0, 0)),
                      (dmix, (tm, GLA_DV), lambda i: (rev(i), 0))],
                 outs=[((T, 2 * GLA_DK + 2 * GLA_DV), MXU, (tm, 2 * GLA_DK + 2 * GLA_DV), lambda i: (rev(i), 0)),
                       ((T, LANES), MXU, (tm, LANES), lambda i: (rev(i), 0)),
                       ((LANES, GLA_DK), F32, (LANES, GLA_DK), fix),
                       ((1, GLA_DK), F32, (1, GLA_DK), fix), ((1, GLA_V), F32, (1, GLA_V), fix)],
                 scratch=[pltpu.VMEM((GLA_H, GLA_V, GLA_K), F32), pltpu.VMEM((tm, GLA_DK), F32),
                          pltpu.VMEM((tm, GLA_DK), F32), pltpu.VMEM((ncb, GLA_H, GLA_V, GLA_K), F32)])


def local_step(x, tgt, W, late=None):
    row1 = lambda a, l: a[l:l + 1]
    if late is None:
        hn0 = rms_fwd(x, row1(W["mix_norm"], 0), name="l0_norm")
    else:
        hn0, landed = rms_fwd(x, row1(W["mix_norm"], 0), name="l0_norm", fused=(late["first_src"], N_CHIPS, GATHER))
        W = {**W, **late["unpack_first"](landed)}
    pg = mm([(hn0, W["wi_g"])], name="l0_proj_gla", tn=512)
    ps = mm([(hn0, W["wi_s"])], name="l0_proj_sb", tn=512)
    glr = mm([(hn0, W["wi_l"])], name="l0_proj_gate", out_dtype=MXU)
    og, st = gla_fwd(pg, glr, W["w2"], W["b_gate"], W["g_gla"], name="gla_fwd")
    qn, kn, vh = sb_prep(ps, W["g_q"], W["g_k"], name="sb_prep")
    if late is None:
        osb, sb_cb = sb_fwd(qn, kn, vh, name="sb_fwd")
    else:
        osb, sb_cb, gathered = sb_fwd(qn, kn, vh, name="sb_fwd", fused=(late["src"][0], N_CHIPS, GATHER))
        W = {**W, **late["unpack"][0](gathered)}
    h1 = mm([(og, W["wo_g"]), (osb, W["wo_s"])], res=x, name="l0_out")
    h2, ffn0, landed = ffn_fwd(h1, row1(W["ffn_norm"], 0), W["wg0"], W["wu0"], W["wd0"], 0,
                               fused=None if late is None else (late["src"][1], N_CHIPS, GATHER))
    if late is not None:
        W = {**W, **late["unpack"][1](landed[0])}
    hn1 = rms_fwd(h2, row1(W["mix_norm"], 1), name="l1_norm")
    A = mm([(hn1, W["pw1"])], bias=W["b_pw1"], name="l1_pw1")
    s, cconv, *landed = conv_fwd(A, W["w_dw"], W["b_dw"], W["ln_g"], W["ln_b"], name="conv_fwd",
                                 fused=None if late is None else (late["src"][2], N_CHIPS, GATHER))
    if late is not None:
        W = {**W, **late["unpack"][2](landed[0])}
    h3 = mm([(s, W["pw2"])], bias=W["b_pw2"], res=h2, name="l1_pw2")
    y, ffn1, _ = ffn_fwd(h3, row1(W["ffn_norm"], 1), W["wg1"], W["wu1"], W["wd1"], 1)
    dy, loss_lanes = loss_grad(y, tgt, name="loss")
    G = {}
    dh3, g_fn1, G["wg1"], G["wu1"], G["wd1"], cs3 = ffn_bwd(
        dy, h3, row1(W["ffn_norm"], 1), W["wg1"], W["wu1"], W["wd1"], ffn1, 1)
    G["b_pw2"] = cs3
    ds = mm([(dh3, W["pw2"])], trans_b=True, name="l1_ds")
    G["pw2"] = mm_tn(s, dh3, name="l1_dpw2", tm=1024, tn=1024)
    dc, G["ln_g"], G["ln_b"] = ln_swish_bwd(cconv, ds, W["ln_g"], W["ln_b"], name="ln_bwd")
    dA, G["w_dw"], G["b_dw"], G["b_pw1"], *slots_a = conv_bwd(
        dc, A, W["w_dw"], name="conv_bwd", fused=None if late is None else (late["pack"][0](G), N_DEV, SCATTER))
    G["pw1"] = mm_tn(hn1, dA, name="l1_dpw1", tm=1024, tn=1024)
    dhn1 = mm([(dA, W["pw1"])], trans_b=True, name="l1_dhn")
    dh2, g_mn1, _ = rms_bwd(h2, row1(W["mix_norm"], 1), dhn1, dh3, name="l1_norm_bwd")
    dh1, g_fn0, G["wg0"], G["wu0"], G["wd0"], _ = ffn_bwd(
        dh2, h1, row1(W["ffn_norm"], 0), W["wg0"], W["wu0"], W["wd0"], ffn0, 0)
    dmix = mm([(dh1, W["wo_gs"])], trans_b=True, name="l0_dmix")
    G["wo_g"] = mm_tn(og, dh1, name="l0_dwo_g", tm=512, tn=1024)
    G["wo_s"] = mm_tn(osb, dh1, name="l0_dwo_s", tm=512, tn=1024)
    slots = None
    if late is None:
        dqn, dkn, dvh = sb_bwd(qn, kn, vh, sb_cb, dmix, GLA_DV, name="sb_bwd")
    else:
        dqn, dkn, dvh, slots_b = sb_bwd(qn, kn, vh, sb_cb, dmix, GLA_DV, name="sb_bwd",
                                        fused=(late["pack"][1](G), N_DEV, SCATTER))
        slots = [slots_a[0], slots_b]
    dps, G["g_q"], G["g_k"] = sb_post_bwd(dqn, dkn, dvh, ps, W["g_q"], W["g_k"], name="sb_post_bwd")
    dpg, dglr, G["w2"], G["b_gate"], G["g_gla"] = gla_bwd(
        pg, glr, W["w2"], W["b_gate"], W["g_gla"], st, dmix, name="gla_bwd")
    G["wi_g"] = mm_tn(hn0, dpg, name="l0_dwi_g", tm=1024, tn=512)
    G["wi_s"] = mm_tn(hn0, dps, name="l0_dwi_s", tm=1024, tn=512)
    G["wi_l"] = mm_tn(hn0, dglr, name="l0_dwi_l", tm=1024, tn=LANES)
    first_slots = None
    dhn_pairs = [(dpg, W["wi_g"]), (dps, W["wi_s"]), (dglr, W["wi_l"])]
    if late is None:
        dhn0 = mm(dhn_pairs, trans_b=True, name="l0_dhn")
    else:
        dhn0, first_slots = mm(dhn_pairs, trans_b=True, name="l0_dhn", fused=(late["pack_first"](G), N_DEV, SCATTER))
    dx, g_mn0, _ = rms_bwd(x, row1(W["mix_norm"], 0), dhn0, dh1, name="l0_norm_bwd")
    G["mix_norm"] = jnp.concatenate([g_mn0, g_mn1], axis=0)
    G["ffn_norm"] = jnp.concatenate([g_fn0, g_fn1], axis=0)
    return loss_lanes, dx, G, slots, first_slots


_C_GLA = 2 * GLA_DK + 2 * GLA_DV
_C_SB0 = _C_GLA + GLA_RANK


ITEMS = {
    "w_in": ("hy_w_in", 0, (D, _C_SB0 + 3 * SB_H * SB_D), 1), "wo_gs": ("hy_w_out", 0, (GLA_DV + SB_H * SB_D, D), 0),
    "w2r": ("hy_w_gate2", 0, (GLA_RANK, GLA_DK), 1), "b_pw1": ("cv_b_pw1", None, (1, 2 * D), 1),
    "w_dwr": ("cv_w_dw", 0, (CONV_W, D), 1), "b_dw": ("cv_b_dw", None, (1, D), 1),
    "ln_g": ("cv_ln_g", None, (1, D), 1), "ln_b": ("cv_ln_b", None, (1, D), 1), "b_pw2": ("cv_b_pw2", None, (1, D), 1),
    "pw1": ("cv_w_pw1", 0, (D, 2 * D), 1), "pw2": ("cv_w_pw2", 0, (D, D), 0),
    "wg0": ("ffn_w_gate", 0, (D, F), 1), "wg1": ("ffn_w_gate", 1, (D, F), 1),
    "wu0": ("ffn_w_up", 0, (D, F), 1), "wu1": ("ffn_w_up", 1, (D, F), 1),
    "wd0": ("ffn_w_down", 0, (F, D), 0), "wd1": ("ffn_w_down", 1, (F, D), 0),
}
FIRST_BIG_ITEMS = ["w_in", "wo_gs"]
SMALL_ITEMS = ["w2r", "b_pw1", "w_dwr", "b_dw", "ln_g", "ln_b", "b_pw2"]
LATE_ITEMS = ["pw1", "pw2", "wg0", "wg1", "wu0", "wu1", "wd0", "wd1"]
LATE_SCATTERS = (["wg1", "wu1", "wd1"], ["pw1", "pw2", "wg0", "wu0", "wd0"])


def layout_first(raw, repl):
    w_in = raw["w_in"]
    W = {
        "wi_g": w_in[:, :_C_GLA].astype(MXU),
        "wi_s": w_in[:, _C_SB0:].astype(MXU),
        "wi_l": jnp.pad(w_in[:, _C_GLA:_C_SB0], ((0, 0), (0, LANES - GLA_RANK))).astype(MXU),
        "w2": jnp.pad(raw["w2r"], ((0, LANES - GLA_RANK), (0, 0))),
        "b_gate": repl["hy_b_gate"], "g_gla": repl["hy_gla_norm"],
        "g_q": repl["hy_sb_q_norm"], "g_k": repl["hy_sb_k_norm"],
        "wo_gs": raw["wo_gs"].astype(MXU),
        "b_pw1": raw["b_pw1"], "w_dw": jnp.pad(raw["w_dwr"], ((0, HALO - CONV_W), (0, 0))),
        "b_dw": raw["b_dw"], "ln_g": raw["ln_g"], "ln_b": raw["ln_b"], "b_pw2": raw["b_pw2"],
        "mix_norm": repl["mix_norm"], "ffn_norm": repl["ffn_norm"],
    }
    W["wo_g"] = W["wo_gs"][:GLA_DV]
    W["wo_s"] = W["wo_gs"][GLA_DV:]
    return W


def layout_late(raw):
    return {k: raw[k].astype(MXU) for k in LATE_ITEMS}


def item_grads(G):
    out = {k: G[k] for k in ITEMS if k in G}
    if "wi_g" in G:
        out["w_in"] = jnp.concatenate([G["wi_g"], G["wi_l"][:, :GLA_RANK], G["wi_s"]], axis=1)
        out["wo_gs"] = jnp.concatenate([G["wo_g"], G["wo_s"]], axis=0)
        out["w2r"] = G["w2"][:GLA_RANK]
        out["w_dwr"] = G["w_dw"][:CONV_W]
    return out


def _chip_major(a, ax):
    r, c = a.shape
    if ax == 1:
        a = a.reshape(r, N_CHIPS, c // N_CHIPS).transpose(1, 0, 2)
    return a.reshape(N_CHIPS, -1, LANES)


def _from_chip_major(p, shape, ax):
    r, c = shape
    if ax == 0:
        return p.reshape(r, c)
    return p.reshape(N_CHIPS, r, c // N_CHIPS).transpose(1, 0, 2).reshape(r, c)


def _item_rows(k):
    n = math.prod(ITEMS[k][2]) // N_CHIPS
    assert n % LANES == 0, k
    return n // LANES


def unpack_items(buf, keys):
    out, off = {}, 0
    for k in keys:
        _, _, shape, ax = ITEMS[k]
        out[k] = _from_chip_major(buf[:, off:off + _item_rows(k)], shape, ax)
        off += _item_rows(k)
    return out


def pack_item_grads(raw, keys, dtype=F32):
    cat = jnp.concatenate([_chip_major(raw[k].astype(dtype), ITEMS[k][3]) for k in keys], axis=1)
    n = cat.shape[1]
    rows = -(-n // GRAD_ROWS) * GRAD_ROWS
    return jnp.pad(cat, ((0, 0), (0, rows - n), (0, 0))).reshape(2 * N_CHIPS, rows // 2, LANES)


def _refs_of(keys):
    return list(dict.fromkeys(ITEMS[k][0] for k in keys))


WEIGHTS = ["mix_norm", "ffn_norm", "hy_w_in", "hy_w_gate2", "hy_b_gate", "hy_gla_norm", "hy_sb_q_norm",
           "hy_sb_k_norm", "hy_w_out", "cv_w_pw1", "cv_b_pw1", "cv_w_dw", "cv_b_dw", "cv_ln_g", "cv_ln_b",
           "cv_w_pw2", "cv_b_pw2", "ffn_w_gate", "ffn_w_up", "ffn_w_down"]
REPL =["mix_norm", "ffn_norm", "hy_b_gate", "hy_gla_norm", "hy_sb_q_norm", "hy_sb_k_norm"]
N_CHIPS = 4
N_DEV = 8
GRAD_ROWS = 1024


def _pack(arrs, rows_multiple, dtype):
    flat = jnp.concatenate([a.reshape(-1).astype(dtype) for a in arrs])
    n = flat.shape[0]
    rows = -(-n // (LANES * rows_multiple)) * rows_multiple
    return jnp.pad(flat, (0, rows * LANES - n)).reshape(rows, LANES)


def _unpack(flat2d, shapes):
    flat = flat2d.reshape(-1)
    out, off = [], 0
    for s in shapes:
        n = math.prod(s)
        out.append(flat[off:off + n].reshape(s))
        off += n
    return out


def _coords():
    return lax.axis_index("x"), lax.axis_index("y"), lax.axis_index("c")


def _flip(pos, f):
    return tuple(1 - p if b else p for p, b in zip(pos, f))


def _exchange_copies(src_ref, dst_ref, send_sems, recv_sems, loc_sem, *, flips, src_idx, dst_idx, local_idx,
                     with_recvs=True):
    me = _coords()
    loc = None
    if local_idx is not None:
        si, di = local_idx(me)
        loc = pltpu.make_async_copy(src_ref.at[si], dst_ref.at[di], loc_sem)
    sends, recvs = [], []
    for k, f in enumerate(flips):
        peer = _flip(me, f)
        sends.append(pltpu.make_async_remote_copy(
            src_ref=src_ref.at[src_idx(me, peer)], dst_ref=dst_ref.at[dst_idx(me)],
            send_sem=send_sems.at[k], recv_sem=recv_sems.at[k],
            device_id=peer, device_id_type=pl.DeviceIdType.MESH))
        if with_recvs:
            recvs.append(pltpu.make_async_remote_copy(
                src_ref=src_ref.at[src_idx(peer, me)], dst_ref=dst_ref.at[dst_idx(peer)],
                send_sem=send_sems.at[k], recv_sem=recv_sems.at[k],
                device_id=peer, device_id_type=pl.DeviceIdType.MESH))
    return loc, sends, recvs


def _exchange_start(*refs, **spec):
    loc, sends, _ = _exchange_copies(*refs, with_recvs=False, **spec)
    if loc is not None:
        loc.start()
    for s in sends:
        s.start()


def _exchange_wait(*refs, **spec):
    loc, sends, recvs = _exchange_copies(*refs, **spec)
    for s in sends:
        s.wait_send()
    for r in recvs:
        r.wait_recv()
    if loc is not None:
        loc.wait()


def exchange(src, n_dst, spec, *, name, in_place=False):
    n = len(spec["flips"])
    assert not in_place or (n_dst == src.shape[0] and spec["local_idx"] is None)

    def body(*refs):
        _exchange_start(*refs, **spec)
        _exchange_wait(*refs, **spec)

    return pl.pallas_call(
        body, name=name,
        out_shape=jax.ShapeDtypeStruct((n_dst,) + src.shape[1:], src.dtype),
        in_specs=[pl.BlockSpec(memory_space=pl.ANY)],
        out_specs=pl.BlockSpec(memory_space=pl.ANY),
        scratch_shapes=[pltpu.SemaphoreType.DMA((n,)), pltpu.SemaphoreType.DMA((n,)), pltpu.SemaphoreType.DMA(())],
        input_output_aliases={0: 0} if in_place else {},
    )(src)


CHIP_FLIPS = [(1, 0, 0), (0, 1, 0), (1, 1, 0)]
SIBLING = [(0, 0, 1)]
ALL_FLIPS = [(a, b, c) for a in (0, 1) for b in (0, 1) for c in (0, 1) if (a, b, c) != (0, 0, 0)]


def _chip(pos):
    return 2 * pos[0] + pos[1]


def _dev(pos):
    return 4 * pos[0] + 2 * pos[1] + pos[2]


GATHER = dict(flips=CHIP_FLIPS, src_idx=lambda me, peer: 0, dst_idx=_chip, local_idx=lambda me: (0, _chip(me)))
SCATTER = dict(flips=ALL_FLIPS, src_idx=lambda me, peer: 2 * _chip(peer) + peer[2], dst_idx=_dev,
               local_idx=lambda me: (2 * _chip(me) + me[2], _dev(me)))
SHARE = dict(flips=SIBLING, src_idx=lambda me, peer: me[2], dst_idx=lambda me: me[2], local_idx=None)
ALL_TO_ALL = dict(flips=ALL_FLIPS, src_idx=lambda me, peer: 0, dst_idx=_dev, local_idx=lambda me: (0, _dev(me)))


def sum_slots(x, *, name, tr=512):
    n, R, L = x.shape
    tr = _rows(R, tr)

    def body(x_ref, o_ref):
        acc = x_ref[0]
        for k in range(1, n):
            acc = acc + x_ref[k]
        o_ref[...] = acc

    return _call(body, name=name, grid=(R // tr,),
                 ins=[(x, (n, tr, L), lambda i: (0, i, 0))],
                 outs=[((R, L), F32, (tr, L), lambda i: (i, 0))])[0]


def sum_slots_into_half(x, half, *, name, tr=512):
    n, R, L = x.shape
    tr = _rows(R, tr)

    def body(h_ref, x_ref, o_ref):
        acc = x_ref[0].astype(F32)
        for k in range(1, n):
            acc = acc + x_ref[k].astype(F32)
        o_ref[0] = acc

    return pl.pallas_call(
        body, name=name,
        grid_spec=pltpu.PrefetchScalarGridSpec(
            num_scalar_prefetch=1, grid=(R // tr,),
            in_specs=[pl.BlockSpec((n, tr, L), lambda i, h: (0, i, 0))],
            out_specs=pl.BlockSpec((1, tr, L), lambda i, h: (h[0], i, 0))),
        out_shape=jax.ShapeDtypeStruct((2, R, L), F32),
        compiler_params=pltpu.CompilerParams(dimension_semantics=("arbitrary",), vmem_limit_bytes=VMEM_LIMIT),
    )(jnp.reshape(half, (1,)).astype(jnp.int32), x)


ADAM_ROWS = 256


def adamw(w, g, m, v, *, name):
    shape = w.shape
    w2, g2, m2, v2 = (a.reshape(-1, shape[-1]) for a in (w, g, m, v))
    R, L = w2.shape
    tr = ADAM_ROWS if R % ADAM_ROWS == 0 else next(
        (t for t in range(min(R, 2 * ADAM_ROWS) // SUBLANES * SUBLANES, 0, -SUBLANES) if R % t == 0), R)

    def body(w_ref, g_ref, m_ref, v_ref, d_ref, mo_ref, vo_ref):
        g = g_ref[...]
        m = ADAM_B1 * m_ref[...] + (1.0 - ADAM_B1) * g
        v = ADAM_B2 * v_ref[...] + (1.0 - ADAM_B2) * (g * g)
        m_hat = m / (1.0 - ADAM_B1 ** ADAM_STEP)
        v_hat = v / (1.0 - ADAM_B2 ** ADAM_STEP)
        d_ref[...] = -ADAM_LR * (m_hat / (jnp.sqrt(v_hat) + ADAM_EPS) + ADAM_WD * w_ref[...])
        mo_ref[...] = m
        vo_ref[...] = v

    row = lambda i: (i, 0)
    outs = _call(body, name=name, grid=(R // tr,),
                 ins=[(a, (tr, L), row) for a in (w2, g2, m2, v2)],
                 outs=[((R, L), F32, (tr, L), row)] * 3)
    return [o.reshape(shape) for o in outs]


def kernel(x, mix_norm, ffn_norm, hy_w_in, hy_w_gate2, hy_b_gate, hy_gla_norm, hy_sb_q_norm, hy_sb_k_norm, hy_w_out, cv_w_pw1, cv_b_pw1, cv_w_dw, cv_b_dw, cv_ln_g, cv_ln_b, cv_w_pw2, cv_b_pw2, ffn_w_gate, ffn_w_up, ffn_w_down, loss_target, m_mix_norm, m_ffn_norm, m_hy_w_in, m_hy_w_gate2, m_hy_b_gate, m_hy_gla_norm, m_hy_sb_q_norm, m_hy_sb_k_norm, m_hy_w_out, m_cv_w_pw1, m_cv_b_pw1, m_cv_w_dw, m_cv_b_dw, m_cv_ln_g, m_cv_ln_b, m_cv_w_pw2, m_cv_b_pw2, m_ffn_w_gate, m_ffn_w_up, m_ffn_w_down, v_mix_norm, v_ffn_norm, v_hy_w_in, v_hy_w_gate2, v_hy_b_gate, v_hy_gla_norm, v_hy_sb_q_norm, v_hy_sb_k_norm, v_hy_w_out, v_cv_w_pw1, v_cv_b_pw1, v_cv_w_dw, v_cv_b_dw, v_cv_ln_g, v_cv_ln_b, v_cv_w_pw2, v_cv_b_pw2, v_ffn_w_gate, v_ffn_w_up, v_ffn_w_down):
    w = dict(zip(WEIGHTS, (mix_norm, ffn_norm, hy_w_in, hy_w_gate2, hy_b_gate, hy_gla_norm, hy_sb_q_norm, hy_sb_k_norm, hy_w_out, cv_w_pw1, cv_b_pw1, cv_w_dw, cv_b_dw, cv_ln_g, cv_ln_b, cv_w_pw2, cv_b_pw2, ffn_w_gate, ffn_w_up, ffn_w_down)))
    m = dict(zip(WEIGHTS, (m_mix_norm, m_ffn_norm, m_hy_w_in, m_hy_w_gate2, m_hy_b_gate, m_hy_gla_norm, m_hy_sb_q_norm, m_hy_sb_k_norm, m_hy_w_out, m_cv_w_pw1, m_cv_b_pw1, m_cv_w_dw, m_cv_b_dw, m_cv_ln_g, m_cv_ln_b, m_cv_w_pw2, m_cv_b_pw2, m_ffn_w_gate, m_ffn_w_up, m_ffn_w_down)))
    v = dict(zip(WEIGHTS, (v_mix_norm, v_ffn_norm, v_hy_w_in, v_hy_w_gate2, v_hy_b_gate, v_hy_gla_norm, v_hy_sb_q_norm, v_hy_sb_k_norm, v_hy_w_out, v_cv_w_pw1, v_cv_b_pw1, v_cv_w_dw, v_cv_b_dw, v_cv_ln_g, v_cv_ln_b, v_cv_w_pw2, v_cv_b_pw2, v_ffn_w_gate, v_ffn_w_up, v_ffn_w_down)))
    def local_shards(keys, rows_multiple, dtype):
        item = lambda ref, idx: w[ref] if idx is None else w[ref][idx]
        return _pack([item(*ITEMS[k][:2]) for k in keys], rows_multiple, dtype)[None]

    def finish_reduce(slots, tag):
        halves = sum_slots_into_half(slots, lax.axis_index("c"), name=f"reduce_{tag}_sum")
        return exchange(halves, 2, SHARE, name=f"reduce_{tag}_share", in_place=True).reshape(2 * slots.shape[1], LANES)

    small = exchange(local_shards(SMALL_ITEMS, 8, F32), N_CHIPS, GATHER, name="gather_small")
    raw_small = unpack_items(small, SMALL_ITEMS)
    repl = {n: w[n] for n in REPL}
    stages = (["wg0", "wu0", "wd0"], ["pw1", "pw2", "wd1"], ["wg1", "wu1"])
    late = {"first_src": local_shards(FIRST_BIG_ITEMS, 32, MXU),
            "unpack_first": lambda buf: layout_first({**unpack_items(buf, FIRST_BIG_ITEMS), **raw_small}, repl),
            "src": [local_shards(keys, 32, MXU) for keys in stages],
            "unpack": [functools.partial(unpack_items, keys=keys) for keys in stages],
            "pack": [lambda G, keys=keys: pack_item_grads(item_grads(G), keys, MXU) for keys in LATE_SCATTERS],
            "pack_first": lambda G: pack_item_grads(item_grads(G), FIRST_BIG_ITEMS, MXU)}

    loss_lanes, dx, Gk, late_slots, first_slots = local_step(x[0], loss_target[0], {"mix_norm": w["mix_norm"]}, late)
    rep_names = {"mix_norm": "mix_norm", "ffn_norm": "ffn_norm", "hy_b_gate": "b_gate", "hy_gla_norm": "g_gla",
                 "hy_sb_q_norm": "g_q", "hy_sb_k_norm": "g_k"}
    Gr = {n: Gk[k] for n, k in rep_names.items()}

    groups = [(FIRST_BIG_ITEMS, finish_reduce(first_slots, "first")),
              (LATE_SCATTERS[0], finish_reduce(late_slots[0], "late0")),
              (LATE_SCATTERS[1], finish_reduce(late_slots[1], "late1"))]

    loss_row = jnp.pad(jnp.sum(loss_lanes).reshape(1), (0, LANES - 1))
    small_g = item_grads(Gk)
    rep = _pack([Gr[n] for n in REPL] + [small_g[k] for k in SMALL_ITEMS] + [loss_row], 8, F32)
    rep_all = exchange(rep[None], N_DEV, ALL_TO_ALL, name="reduce_small")
    rep_sum = sum_slots(rep_all, name="reduce_small_sum", tr=rep.shape[0])
    rep_shapes = [w[n].shape for n in REPL] + [ITEMS[k][2] for k in SMALL_ITEMS]
    rep_parts = _unpack(rep_sum, rep_shapes)
    loss = rep_sum.reshape(-1)[sum(math.prod(s) for s in rep_shapes)]
    grads = dict(zip(REPL, rep_parts))
    chip = 2 * lax.axis_index("x") + lax.axis_index("y")
    for k, full in zip(SMALL_ITEMS, rep_parts[len(REPL):]):
        ref, _, (_, c), ax = ITEMS[k]
        assert ax == 1
        grads[ref] = lax.dynamic_slice_in_dim(full, chip * (c // N_CHIPS), c // N_CHIPS, axis=1).reshape(w[ref].shape)

    item_g = {}
    for keys, g_flat in groups:
        off = 0
        for k in keys:
            ref, idx = ITEMS[k][:2]
            item_g[k] = g_flat[off:off + _item_rows(k)].reshape(w[ref].shape if idx is None else w[ref].shape[1:])
            off += _item_rows(k)
    for ref in _refs_of(list(item_g)):
        layers = sorted((ITEMS[k][1], k) for k in item_g if ITEMS[k][0] == ref)
        grads[ref] = jnp.stack([item_g[k] for _, k in layers])
    out = {"grad": grads, "delta": {}, "new_m": {}, "new_v": {}}
    small_refs = _refs_of(SMALL_ITEMS) + REPL
    for n in WEIGHTS:
        if n not in small_refs:
            out["delta"][n], out["new_m"][n], out["new_v"][n] = adamw(w[n], grads[n], m[n], v[n], name=f"adamw_{n}")
    packed = [_pack([t[n] for n in small_refs], 8, F32) for t in (w, grads, m, v)]
    small_shapes = [w[n].shape for n in small_refs]
    for key, buf in zip(("delta", "new_m", "new_v"), adamw(*packed, name="adamw_small")):
        out[key].update(dict(zip(small_refs, _unpack(buf, small_shapes))))

    return (loss, dx[None], *[out["grad"][n] for n in WEIGHTS], *[out["delta"][n] for n in WEIGHTS],
            *[out["new_m"][n] for n in WEIGHTS], *[out["new_v"][n] for n in WEIGHTS])
```

```python
import functools
import math

import numpy as np
import jax
import jax.numpy as jnp
from jax import lax
from jax.experimental import pallas as pl
from jax.experimental.pallas import tpu as pltpu

F32 = jnp.float32
MXU = jnp.bfloat16
EPS = 1e-6
LANES = 128
VMEM_LIMIT = 56 * 1024 * 1024

D = 1024
F = 2816
CHUNK = 64
GLA_H, GLA_K, GLA_V, GLA_RANK = 4, 64, 128, 16
SB_H, SB_D = 8, 64
CONV_W = 31
HALO = 32

ADAM_LR, ADAM_B1, ADAM_B2, ADAM_EPS, ADAM_WD, ADAM_STEP = 0.001, 0.9, 0.999, 1e-08, 0.01, 10

NN = (((1,), (0,)), ((), ()))
NT = (((1,), (1,)), ((), ()))
TN = (((0,), (0,)), ((), ()))


def _dot(a, b, dims=NN):
    return lax.dot_general(a.astype(MXU), b.astype(MXU), dims, preferred_element_type=F32)


def _dot_split(a, b, dims=NN, a_split=True):
    x = a if a_split else b
    hi = x.astype(MXU)
    lo = (x - hi.astype(F32)).astype(MXU)
    if a_split:
        return _dot(hi, b, dims) + _dot(lo, b, dims)
    return _dot(a, hi, dims) + _dot(a, lo, dims)


def _sigmoid(x):
    return 1.0 / (1.0 + jnp.exp(-x))


def _softplus_neg_abs(z):
    return jnp.log(1.0 + jnp.exp(-jnp.abs(z)))


def _call(body, *, name, grid, ins, outs, scratch=()):
    spec = lambda b, m: pl.BlockSpec(memory_space=pl.ANY) if b is None else pl.BlockSpec(b, m)
    res = pl.pallas_call(
        body,
        name=name,
        grid=grid,
        in_specs=[spec(b, m) for _, b, m in ins],
        out_specs=[spec(b, m) for _, _, b, m in outs],
        out_shape=[jax.ShapeDtypeStruct(s, d) for s, d, _, _ in outs],
        scratch_shapes=list(scratch),
        compiler_params=pltpu.CompilerParams(
            dimension_semantics=("arbitrary",) * len(grid), vmem_limit_bytes=VMEM_LIMIT),
    )(*[a for a, _, _ in ins])
    return res


def _rows(T, tm):
    tm = min(tm, T)
    assert T % tm == 0, (T, tm)
    return tm


def mm(pairs, *, name, trans_b=False, bias=None, res=None, out_dtype=F32, tm=512, tn=512, fused=None):
    M = pairs[0][0].shape[0]
    N = pairs[0][1].shape[0] if trans_b else pairs[0][1].shape[1]
    tm = _rows(M, tm)
    tn = min(tn, N)
    assert N % tn == 0, (N, tn)
    np_ = len(pairs)
    ni, nj = M // tm, N // tn
    pred = lambda: ((pl.program_id(0) == 0) & (pl.program_id(1) == 0),
                    (pl.program_id(0) == ni - 1) & (pl.program_id(1) == nj - 1))
    x_ins, x_outs, x_scratch, x_start, x_end = _fused_exchange(fused, pred)
    n_in = 2 * np_ + (bias is not None) + (res is not None)

    def body(*refs):
        o_ref = refs[n_in + len(x_ins)]
        x_refs = refs[n_in:n_in + len(x_ins)] + refs[n_in + len(x_ins) + 1:]
        x_start(x_refs)
        acc = None
        for p in range(np_):
            d = _dot(refs[2 * p][...], refs[2 * p + 1][...], NT if trans_b else NN)
            acc = d if acc is None else acc + d
        k = 2 * np_
        if bias is not None:
            acc = acc + refs[k][...]
            k += 1
        if res is not None:
            acc = acc + refs[k][...]
        o_ref[...] = acc.astype(out_dtype)
        x_end(x_refs)

    ins = []
    for a, b in pairs:
        K = a.shape[1]
        ins.append((a, (tm, K), lambda i, j: (i, 0)))
        if trans_b:
            ins.append((b, (tn, K), lambda i, j: (j, 0)))
        else:
            ins.append((b, (K, tn), lambda i, j: (0, j)))
    if bias is not None:
        ins.append((bias, (1, tn), lambda i, j: (0, j)))
    if res is not None:
        ins.append((res, (tm, tn), lambda i, j: (i, j)))
    outs = _call(body, name=name, grid=(ni, nj), ins=ins + x_ins,
                 outs=[((M, N), out_dtype, (tm, tn), lambda i, j: (i, j))] + x_outs, scratch=x_scratch)
    return outs[0] if fused is None else outs


def mm_tn(a, b, *, name, tm=512, tn=512, tk=1024):
    T, M = a.shape
    N = b.shape[1]
    tm, tn, tk = min(tm, M), min(tn, N), min(tk, T)
    assert M % tm == 0 and N % tn == 0 and T % tk == 0, (M, N, T, tm, tn, tk)

    def body(a_ref, b_ref, o_ref):
        @pl.when(pl.program_id(2) == 0)
        def _():
            o_ref[...] = jnp.zeros_like(o_ref)
        o_ref[...] += _dot(a_ref[...], b_ref[...], TN)

    return _call(body, name=name, grid=(M // tm, N // tn, T // tk),
                 ins=[(a, (tk, tm), lambda i, j, k: (k, i)), (b, (tk, tn), lambda i, j, k: (k, j))],
                 outs=[((M, N), F32, (tm, tn), lambda i, j, k: (i, j))])[0]


def rms_fwd(x, g, *, name, tm=256, fused=None):
    T = x.shape[0]
    tm = _rows(T, tm)
    n = T // tm
    pred = lambda: (pl.program_id(0) == 0, pl.program_id(0) == n - 1)
    x_ins, x_outs, x_scratch, x_start, x_end = _fused_exchange(fused, pred)
    nx = len(x_ins)

    def body(*refs):
        x_ref, g_ref = refs[:2]
        o_ref = refs[2 + nx]
        x_refs = refs[2:2 + nx] + refs[3 + nx:]
        x_start(x_refs)
        x = x_ref[...]
        r = lax.rsqrt(jnp.mean(x * x, axis=-1, keepdims=True) + EPS)
        o_ref[...] = (x * r * g_ref[...]).astype(MXU)
        x_end(x_refs)

    outs = _call(body, name=name, grid=(n,),
                 ins=[(x, (tm, D), lambda i: (i, 0)), (g, (1, D), lambda i: (0, 0))] + x_ins,
                 outs=[((T, D), MXU, (tm, D), lambda i: (i, 0))] + x_outs, scratch=x_scratch)
    return outs[0] if fused is None else outs


def rms_bwd(x, g, dhn, dres, *, name, tm=256):
    T = x.shape[0]
    tm = _rows(T, tm)

    def body(x_ref, g_ref, d_ref, r_ref, dx_ref, dg_ref, cs_ref):
        @pl.when(pl.program_id(0) == 0)
        def _():
            dg_ref[...] = jnp.zeros_like(dg_ref)
            cs_ref[...] = jnp.zeros_like(cs_ref)
        x = x_ref[...]
        d = d_ref[...]
        r = lax.rsqrt(jnp.mean(x * x, axis=-1, keepdims=True) + EPS)
        dg_ref[...] += jnp.sum(d * x * r, axis=0, keepdims=True)
        t = d * g_ref[...]
        m = jnp.mean(t * x, axis=-1, keepdims=True)
        dx = r_ref[...] + t * r - x * (r * r * r) * m
        dx_ref[...] = dx
        cs_ref[...] += jnp.sum(dx, axis=0, keepdims=True)

    row = lambda i: (i, 0)
    fix = lambda i: (0, 0)
    return _call(body, name=name, grid=(T // tm,),
                 ins=[(x, (tm, D), row), (g, (1, D), fix), (dhn, (tm, D), row), (dres, (tm, D), row)],
                 outs=[((T, D), F32, (tm, D), row), ((1, D), F32, (1, D), fix), ((1, D), F32, (1, D), fix)])


def loss_grad(y, t, *, name, tm=256):
    T = y.shape[0]
    tm = _rows(T, tm)

    def body(y_ref, t_ref, dy_ref, l_ref):
        @pl.when(pl.program_id(0) == 0)
        def _():
            l_ref[...] = jnp.zeros_like(l_ref)
        e = y_ref[...] - t_ref[...]
        dy_ref[...] = e * (1.0 / D)
        l_ref[...] += jnp.sum(e * e, axis=0, keepdims=True) * (0.5 / D)

    row = lambda i: (i, 0)
    return _call(body, name=name, grid=(T // tm,),
                 ins=[(y, (tm, D), row), (t, (tm, D), row)],
                 outs=[((T, D), F32, (tm, D), row), ((1, D), F32, (1, D), lambda i: (0, 0))])


def ffn_up(hn, wg, wu, *, name, tm=1024, tn=256, fused=None):
    T = hn.shape[0]
    tm = _rows(T, tm)
    ni, nj = T // tm, F // tn
    pred = lambda: ((pl.program_id(0) == 0) & (pl.program_id(1) == 0),
                    (pl.program_id(0) == ni - 1) & (pl.program_id(1) == nj - 1))
    x_ins, x_outs, x_scratch, x_start, x_end = _fused_exchange(fused, pred)
    nx = len(x_ins)

    def body(*refs):
        h_ref, wg_ref, wu_ref = refs[:3]
        g_ref, u_ref, a_ref = refs[3 + nx:6 + nx]
        x_refs = refs[3:3 + nx] + refs[6 + nx:]
        x_start(x_refs)
        h = h_ref[...]
        g = _dot(h, wg_ref[...])
        u = _dot(h, wu_ref[...])
        g_ref[...] = g.astype(MXU)
        u_ref[...] = u.astype(MXU)
        a_ref[...] = (g * _sigmoid(g) * u).astype(MXU)
        x_end(x_refs)

    tile = lambda i, j: (i, j)
    return _call(body, name=name, grid=(ni, nj),
                 ins=[(hn, (tm, D), lambda i, j: (i, 0)), (wg, (D, tn), lambda i, j: (0, j)),
                      (wu, (D, tn), lambda i, j: (0, j))] + x_ins,
                 outs=[((T, F), MXU, (tm, tn), tile), ((T, F), MXU, (tm, tn), tile),
                       ((T, F), MXU, (tm, tn), tile)] + x_outs,
                 scratch=x_scratch)


def ffn_bwd_act(dy, wd, G, U, *, name, tm=1024, tn=256):
    T = dy.shape[0]
    tm = _rows(T, tm)

    def body(dy_ref, wd_ref, g_ref, u_ref, dg_ref, du_ref):
        da = _dot(dy_ref[...], wd_ref[...], NT)
        g = g_ref[...].astype(F32)
        u = u_ref[...].astype(F32)
        s = _sigmoid(g)
        sil = g * s
        du_ref[...] = (da * sil).astype(MXU)
        dg_ref[...] = (da * u * (s * (1.0 + g * (1.0 - s)))).astype(MXU)

    tile = lambda i, j: (i, j)
    return _call(body, name=name, grid=(T // tm, F // tn),
                 ins=[(dy, (tm, D), lambda i, j: (i, 0)), (wd, (tn, D), lambda i, j: (j, 0)),
                      (G, (tm, tn), tile), (U, (tm, tn), tile)],
                 outs=[((T, F), MXU, (tm, tn), tile), ((T, F), MXU, (tm, tn), tile)])


def ffn_fwd(h, g_norm, wg, wu, wd, tag, fused=None):
    hn = rms_fwd(h, g_norm, name=f"ffn{tag}_norm")
    G, U, act, *landed = ffn_up(hn, wg, wu, name=f"ffn{tag}_up", fused=fused)
    h_out = mm([(act, wd)], res=h, name=f"ffn{tag}_down")
    return h_out, (hn, G, U, act), landed


def ffn_bwd(dy, h, g_norm, wg, wu, wd, saved, tag):
    hn, G, U, act = saved
    dG, dU = ffn_bwd_act(dy, wd, G, U, name=f"ffn{tag}_bwd_act")
    d_wd = mm_tn(act, dy, name=f"ffn{tag}_dwd", tm=1408, tn=512)
    d_wg = mm_tn(hn, dG, name=f"ffn{tag}_dwg", tm=512, tn=1408)
    d_wu = mm_tn(hn, dU, name=f"ffn{tag}_dwu", tm=512, tn=1408)
    dhn = mm([(dG, wg), (dU, wu)], trans_b=True, name=f"ffn{tag}_dhn")
    dh, d_g, cs = rms_bwd(h, g_norm, dhn, dy, name=f"ffn{tag}_norm_bwd")
    return dh, d_g, d_wg, d_wu, d_wd, cs


SUBLANES = 8


def _window_scratch(tm):
    return [pltpu.VMEM((tm + HALO + SUBLANES, D), F32), pltpu.VMEM((SUBLANES, tm + HALO, D), F32)]


def _shift_copies(win, sh):
    rows = sh.shape[1]
    win[pl.ds(rows, SUBLANES), :] = jnp.zeros((SUBLANES, D), F32)
    for s in range(SUBLANES):
        sh[s] = win[pl.ds(s, rows), :]


def _tap(sh, off, rb):
    s = off % SUBLANES
    return sh[s, pl.ds(off - s, rb), :]


def conv_fwd(A, w_dw, b_dw, ln_g, ln_b, *, name, tm=256, rb=32, fused=None):
    T = A.shape[0]
    tm = _rows(T, tm)
    n = T // tm
    pred = lambda: (pl.program_id(0) == 0, pl.program_id(0) == n - 1)
    x_ins, x_outs, x_scratch, x_start, x_end = _fused_exchange(fused, pred)
    nx = len(x_ins)

    def body(*refs):
        a_ref, ap_ref, w_ref, b_ref, g_ref, bb_ref = refs[:6]
        s_ref, c_ref = refs[6 + nx:8 + nx]
        win, sh = refs[8 + 2 * nx:10 + 2 * nx]
        x_refs = refs[6:6 + nx] + refs[8 + nx:8 + 2 * nx] + refs[10 + 2 * nx:]
        x_start(x_refs)
        i = pl.program_id(0)
        a = a_ref[...]
        win[pl.ds(HALO, tm), :] = a[:, :D] * _sigmoid(a[:, D:])
        ap = ap_ref[pl.ds(tm - HALO, HALO), :]
        up = ap[:, :D] * _sigmoid(ap[:, D:])
        win[pl.ds(0, HALO), :] = jnp.where(i > 0, up, 0.0)
        _shift_copies(win, sh)
        for r0 in range(0, tm, rb):
            acc = jnp.broadcast_to(b_ref[...], (rb, D))
            for k in range(CONV_W):
                acc = acc + w_ref[pl.ds(k, 1), :] * _tap(sh, r0 + k + HALO - (CONV_W - 1), rb)
            c_ref[pl.ds(r0, rb), :] = acc
        c = c_ref[...]
        mu = jnp.mean(c, axis=-1, keepdims=True)
        cc = c - mu
        var = jnp.mean(cc * cc, axis=-1, keepdims=True)
        z = cc * lax.rsqrt(var + EPS) * g_ref[...] + bb_ref[...]
        s_ref[...] = (z * _sigmoid(z)).astype(MXU)
        x_end(x_refs)

    row = lambda i: (i, 0)
    fix = lambda i: (0, 0)
    return _call(body, name=name, grid=(n,),
                 ins=[(A, (tm, 2 * D), row), (A, (tm, 2 * D), lambda i: (jnp.maximum(i - 1, 0), 0)),
                      (w_dw, (HALO, D), fix), (b_dw, (1, D), fix), (ln_g, (1, D), fix), (ln_b, (1, D), fix)] + x_ins,
                 outs=[((T, D), MXU, (tm, D), row), ((T, D), F32, (tm, D), row)] + x_outs,
                 scratch=_window_scratch(tm) + x_scratch)


def ln_swish_bwd(c, ds, ln_g, ln_b, *, name, tm=256):
    T = c.shape[0]
    tm = _rows(T, tm)

    def body(c_ref, ds_ref, g_ref, b_ref, dc_ref, dg_ref, db_ref):
        @pl.when(pl.program_id(0) == 0)
        def _():
            dg_ref[...] = jnp.zeros_like(dg_ref)
            db_ref[...] = jnp.zeros_like(db_ref)
        c = c_ref[...]
        mu = jnp.mean(c, axis=-1, keepdims=True)
        cc = c - mu
        rstd = lax.rsqrt(jnp.mean(cc * cc, axis=-1, keepdims=True) + EPS)
        n = cc * rstd
        z = n * g_ref[...] + b_ref[...]
        s = _sigmoid(z)
        dz = ds_ref[...] * (s * (1.0 + z * (1.0 - s)))
        dg_ref[...] += jnp.sum(dz * n, axis=0, keepdims=True)
        db_ref[...] += jnp.sum(dz, axis=0, keepdims=True)
        dn = dz * g_ref[...]
        dc_ref[...] = rstd * (dn - jnp.mean(dn, axis=-1, keepdims=True)
                              - n * jnp.mean(dn * n, axis=-1, keepdims=True))

    row = lambda i: (i, 0)
    fix = lambda i: (0, 0)
    return _call(body, name=name, grid=(T // tm,),
                 ins=[(c, (tm, D), row), (ds, (tm, D), row), (ln_g, (1, D), fix), (ln_b, (1, D), fix)],
                 outs=[((T, D), F32, (tm, D), row), ((1, D), F32, (1, D), fix), ((1, D), F32, (1, D), fix)])


def conv_bwd(dc, A, w_dw, *, name, tm=256, rb=32, fused=None):
    T = A.shape[0]
    tm = _rows(T, tm)
    n = T // tm
    SUB = 8
    pred = lambda: (pl.program_id(0) == 0, pl.program_id(0) == n - 1)
    x_ins, x_outs, x_scratch, x_start, x_end = _fused_exchange(fused, pred)
    nx = len(x_ins)

    def body(*refs):
        dc_ref, dn_ref, a_ref, ap_ref, w_ref = refs[:5]
        da_ref, dw_ref, dbd_ref, dbp_ref = refs[5 + nx:9 + nx]
        wdc, sdc, wu, su, accw = refs[9 + 2 * nx:14 + 2 * nx]
        x_refs = refs[5:5 + nx] + refs[9 + nx:9 + 2 * nx] + refs[14 + 2 * nx:]
        x_start(x_refs)
        i = pl.program_id(0)

        @pl.when(i == 0)
        def _():
            accw[...] = jnp.zeros_like(accw)
            dbd_ref[...] = jnp.zeros_like(dbd_ref)
            dbp_ref[...] = jnp.zeros_like(dbp_ref)
        a = a_ref[...]
        a1, a2 = a[:, :D], a[:, D:]
        sg = _sigmoid(a2)
        wu[pl.ds(HALO, tm), :] = a1 * sg
        ap = ap_ref[pl.ds(tm - HALO, HALO), :]
        wu[pl.ds(0, HALO), :] = jnp.where(i > 0, ap[:, :D] * _sigmoid(ap[:, D:]), 0.0)
        dc = dc_ref[...]
        wdc[pl.ds(0, tm), :] = dc
        wdc[pl.ds(tm, HALO), :] = jnp.where(i < n - 1, dn_ref[pl.ds(0, HALO), :], 0.0)
        dbd_ref[...] += jnp.sum(dc, axis=0, keepdims=True)
        _shift_copies(wdc, sdc)
        _shift_copies(wu, su)
        for r0 in range(0, tm, rb):
            du = jnp.zeros((rb, D), F32)
            dcs = wdc[pl.ds(r0, rb), :]
            for k in range(CONV_W):
                du = du + w_ref[pl.ds(k, 1), :] * _tap(sdc, r0 + (CONV_W - 1) - k, rb)
                p = dcs * _tap(su, r0 + k + HALO - (CONV_W - 1), rb)
                accw[pl.ds(SUB * k, SUB), :] += jnp.sum(p.reshape(rb // SUB, SUB, D), axis=0)
            s = sg[r0:r0 + rb]
            da_ref[pl.ds(r0, rb), pl.ds(0, D)] = (du * s).astype(MXU)
            da_ref[pl.ds(r0, rb), pl.ds(D, D)] = (du * a1[r0:r0 + rb] * s * (1.0 - s)).astype(MXU)
        da = da_ref[...].astype(F32)
        dbp_ref[...] += jnp.sum(da, axis=0, keepdims=True)

        @pl.when(i == n - 1)
        def _():
            dw_ref[...] = jnp.zeros_like(dw_ref)
            for k in range(CONV_W):
                dw_ref[pl.ds(k, 1), :] = jnp.sum(accw[pl.ds(SUB * k, SUB), :], axis=0, keepdims=True)
        x_end(x_refs)

    row = lambda i: (i, 0)
    fix = lambda i: (0, 0)
    return _call(body, name=name, grid=(n,),
                 ins=[(dc, (tm, D), row), (dc, (tm, D), lambda i: (jnp.minimum(i + 1, n - 1), 0)),
                      (A, (tm, 2 * D), row), (A, (tm, 2 * D), lambda i: (jnp.maximum(i - 1, 0), 0)),
                      (w_dw, (HALO, D), fix)] + x_ins,
                 outs=[((T, 2 * D), MXU, (tm, 2 * D), row), ((HALO, D), F32, (HALO, D), fix),
                       ((1, D), F32, (1, D), fix), ((1, 2 * D), F32, (1, 2 * D), fix)] + x_outs,
                 scratch=_window_scratch(tm) + _window_scratch(tm) + [pltpu.VMEM((SUB * HALO, D), F32)] + x_scratch)


def sb_prep(ps, gq, gk, *, name, tm=256):
    T = ps.shape[0]
    tm = _rows(T, tm)
    W = SB_H * SB_D

    def body(q_ref, k_ref, v_ref, gq_ref, gk_ref, qo, ko, vo):
        qs, ks = [], []
        for h in range(SB_H):
            sl = slice(h * SB_D, (h + 1) * SB_D)
            q = q_ref[:, sl]
            k = k_ref[:, sl]
            rq = lax.rsqrt(jnp.mean(q * q, axis=-1, keepdims=True) + EPS)
            rk = lax.rsqrt(jnp.mean(k * k, axis=-1, keepdims=True) + EPS)
            qs.append(q * rq * gq_ref[...] * (SB_D ** -0.5))
            ks.append(k * rk * gk_ref[...])
        qo[...] = jnp.concatenate(qs, axis=-1).astype(MXU)
        ko[...] = jnp.concatenate(ks, axis=-1).astype(MXU)
        vo[...] = v_ref[...].astype(MXU)

    row = lambda i: (i, 0)
    fix = lambda i: (0, 0)
    return _call(body, name=name, grid=(T // tm,),
                 ins=[(ps, (tm, W), lambda i: (i, 0)), (ps, (tm, W), lambda i: (i, 1)),
                      (ps, (tm, W), lambda i: (i, 2)), (gq, (1, SB_D), fix), (gk, (1, SB_D), fix)],
                 outs=[((T, W), MXU, (tm, W), row)] * 3)


def _sb_masks(tq):
    row = lax.broadcasted_iota(jnp.int32, (tq, tq), 0)
    col = lax.broadcasted_iota(jnp.int32, (tq, tq), 1)
    tri = jnp.where(row >= col, 1.0, 0.0).astype(MXU)
    past = col < row
    return tri, past


def _sb_scores(q, k, past):
    return _sb_logs(_dot(q, k, NT), past)


def _sb_logs(z, past):
    ls = jnp.minimum(z, 0.0) - _softplus_neg_abs(z)
    lk = ls - z
    if past is not None:
        lk = jnp.where(past, lk, 0.0)
    return ls, lk


def _fused_exchange(fused, steps_pred):
    if fused is None:
        return [], [], [], lambda refs: None, lambda refs: None
    src, n_dst, spec = fused
    n = len(spec["flips"])
    ins = [(src, None, None)]
    outs = [((n_dst,) + src.shape[1:], src.dtype, None, None)]
    scratch = [pltpu.SemaphoreType.DMA((n,)), pltpu.SemaphoreType.DMA((n,)), pltpu.SemaphoreType.DMA(())]

    def at_start(refs):
        @pl.when(steps_pred()[0])
        def _():
            _exchange_start(*refs, **spec)

    def at_end(refs):
        @pl.when(steps_pred()[1])
        def _():
            _exchange_wait(*refs, **spec)

    return ins, outs, scratch, at_start, at_end


PAIR = 2 * SB_D
SB_DEAD = -120.0
SB_UNVISITED = -1e30


def _split_pair(x2, lo):
    zero = jnp.zeros_like(x2)
    return [jnp.where(lo, x2, zero), jnp.where(lo, zero, x2)]


def sb_fwd(qn, kn, v, *, name, tq=256, pairs=2, fused=None):
    T, W = qn.shape
    tq = _rows(T, tq)
    wb, hg = pairs * PAIR, 2 * pairs
    assert T // tq <= LANES and W % wb == 0
    ng, ni = W // wb, T // tq
    pred = lambda: ((pl.program_id(0) == 0) & (pl.program_id(1) == 0),
                    (pl.program_id(0) == ng - 1) & (pl.program_id(1) == ni - 1))
    x_ins, x_outs, x_scratch, x_start, x_end = _fused_exchange(fused, pred)
    nx = len(x_ins)
    heads = range(hg)
    pcols = [slice(p * PAIR, (p + 1) * PAIR) for p in range(pairs)]

    def body(*refs):
        q_ref, k_ref, v_ref = refs[:3]
        o_ref, cb_ref = refs[3 + nx:5 + nx]
        x_refs = refs[3:3 + nx] + refs[5 + nx:]
        x_start(x_refs)
        i = pl.program_id(1)
        tri, past = _sb_masks(tq)
        lane = lax.broadcasted_iota(jnp.int32, (tq, LANES), 1)
        lo = lane < SB_D
        cb_ref[...] = jnp.full((hg, tq, LANES), SB_UNVISITED, F32)
        qs = [m for p in range(pairs) for m in _split_pair(q_ref[:, pcols[p]], lo)]

        def block(j, carry, masked):
            accs, cs = carry
            kblk = pl.ds(pl.multiple_of(j * tq, tq), tq)
            for h in heads:
                cb_ref[h] = jnp.where(lane == j, cs[h], cb_ref[h])
            zs = [_dot(qs[h], k_ref[kblk, pcols[h // 2]], NT) for h in heads]
            sc = [_sb_logs(zs[h], past if masked else None) for h in heads]
            bincs = [_dot_split(sc[h][1], tri) for h in heads]
            ws = [jnp.exp(sc[h][0] + cs[h] + bincs[h] - sc[h][1]) for h in heads]
            if masked:
                ws = [jnp.where(past, w, 0.0) for w in ws]
            pv = [_dot(ws[h], v_ref[kblk, pcols[h // 2]]) for h in heads]
            new_a = [accs[p] + jnp.where(lo, pv[2 * p], pv[2 * p + 1]) for p in range(pairs)]
            new_c = [cs[h] + jnp.sum(sc[h][1], axis=-1, keepdims=True) for h in heads]
            return tuple(new_a), tuple(new_c)

        carry = ((jnp.zeros((tq, PAIR), F32),) * pairs, (jnp.zeros((tq, 1), F32),) * hg)
        carry = block(i, carry, True)

        def live(state):
            jj, _, cs = state
            return (jj < i) & (jnp.max(functools.reduce(jnp.maximum, cs)) > SB_DEAD)

        def step(state):
            jj, accs, cs = state
            accs, cs = block(i - 1 - jj, (accs, cs), False)
            return jj + 1, accs, cs

        _, accs, _ = lax.while_loop(live, step, (jnp.int32(0),) + carry)
        for p in range(pairs):
            o_ref[:, pcols[p]] = accs[p]
        x_end(x_refs)

    blk = lambda g, i: (i, g)
    full = lambda g, i: (0, g)
    return _call(body, name=name, grid=(ng, ni),
                 ins=[(qn, (tq, wb), blk), (kn, (T, wb), full), (v, (T, wb), full)] + x_ins,
                 outs=[((T, W), F32, (tq, wb), blk),
                       ((W // SB_D, T, LANES), F32, (hg, tq, LANES), lambda g, i: (g, i, 0))] + x_outs,
                 scratch=x_scratch)


def sb_bwd(qn, kn, v, cb, dmix, do_col, *, name, tq=256, pairs=2, fused=None):
    T, W = qn.shape
    tq = _rows(T, tq)
    wb, hg = pairs * PAIR, 2 * pairs
    assert W % wb == 0 and do_col % wb == 0
    ng, ni = W // wb, T // tq
    pred = lambda: ((pl.program_id(0) == 0) & (pl.program_id(1) == 0),
                    (pl.program_id(0) == ng - 1) & (pl.program_id(1) == ni - 1))
    x_ins, x_outs, x_scratch, x_start, x_end = _fused_exchange(fused, pred)
    nx = len(x_ins)
    heads = range(hg)
    pcols = [slice(p * PAIR, (p + 1) * PAIR) for p in range(pairs)]

    def body(*refs):
        q_ref, k_ref, v_ref, cb_ref, do_ref = refs[:5]
        dq_ref, dk_hbm, dv_hbm = refs[5 + nx:8 + nx]
        dk_ref, dv_ref = refs[8 + 2 * nx:10 + 2 * nx]
        x_refs = refs[5:5 + nx] + refs[8 + nx:8 + 2 * nx] + refs[10 + 2 * nx:]
        x_start(x_refs)
        i = pl.program_id(1)

        @pl.when(i == 0)
        def _():
            dk_ref[...] = jnp.zeros_like(dk_ref)
            dv_ref[...] = jnp.zeros_like(dv_ref)
        tri, past = _sb_masks(tq)
        row = lax.broadcasted_iota(jnp.int32, (tq, tq), 0)
        col = lax.broadcasted_iota(jnp.int32, (tq, tq), 1)
        tri_lt = jnp.where(row < col, 1.0, 0.0).astype(MXU)
        lane = lax.broadcasted_iota(jnp.int32, (tq, LANES), 1)
        lo = lane < SB_D
        qs = [m for p in range(pairs) for m in _split_pair(q_ref[:, pcols[p]], lo)]
        dos = [m for p in range(pairs) for m in _split_pair(do_ref[:, pcols[p]].astype(MXU), lo)]

        def block(j, carry, masked):
            dqs, cgs = carry
            kblk = pl.ds(pl.multiple_of(j * tq, tq), tq)
            k2 = [k_ref[kblk, pcols[p]] for p in range(pairs)]
            zs = [_dot(qs[h], k2[h // 2], NT) for h in heads]
            dws = [_dot(dos[h], v_ref[kblk, pcols[h // 2]], NT) for h in heads]
            sc = [_sb_logs(zs[h], past if masked else None) for h in heads]
            cs = [jnp.sum(jnp.where(lane == j, cb_ref[h], 0.0), axis=-1, keepdims=True) for h in heads]
            bincs = [_dot_split(sc[h][1], tri) for h in heads]
            ws = [jnp.exp(sc[h][0] + cs[h] + bincs[h] - sc[h][1]) for h in heads]
            if masked:
                ws = [jnp.where(past, w, 0.0) for w in ws]
            wbs = [w.astype(MXU) for w in ws]
            gs = [ws[h] * dws[h] for h in heads]
            gpres = [cgs[h] + _dot(gs[h], tri_lt) for h in heads]
            sigs = [jnp.exp(sc[h][0]) for h in heads]
            dzs = [gs[h] - sigs[h] * (gs[h] + gpres[h]) for h in heads]
            if masked:
                dzs = [jnp.where(past, dz, 0.0) for dz in dzs]
            dzbs = [dz.astype(MXU) for dz in dzs]
            dqp = [_dot(dzbs[h], k2[h // 2]) for h in heads]
            new_dq = [dqs[p] + jnp.where(lo, dqp[2 * p], dqp[2 * p + 1]) for p in range(pairs)]
            for p in range(pairs):
                dk_ref[kblk, pcols[p]] += _dot(dzbs[2 * p], qs[2 * p], TN) + _dot(dzbs[2 * p + 1], qs[2 * p + 1], TN)
            for p in range(pairs):
                dv_ref[kblk, pcols[p]] += _dot(wbs[2 * p], dos[2 * p], TN) + _dot(wbs[2 * p + 1], dos[2 * p + 1], TN)
            new_cg = [cgs[h] + jnp.sum(gs[h], axis=-1, keepdims=True) for h in heads]
            return tuple(new_dq), tuple(new_cg)

        colmax = functools.reduce(jnp.maximum, [jnp.max(cb_ref[h], axis=0, keepdims=True) for h in heads])
        lane_row = lax.broadcasted_iota(jnp.int32, (1, LANES), 1)
        is_live = (colmax > SB_DEAD) & (lane_row < i)
        first = jnp.min(jnp.where(is_live, lane_row, i).astype(F32)).astype(jnp.int32)
        carry = ((jnp.zeros((tq, PAIR), F32),) * pairs, (jnp.zeros((tq, 1), F32),) * hg)
        carry = lax.fori_loop(first, i, lambda j, car: block(j, car, False), carry)
        dqs, _ = block(i, carry, True)
        for p in range(pairs):
            dq_ref[:, pcols[p]] = dqs[p]

        cols = pl.ds(pl.multiple_of(pl.program_id(0) * wb, wb), wb)

        @pl.when(i == ni - 1)
        def _():
            pltpu.sync_copy(dk_ref, dk_hbm.at[:, cols])
            pltpu.sync_copy(dv_ref, dv_hbm.at[:, cols])
        x_end(x_refs)

    blk = lambda g, i: (i, g)
    full = lambda g, i: (0, g)
    return _call(body, name=name, grid=(ng, ni),
                 ins=[(qn, (tq, wb), blk), (kn, (T, wb), full), (v, (T, wb), full),
                      (cb, (hg, tq, LANES), lambda g, i: (g, i, 0)),
                      (dmix, (tq, wb), lambda g, i: (i, do_col // wb + g))] + x_ins,
                 outs=[((T, W), F32, (tq, wb), blk), ((T, W), F32, None, None), ((T, W), F32, None, None)] + x_outs,
                 scratch=[pltpu.VMEM((T, wb), F32), pltpu.VMEM((T, wb), F32)] + x_scratch)


def sb_post_bwd(dqn, dkn, dv, ps, gq, gk, *, name, tm=256):
    T = ps.shape[0]
    tm = _rows(T, tm)
    W = SB_H * SB_D

    def body(dq_ref, dk_ref, dv_ref, q_ref, k_ref, gq_ref, gk_ref, out_ref, dgq_ref, dgk_ref):
        @pl.when(pl.program_id(0) == 0)
        def _():
            dgq_ref[...] = jnp.zeros_like(dgq_ref)
            dgk_ref[...] = jnp.zeros_like(dgk_ref)

        def norm_bwd(x, d, g):
            r = lax.rsqrt(jnp.mean(x * x, axis=-1, keepdims=True) + EPS)
            dg = jnp.sum(d * x * r, axis=0, keepdims=True)
            t = d * g
            return t * r - x * (r * r * r) * jnp.mean(t * x, axis=-1, keepdims=True), dg

        dqs, dks = [], []
        dgq = jnp.zeros((1, SB_D), F32)
        dgk = jnp.zeros((1, SB_D), F32)
        for h in range(SB_H):
            sl = slice(h * SB_D, (h + 1) * SB_D)
            a, ga = norm_bwd(q_ref[:, sl], dq_ref[:, sl] * (SB_D ** -0.5), gq_ref[...])
            b, gb = norm_bwd(k_ref[:, sl], dk_ref[:, sl], gk_ref[...])
            dqs.append(a)
            dks.append(b)
            dgq = dgq + ga
            dgk = dgk + gb
        out_ref[...] = jnp.concatenate(dqs + dks + [dv_ref[...]], axis=-1).astype(MXU)
        dgq_ref[...] += dgq
        dgk_ref[...] += dgk

    row = lambda i: (i, 0)
    fix = lambda i: (0, 0)
    return _call(body, name=name, grid=(T // tm,),
                 ins=[(dqn, (tm, W), row), (dkn, (tm, W), row), (dv, (tm, W), row),
                      (ps, (tm, W), lambda i: (i, 0)), (ps, (tm, W), lambda i: (i, 1)),
                      (gq, (1, SB_D), fix), (gk, (1, SB_D), fix)],
                 outs=[((T, 3 * W), MXU, (tm, 3 * W), lambda i: (i, 0)),
                       ((1, SB_D), F32, (1, SB_D), fix), ((1, SB_D), F32, (1, SB_D), fix)])


GLA_TM = 512
GLA_DK = GLA_H * GLA_K
GLA_DV = GLA_H * GLA_V


def _gla_masks(tm):
    row = lax.broadcasted_iota(jnp.int32, (tm, tm), 0)
    col = lax.broadcasted_iota(jnp.int32, (tm, tm), 1)
    same = (row // CHUNK) == (col // CHUNK)
    return row, col, same


def _gla_gate(glr, w2, b, tm):
    pre = _dot(glr, w2) + b
    la = (jnp.minimum(pre, 0.0) - _softplus_neg_abs(pre)) * (1.0 / 16.0)
    row, col, same = _gla_masks(tm)
    m_incl = jnp.where(same & (col <= row), 1.0, 0.0).astype(MXU)
    m_full = jnp.where(same, 1.0, 0.0).astype(MXU)
    bc = _dot_split(m_incl, la, a_split=False)
    tot = _dot_split(m_full, la, a_split=False)
    return pre, bc, tot


def gla_fwd(pg, glr, w2, b_gate, g_out, *, name):
    T = pg.shape[0]
    tm = _rows(T, GLA_TM)
    ncb = tm // CHUNK
    NC = T // CHUNK

    def body(q_ref, k_ref, v_ref, r_ref, l_ref, w2_ref, b_ref, g_ref, o_ref, st_ref, S):
        @pl.when(pl.program_id(0) == 0)
        def _():
            S[...] = jnp.zeros_like(S)
        _, bc, tot = _gla_gate(l_ref[...], w2_ref[...], b_ref[...], tm)
        kend = k_ref[...] * jnp.exp(tot - bc)
        qs = q_ref[...] * (GLA_K ** -0.5)
        a_all = jnp.exp(tot)
        v = v_ref[...]
        r = r_ref[...]
        rows = [slice(c * CHUNK, (c + 1) * CHUNK) for c in range(ncb)]
        hks = [slice(h * GLA_K, (h + 1) * GLA_K) for h in range(GLA_H)]
        hvs = [slice(h * GLA_V, (h + 1) * GLA_V) for h in range(GLA_H)]
        uts = [[_dot(v[rows[c], hvs[h]], kend[rows[c], hks[h]], TN) for h in range(GLA_H)] for c in range(ncb)]
        for h in range(GLA_H):
            s = S[h]
            for c in range(ncb):
                s = s * a_all[c * CHUNK:c * CHUNK + 1, hks[h]] + uts[c][h]
                st_ref[c, h] = s
            S[h] = s
        for c in range(ncb):
            outs = []
            for h in range(GLA_H):
                o = _dot(qs[rows[c], hks[h]], st_ref[c, h], NT)
                rinv = lax.rsqrt(jnp.mean(o * o, axis=-1, keepdims=True) + EPS)
                rr = r[rows[c], hvs[h]]
                outs.append(o * rinv * g_ref[...] * (rr * _sigmoid(rr)))
            o_ref[pl.ds(c * CHUNK, CHUNK), :] = jnp.concatenate(outs, axis=-1).astype(MXU)

    fix = lambda i: (0, 0)
    return _call(body, name=name, grid=(T // tm,),
                 ins=[(pg, (tm, GLA_DK), lambda i: (i, 0)), (pg, (tm, GLA_DK), lambda i: (i, 1)),
                      (pg, (tm, GLA_DV), lambda i: (i, 1)), (pg, (tm, GLA_DV), lambda i: (i, 2)),
                      (glr, (tm, LANES), lambda i: (i, 0)), (w2, (LANES, GLA_DK), fix),
                      (b_gate, (1, GLA_DK), fix), (g_out, (1, GLA_V), fix)],
                 outs=[((T, GLA_DV), MXU, (tm, GLA_DV), lambda i: (i, 0)),
                       ((NC, GLA_H, GLA_V, GLA_K), F32, (ncb, GLA_H, GLA_V, GLA_K), lambda i: (i, 0, 0, 0))],
                 scratch=[pltpu.VMEM((GLA_H, GLA_V, GLA_K), F32)])


def gla_bwd(pg, glr, w2, b_gate, g_out, st, dmix, *, name):
    T = pg.shape[0]
    tm = _rows(T, GLA_TM)
    ncb = tm // CHUNK
    n = T // tm

    def body(q_ref, k_ref, v_ref, r_ref, l_ref, w2_ref, b_ref, g_ref, st_ref, sp_ref, d_ref,
             dpg_ref, dl_ref, dw2_ref, db_ref, dg_ref, dS, dkend, extra, dst_ref):
        i = pl.program_id(0)

        @pl.when(i == 0)
        def _():
            dS[...] = jnp.zeros_like(dS)
            dw2_ref[...] = jnp.zeros_like(dw2_ref)
            db_ref[...] = jnp.zeros_like(db_ref)
            dg_ref[...] = jnp.zeros_like(dg_ref)
        first_tile = i == n - 1
        glr_v = l_ref[...]
        pre, bc, tot = _gla_gate(glr_v, w2_ref[...], b_ref[...], tm)
        dec = jnp.exp(tot - bc)
        k = k_ref[...]
        kend = k * dec
        qs = q_ref[...] * (GLA_K ** -0.5)
        a_all = jnp.exp(tot)
        v = v_ref[...]
        r = r_ref[...]
        g = g_ref[...]
        dgg = jnp.zeros((1, GLA_V), F32)
        rows = [slice(c * CHUNK, (c + 1) * CHUNK) for c in range(ncb)]
        hks = [slice(h * GLA_K, (h + 1) * GLA_K) for h in range(GLA_H)]
        hvs = [slice(h * GLA_V, (h + 1) * GLA_V) for h in range(GLA_H)]
        heads = range(GLA_H)
        units = [(c, h) for c in range(ncb) for h in heads]
        qh = {u: qs[rows[u[0]], hks[u[1]]] for u in units}
        o = {u: _dot(qh[u], st_ref[u[0], u[1]], NT) for u in units}
        do, dr = {}, {}
        for c, h in units:
            ou = o[c, h]
            rinv = lax.rsqrt(jnp.mean(ou * ou, axis=-1, keepdims=True) + EPS)
            nrm = ou * rinv
            rr = r[rows[c], hvs[h]]
            sg = _sigmoid(rr)
            sil = rr * sg
            d = d_ref[pl.ds(c * CHUNK, CHUNK), pl.ds(h * GLA_V, GLA_V)]
            dr[c, h] = d * nrm * g * (sg * (1.0 + rr * (1.0 - sg)))
            dgg = dgg + jnp.sum(d * sil * nrm, axis=0, keepdims=True)
            dn = d * sil * g
            do[c, h] = (rinv * (dn - nrm * jnp.mean(dn * nrm, axis=-1, keepdims=True))).astype(MXU)
        dq = {u: _dot(do[u], st_ref[u[0], u[1]]) * (GLA_K ** -0.5) for u in units}
        qdo = [[_dot(do[c, h], qh[c, h], TN) for h in heads] for c in range(ncb)]
        for c in range(ncb):
            cr = pl.ds(c * CHUNK, CHUNK)
            dpg_ref[cr, pl.ds(0, GLA_DK)] = jnp.concatenate([dq[c, h] for h in heads], axis=-1).astype(MXU)
            dpg_ref[cr, pl.ds(2 * GLA_DK + GLA_DV, GLA_DV)] = jnp.concatenate([dr[c, h] for h in heads], axis=-1).astype(MXU)
        ex = [[None] * GLA_H for _ in range(ncb)]
        for h in heads:
            ds_h = dS[h]
            for c in reversed(range(ncb)):
                dst = ds_h + qdo[c][h]
                dst_ref[c, h] = dst
                s_p = st_ref[c - 1, h] if c > 0 else jnp.where(first_tile, 0.0, sp_ref[0, h])
                a = a_all[c * CHUNK:c * CHUNK + 1, hks[h]]
                da = jnp.sum(dst * s_p, axis=0, keepdims=True)
                ds_h = dst * a
                ex[c][h] = jnp.broadcast_to(da * a, (CHUNK, GLA_K))
            dS[h] = ds_h
        for c in range(ncb):
            extra[pl.ds(c * CHUNK, CHUNK), :] = jnp.concatenate(ex[c], axis=-1)
        for c in range(ncb):
            cr = pl.ds(c * CHUNK, CHUNK)
            dk_l = [_dot(v[rows[c], hvs[h]], dst_ref[c, h]) for h in heads]
            dv_l = [_dot(kend[rows[c], hks[h]], dst_ref[c, h], NT) for h in heads]
            dpg_ref[cr, pl.ds(2 * GLA_DK, GLA_DV)] = jnp.concatenate(dv_l, axis=-1).astype(MXU)
            dkend[cr, :] = jnp.concatenate(dk_l, axis=-1)
        dke = dkend[...]
        dpg_ref[:, pl.ds(GLA_DK, GLA_DK)] = (dke * dec).astype(MXU)
        e = dke * kend
        row, col, same = _gla_masks(tm)
        m_lt = jnp.where(same & (col < row), 1.0, 0.0).astype(MXU)
        dla = _dot_split(m_lt, e, a_split=False) + extra[...]
        sp = _softplus_neg_abs(pre)
        one_m_sig = jnp.exp(-jnp.maximum(pre, 0.0) - sp)
        dpre = dla * (1.0 / 16.0) * one_m_sig
        dl_ref[...] = _dot(dpre, w2_ref[...], NT).astype(MXU)
        dw2_ref[...] += _dot(glr_v, dpre, TN)
        db_ref[...] += jnp.sum(dpre, axis=0, keepdims=True)
        dg_ref[...] += dgg

    fix = lambda i: (0, 0)
    rev = lambda i: n - 1 - i
    return _call(body, name=name, grid=(n,),
                 ins=[(pg, (tm, GLA_DK), lambda i: (rev(i), 0)), (pg, (tm, GLA_DK), lambda i: (rev(i), 1)),
                      (pg, (tm, GLA_DV), lambda i: (rev(i), 1)), (pg, (tm, GLA_DV), lambda i: (rev(i), 2)),
                      (glr, (tm, LANES), lambda i: (rev(i), 0)), (w2, (LANES, GLA_DK), fix),
                      (b_gate, (1, GLA_DK), fix), (g_out, (1, GLA_V), fix),
                      (st, (ncb, GLA_H, GLA_V, GLA_K), lambda i: (rev(i), 0, 0, 0)),
                      (st, (1, GLA_H, GLA_V, GLA_K), lambda i: (jnp.maximum(rev(i) * ncb - 1, 0), 0, 0, 0)),
                      (dmix, (tm, GLA_DV), lambda i: (rev(i), 0))],
                 outs=[((T, 2 * GLA_DK + 2 * GLA_DV), MXU, (tm, 2 * GLA_DK + 2 * GLA_DV), lambda i: (rev(i), 0)),
                       ((T, LANES), MXU, (tm, LANES), lambda i: (rev(i), 0)),
                       ((LANES, GLA_DK), F32, (LANES, GLA_DK), fix),
                       ((1, GLA_DK), F32, (1, GLA_DK), fix), ((1, GLA_V), F32, (1, GLA_V), fix)],
                 scratch=[pltpu.VMEM((GLA_H, GLA_V, GLA_K), F32), pltpu.VMEM((tm, GLA_DK), F32),
                          pltpu.VMEM((tm, GLA_DK), F32), pltpu.VMEM((ncb, GLA_H, GLA_V, GLA_K), F32)])


def local_step(x, tgt, W, late=None):
    row1 = lambda a, l: a[l:l + 1]
    if late is None:
        hn0 = rms_fwd(x, row1(W["mix_norm"], 0), name="l0_norm")
    else:
        hn0, landed = rms_fwd(x, row1(W["mix_norm"], 0), name="l0_norm", fused=(late["first_src"], N_CHIPS, GATHER))
        W = {**W, **late["unpack_first"](landed)}
    pg = mm([(hn0, W["wi_g"])], name="l0_proj_gla", tn=512)
    ps = mm([(hn0, W["wi_s"])], name="l0_proj_sb", tn=512)
    glr = mm([(hn0, W["wi_l"])], name="l0_proj_gate", out_dtype=MXU)
    og, st = gla_fwd(pg, glr, W["w2"], W["b_gate"], W["g_gla"], name="gla_fwd")
    qn, kn, vh = sb_prep(ps, W["g_q"], W["g_k"], name="sb_prep")
    if late is None:
        osb, sb_cb = sb_fwd(qn, kn, vh, name="sb_fwd")
    else:
        osb, sb_cb, gathered = sb_fwd(qn, kn, vh, name="sb_fwd", fused=(late["src"][0], N_CHIPS, GATHER))
        W = {**W, **late["unpack"][0](gathered)}
    h1 = mm([(og, W["wo_g"]), (osb, W["wo_s"])], res=x, name="l0_out")
    h2, ffn0, landed = ffn_fwd(h1, row1(W["ffn_norm"], 0), W["wg0"], W["wu0"], W["wd0"], 0,
                               fused=None if late is None else (late["src"][1], N_CHIPS, GATHER))
    if late is not None:
        W = {**W, **late["unpack"][1](landed[0])}
    hn1 = rms_fwd(h2, row1(W["mix_norm"], 1), name="l1_norm")
    A = mm([(hn1, W["pw1"])], bias=W["b_pw1"], name="l1_pw1")
    s, cconv, *landed = conv_fwd(A, W["w_dw"], W["b_dw"], W["ln_g"], W["ln_b"], name="conv_fwd",
                                 fused=None if late is None else (late["src"][2], N_CHIPS, GATHER))
    if late is not None:
        W = {**W, **late["unpack"][2](landed[0])}
    h3 = mm([(s, W["pw2"])], bias=W["b_pw2"], res=h2, name="l1_pw2")
    y, ffn1, _ = ffn_fwd(h3, row1(W["ffn_norm"], 1), W["wg1"], W["wu1"], W["wd1"], 1)
    dy, loss_lanes = loss_grad(y, tgt, name="loss")
    G = {}
    dh3, g_fn1, G["wg1"], G["wu1"], G["wd1"], cs3 = ffn_bwd(
        dy, h3, row1(W["ffn_norm"], 1), W["wg1"], W["wu1"], W["wd1"], ffn1, 1)
    G["b_pw2"] = cs3
    ds = mm([(dh3, W["pw2"])], trans_b=True, name="l1_ds")
    G["pw2"] = mm_tn(s, dh3, name="l1_dpw2", tm=1024, tn=1024)
    dc, G["ln_g"], G["ln_b"] = ln_swish_bwd(cconv, ds, W["ln_g"], W["ln_b"], name="ln_bwd")
    dA, G["w_dw"], G["b_dw"], G["b_pw1"], *slots_a = conv_bwd(
        dc, A, W["w_dw"], name="conv_bwd", fused=None if late is None else (late["pack"][0](G), N_DEV, SCATTER))
    G["pw1"] = mm_tn(hn1, dA, name="l1_dpw1", tm=1024, tn=1024)
    dhn1 = mm([(dA, W["pw1"])], trans_b=True, name="l1_dhn")
    dh2, g_mn1, _ = rms_bwd(h2, row1(W["mix_norm"], 1), dhn1, dh3, name="l1_norm_bwd")
    dh1, g_fn0, G["wg0"], G["wu0"], G["wd0"], _ = ffn_bwd(
        dh2, h1, row1(W["ffn_norm"], 0), W["wg0"], W["wu0"], W["wd0"], ffn0, 0)
    dmix = mm([(dh1, W["wo_gs"])], trans_b=True, name="l0_dmix")
    G["wo_g"] = mm_tn(og, dh1, name="l0_dwo_g", tm=512, tn=1024)
    G["wo_s"] = mm_tn(osb, dh1, name="l0_dwo_s", tm=512, tn=1024)
    slots = None
    if late is None:
        dqn, dkn, dvh = sb_bwd(qn, kn, vh, sb_cb, dmix, GLA_DV, name="sb_bwd")
    else:
        dqn, dkn, dvh, slots_b = sb_bwd(qn, kn, vh, sb_cb, dmix, GLA_DV, name="sb_bwd",
                                        fused=(late["pack"][1](G), N_DEV, SCATTER))
        slots = [slots_a[0], slots_b]
    dps, G["g_q"], G["g_k"] = sb_post_bwd(dqn, dkn, dvh, ps, W["g_q"], W["g_k"], name="sb_post_bwd")
    dpg, dglr, G["w2"], G["b_gate"], G["g_gla"] = gla_bwd(
        pg, glr, W["w2"], W["b_gate"], W["g_gla"], st, dmix, name="gla_bwd")
    G["wi_g"] = mm_tn(hn0, dpg, name="l0_dwi_g", tm=1024, tn=512)
    G["wi_s"] = mm_tn(hn0, dps, name="l0_dwi_s", tm=1024, tn=512)
    G["wi_l"] = mm_tn(hn0, dglr, name="l0_dwi_l", tm=1024, tn=LANES)
    first_slots = None
    dhn_pairs = [(dpg, W["wi_g"]), (dps, W["wi_s"]), (dglr, W["wi_l"])]
    if late is None:
        dhn0 = mm(dhn_pairs, trans_b=True, name="l0_dhn")
    else:
        dhn0, first_slots = mm(dhn_pairs, trans_b=True, name="l0_dhn", fused=(late["pack_first"](G), N_DEV, SCATTER))
    dx, g_mn0, _ = rms_bwd(x, row1(W["mix_norm"], 0), dhn0, dh1, name="l0_norm_bwd")
    G["mix_norm"] = jnp.concatenate([g_mn0, g_mn1], axis=0)
    G["ffn_norm"] = jnp.concatenate([g_fn0, g_fn1], axis=0)
    return loss_lanes, dx, G, slots, first_slots


_C_GLA = 2 * GLA_DK + 2 * GLA_DV
_C_SB0 = _C_GLA + GLA_RANK


ITEMS = {
    "w_in": ("hy_w_in", 0, (D, _C_SB0 + 3 * SB_H * SB_D), 1), "wo_gs": ("hy_w_out", 0, (GLA_DV + SB_H * SB_D, D), 0),
    "w2r": ("hy_w_gate2", 0, (GLA_RANK, GLA_DK), 1), "b_pw1": ("cv_b_pw1", None, (1, 2 * D), 1),
    "w_dwr": ("cv_w_dw", 0, (CONV_W, D), 1), "b_dw": ("cv_b_dw", None, (1, D), 1),
    "ln_g": ("cv_ln_g", None, (1, D), 1), "ln_b": ("cv_ln_b", None, (1, D), 1), "b_pw2": ("cv_b_pw2", None, (1, D), 1),
    "pw1": ("cv_w_pw1", 0, (D, 2 * D), 1), "pw2": ("cv_w_pw2", 0, (D, D), 0),
    "wg0": ("ffn_w_gate", 0, (D, F), 1), "wg1": ("ffn_w_gate", 1, (D, F), 1),
    "wu0": ("ffn_w_up", 0, (D, F), 1), "wu1": ("ffn_w_up", 1, (D, F), 1),
    "wd0": ("ffn_w_down", 0, (F, D), 0), "wd1": ("ffn_w_down", 1, (F, D), 0),
}
FIRST_BIG_ITEMS = ["w_in", "wo_gs"]
SMALL_ITEMS = ["w2r", "b_pw1", "w_dwr", "b_dw", "ln_g", "ln_b", "b_pw2"]
LATE_ITEMS = ["pw1", "pw2", "wg0", "wg1", "wu0", "wu1", "wd0", "wd1"]
LATE_SCATTERS = (["wg1", "wu1", "wd1"], ["pw1", "pw2", "wg0", "wu0", "wd0"])


def layout_first(raw, repl):
    w_in = raw["w_in"]
    W = {
        "wi_g": w_in[:, :_C_GLA].astype(MXU),
        "wi_s": w_in[:, _C_SB0:].astype(MXU),
        "wi_l": jnp.pad(w_in[:, _C_GLA:_C_SB0], ((0, 0), (0, LANES - GLA_RANK))).astype(MXU),
        "w2": jnp.pad(raw["w2r"], ((0, LANES - GLA_RANK), (0, 0))),
        "b_gate": repl["hy_b_gate"], "g_gla": repl["hy_gla_norm"],
        "g_q": repl["hy_sb_q_norm"], "g_k": repl["hy_sb_k_norm"],
        "wo_gs": raw["wo_gs"].astype(MXU),
        "b_pw1": raw["b_pw1"], "w_dw": jnp.pad(raw["w_dwr"], ((0, HALO - CONV_W), (0, 0))),
        "b_dw": raw["b_dw"], "ln_g": raw["ln_g"], "ln_b": raw["ln_b"], "b_pw2": raw["b_pw2"],
        "mix_norm": repl["mix_norm"], "ffn_norm": repl["ffn_norm"],
    }
    W["wo_g"] = W["wo_gs"][:GLA_DV]
    W["wo_s"] = W["wo_gs"][GLA_DV:]
    return W


def layout_late(raw):
    return {k: raw[k].astype(MXU) for k in LATE_ITEMS}


def item_grads(G):
    out = {k: G[k] for k in ITEMS if k in G}
    if "wi_g" in G:
        out["w_in"] = jnp.concatenate([G["wi_g"], G["wi_l"][:, :GLA_RANK], G["wi_s"]], axis=1)
        out["wo_gs"] = jnp.concatenate([G["wo_g"], G["wo_s"]], axis=0)
        out["w2r"] = G["w2"][:GLA_RANK]
        out["w_dwr"] = G["w_dw"][:CONV_W]
    return out


def _chip_major(a, ax):
    r, c = a.shape
    if ax == 1:
        a = a.reshape(r, N_CHIPS, c // N_CHIPS).transpose(1, 0, 2)
    return a.reshape(N_CHIPS, -1, LANES)


def _from_chip_major(p, shape, ax):
    r, c = shape
    if ax == 0:
        return p.reshape(r, c)
    return p.reshape(N_CHIPS, r, c // N_CHIPS).transpose(1, 0, 2).reshape(r, c)


def _item_rows(k):
    n = math.prod(ITEMS[k][2]) // N_CHIPS
    assert n % LANES == 0, k
    return n // LANES


def unpack_items(buf, keys):
    out, off = {}, 0
    for k in keys:
        _, _, shape, ax = ITEMS[k]
        out[k] = _from_chip_major(buf[:, off:off + _item_rows(k)], shape, ax)
        off += _item_rows(k)
    return out


def pack_item_grads(raw, keys, dtype=F32):
    cat = jnp.concatenate([_chip_major(raw[k].astype(dtype), ITEMS[k][3]) for k in keys], axis=1)
    n = cat.shape[1]
    rows = -(-n // GRAD_ROWS) * GRAD_ROWS
    return jnp.pad(cat, ((0, 0), (0, rows - n), (0, 0))).reshape(2 * N_CHIPS, rows // 2, LANES)


def _refs_of(keys):
    return list(dict.fromkeys(ITEMS[k][0] for k in keys))


WEIGHTS = ["mix_norm", "ffn_norm", "hy_w_in", "hy_w_gate2", "hy_b_gate", "hy_gla_norm", "hy_sb_q_norm",
           "hy_sb_k_norm", "hy_w_out", "cv_w_pw1", "cv_b_pw1", "cv_w_dw", "cv_b_dw", "cv_ln_g", "cv_ln_b",
           "cv_w_pw2", "cv_b_pw2", "ffn_w_gate", "ffn_w_up", "ffn_w_down"]
REPL =["mix_norm", "ffn_norm", "hy_b_gate", "hy_gla_norm", "hy_sb_q_norm", "hy_sb_k_norm"]
N_CHIPS = 4
N_DEV = 8
GRAD_ROWS = 1024


def _pack(arrs, rows_multiple, dtype):
    flat = jnp.concatenate([a.reshape(-1).astype(dtype) for a in arrs])
    n = flat.shape[0]
    rows = -(-n // (LANES * rows_multiple)) * rows_multiple
    return jnp.pad(flat, (0, rows * LANES - n)).reshape(rows, LANES)


def _unpack(flat2d, shapes):
    flat = flat2d.reshape(-1)
    out, off = [], 0
    for s in shapes:
        n = math.prod(s)
        out.append(flat[off:off + n].reshape(s))
        off += n
    return out


def _coords():
    return lax.axis_index("x"), lax.axis_index("y"), lax.axis_index("c")


def _flip(pos, f):
    return tuple(1 - p if b else p for p, b in zip(pos, f))


def _exchange_copies(src_ref, dst_ref, send_sems, recv_sems, loc_sem, *, flips, src_idx, dst_idx, local_idx,
                     with_recvs=True):
    me = _coords()
    loc = None
    if local_idx is not None:
        si, di = local_idx(me)
        loc = pltpu.make_async_copy(src_ref.at[si], dst_ref.at[di], loc_sem)
    sends, recvs = [], []
    for k, f in enumerate(flips):
        peer = _flip(me, f)
        sends.append(pltpu.make_async_remote_copy(
            src_ref=src_ref.at[src_idx(me, peer)], dst_ref=dst_ref.at[dst_idx(me)],
            send_sem=send_sems.at[k], recv_sem=recv_sems.at[k],
            device_id=peer, device_id_type=pl.DeviceIdType.MESH))
        if with_recvs:
            recvs.append(pltpu.make_async_remote_copy(
                src_ref=src_ref.at[src_idx(peer, me)], dst_ref=dst_ref.at[dst_idx(peer)],
                send_sem=send_sems.at[k], recv_sem=recv_sems.at[k],
                device_id=peer, device_id_type=pl.DeviceIdType.MESH))
    return loc, sends, recvs


def _exchange_start(*refs, **spec):
    loc, sends, _ = _exchange_copies(*refs, with_recvs=False, **spec)
    if loc is not None:
        loc.start()
    for s in sends:
        s.start()


def _exchange_wait(*refs, **spec):
    loc, sends, recvs = _exchange_copies(*refs, **spec)
    for s in sends:
        s.wait_send()
    for r in recvs:
        r.wait_recv()
    if loc is not None:
        loc.wait()


def exchange(src, n_dst, spec, *, name, in_place=False):
    n = len(spec["flips"])
    assert not in_place or (n_dst == src.shape[0] and spec["local_idx"] is None)

    def body(*refs):
        _exchange_start(*refs, **spec)
        _exchange_wait(*refs, **spec)

    return pl.pallas_call(
        body, name=name,
        out_shape=jax.ShapeDtypeStruct((n_dst,) + src.shape[1:], src.dtype),
        in_specs=[pl.BlockSpec(memory_space=pl.ANY)],
        out_specs=pl.BlockSpec(memory_space=pl.ANY),
        scratch_shapes=[pltpu.SemaphoreType.DMA((n,)), pltpu.SemaphoreType.DMA((n,)), pltpu.SemaphoreType.DMA(())],
        input_output_aliases={0: 0} if in_place else {},
    )(src)


CHIP_FLIPS = [(1, 0, 0), (0, 1, 0), (1, 1, 0)]
SIBLING = [(0, 0, 1)]
ALL_FLIPS = [(a, b, c) for a in (0, 1) for b in (0, 1) for c in (0, 1) if (a, b, c) != (0, 0, 0)]


def _chip(pos):
    return 2 * pos[0] + pos[1]


def _dev(pos):
    return 4 * pos[0] + 2 * pos[1] + pos[2]


GATHER = dict(flips=CHIP_FLIPS, src_idx=lambda me, peer: 0, dst_idx=_chip, local_idx=lambda me: (0, _chip(me)))
SCATTER = dict(flips=ALL_FLIPS, src_idx=lambda me, peer: 2 * _chip(peer) + peer[2], dst_idx=_dev,
               local_idx=lambda me: (2 * _chip(me) + me[2], _dev(me)))
SHARE = dict(flips=SIBLING, src_idx=lambda me, peer: me[2], dst_idx=lambda me: me[2], local_idx=None)
ALL_TO_ALL = dict(flips=ALL_FLIPS, src_idx=lambda me, peer: 0, dst_idx=_dev, local_idx=lambda me: (0, _dev(me)))


def sum_slots(x, *, name, tr=512):
    n, R, L = x.shape
    tr = _rows(R, tr)

    def body(x_ref, o_ref):
        acc = x_ref[0]
        for k in range(1, n):
            acc = acc + x_ref[k]
        o_ref[...] = acc

    return _call(body, name=name, grid=(R // tr,),
                 ins=[(x, (n, tr, L), lambda i: (0, i, 0))],
                 outs=[((R, L), F32, (tr, L), lambda i: (i, 0))])[0]


def sum_slots_into_half(x, half, *, name, tr=512):
    n, R, L = x.shape
    tr = _rows(R, tr)

    def body(h_ref, x_ref, o_ref):
        acc = x_ref[0].astype(F32)
        for k in range(1, n):
            acc = acc + x_ref[k].astype(F32)
        o_ref[0] = acc

    return pl.pallas_call(
        body, name=name,
        grid_spec=pltpu.PrefetchScalarGridSpec(
            num_scalar_prefetch=1, grid=(R // tr,),
            in_specs=[pl.BlockSpec((n, tr, L), lambda i, h: (0, i, 0))],
            out_specs=pl.BlockSpec((1, tr, L), lambda i, h: (h[0], i, 0))),
        out_shape=jax.ShapeDtypeStruct((2, R, L), F32),
        compiler_params=pltpu.CompilerParams(dimension_semantics=("arbitrary",), vmem_limit_bytes=VMEM_LIMIT),
    )(jnp.reshape(half, (1,)).astype(jnp.int32), x)


ADAM_ROWS = 256


def adamw(w, g, m, v, *, name):
    shape = w.shape
    w2, g2, m2, v2 = (a.reshape(-1, shape[-1]) for a in (w, g, m, v))
    R, L = w2.shape
    tr = ADAM_ROWS if R % ADAM_ROWS == 0 else next(
        (t for t in range(min(R, 2 * ADAM_ROWS) // SUBLANES * SUBLANES, 0, -SUBLANES) if R % t == 0), R)

    def body(w_ref, g_ref, m_ref, v_ref, d_ref, mo_ref, vo_ref):
        g = g_ref[...]
        m = ADAM_B1 * m_ref[...] + (1.0 - ADAM_B1) * g
        v = ADAM_B2 * v_ref[...] + (1.0 - ADAM_B2) * (g * g)
        m_hat = m / (1.0 - ADAM_B1 ** ADAM_STEP)
        v_hat = v / (1.0 - ADAM_B2 ** ADAM_STEP)
        d_ref[...] = -ADAM_LR * (m_hat / (jnp.sqrt(v_hat) + ADAM_EPS) + ADAM_WD * w_ref[...])
        mo_ref[...] = m
        vo_ref[...] = v

    row = lambda i: (i, 0)
    outs = _call(body, name=name, grid=(R // tr,),
                 ins=[(a, (tr, L), row) for a in (w2, g2, m2, v2)],
                 outs=[((R, L), F32, (tr, L), row)] * 3)
    return [o.reshape(shape) for o in outs]


def kernel(x, mix_norm, ffn_norm, hy_w_in, hy_w_gate2, hy_b_gate, hy_gla_norm, hy_sb_q_norm, hy_sb_k_norm, hy_w_out, cv_w_pw1, cv_b_pw1, cv_w_dw, cv_b_dw, cv_ln_g, cv_ln_b, cv_w_pw2, cv_b_pw2, ffn_w_gate, ffn_w_up, ffn_w_down, loss_target, m_mix_norm, m_ffn_norm, m_hy_w_in, m_hy_w_gate2, m_hy_b_gate, m_hy_gla_norm, m_hy_sb_q_norm, m_hy_sb_k_norm, m_hy_w_out, m_cv_w_pw1, m_cv_b_pw1, m_cv_w_dw, m_cv_b_dw, m_cv_ln_g, m_cv_ln_b, m_cv_w_pw2, m_cv_b_pw2, m_ffn_w_gate, m_ffn_w_up, m_ffn_w_down, v_mix_norm, v_ffn_norm, v_hy_w_in, v_hy_w_gate2, v_hy_b_gate, v_hy_gla_norm, v_hy_sb_q_norm, v_hy_sb_k_norm, v_hy_w_out, v_cv_w_pw1, v_cv_b_pw1, v_cv_w_dw, v_cv_b_dw, v_cv_ln_g, v_cv_ln_b, v_cv_w_pw2, v_cv_b_pw2, v_ffn_w_gate, v_ffn_w_up, v_ffn_w_down):
    w = dict(zip(WEIGHTS, (mix_norm, ffn_norm, hy_w_in, hy_w_gate2, hy_b_gate, hy_gla_norm, hy_sb_q_norm, hy_sb_k_norm, hy_w_out, cv_w_pw1, cv_b_pw1, cv_w_dw, cv_b_dw, cv_ln_g, cv_ln_b, cv_w_pw2, cv_b_pw2, ffn_w_gate, ffn_w_up, ffn_w_down)))
    m = dict(zip(WEIGHTS, (m_mix_norm, m_ffn_norm, m_hy_w_in, m_hy_w_gate2, m_hy_b_gate, m_hy_gla_norm, m_hy_sb_q_norm, m_hy_sb_k_norm, m_hy_w_out, m_cv_w_pw1, m_cv_b_pw1, m_cv_w_dw, m_cv_b_dw, m_cv_ln_g, m_cv_ln_b, m_cv_w_pw2, m_cv_b_pw2, m_ffn_w_gate, m_ffn_w_up, m_ffn_w_down)))
    v = dict(zip(WEIGHTS, (v_mix_norm, v_ffn_norm, v_hy_w_in, v_hy_w_gate2, v_hy_b_gate, v_hy_gla_norm, v_hy_sb_q_norm, v_hy_sb_k_norm, v_hy_w_out, v_cv_w_pw1, v_cv_b_pw1, v_cv_w_dw, v_cv_b_dw, v_cv_ln_g, v_cv_ln_b, v_cv_w_pw2, v_cv_b_pw2, v_ffn_w_gate, v_ffn_w_up, v_ffn_w_down)))
    def local_shards(keys, rows_multiple, dtype):
        item = lambda ref, idx: w[ref] if idx is None else w[ref][idx]
        return _pack([item(*ITEMS[k][:2]) for k in keys], rows_multiple, dtype)[None]

    def finish_reduce(slots, tag):
        halves = sum_slots_into_half(slots, lax.axis_index("c"), name=f"reduce_{tag}_sum")
        return exchange(halves, 2, SHARE, name=f"reduce_{tag}_share", in_place=True).reshape(2 * slots.shape[1], LANES)

    small = exchange(local_shards(SMALL_ITEMS, 8, F32), N_CHIPS, GATHER, name="gather_small")
    raw_small = unpack_items(small, SMALL_ITEMS)
    repl = {n: w[n] for n in REPL}
    stages = (["wg0", "wu0", "wd0"], ["pw1", "pw2", "wd1"], ["wg1", "wu1"])
    late = {"first_src": local_shards(FIRST_BIG_ITEMS, 32, MXU),
            "unpack_first": lambda buf: layout_first({**unpack_items(buf, FIRST_BIG_ITEMS), **raw_small}, repl),
            "src": [local_shards(keys, 32, MXU) for keys in stages],
            "unpack": [functools.partial(unpack_items, keys=keys) for keys in stages],
            "pack": [lambda G, keys=keys: pack_item_grads(item_grads(G), keys, MXU) for keys in LATE_SCATTERS],
            "pack_first": lambda G: pack_item_grads(item_grads(G), FIRST_BIG_ITEMS, MXU)}

    loss_lanes, dx, Gk, late_slots, first_slots = local_step(x[0], loss_target[0], {"mix_norm": w["mix_norm"]}, late)
    rep_names = {"mix_norm": "mix_norm", "ffn_norm": "ffn_norm", "hy_b_gate": "b_gate", "hy_gla_norm": "g_gla",
                 "hy_sb_q_norm": "g_q", "hy_sb_k_norm": "g_k"}
    Gr = {n: Gk[k] for n, k in rep_names.items()}

    groups = [(FIRST_BIG_ITEMS, finish_reduce(first_slots, "first")),
              (LATE_SCATTERS[0], finish_reduce(late_slots[0], "late0")),
              (LATE_SCATTERS[1], finish_reduce(late_slots[1], "late1"))]

    loss_row = jnp.pad(jnp.sum(loss_lanes).reshape(1), (0, LANES - 1))
    small_g = item_grads(Gk)
    rep = _pack([Gr[n] for n in REPL] + [small_g[k] for k in SMALL_ITEMS] + [loss_row], 8, F32)
    rep_all = exchange(rep[None], N_DEV, ALL_TO_ALL, name="reduce_small")
    rep_sum = sum_slots(rep_all, name="reduce_small_sum", tr=rep.shape[0])
    rep_shapes = [w[n].shape for n in REPL] + [ITEMS[k][2] for k in SMALL_ITEMS]
    rep_parts = _unpack(rep_sum, rep_shapes)
    loss = rep_sum.reshape(-1)[sum(math.prod(s) for s in rep_shapes)]
    grads = dict(zip(REPL, rep_parts))
    chip = 2 * lax.axis_index("x") + lax.axis_index("y")
    for k, full in zip(SMALL_ITEMS, rep_parts[len(REPL):]):
        ref, _, (_, c), ax = ITEMS[k]
        assert ax == 1
        grads[ref] = lax.dynamic_slice_in_dim(full, chip * (c // N_CHIPS), c // N_CHIPS, axis=1).reshape(w[ref].shape)

    item_g = {}
    for keys, g_flat in groups:
        off = 0
        for k in keys:
            ref, idx = ITEMS[k][:2]
            item_g[k] = g_flat[off:off + _item_rows(k)].reshape(w[ref].shape if idx is None else w[ref].shape[1:])
            off += _item_rows(k)
    for ref in _refs_of(list(item_g)):
        layers = sorted((ITEMS[k][1], k) for k in item_g if ITEMS[k][0] == ref)
        grads[ref] = jnp.stack([item_g[k] for _, k in layers])
    out = {"grad": grads, "delta": {}, "new_m": {}, "new_v": {}}
    small_refs = _refs_of(SMALL_ITEMS) + REPL
    for n in WEIGHTS:
        if n not in small_refs:
            out["delta"][n], out["new_m"][n], out["new_v"][n] = adamw(w[n], grads[n], m[n], v[n], name=f"adamw_{n}")
    packed = [_pack([t[n] for n in small_refs], 8, F32) for t in (w, grads, m, v)]
    small_shapes = [w[n].shape for n in small_refs]
    for key, buf in zip(("delta", "new_m", "new_v"), adamw(*packed, name="adamw_small")):
        out[key].update(dict(zip(small_refs, _unpack(buf, small_shapes))))

    return (loss, dx[None], *[out["grad"][n] for n in WEIGHTS], *[out["delta"][n] for n in WEIGHTS],
            *[out["new_m"][n] for n in WEIGHTS], *[out["new_v"][n] for n in WEIGHTS])
```

```python
import functools
import math

import numpy as np
import jax
import jax.numpy as jnp
from jax import lax
from jax.experimental import pallas as pl
from jax.experimental.pallas import tpu as pltpu

F32 = jnp.float32
MXU = jnp.bfloat16
EPS = 1e-6
LANES = 128
VMEM_LIMIT = 56 * 1024 * 1024

D = 1024
F = 2816
CHUNK = 64
GLA_H, GLA_K, GLA_V, GLA_RANK = 4, 64, 128, 16
SB_H, SB_D = 8, 64
CONV_W = 31
HALO = 32

ADAM_LR, ADAM_B1, ADAM_B2, ADAM_EPS, ADAM_WD, ADAM_STEP = 0.001, 0.9, 0.999, 1e-08, 0.01, 10

NN = (((1,), (0,)), ((), ()))
NT = (((1,), (1,)), ((), ()))
TN = (((0,), (0,)), ((), ()))


def _dot(a, b, dims=NN):
    return lax.dot_general(a.astype(MXU), b.astype(MXU), dims, preferred_element_type=F32)


def _dot_split(a, b, dims=NN, a_split=True):
    x = a if a_split else b
    hi = x.astype(MXU)
    lo = (x - hi.astype(F32)).astype(MXU)
    if a_split:
        return _dot(hi, b, dims) + _dot(lo, b, dims)
    return _dot(a, hi, dims) + _dot(a, lo, dims)


def _sigmoid(x):
    return 1.0 / (1.0 + jnp.exp(-x))


def _softplus_neg_abs(z):
    return jnp.log(1.0 + jnp.exp(-jnp.abs(z)))


def _call(body, *, name, grid, ins, outs, scratch=()):
    spec = lambda b, m: pl.BlockSpec(memory_space=pl.ANY) if b is None else pl.BlockSpec(b, m)
    res = pl.pallas_call(
        body,
        name=name,
        grid=grid,
        in_specs=[spec(b, m) for _, b, m in ins],
        out_specs=[spec(b, m) for _, _, b, m in outs],
        out_shape=[jax.ShapeDtypeStruct(s, d) for s, d, _, _ in outs],
        scratch_shapes=list(scratch),
        compiler_params=pltpu.CompilerParams(
            dimension_semantics=("arbitrary",) * len(grid), vmem_limit_bytes=VMEM_LIMIT),
    )(*[a for a, _, _ in ins])
    return res


def _rows(T, tm):
    tm = min(tm, T)
    assert T % tm == 0, (T, tm)
    return tm


def mm(pairs, *, name, trans_b=False, bias=None, res=None, out_dtype=F32, tm=512, tn=512, fused=None):
    M = pairs[0][0].shape[0]
    N = pairs[0][1].shape[0] if trans_b else pairs[0][1].shape[1]
    tm = _rows(M, tm)
    tn = min(tn, N)
    assert N % tn == 0, (N, tn)
    np_ = len(pairs)
    ni, nj = M // tm, N // tn
    pred = lambda: ((pl.program_id(0) == 0) & (pl.program_id(1) == 0),
                    (pl.program_id(0) == ni - 1) & (pl.program_id(1) == nj - 1))
    x_ins, x_outs, x_scratch, x_start, x_end = _fused_exchange(fused, pred)
    n_in = 2 * np_ + (bias is not None) + (res is not None)

    def body(*refs):
        o_ref = refs[n_in + len(x_ins)]
        x_refs = refs[n_in:n_in + len(x_ins)] + refs[n_in + len(x_ins) + 1:]
        x_start(x_refs)
        acc = None
        for p in range(np_):
            d = _dot(refs[2 * p][...], refs[2 * p + 1][...], NT if trans_b else NN)
            acc = d if acc is None else acc + d
        k = 2 * np_
        if bias is not None:
            acc = acc + refs[k][...]
            k += 1
        if res is not None:
            acc = acc + refs[k][...]
        o_ref[...] = acc.astype(out_dtype)
        x_end(x_refs)

    ins = []
    for a, b in pairs:
        K = a.shape[1]
        ins.append((a, (tm, K), lambda i, j: (i, 0)))
        if trans_b:
            ins.append((b, (tn, K), lambda i, j: (j, 0)))
        else:
            ins.append((b, (K, tn), lambda i, j: (0, j)))
    if bias is not None:
        ins.append((bias, (1, tn), lambda i, j: (0, j)))
    if res is not None:
        ins.append((res, (tm, tn), lambda i, j: (i, j)))
    outs = _call(body, name=name, grid=(ni, nj), ins=ins + x_ins,
                 outs=[((M, N), out_dtype, (tm, tn), lambda i, j: (i, j))] + x_outs, scratch=x_scratch)
    return outs[0] if fused is None else outs


def mm_tn(a, b, *, name, tm=512, tn=512, tk=1024):
    T, M = a.shape
    N = b.shape[1]
    tm, tn, tk = min(tm, M), min(tn, N), min(tk, T)
    assert M % tm == 0 and N % tn == 0 and T % tk == 0, (M, N, T, tm, tn, tk)

    def body(a_ref, b_ref, o_ref):
        @pl.when(pl.program_id(2) == 0)
        def _():
            o_ref[...] = jnp.zeros_like(o_ref)
        o_ref[...] += _dot(a_ref[...], b_ref[...], TN)

    return _call(body, name=name, grid=(M // tm, N // tn, T // tk),
                 ins=[(a, (tk, tm), lambda i, j, k: (k, i)), (b, (tk, tn), lambda i, j, k: (k, j))],
                 outs=[((M, N), F32, (tm, tn), lambda i, j, k: (i, j))])[0]


def rms_fwd(x, g, *, name, tm=256, fused=None):
    T = x.shape[0]
    tm = _rows(T, tm)
    n = T // tm
    pred = lambda: (pl.program_id(0) == 0, pl.program_id(0) == n - 1)
    x_ins, x_outs, x_scratch, x_start, x_end = _fused_exchange(fused, pred)
    nx = len(x_ins)

    def body(*refs):
        x_ref, g_ref = refs[:2]
        o_ref = refs[2 + nx]
        x_refs = refs[2:2 + nx] + refs[3 + nx:]
        x_start(x_refs)
        x = x_ref[...]
        r = lax.rsqrt(jnp.mean(x * x, axis=-1, keepdims=True) + EPS)
        o_ref[...] = (x * r * g_ref[...]).astype(MXU)
        x_end(x_refs)

    outs = _call(body, name=name, grid=(n,),
                 ins=[(x, (tm, D), lambda i: (i, 0)), (g, (1, D), lambda i: (0, 0))] + x_ins,
                 outs=[((T, D), MXU, (tm, D), lambda i: (i, 0))] + x_outs, scratch=x_scratch)
    return outs[0] if fused is None else outs


def rms_bwd(x, g, dhn, dres, *, name, tm=256):
    T = x.shape[0]
    tm = _rows(T, tm)

    def body(x_ref, g_ref, d_ref, r_ref, dx_ref, dg_ref, cs_ref):
        @pl.when(pl.program_id(0) == 0)
        def _():
            dg_ref[...] = jnp.zeros_like(dg_ref)
            cs_ref[...] = jnp.zeros_like(cs_ref)
        x = x_ref[...]
        d = d_ref[...]
        r = lax.rsqrt(jnp.mean(x * x, axis=-1, keepdims=True) + EPS)
        dg_ref[...] += jnp.sum(d * x * r, axis=0, keepdims=True)
        t = d * g_ref[...]
        m = jnp.mean(t * x, axis=-1, keepdims=True)
        dx = r_ref[...] + t * r - x * (r * r * r) * m
        dx_ref[...] = dx
        cs_ref[...] += jnp.sum(dx, axis=0, keepdims=True)

    row = lambda i: (i, 0)
    fix = lambda i: (0, 0)
    return _call(body, name=name, grid=(T // tm,),
                 ins=[(x, (tm, D), row), (g, (1, D), fix), (dhn, (tm, D), row), (dres, (tm, D), row)],
                 outs=[((T, D), F32, (tm, D), row), ((1, D), F32, (1, D), fix), ((1, D), F32, (1, D), fix)])


def loss_grad(y, t, *, name, tm=256):
    T = y.shape[0]
    tm = _rows(T, tm)

    def body(y_ref, t_ref, dy_ref, l_ref):
        @pl.when(pl.program_id(0) == 0)
        def _():
            l_ref[...] = jnp.zeros_like(l_ref)
        e = y_ref[...] - t_ref[...]
        dy_ref[...] = e * (1.0 / D)
        l_ref[...] += jnp.sum(e * e, axis=0, keepdims=True) * (0.5 / D)

    row = lambda i: (i, 0)
    return _call(body, name=name, grid=(T // tm,),
                 ins=[(y, (tm, D), row), (t, (tm, D), row)],
                 outs=[((T, D), F32, (tm, D), row), ((1, D), F32, (1, D), lambda i: (0, 0))])


def ffn_up(hn, wg, wu, *, name, tm=1024, tn=256, fused=None):
    T = hn.shape[0]
    tm = _rows(T, tm)
    ni, nj = T // tm, F // tn
    pred = lambda: ((pl.program_id(0) == 0) & (pl.program_id(1) == 0),
                    (pl.program_id(0) == ni - 1) & (pl.program_id(1) == nj - 1))
    x_ins, x_outs, x_scratch, x_start, x_end = _fused_exchange(fused, pred)
    nx = len(x_ins)

    def body(*refs):
        h_ref, wg_ref, wu_ref = refs[:3]
        g_ref, u_ref, a_ref = refs[3 + nx:6 + nx]
        x_refs = refs[3:3 + nx] + refs[6 + nx:]
        x_start(x_refs)
        h = h_ref[...]
        g = _dot(h, wg_ref[...])
        u = _dot(h, wu_ref[...])
        g_ref[...] = g.astype(MXU)
        u_ref[...] = u.astype(MXU)
        a_ref[...] = (g * _sigmoid(g) * u).astype(MXU)
        x_end(x_refs)

    tile = lambda i, j: (i, j)
    return _call(body, name=name, grid=(ni, nj),
                 ins=[(hn, (tm, D), lambda i, j: (i, 0)), (wg, (D, tn), lambda i, j: (0, j)),
                      (wu, (D, tn), lambda i, j: (0, j))] + x_ins,
                 outs=[((T, F), MXU, (tm, tn), tile), ((T, F), MXU, (tm, tn), tile),
                       ((T, F), MXU, (tm, tn), tile)] + x_outs,
                 scratch=x_scratch)


def ffn_bwd_act(dy, wd, G, U, *, name, tm=1024, tn=256):
    T = dy.shape[0]
    tm = _rows(T, tm)

    def body(dy_ref, wd_ref, g_ref, u_ref, dg_ref, du_ref):
        da = _dot(dy_ref[...], wd_ref[...], NT)
        g = g_ref[...].astype(F32)
        u = u_ref[...].astype(F32)
        s = _sigmoid(g)
        sil = g * s
        du_ref[...] = (da * sil).astype(MXU)
        dg_ref[...] = (da * u * (s * (1.0 + g * (1.0 - s)))).astype(MXU)

    tile = lambda i, j: (i, j)
    return _call(body, name=name, grid=(T // tm, F // tn),
                 ins=[(dy, (tm, D), lambda i, j: (i, 0)), (wd, (tn, D), lambda i, j: (j, 0)),
                      (G, (tm, tn), tile), (U, (tm, tn), tile)],
                 outs=[((T, F), MXU, (tm, tn), tile), ((T, F), MXU, (tm, tn), tile)])


def ffn_fwd(h, g_norm, wg, wu, wd, tag, fused=None):
    hn = rms_fwd(h, g_norm, name=f"ffn{tag}_norm")
    G, U, act, *landed = ffn_up(hn, wg, wu, name=f"ffn{tag}_up", fused=fused)
    h_out = mm([(act, wd)], res=h, name=f"ffn{tag}_down")
    return h_out, (hn, G, U, act), landed


def ffn_bwd(dy, h, g_norm, wg, wu, wd, saved, tag):
    hn, G, U, act = saved
    dG, dU = ffn_bwd_act(dy, wd, G, U, name=f"ffn{tag}_bwd_act")
    d_wd = mm_tn(act, dy, name=f"ffn{tag}_dwd", tm=1408, tn=512)
    d_wg = mm_tn(hn, dG, name=f"ffn{tag}_dwg", tm=512, tn=1408)
    d_wu = mm_tn(hn, dU, name=f"ffn{tag}_dwu", tm=512, tn=1408)
    dhn = mm([(dG, wg), (dU, wu)], trans_b=True, name=f"ffn{tag}_dhn")
    dh, d_g, cs = rms_bwd(h, g_norm, dhn, dy, name=f"ffn{tag}_norm_bwd")
    return dh, d_g, d_wg, d_wu, d_wd, cs


SUBLANES = 8


def _window_scratch(tm):
    return [pltpu.VMEM((tm + HALO + SUBLANES, D), F32), pltpu.VMEM((SUBLANES, tm + HALO, D), F32)]


def _shift_copies(win, sh):
    rows = sh.shape[1]
    win[pl.ds(rows, SUBLANES), :] = jnp.zeros((SUBLANES, D), F32)
    for s in range(SUBLANES):
        sh[s] = win[pl.ds(s, rows), :]


def _tap(sh, off, rb):
    s = off % SUBLANES
    return sh[s, pl.ds(off - s, rb), :]


def conv_fwd(A, w_dw, b_dw, ln_g, ln_b, *, name, tm=256, rb=32, fused=None):
    T = A.shape[0]
    tm = _rows(T, tm)
    n = T // tm
    pred = lambda: (pl.program_id(0) == 0, pl.program_id(0) == n - 1)
    x_ins, x_outs, x_scratch, x_start, x_end = _fused_exchange(fused, pred)
    nx = len(x_ins)

    def body(*refs):
        a_ref, ap_ref, w_ref, b_ref, g_ref, bb_ref = refs[:6]
        s_ref, c_ref = refs[6 + nx:8 + nx]
        win, sh = refs[8 + 2 * nx:10 + 2 * nx]
        x_refs = refs[6:6 + nx] + refs[8 + nx:8 + 2 * nx] + refs[10 + 2 * nx:]
        x_start(x_refs)
        i = pl.program_id(0)
        a = a_ref[...]
        win[pl.ds(HALO, tm), :] = a[:, :D] * _sigmoid(a[:, D:])
        ap = ap_ref[pl.ds(tm - HALO, HALO), :]
        up = ap[:, :D] * _sigmoid(ap[:, D:])
        win[pl.ds(0, HALO), :] = jnp.where(i > 0, up, 0.0)
        _shift_copies(win, sh)
        for r0 in range(0, tm, rb):
            acc = jnp.broadcast_to(b_ref[...], (rb, D))
            for k in range(CONV_W):
                acc = acc + w_ref[pl.ds(k, 1), :] * _tap(sh, r0 + k + HALO - (CONV_W - 1), rb)
            c_ref[pl.ds(r0, rb), :] = acc
        c = c_ref[...]
        mu = jnp.mean(c, axis=-1, keepdims=True)
        cc = c - mu
        var = jnp.mean(cc * cc, axis=-1, keepdims=True)
        z = cc * lax.rsqrt(var + EPS) * g_ref[...] + bb_ref[...]
        s_ref[...] = (z * _sigmoid(z)).astype(MXU)
        x_end(x_refs)

    row = lambda i: (i, 0)
    fix = lambda i: (0, 0)
    return _call(body, name=name, grid=(n,),
                 ins=[(A, (tm, 2 * D), row), (A, (tm, 2 * D), lambda i: (jnp.maximum(i - 1, 0), 0)),
                      (w_dw, (HALO, D), fix), (b_dw, (1, D), fix), (ln_g, (1, D), fix), (ln_b, (1, D), fix)] + x_ins,
                 outs=[((T, D), MXU, (tm, D), row), ((T, D), F32, (tm, D), row)] + x_outs,
                 scratch=_window_scratch(tm) + x_scratch)


def ln_swish_bwd(c, ds, ln_g, ln_b, *, name, tm=256):
    T = c.shape[0]
    tm = _rows(T, tm)

    def body(c_ref, ds_ref, g_ref, b_ref, dc_ref, dg_ref, db_ref):
        @pl.when(pl.program_id(0) == 0)
        def _():
            dg_ref[...] = jnp.zeros_like(dg_ref)
            db_ref[...] = jnp.zeros_like(db_ref)
        c = c_ref[...]
        mu = jnp.mean(c, axis=-1, keepdims=True)
        cc = c - mu
        rstd = lax.rsqrt(jnp.mean(cc * cc, axis=-1, keepdims=True) + EPS)
        n = cc * rstd
        z = n * g_ref[...] + b_ref[...]
        s = _sigmoid(z)
        dz = ds_ref[...] * (s * (1.0 + z * (1.0 - s)))
        dg_ref[...] += jnp.sum(dz * n, axis=0, keepdims=True)
        db_ref[...] += jnp.sum(dz, axis=0, keepdims=True)
        dn = dz * g_ref[...]
        dc_ref[...] = rstd * (dn - jnp.mean(dn, axis=-1, keepdims=True)
                              - n * jnp.mean(dn * n, axis=-1, keepdims=True))

    row = lambda i: (i, 0)
    fix = lambda i: (0, 0)
    return _call(body, name=name, grid=(T // tm,),
                 ins=[(c, (tm, D), row), (ds, (tm, D), row), (ln_g, (1, D), fix), (ln_b, (1, D), fix)],
                 outs=[((T, D), F32, (tm, D), row), ((1, D), F32, (1, D), fix), ((1, D), F32, (1, D), fix)])


def conv_bwd(dc, A, w_dw, *, name, tm=256, rb=32, fused=None):
    T = A.shape[0]
    tm = _rows(T, tm)
    n = T // tm
    SUB = 8
    pred = lambda: (pl.program_id(0) == 0, pl.program_id(0) == n - 1)
    x_ins, x_outs, x_scratch, x_start, x_end = _fused_exchange(fused, pred)
    nx = len(x_ins)

    def body(*refs):
        dc_ref, dn_ref, a_ref, ap_ref, w_ref = refs[:5]
        da_ref, dw_ref, dbd_ref, dbp_ref = refs[5 + nx:9 + nx]
        wdc, sdc, wu, su, accw = refs[9 + 2 * nx:14 + 2 * nx]
        x_refs = refs[5:5 + nx] + refs[9 + nx:9 + 2 * nx] + refs[14 + 2 * nx:]
        x_start(x_refs)
        i = pl.program_id(0)

        @pl.when(i == 0)
        def _():
            accw[...] = jnp.zeros_like(accw)
            dbd_ref[...] = jnp.zeros_like(dbd_ref)
            dbp_ref[...] = jnp.zeros_like(dbp_ref)
        a = a_ref[...]
        a1, a2 = a[:, :D], a[:, D:]
        sg = _sigmoid(a2)
        wu[pl.ds(HALO, tm), :] = a1 * sg
        ap = ap_ref[pl.ds(tm - HALO, HALO), :]
        wu[pl.ds(0, HALO), :] = jnp.where(i > 0, ap[:, :D] * _sigmoid(ap[:, D:]), 0.0)
        dc = dc_ref[...]
        wdc[pl.ds(0, tm), :] = dc
        wdc[pl.ds(tm, HALO), :] = jnp.where(i < n - 1, dn_ref[pl.ds(0, HALO), :], 0.0)
        dbd_ref[...] += jnp.sum(dc, axis=0, keepdims=True)
        _shift_copies(wdc, sdc)
        _shift_copies(wu, su)
        for r0 in range(0, tm, rb):
            du = jnp.zeros((rb, D), F32)
            dcs = wdc[pl.ds(r0, rb), :]
            for k in range(CONV_W):
                du = du + w_ref[pl.ds(k, 1), :] * _tap(sdc, r0 + (CONV_W - 1) - k, rb)
                p = dcs * _tap(su, r0 + k + HALO - (CONV_W - 1), rb)
                accw[pl.ds(SUB * k, SUB), :] += jnp.sum(p.reshape(rb // SUB, SUB, D), axis=0)
            s = sg[r0:r0 + rb]
            da_ref[pl.ds(r0, rb), pl.ds(0, D)] = (du * s).astype(MXU)
            da_ref[pl.ds(r0, rb), pl.ds(D, D)] = (du * a1[r0:r0 + rb] * s * (1.0 - s)).astype(MXU)
        da = da_ref[...].astype(F32)
        dbp_ref[...] += jnp.sum(da, axis=0, keepdims=True)

        @pl.when(i == n - 1)
        def _():
            dw_ref[...] = jnp.zeros_like(dw_ref)
            for k in range(CONV_W):
                dw_ref[pl.ds(k, 1), :] = jnp.sum(accw[pl.ds(SUB * k, SUB), :], axis=0, keepdims=True)
        x_end(x_refs)

    row = lambda i: (i, 0)
    fix = lambda i: (0, 0)
    return _call(body, name=name, grid=(n,),
                 ins=[(dc, (tm, D), row), (dc, (tm, D), lambda i: (jnp.minimum(i + 1, n - 1), 0)),
                      (A, (tm, 2 * D), row), (A, (tm, 2 * D), lambda i: (jnp.maximum(i - 1, 0), 0)),
                      (w_dw, (HALO, D), fix)] + x_ins,
                 outs=[((T, 2 * D), MXU, (tm, 2 * D), row), ((HALO, D), F32, (HALO, D), fix),
                       ((1, D), F32, (1, D), fix), ((1, 2 * D), F32, (1, 2 * D), fix)] + x_outs,
                 scratch=_window_scratch(tm) + _window_scratch(tm) + [pltpu.VMEM((SUB * HALO, D), F32)] + x_scratch)


def sb_prep(ps, gq, gk, *, name, tm=256):
    T = ps.shape[0]
    tm = _rows(T, tm)
    W = SB_H * SB_D

    def body(q_ref, k_ref, v_ref, gq_ref, gk_ref, qo, ko, vo):
        qs, ks = [], []
        for h in range(SB_H):
            sl = slice(h * SB_D, (h + 1) * SB_D)
            q = q_ref[:, sl]
            k = k_ref[:, sl]
            rq = lax.rsqrt(jnp.mean(q * q, axis=-1, keepdims=True) + EPS)
            rk = lax.rsqrt(jnp.mean(k * k, axis=-1, keepdims=True) + EPS)
            qs.append(q * rq * gq_ref[...] * (SB_D ** -0.5))
            ks.append(k * rk * gk_ref[...])
        qo[...] = jnp.concatenate(qs, axis=-1).astype(MXU)
        ko[...] = jnp.concatenate(ks, axis=-1).astype(MXU)
        vo[...] = v_ref[...].astype(MXU)

    row = lambda i: (i, 0)
    fix = lambda i: (0, 0)
    return _call(body, name=name, grid=(T // tm,),
                 ins=[(ps, (tm, W), lambda i: (i, 0)), (ps, (tm, W), lambda i: (i, 1)),
                      (ps, (tm, W), lambda i: (i, 2)), (gq, (1, SB_D), fix), (gk, (1, SB_D), fix)],
                 outs=[((T, W), MXU, (tm, W), row)] * 3)


def _sb_past(tq):
    row = lax.broadcasted_iota(jnp.int32, (tq, tq), 0)
    col = lax.broadcasted_iota(jnp.int32, (tq, tq), 1)
    return col < row


def _sb_tri(tq):
    ones = jnp.ones((tq, tq), MXU)
    return jnp.tril(ones), jnp.triu(ones, k=1)


def _sb_scores(q, k, past):
    return _sb_logs(_dot(q, k, NT), past)


def _sb_logs(z, past):
    ls = jnp.minimum(z, 0.0) - _softplus_neg_abs(z)
    lk = ls - z
    if past is not None:
        lk = jnp.where(past, lk, 0.0)
    return ls, lk


def _fused_exchange(fused, steps_pred):
    if fused is None:
        return [], [], [], lambda refs: None, lambda refs: None
    src, n_dst, spec = fused
    n = len(spec["flips"])
    ins = [(src, None, None)]
    outs = [((n_dst,) + src.shape[1:], src.dtype, None, None)]
    scratch = [pltpu.SemaphoreType.DMA((n,)), pltpu.SemaphoreType.DMA((n,)), pltpu.SemaphoreType.DMA(())]

    def at_start(refs):
        @pl.when(steps_pred()[0])
        def _():
            _exchange_start(*refs, **spec)

    def at_end(refs):
        @pl.when(steps_pred()[1])
        def _():
            _exchange_wait(*refs, **spec)

    return ins, outs, scratch, at_start, at_end


PAIR = 2 * SB_D
SB_DEAD = -120.0
SB_UNVISITED = -1e30


def _split_pair(x2, lo):
    zero = jnp.zeros_like(x2)
    return [jnp.where(lo, x2, zero), jnp.where(lo, zero, x2)]


def sb_fwd(qn, kn, v, *, name, tq=256, pairs=2, fused=None):
    T, W = qn.shape
    tq = _rows(T, tq)
    wb, hg = pairs * PAIR, 2 * pairs
    assert T // tq <= LANES and W % wb == 0
    ng, ni = W // wb, T // tq
    pred = lambda: ((pl.program_id(0) == 0) & (pl.program_id(1) == 0),
                    (pl.program_id(0) == ng - 1) & (pl.program_id(1) == ni - 1))
    x_ins, x_outs, x_scratch, x_start, x_end = _fused_exchange(fused, pred)
    nx = len(x_ins)
    heads = range(hg)
    pcols = [slice(p * PAIR, (p + 1) * PAIR) for p in range(pairs)]

    def body(*refs):
        tri_ref, refs = refs[0], tuple(refs[1:])
        q_ref, k_ref, v_ref = refs[:3]
        o_ref, cb_ref = refs[3 + nx:5 + nx]
        x_refs = refs[3:3 + nx] + refs[5 + nx:]
        x_start(x_refs)
        i = pl.program_id(1)
        tri, past = tri_ref[...], _sb_past(tq)
        lane = lax.broadcasted_iota(jnp.int32, (tq, LANES), 1)
        lo = lane < SB_D
        cb_ref[...] = jnp.full((hg, tq, LANES), SB_UNVISITED, F32)
        qs = [m for p in range(pairs) for m in _split_pair(q_ref[:, pcols[p]], lo)]

        def block(j, carry, masked):
            accs, cs = carry
            kblk = pl.ds(pl.multiple_of(j * tq, tq), tq)
            for h in heads:
                cb_ref[h] = jnp.where(lane == j, cs[h], cb_ref[h])
            zs = [_dot(qs[h], k_ref[kblk, pcols[h // 2]], NT) for h in heads]
            sc = [_sb_logs(zs[h], past if masked else None) for h in heads]
            bincs = [_dot_split(sc[h][1], tri) for h in heads]
            ws = [jnp.exp(sc[h][0] + cs[h] + bincs[h] - sc[h][1]) for h in heads]
            if masked:
                ws = [jnp.where(past, w, 0.0) for w in ws]
            pv = [_dot(ws[h], v_ref[kblk, pcols[h // 2]]) for h in heads]
            new_a = [accs[p] + jnp.where(lo, pv[2 * p], pv[2 * p + 1]) for p in range(pairs)]
            new_c = [cs[h] + jnp.sum(sc[h][1], axis=-1, keepdims=True) for h in heads]
            return tuple(new_a), tuple(new_c)

        carry = ((jnp.zeros((tq, PAIR), F32),) * pairs, (jnp.zeros((tq, 1), F32),) * hg)
        carry = block(i, carry, True)

        def live(state):
            jj, _, cs = state
            return (jj < i) & (jnp.max(functools.reduce(jnp.maximum, cs)) > SB_DEAD)

        def step(state):
            jj, accs, cs = state
            accs, cs = block(i - 1 - jj, (accs, cs), False)
            return jj + 1, accs, cs

        _, accs, _ = lax.while_loop(live, step, (jnp.int32(0),) + carry)
        for p in range(pairs):
            o_ref[:, pcols[p]] = accs[p]
        x_end(x_refs)

    blk = lambda g, i: (i, g)
    full = lambda g, i: (0, g)
    return _call(body, name=name, grid=(ng, ni),
                 ins=[(_sb_tri(tq)[0], (tq, tq), lambda g, i: (0, 0)),
                      (qn, (tq, wb), blk), (kn, (T, wb), full), (v, (T, wb), full)] + x_ins,
                 outs=[((T, W), F32, (tq, wb), blk),
                       ((W // SB_D, T, LANES), F32, (hg, tq, LANES), lambda g, i: (g, i, 0))] + x_outs,
                 scratch=x_scratch)


def sb_bwd(qn, kn, v, cb, dmix, do_col, *, name, tq=256, pairs=2, fused=None):
    T, W = qn.shape
    tq = _rows(T, tq)
    wb, hg = pairs * PAIR, 2 * pairs
    assert W % wb == 0 and do_col % wb == 0
    ng, ni = W // wb, T // tq
    pred = lambda: ((pl.program_id(0) == 0) & (pl.program_id(1) == 0),
                    (pl.program_id(0) == ng - 1) & (pl.program_id(1) == ni - 1))
    x_ins, x_outs, x_scratch, x_start, x_end = _fused_exchange(fused, pred)
    nx = len(x_ins)
    heads = range(hg)
    pcols = [slice(p * PAIR, (p + 1) * PAIR) for p in range(pairs)]

    def body(*refs):
        tri_ref, tri_lt_ref, refs = refs[0], refs[1], tuple(refs[2:])
        q_ref, k_ref, v_ref, cb_ref, do_ref = refs[:5]
        dq_ref, dk_hbm, dv_hbm = refs[5 + nx:8 + nx]
        dk_ref, dv_ref = refs[8 + 2 * nx:10 + 2 * nx]
        x_refs = refs[5:5 + nx] + refs[8 + nx:8 + 2 * nx] + refs[10 + 2 * nx:]
        x_start(x_refs)
        i = pl.program_id(1)

        @pl.when(i == 0)
        def _():
            dk_ref[...] = jnp.zeros_like(dk_ref)
            dv_ref[...] = jnp.zeros_like(dv_ref)
        tri, tri_lt, past = tri_ref[...], tri_lt_ref[...], _sb_past(tq)
        lane = lax.broadcasted_iota(jnp.int32, (tq, LANES), 1)
        lo = lane < SB_D
        qs = [m for p in range(pairs) for m in _split_pair(q_ref[:, pcols[p]], lo)]
        dos = [m for p in range(pairs) for m in _split_pair(do_ref[:, pcols[p]].astype(MXU), lo)]

        def block(j, carry, masked):
            dqs, cgs = carry
            kblk = pl.ds(pl.multiple_of(j * tq, tq), tq)
            k2 = [k_ref[kblk, pcols[p]] for p in range(pairs)]
            zs = [_dot(qs[h], k2[h // 2], NT) for h in heads]
            dws = [_dot(dos[h], v_ref[kblk, pcols[h // 2]], NT) for h in heads]
            sc = [_sb_logs(zs[h], past if masked else None) for h in heads]
            cs = [jnp.sum(jnp.where(lane == j, cb_ref[h], 0.0), axis=-1, keepdims=True) for h in heads]
            bincs = [_dot_split(sc[h][1], tri) for h in heads]
            ws = [jnp.exp(sc[h][0] + cs[h] + bincs[h] - sc[h][1]) for h in heads]
            if masked:
                ws = [jnp.where(past, w, 0.0) for w in ws]
            wbs = [w.astype(MXU) for w in ws]
            gs = [ws[h] * dws[h] for h in heads]
            gpres = [cgs[h] + _dot(gs[h], tri_lt) for h in heads]
            sigs = [jnp.exp(sc[h][0]) for h in heads]
            dzs = [gs[h] - sigs[h] * (gs[h] + gpres[h]) for h in heads]
            if masked:
                dzs = [jnp.where(past, dz, 0.0) for dz in dzs]
            dzbs = [dz.astype(MXU) for dz in dzs]
            dqp = [_dot(dzbs[h], k2[h // 2]) for h in heads]
            new_dq = [dqs[p] + jnp.where(lo, dqp[2 * p], dqp[2 * p + 1]) for p in range(pairs)]
            for p in range(pairs):
                dk_ref[kblk, pcols[p]] += _dot(dzbs[2 * p], qs[2 * p], TN) + _dot(dzbs[2 * p + 1], qs[2 * p + 1], TN)
            for p in range(pairs):
                dv_ref[kblk, pcols[p]] += _dot(wbs[2 * p], dos[2 * p], TN) + _dot(wbs[2 * p + 1], dos[2 * p + 1], TN)
            new_cg = [cgs[h] + jnp.sum(gs[h], axis=-1, keepdims=True) for h in heads]
            return tuple(new_dq), tuple(new_cg)

        colmax = functools.reduce(jnp.maximum, [jnp.max(cb_ref[h], axis=0, keepdims=True) for h in heads])
        lane_row = lax.broadcasted_iota(jnp.int32, (1, LANES), 1)
        is_live = (colmax > SB_DEAD) & (lane_row < i)
        first = jnp.min(jnp.where(is_live, lane_row, i).astype(F32)).astype(jnp.int32)
        carry = ((jnp.zeros((tq, PAIR), F32),) * pairs, (jnp.zeros((tq, 1), F32),) * hg)
        carry = lax.fori_loop(first, i, lambda j, car: block(j, car, False), carry)
        dqs, _ = block(i, carry, True)
        for p in range(pairs):
            dq_ref[:, pcols[p]] = dqs[p]

        cols = pl.ds(pl.multiple_of(pl.program_id(0) * wb, wb), wb)

        @pl.when(i == ni - 1)
        def _():
            pltpu.sync_copy(dk_ref, dk_hbm.at[:, cols])
            pltpu.sync_copy(dv_ref, dv_hbm.at[:, cols])
        x_end(x_refs)

    blk = lambda g, i: (i, g)
    full = lambda g, i: (0, g)
    return _call(body, name=name, grid=(ng, ni),
                 ins=[(_sb_tri(tq)[0], (tq, tq), lambda g, i: (0, 0)), (_sb_tri(tq)[1], (tq, tq), lambda g, i: (0, 0)),
                      (qn, (tq, wb), blk), (kn, (T, wb), full), (v, (T, wb), full),
                      (cb, (hg, tq, LANES), lambda g, i: (g, i, 0)),
                      (dmix, (tq, wb), lambda g, i: (i, do_col // wb + g))] + x_ins,
                 outs=[((T, W), F32, (tq, wb), blk), ((T, W), F32, None, None), ((T, W), F32, None, None)] + x_outs,
                 scratch=[pltpu.VMEM((T, wb), F32), pltpu.VMEM((T, wb), F32)] + x_scratch)


def sb_post_bwd(dqn, dkn, dv, ps, gq, gk, *, name, tm=256):
    T = ps.shape[0]
    tm = _rows(T, tm)
    W = SB_H * SB_D

    def body(dq_ref, dk_ref, dv_ref, q_ref, k_ref, gq_ref, gk_ref, out_ref, dgq_ref, dgk_ref):
        @pl.when(pl.program_id(0) == 0)
        def _():
            dgq_ref[...] = jnp.zeros_like(dgq_ref)
            dgk_ref[...] = jnp.zeros_like(dgk_ref)

        def norm_bwd(x, d, g):
            r = lax.rsqrt(jnp.mean(x * x, axis=-1, keepdims=True) + EPS)
            dg = jnp.sum(d * x * r, axis=0, keepdims=True)
            t = d * g
            return t * r - x * (r * r * r) * jnp.mean(t * x, axis=-1, keepdims=True), dg

        dqs, dks = [], []
        dgq = jnp.zeros((1, SB_D), F32)
        dgk = jnp.zeros((1, SB_D), F32)
        for h in range(SB_H):
            sl = slice(h * SB_D, (h + 1) * SB_D)
            a, ga = norm_bwd(q_ref[:, sl], dq_ref[:, sl] * (SB_D ** -0.5), gq_ref[...])
            b, gb = norm_bwd(k_ref[:, sl], dk_ref[:, sl], gk_ref[...])
            dqs.append(a)
            dks.append(b)
            dgq = dgq + ga
            dgk = dgk + gb
        out_ref[...] = jnp.concatenate(dqs + dks + [dv_ref[...]], axis=-1).astype(MXU)
        dgq_ref[...] += dgq
        dgk_ref[...] += dgk

    row = lambda i: (i, 0)
    fix = lambda i: (0, 0)
    return _call(body, name=name, grid=(T // tm,),
                 ins=[(dqn, (tm, W), row), (dkn, (tm, W), row), (dv, (tm, W), row),
                      (ps, (tm, W), lambda i: (i, 0)), (ps, (tm, W), lambda i: (i, 1)),
                      (gq, (1, SB_D), fix), (gk, (1, SB_D), fix)],
                 outs=[((T, 3 * W), MXU, (tm, 3 * W), lambda i: (i, 0)),
                       ((1, SB_D), F32, (1, SB_D), fix), ((1, SB_D), F32, (1, SB_D), fix)])


GLA_TM = 512
GLA_DK = GLA_H * GLA_K
GLA_DV = GLA_H * GLA_V


def _gla_masks(tm):
    row = lax.broadcasted_iota(jnp.int32, (tm, tm), 0)
    col = lax.broadcasted_iota(jnp.int32, (tm, tm), 1)
    same = (row // CHUNK) == (col // CHUNK)
    return row, col, same


def _gla_gate(glr, w2, b, tm):
    pre = _dot(glr, w2) + b
    la = (jnp.minimum(pre, 0.0) - _softplus_neg_abs(pre)) * (1.0 / 16.0)
    row, col, same = _gla_masks(tm)
    m_incl = jnp.where(same & (col <= row), 1.0, 0.0).astype(MXU)
    m_full = jnp.where(same, 1.0, 0.0).astype(MXU)
    bc = _dot_split(m_incl, la, a_split=False)
    tot = _dot_split(m_full, la, a_split=False)
    return pre, bc, tot


def gla_fwd(pg, glr, w2, b_gate, g_out, *, name):
    T = pg.shape[0]
    tm = _rows(T, GLA_TM)
    ncb = tm // CHUNK
    NC = T // CHUNK

    def body(q_ref, k_ref, v_ref, r_ref, l_ref, w2_ref, b_ref, g_ref, o_ref, st_ref, S):
        @pl.when(pl.program_id(0) == 0)
        def _():
            S[...] = jnp.zeros_like(S)
        _, bc, tot = _gla_gate(l_ref[...], w2_ref[...], b_ref[...], tm)
        kend = k_ref[...] * jnp.exp(tot - bc)
        qs = q_ref[...] * (GLA_K ** -0.5)
        a_all = jnp.exp(tot)
        v = v_ref[...]
        r = r_ref[...]
        rows = [slice(c * CHUNK, (c + 1) * CHUNK) for c in range(ncb)]
        hks = [slice(h * GLA_K, (h + 1) * GLA_K) for h in range(GLA_H)]
        hvs = [slice(h * GLA_V, (h + 1) * GLA_V) for h in range(GLA_H)]
        uts = [[_dot(v[rows[c], hvs[h]], kend[rows[c], hks[h]], TN) for h in range(GLA_H)] for c in range(ncb)]
        for h in range(GLA_H):
            s = S[h]
            for c in range(ncb):
                s = s * a_all[c * CHUNK:c * CHUNK + 1, hks[h]] + uts[c][h]
                st_ref[c, h] = s
            S[h] = s
        for c in range(ncb):
            outs = []
            for h in range(GLA_H):
                o = _dot(qs[rows[c], hks[h]], st_ref[c, h], NT)
                rinv = lax.rsqrt(jnp.mean(o * o, axis=-1, keepdims=True) + EPS)
                rr = r[rows[c], hvs[h]]
                outs.append(o * rinv * g_ref[...] * (rr * _sigmoid(rr)))
            o_ref[pl.ds(c * CHUNK, CHUNK), :] = jnp.concatenate(outs, axis=-1).astype(MXU)

    fix = lambda i: (0, 0)
    return _call(body, name=name, grid=(T // tm,),
                 ins=[(pg, (tm, GLA_DK), lambda i: (i, 0)), (pg, (tm, GLA_DK), lambda i: (i, 1)),
                      (pg, (tm, GLA_DV), lambda i: (i, 1)), (pg, (tm, GLA_DV), lambda i: (i, 2)),
                      (glr, (tm, LANES), lambda i: (i, 0)), (w2, (LANES, GLA_DK), fix),
                      (b_gate, (1, GLA_DK), fix), (g_out, (1, GLA_V), fix)],
                 outs=[((T, GLA_DV), MXU, (tm, GLA_DV), lambda i: (i, 0)),
                       ((NC, GLA_H, GLA_V, GLA_K), F32, (ncb, GLA_H, GLA_V, GLA_K), lambda i: (i, 0, 0, 0))],
                 scratch=[pltpu.VMEM((GLA_H, GLA_V, GLA_K), F32)])


def gla_bwd(pg, glr, w2, b_gate, g_out, st, dmix, *, name):
    T = pg.shape[0]
    tm = _rows(T, GLA_TM)
    ncb = tm // CHUNK
    n = T // tm

    def body(q_ref, k_ref, v_ref, r_ref, l_ref, w2_ref, b_ref, g_ref, st_ref, sp_ref, d_ref,
             dpg_ref, dl_ref, dw2_ref, db_ref, dg_ref, dS, dkend, extra, dst_ref):
        i = pl.program_id(0)

        @pl.when(i == 0)
        def _():
            dS[...] = jnp.zeros_like(dS)
            dw2_ref[...] = jnp.zeros_like(dw2_ref)
            db_ref[...] = jnp.zeros_like(db_ref)
            dg_ref[...] = jnp.zeros_like(dg_ref)
        first_tile = i == n - 1
        glr_v = l_ref[...]
        pre, bc, tot = _gla_gate(glr_v, w2_ref[...], b_ref[...], tm)
        dec = jnp.exp(tot - bc)
        k = k_ref[...]
        kend = k * dec
        qs = q_ref[...] * (GLA_K ** -0.5)
        a_all = jnp.exp(tot)
        v = v_ref[...]
        r = r_ref[...]
        g = g_ref[...]
        dgg = jnp.zeros((1, GLA_V), F32)
        rows = [slice(c * CHUNK, (c + 1) * CHUNK) for c in range(ncb)]
        hks = [slice(h * GLA_K, (h + 1) * GLA_K) for h in range(GLA_H)]
        hvs = [slice(h * GLA_V, (h + 1) * GLA_V) for h in range(GLA_H)]
        heads = range(GLA_H)
        units = [(c, h) for c in range(ncb) for h in heads]
        qh = {u: qs[rows[u[0]], hks[u[1]]] for u in units}
        o = {u: _dot(qh[u], st_ref[u[0], u[1]], NT) for u in units}
        do, dr = {}, {}
        for c, h in units:
            ou = o[c, h]
            rinv = lax.rsqrt(jnp.mean(ou * ou, axis=-1, keepdims=True) + EPS)
            nrm = ou * rinv
            rr = r[rows[c], hvs[h]]
            sg = _sigmoid(rr)
            sil = rr * sg
            d = d_ref[pl.ds(c * CHUNK, CHUNK), pl.ds(h * GLA_V, GLA_V)]
            dr[c, h] = d * nrm * g * (sg * (1.0 + rr * (1.0 - sg)))
            dgg = dgg + jnp.sum(d * sil * nrm, axis=0, keepdims=True)
            dn = d * sil * g
            do[c, h] = (rinv * (dn - nrm * jnp.mean(dn * nrm, axis=-1, keepdims=True))).astype(MXU)
        dq = {u: _dot(do[u], st_ref[u[0], u[1]]) * (GLA_K ** -0.5) for u in units}
        qdo = [[_dot(do[c, h], qh[c, h], TN) for h in heads] for c in range(ncb)]
        for c in range(ncb):
            cr = pl.ds(c * CHUNK, CHUNK)
            dpg_ref[cr, pl.ds(0, GLA_DK)] = jnp.concatenate([dq[c, h] for h in heads], axis=-1).astype(MXU)
            dpg_ref[cr, pl.ds(2 * GLA_DK + GLA_DV, GLA_DV)] = jnp.concatenate([dr[c, h] for h in heads], axis=-1).astype(MXU)
        ex = [[None] * GLA_H for _ in range(ncb)]
        for h in heads:
            ds_h = dS[h]
            for c in reversed(range(ncb)):
                dst = ds_h + qdo[c][h]
                dst_ref[c, h] = dst
                s_p = st_ref[c - 1, h] if c > 0 else jnp.where(first_tile, 0.0, sp_ref[0, h])
                a = a_all[c * CHUNK:c * CHUNK + 1, hks[h]]
                da = jnp.sum(dst * s_p, axis=0, keepdims=True)
                ds_h = dst * a
                ex[c][h] = jnp.broadcast_to(da * a, (CHUNK, GLA_K))
            dS[h] = ds_h
        for c in range(ncb):
            extra[pl.ds(c * CHUNK, CHUNK), :] = jnp.concatenate(ex[c], axis=-1)
        for c in range(ncb):
            cr = pl.ds(c * CHUNK, CHUNK)
            dk_l = [_dot(v[rows[c], hvs[h]], dst_ref[c, h]) for h in heads]
            dv_l = [_dot(kend[rows[c], hks[h]], dst_ref[c, h], NT) for h in heads]
            dpg_ref[cr, pl.ds(2 * GLA_DK, GLA_DV)] = jnp.concatenate(dv_l, axis=-1).astype(MXU)
            dkend[cr, :] = jnp.concatenate(dk_l, axis=-1)
        dke = dkend[...]
        dpg_ref[:, pl.ds(GLA_DK, GLA_DK)] = (dke * dec).astype(MXU)
        e = dke * kend
        row, col, same = _gla_masks(tm)
        m_lt = jnp.where(same & (col < row), 1.0, 0.0).astype(MXU)
        dla = _dot_split(m_lt, e, a_split=False) + extra[...]
        sp = _softplus_neg_abs(pre)
        one_m_sig = jnp.exp(-jnp.maximum(pre, 0.0) - sp)
        dpre = dla * (1.0 / 16.0) * one_m_sig
        dl_ref[...] = _dot(dpre, w2_ref[...], NT).astype(MXU)
        dw2_ref[...] += _dot(glr_v, dpre, TN)
        db_ref[...] += jnp.sum(dpre, axis=0, keepdims=True)
        dg_ref[...] += dgg

    fix = lambda i: (0, 0)
    rev = lambda i: n - 1 - i
    return _call(body, name=name, grid=(n,),
                 ins=[(pg, (tm, GLA_DK), lambda i: (rev(i), 0)), (pg, (tm, GLA_DK), lambda i: (rev(i), 1)),
                      (pg, (tm, GLA_DV), lambda i: (rev(i), 1)), (pg, (tm, GLA_DV), lambda i: (rev(i), 2)),
                      (glr, (tm, LANES), lambda i: (rev(i), 0)), (w2, (LANES, GLA_DK), fix),
                      (b_gate, (1, GLA_DK), fix), (g_out, (1, GLA_V), fix),
                      (st, (ncb, GLA_H, GLA_V, GLA_K), lambda i: (rev(i), 0, 0, 0)),
                      (st, (1, GLA_H, GLA_V, GLA_K), lambda i: (jnp.maximum(rev(i) * ncb - 1, 0), 0, 0, 0)),
                      (dmix, (tm, GLA_DV), lambda i: (rev(i), 0))],
                 outs=[((T, 2 * GLA_DK + 2 * GLA_DV), MXU, (tm, 2 * GLA_DK + 2 * GLA_DV), lambda i: (rev(i), 0)),
                       ((T, LANES), MXU, (tm, LANES), lambda i: (rev(i), 0)),
                       ((LANES, GLA_DK), F32, (LANES, GLA_DK), fix),
                       ((1, GLA_DK), F32, (1, GLA_DK), fix), ((1, GLA_V), F32, (1, GLA_V), fix)],
                 scratch=[pltpu.VMEM((GLA_H, GLA_V, GLA_K), F32), pltpu.VMEM((tm, GLA_DK), F32),
                          pltpu.VMEM((tm, GLA_DK), F32), pltpu.VMEM((ncb, GLA_H, GLA_V, GLA_K), F32)])


def local_step(x, tgt, W, late=None):
    row1 = lambda a, l: a[l:l + 1]
    if late is None:
        hn0 = rms_fwd(x, row1(W["mix_norm"], 0), name="l0_norm")
    else:
        hn0, landed = rms_fwd(x, row1(W["mix_norm"], 0), name="l0_norm", fused=(late["first_src"], N_CHIPS, GATHER))
        W = {**W, **late["unpack_first"](landed)}
    pg = mm([(hn0, W["wi_g"])], name="l0_proj_gla", tn=512)
    ps = mm([(hn0, W["wi_s"])], name="l0_proj_sb", tn=512)
    glr = mm([(hn0, W["wi_l"])], name="l0_proj_gate", out_dtype=MXU)
    og, st = gla_fwd(pg, glr, W["w2"], W["b_gate"], W["g_gla"], name="gla_fwd")
    qn, kn, vh = sb_prep(ps, W["g_q"], W["g_k"], name="sb_prep")
    if late is None:
        osb, sb_cb = sb_fwd(qn, kn, vh, name="sb_fwd")
    else:
        osb, sb_cb, gathered = sb_fwd(qn, kn, vh, name="sb_fwd", fused=(late["src"][0], N_CHIPS, GATHER))
        W = {**W, **late["unpack"][0](gathered)}
    h1 = mm([(og, W["wo_g"]), (osb, W["wo_s"])], res=x, name="l0_out")
    h2, ffn0, landed = ffn_fwd(h1, row1(W["ffn_norm"], 0), W["wg0"], W["wu0"], W["wd0"], 0,
                               fused=None if late is None else (late["src"][1], N_CHIPS, GATHER))
    if late is not None:
        W = {**W, **late["unpack"][1](landed[0])}
    hn1 = rms_fwd(h2, row1(W["mix_norm"], 1), name="l1_norm")
    A = mm([(hn1, W["pw1"])], bias=W["b_pw1"], name="l1_pw1")
    s, cconv, *landed = conv_fwd(A, W["w_dw"], W["b_dw"], W["ln_g"], W["ln_b"], name="conv_fwd",
                                 fused=None if late is None else (late["src"][2], N_CHIPS, GATHER))
    if late is not None:
        W = {**W, **late["unpack"][2](landed[0])}
    h3 = mm([(s, W["pw2"])], bias=W["b_pw2"], res=h2, name="l1_pw2")
    y, ffn1, _ = ffn_fwd(h3, row1(W["ffn_norm"], 1), W["wg1"], W["wu1"], W["wd1"], 1)
    dy, loss_lanes = loss_grad(y, tgt, name="loss")
    G = {}
    dh3, g_fn1, G["wg1"], G["wu1"], G["wd1"], cs3 = ffn_bwd(
        dy, h3, row1(W["ffn_norm"], 1), W["wg1"], W["wu1"], W["wd1"], ffn1, 1)
    G["b_pw2"] = cs3
    ds = mm([(dh3, W["pw2"])], trans_b=True, name="l1_ds")
    G["pw2"] = mm_tn(s, dh3, name="l1_dpw2", tm=1024, tn=1024)
    dc, G["ln_g"], G["ln_b"] = ln_swish_bwd(cconv, ds, W["ln_g"], W["ln_b"], name="ln_bwd")
    dA, G["w_dw"], G["b_dw"], G["b_pw1"], *slots_a = conv_bwd(
        dc, A, W["w_dw"], name="conv_bwd", fused=None if late is None else (late["pack"][0](G), N_DEV, SCATTER))
    G["pw1"] = mm_tn(hn1, dA, name="l1_dpw1", tm=1024, tn=1024)
    dhn1 = mm([(dA, W["pw1"])], trans_b=True, name="l1_dhn")
    dh2, g_mn1, _ = rms_bwd(h2, row1(W["mix_norm"], 1), dhn1, dh3, name="l1_norm_bwd")
    dh1, g_fn0, G["wg0"], G["wu0"], G["wd0"], _ = ffn_bwd(
        dh2, h1, row1(W["ffn_norm"], 0), W["wg0"], W["wu0"], W["wd0"], ffn0, 0)
    dmix = mm([(dh1, W["wo_gs"])], trans_b=True, name="l0_dmix")
    G["wo_g"] = mm_tn(og, dh1, name="l0_dwo_g", tm=512, tn=1024)
    G["wo_s"] = mm_tn(osb, dh1, name="l0_dwo_s", tm=512, tn=1024)
    slots = None
    if late is None:
        dqn, dkn, dvh = sb_bwd(qn, kn, vh, sb_cb, dmix, GLA_DV, name="sb_bwd")
    else:
        dqn, dkn, dvh, slots_b = sb_bwd(qn, kn, vh, sb_cb, dmix, GLA_DV, name="sb_bwd",
                                        fused=(late["pack"][1](G), N_DEV, SCATTER))
        slots = [slots_a[0], slots_b]
    dps, G["g_q"], G["g_k"] = sb_post_bwd(dqn, dkn, dvh, ps, W["g_q"], W["g_k"], name="sb_post_bwd")
    dpg, dglr, G["w2"], G["b_gate"], G["g_gla"] = gla_bwd(
        pg, glr, W["w2"], W["b_gate"], W["g_gla"], st, dmix, name="gla_bwd")
    G["wi_g"] = mm_tn(hn0, dpg, name="l0_dwi_g", tm=1024, tn=512)
    G["wi_s"] = mm_tn(hn0, dps, name="l0_dwi_s", tm=1024, tn=512)
    G["wi_l"] = mm_tn(hn0, dglr, name="l0_dwi_l", tm=1024, tn=LANES)
    first_slots = None
    dhn_pairs = [(dpg, W["wi_g"]), (dps, W["wi_s"]), (dglr, W["wi_l"])]
    if late is None:
        dhn0 = mm(dhn_pairs, trans_b=True, name="l0_dhn")
    else:
        dhn0, first_slots = mm(dhn_pairs, trans_b=True, name="l0_dhn", fused=(late["pack_first"](G), N_DEV, SCATTER))
    dx, g_mn0, _ = rms_bwd(x, row1(W["mix_norm"], 0), dhn0, dh1, name="l0_norm_bwd")
    G["mix_norm"] = jnp.concatenate([g_mn0, g_mn1], axis=0)
    G["ffn_norm"] = jnp.concatenate([g_fn0, g_fn1], axis=0)
    return loss_lanes, dx, G, slots, first_slots


_C_GLA = 2 * GLA_DK + 2 * GLA_DV
_C_SB0 = _C_GLA + GLA_RANK


ITEMS = {
    "w_in": ("hy_w_in", 0, (D, _C_SB0 + 3 * SB_H * SB_D), 1), "wo_gs": ("hy_w_out", 0, (GLA_DV + SB_H * SB_D, D), 0),
    "w2r": ("hy_w_gate2", 0, (GLA_RANK, GLA_DK), 1), "b_pw1": ("cv_b_pw1", None, (1, 2 * D), 1),
    "w_dwr": ("cv_w_dw", 0, (CONV_W, D), 1), "b_dw": ("cv_b_dw", None, (1, D), 1),
    "ln_g": ("cv_ln_g", None, (1, D), 1), "ln_b": ("cv_ln_b", None, (1, D), 1), "b_pw2": ("cv_b_pw2", None, (1, D), 1),
    "pw1": ("cv_w_pw1", 0, (D, 2 * D), 1), "pw2": ("cv_w_pw2", 0, (D, D), 0),
    "wg0": ("ffn_w_gate", 0, (D, F), 1), "wg1": ("ffn_w_gate", 1, (D, F), 1),
    "wu0": ("ffn_w_up", 0, (D, F), 1), "wu1": ("ffn_w_up", 1, (D, F), 1),
    "wd0": ("ffn_w_down", 0, (F, D), 0), "wd1": ("ffn_w_down", 1, (F, D), 0),
}
FIRST_BIG_ITEMS = ["w_in", "wo_gs"]
SMALL_ITEMS = ["w2r", "b_pw1", "w_dwr", "b_dw", "ln_g", "ln_b", "b_pw2"]
LATE_ITEMS = ["pw1", "pw2", "wg0", "wg1", "wu0", "wu1", "wd0", "wd1"]
LATE_SCATTERS = (["wg1", "wu1", "wd1"], ["pw1", "pw2", "wg0", "wu0", "wd0"])


def layout_first(raw, repl):
    w_in = raw["w_in"]
    W = {
        "wi_g": w_in[:, :_C_GLA].astype(MXU),
        "wi_s": w_in[:, _C_SB0:].astype(MXU),
        "wi_l": jnp.pad(w_in[:, _C_GLA:_C_SB0], ((0, 0), (0, LANES - GLA_RANK))).astype(MXU),
        "w2": jnp.pad(raw["w2r"], ((0, LANES - GLA_RANK), (0, 0))),
        "b_gate": repl["hy_b_gate"], "g_gla": repl["hy_gla_norm"],
        "g_q": repl["hy_sb_q_norm"], "g_k": repl["hy_sb_k_norm"],
        "wo_gs": raw["wo_gs"].astype(MXU),
        "b_pw1": raw["b_pw1"], "w_dw": jnp.pad(raw["w_dwr"], ((0, HALO - CONV_W), (0, 0))),
        "b_dw": raw["b_dw"], "ln_g": raw["ln_g"], "ln_b": raw["ln_b"], "b_pw2": raw["b_pw2"],
        "mix_norm": repl["mix_norm"], "ffn_norm": repl["ffn_norm"],
    }
    W["wo_g"] = W["wo_gs"][:GLA_DV]
    W["wo_s"] = W["wo_gs"][GLA_DV:]
    return W


def layout_late(raw):
    return {k: raw[k].astype(MXU) for k in LATE_ITEMS}


def item_grads(G):
    out = {k: G[k] for k in ITEMS if k in G}
    if "wi_g" in G:
        out["w_in"] = jnp.concatenate([G["wi_g"], G["wi_l"][:, :GLA_RANK], G["wi_s"]], axis=1)
        out["wo_gs"] = jnp.concatenate([G["wo_g"], G["wo_s"]], axis=0)
        out["w2r"] = G["w2"][:GLA_RANK]
        out["w_dwr"] = G["w_dw"][:CONV_W]
    return out


def _chip_major(a, ax):
    r, c = a.shape
    if ax == 1:
        a = a.reshape(r, N_CHIPS, c // N_CHIPS).transpose(1, 0, 2)
    return a.reshape(N_CHIPS, -1, LANES)


def _from_chip_major(p, shape, ax):
    r, c = shape
    if ax == 0:
        return p.reshape(r, c)
    return p.reshape(N_CHIPS, r, c // N_CHIPS).transpose(1, 0, 2).reshape(r, c)


def _item_rows(k):
    n = math.prod(ITEMS[k][2]) // N_CHIPS
    assert n % LANES == 0, k
    return n // LANES


def unpack_items(buf, keys):
    out, off = {}, 0
    for k in keys:
        _, _, shape, ax = ITEMS[k]
        out[k] = _from_chip_major(buf[:, off:off + _item_rows(k)], shape, ax)
        off += _item_rows(k)
    return out


def pack_item_grads(raw, keys, dtype=F32):
    cat = jnp.concatenate([_chip_major(raw[k].astype(dtype), ITEMS[k][3]) for k in keys], axis=1)
    n = cat.shape[1]
    rows = -(-n // GRAD_ROWS) * GRAD_ROWS
    return jnp.pad(cat, ((0, 0), (0, rows - n), (0, 0))).reshape(2 * N_CHIPS, rows // 2, LANES)


def _refs_of(keys):
    return list(dict.fromkeys(ITEMS[k][0] for k in keys))


WEIGHTS = ["mix_norm", "ffn_norm", "hy_w_in", "hy_w_gate2", "hy_b_gate", "hy_gla_norm", "hy_sb_q_norm",
           "hy_sb_k_norm", "hy_w_out", "cv_w_pw1", "cv_b_pw1", "cv_w_dw", "cv_b_dw", "cv_ln_g", "cv_ln_b",
           "cv_w_pw2", "cv_b_pw2", "ffn_w_gate", "ffn_w_up", "ffn_w_down"]
REPL =["mix_norm", "ffn_norm", "hy_b_gate", "hy_gla_norm", "hy_sb_q_norm", "hy_sb_k_norm"]
N_CHIPS = 4
N_DEV = 8
GRAD_ROWS = 1024


def _pack(arrs, rows_multiple, dtype):
    flat = jnp.concatenate([a.reshape(-1).astype(dtype) for a in arrs])
    n = flat.shape[0]
    rows = -(-n // (LANES * rows_multiple)) * rows_multiple
    return jnp.pad(flat, (0, rows * LANES - n)).reshape(rows, LANES)


def _unpack(flat2d, shapes):
    flat = flat2d.reshape(-1)
    out, off = [], 0
    for s in shapes:
        n = math.prod(s)
        out.append(flat[off:off + n].reshape(s))
        off += n
    return out


def _coords():
    return lax.axis_index("x"), lax.axis_index("y"), lax.axis_index("c")


def _flip(pos, f):
    return tuple(1 - p if b else p for p, b in zip(pos, f))


def _exchange_copies(src_ref, dst_ref, send_sems, recv_sems, loc_sem, *, flips, src_idx, dst_idx, local_idx,
                     with_recvs=True):
    me = _coords()
    loc = None
    if local_idx is not None:
        si, di = local_idx(me)
        loc = pltpu.make_async_copy(src_ref.at[si], dst_ref.at[di], loc_sem)
    sends, recvs = [], []
    for k, f in enumerate(flips):
        peer = _flip(me, f)
        sends.append(pltpu.make_async_remote_copy(
            src_ref=src_ref.at[src_idx(me, peer)], dst_ref=dst_ref.at[dst_idx(me)],
            send_sem=send_sems.at[k], recv_sem=recv_sems.at[k],
            device_id=peer, device_id_type=pl.DeviceIdType.MESH))
        if with_recvs:
            recvs.append(pltpu.make_async_remote_copy(
                src_ref=src_ref.at[src_idx(peer, me)], dst_ref=dst_ref.at[dst_idx(peer)],
                send_sem=send_sems.at[k], recv_sem=recv_sems.at[k],
                device_id=peer, device_id_type=pl.DeviceIdType.MESH))
    return loc, sends, recvs


def _exchange_start(*refs, **spec):
    loc, sends, _ = _exchange_copies(*refs, with_recvs=False, **spec)
    if loc is not None:
        loc.start()
    for s in sends:
        s.start()


def _exchange_wait(*refs, **spec):
    loc, sends, recvs = _exchange_copies(*refs, **spec)
    for s in sends:
        s.wait_send()
    for r in recvs:
        r.wait_recv()
    if loc is not None:
        loc.wait()


def exchange(src, n_dst, spec, *, name, in_place=False):
    n = len(spec["flips"])
    assert not in_place or (n_dst == src.shape[0] and spec["local_idx"] is None)

    def body(*refs):
        _exchange_start(*refs, **spec)
        _exchange_wait(*refs, **spec)

    return pl.pallas_call(
        body, name=name,
        out_shape=jax.ShapeDtypeStruct((n_dst,) + src.shape[1:], src.dtype),
        in_specs=[pl.BlockSpec(memory_space=pl.ANY)],
        out_specs=pl.BlockSpec(memory_space=pl.ANY),
        scratch_shapes=[pltpu.SemaphoreType.DMA((n,)), pltpu.SemaphoreType.DMA((n,)), pltpu.SemaphoreType.DMA(())],
        input_output_aliases={0: 0} if in_place else {},
    )(src)


CHIP_FLIPS = [(1, 0, 0), (0, 1, 0), (1, 1, 0)]
SIBLING = [(0, 0, 1)]
ALL_FLIPS = [(a, b, c) for a in (0, 1) for b in (0, 1) for c in (0, 1) if (a, b, c) != (0, 0, 0)]


def _chip(pos):
    return 2 * pos[0] + pos[1]


def _dev(pos):
    return 4 * pos[0] + 2 * pos[1] + pos[2]


GATHER = dict(flips=CHIP_FLIPS, src_idx=lambda me, peer: 0, dst_idx=_chip, local_idx=lambda me: (0, _chip(me)))
SCATTER = dict(flips=ALL_FLIPS, src_idx=lambda me, peer: 2 * _chip(peer) + peer[2], dst_idx=_dev,
               local_idx=lambda me: (2 * _chip(me) + me[2], _dev(me)))
SHARE = dict(flips=SIBLING, src_idx=lambda me, peer: me[2], dst_idx=lambda me: me[2], local_idx=None)
ALL_TO_ALL = dict(flips=ALL_FLIPS, src_idx=lambda me, peer: 0, dst_idx=_dev, local_idx=lambda me: (0, _dev(me)))


def sum_slots(x, *, name, tr=512):
    n, R, L = x.shape
    tr = _rows(R, tr)

    def body(x_ref, o_ref):
        acc = x_ref[0]
        for k in range(1, n):
            acc = acc + x_ref[k]
        o_ref[...] = acc

    return _call(body, name=name, grid=(R // tr,),
                 ins=[(x, (n, tr, L), lambda i: (0, i, 0))],
                 outs=[((R, L), F32, (tr, L), lambda i: (i, 0))])[0]


def sum_slots_into_half(x, half, *, name, tr=512):
    n, R, L = x.shape
    tr = _rows(R, tr)

    def body(h_ref, x_ref, o_ref):
        acc = x_ref[0].astype(F32)
        for k in range(1, n):
            acc = acc + x_ref[k].astype(F32)
        o_ref[0] = acc

    return pl.pallas_call(
        body, name=name,
        grid_spec=pltpu.PrefetchScalarGridSpec(
            num_scalar_prefetch=1, grid=(R // tr,),
            in_specs=[pl.BlockSpec((n, tr, L), lambda i, h: (0, i, 0))],
            out_specs=pl.BlockSpec((1, tr, L), lambda i, h: (h[0], i, 0))),
        out_shape=jax.ShapeDtypeStruct((2, R, L), F32),
        compiler_params=pltpu.CompilerParams(dimension_semantics=("arbitrary",), vmem_limit_bytes=VMEM_LIMIT),
    )(jnp.reshape(half, (1,)).astype(jnp.int32), x)


ADAM_ROWS = 256


def adamw(w, g, m, v, *, name):
    shape = w.shape
    w2, g2, m2, v2 = (a.reshape(-1, shape[-1]) for a in (w, g, m, v))
    R, L = w2.shape
    tr = ADAM_ROWS if R % ADAM_ROWS == 0 else next(
        (t for t in range(min(R, 2 * ADAM_ROWS) // SUBLANES * SUBLANES, 0, -SUBLANES) if R % t == 0), R)

    def body(w_ref, g_ref, m_ref, v_ref, d_ref, mo_ref, vo_ref):
        g = g_ref[...]
        m = ADAM_B1 * m_ref[...] + (1.0 - ADAM_B1) * g
        v = ADAM_B2 * v_ref[...] + (1.0 - ADAM_B2) * (g * g)
        m_hat = m / (1.0 - ADAM_B1 ** ADAM_STEP)
        v_hat = v / (1.0 - ADAM_B2 ** ADAM_STEP)
        d_ref[...] = -ADAM_LR * (m_hat / (jnp.sqrt(v_hat) + ADAM_EPS) + ADAM_WD * w_ref[...])
        mo_ref[...] = m
        vo_ref[...] = v

    row = lambda i: (i, 0)
    outs = _call(body, name=name, grid=(R // tr,),
                 ins=[(a, (tr, L), row) for a in (w2, g2, m2, v2)],
                 outs=[((R, L), F32, (tr, L), row)] * 3)
    return [o.reshape(shape) for o in outs]


def kernel(x, mix_norm, ffn_norm, hy_w_in, hy_w_gate2, hy_b_gate, hy_gla_norm, hy_sb_q_norm, hy_sb_k_norm, hy_w_out, cv_w_pw1, cv_b_pw1, cv_w_dw, cv_b_dw, cv_ln_g, cv_ln_b, cv_w_pw2, cv_b_pw2, ffn_w_gate, ffn_w_up, ffn_w_down, loss_target, m_mix_norm, m_ffn_norm, m_hy_w_in, m_hy_w_gate2, m_hy_b_gate, m_hy_gla_norm, m_hy_sb_q_norm, m_hy_sb_k_norm, m_hy_w_out, m_cv_w_pw1, m_cv_b_pw1, m_cv_w_dw, m_cv_b_dw, m_cv_ln_g, m_cv_ln_b, m_cv_w_pw2, m_cv_b_pw2, m_ffn_w_gate, m_ffn_w_up, m_ffn_w_down, v_mix_norm, v_ffn_norm, v_hy_w_in, v_hy_w_gate2, v_hy_b_gate, v_hy_gla_norm, v_hy_sb_q_norm, v_hy_sb_k_norm, v_hy_w_out, v_cv_w_pw1, v_cv_b_pw1, v_cv_w_dw, v_cv_b_dw, v_cv_ln_g, v_cv_ln_b, v_cv_w_pw2, v_cv_b_pw2, v_ffn_w_gate, v_ffn_w_up, v_ffn_w_down):
    w = dict(zip(WEIGHTS, (mix_norm, ffn_norm, hy_w_in, hy_w_gate2, hy_b_gate, hy_gla_norm, hy_sb_q_norm, hy_sb_k_norm, hy_w_out, cv_w_pw1, cv_b_pw1, cv_w_dw, cv_b_dw, cv_ln_g, cv_ln_b, cv_w_pw2, cv_b_pw2, ffn_w_gate, ffn_w_up, ffn_w_down)))
    m = dict(zip(WEIGHTS, (m_mix_norm, m_ffn_norm, m_hy_w_in, m_hy_w_gate2, m_hy_b_gate, m_hy_gla_norm, m_hy_sb_q_norm, m_hy_sb_k_norm, m_hy_w_out, m_cv_w_pw1, m_cv_b_pw1, m_cv_w_dw, m_cv_b_dw, m_cv_ln_g, m_cv_ln_b, m_cv_w_pw2, m_cv_b_pw2, m_ffn_w_gate, m_ffn_w_up, m_ffn_w_down)))
    v = dict(zip(WEIGHTS, (v_mix_norm, v_ffn_norm, v_hy_w_in, v_hy_w_gate2, v_hy_b_gate, v_hy_gla_norm, v_hy_sb_q_norm, v_hy_sb_k_norm, v_hy_w_out, v_cv_w_pw1, v_cv_b_pw1, v_cv_w_dw, v_cv_b_dw, v_cv_ln_g, v_cv_ln_b, v_cv_w_pw2, v_cv_b_pw2, v_ffn_w_gate, v_ffn_w_up, v_ffn_w_down)))
    def local_shards(keys, rows_multiple, dtype):
        item = lambda ref, idx: w[ref] if idx is None else w[ref][idx]
        return _pack([item(*ITEMS[k][:2]) for k in keys], rows_multiple, dtype)[None]

    def finish_reduce(slots, tag):
        halves = sum_slots_into_half(slots, lax.axis_index("c"), name=f"reduce_{tag}_sum")
        return exchange(halves, 2, SHARE, name=f"reduce_{tag}_share", in_place=True).reshape(2 * slots.shape[1], LANES)

    small = exchange(local_shards(SMALL_ITEMS, 8, F32), N_CHIPS, GATHER, name="gather_small")
    raw_small = unpack_items(small, SMALL_ITEMS)
    repl = {n: w[n] for n in REPL}
    stages = (["wg0", "wu0", "wd0"], ["pw1", "pw2", "wd1"], ["wg1", "wu1"])
    late = {"first_src": local_shards(FIRST_BIG_ITEMS, 32, MXU),
            "unpack_first": lambda buf: layout_first({**unpack_items(buf, FIRST_BIG_ITEMS), **raw_small}, repl),
            "src": [local_shards(keys, 32, MXU) for keys in stages],
            "unpack": [functools.partial(unpack_items, keys=keys) for keys in stages],
            "pack": [lambda G, keys=keys: pack_item_grads(item_grads(G), keys, MXU) for keys in LATE_SCATTERS],
            "pack_first": lambda G: pack_item_grads(item_grads(G), FIRST_BIG_ITEMS, MXU)}

    loss_lanes, dx, Gk, late_slots, first_slots = local_step(x[0], loss_target[0], {"mix_norm": w["mix_norm"]}, late)
    rep_names = {"mix_norm": "mix_norm", "ffn_norm": "ffn_norm", "hy_b_gate": "b_gate", "hy_gla_norm": "g_gla",
                 "hy_sb_q_norm": "g_q", "hy_sb_k_norm": "g_k"}
    Gr = {n: Gk[k] for n, k in rep_names.items()}

    groups = [(FIRST_BIG_ITEMS, finish_reduce(first_slots, "first")),
              (LATE_SCATTERS[0], finish_reduce(late_slots[0], "late0")),
              (LATE_SCATTERS[1], finish_reduce(late_slots[1], "late1"))]

    loss_row = jnp.pad(jnp.sum(loss_lanes).reshape(1), (0, LANES - 1))
    small_g = item_grads(Gk)
    rep = _pack([Gr[n] for n in REPL] + [small_g[k] for k in SMALL_ITEMS] + [loss_row], 8, F32)
    rep_all = exchange(rep[None], N_DEV, ALL_TO_ALL, name="reduce_small")
    rep_sum = sum_slots(rep_all, name="reduce_small_sum", tr=rep.shape[0])
    rep_shapes = [w[n].shape for n in REPL] + [ITEMS[k][2] for k in SMALL_ITEMS]
    rep_parts = _unpack(rep_sum, rep_shapes)
    loss = rep_sum.reshape(-1)[sum(math.prod(s) for s in rep_shapes)]
    grads = dict(zip(REPL, rep_parts))
    chip = 2 * lax.axis_index("x") + lax.axis_index("y")
    for k, full in zip(SMALL_ITEMS, rep_parts[len(REPL):]):
        ref, _, (_, c), ax = ITEMS[k]
        assert ax == 1
        grads[ref] = lax.dynamic_slice_in_dim(full, chip * (c // N_CHIPS), c // N_CHIPS, axis=1).reshape(w[ref].shape)

    item_g = {}
    for keys, g_flat in groups:
        off = 0
        for k in keys:
            ref, idx = ITEMS[k][:2]
            item_g[k] = g_flat[off:off + _item_rows(k)].reshape(w[ref].shape if idx is None else w[ref].shape[1:])
            off += _item_rows(k)
    for ref in _refs_of(list(item_g)):
        layers = sorted((ITEMS[k][1], k) for k in item_g if ITEMS[k][0] == ref)
        grads[ref] = jnp.stack([item_g[k] for _, k in layers])
    out = {"grad": grads, "delta": {}, "new_m": {}, "new_v": {}}
    small_refs = _refs_of(SMALL_ITEMS) + REPL
    for n in WEIGHTS:
        if n not in small_refs:
            out["delta"][n], out["new_m"][n], out["new_v"][n] = adamw(w[n], grads[n], m[n], v[n], name=f"adamw_{n}")
    packed = [_pack([t[n] for n in small_refs], 8, F32) for t in (w, grads, m, v)]
    small_shapes = [w[n].shape for n in small_refs]
    for key, buf in zip(("delta", "new_m", "new_v"), adamw(*packed, name="adamw_small")):
        out[key].update(dict(zip(small_refs, _unpack(buf, small_shapes))))

    return (loss, dx[None], *[out["grad"][n] for n in WEIGHTS], *[out["delta"][n] for n in WEIGHTS],
            *[out["new_m"][n] for n in WEIGHTS], *[out["new_v"][n] for n in WEIGHTS])
```
